```python
import jax, jax.numpy as jnp
from jax import lax
import numpy as np

D_MODEL = 2048
BATCH = 2
SEQ = 4096
DEPTH = 4

F32 = jnp.float32
N_EVEN = (DEPTH + 1) // 2
N_ODD = DEPTH // 2
MEM_LEN = 256
CHUNK = 64
MAX_POS_OFFSET = 1024
LN_EPS = 1e-5
NEG_BIG = -1e30

ML_HEADS = 4
ML_DK = D_MODEL // 16
ML_DV = D_MODEL // 8
ML_QK = ML_HEADS * ML_DK
ML_WIDTH = ML_HEADS * ML_DV
CONV_K = 4
ML_COLS = 2 * ML_QK + 2 * ML_WIDTH + 2 * ML_HEADS

RW_HEAD = 64
RW_WIDTH = D_MODEL // 2
RW_HEADS = RW_WIDTH // RW_HEAD
RW_LORA_W = 64
RW_LORA_A = 64
RW_LORA_G = 128
RW_GN_EPS = 64e-5
RW_COLS = 3 * RW_WIDTH + RW_LORA_W + RW_LORA_A + RW_LORA_G
EVEN_COLS = ML_COLS + RW_COLS

RET_HEADS = 8
RET_DK = D_MODEL // RET_HEADS
RET_DV = D_MODEL // RET_HEADS
RET_WIDTH = RET_HEADS * RET_DV
ODD_COLS = 2 * RET_HEADS * RET_DK + 2 * RET_WIDTH
ROPE_BASE = 10000.0

XA_HEADS = 4
XA_HD = D_MODEL // XA_HEADS

N_EXPERTS = 64
TOP_K = 8
N_GROUPS = 8
TOPK_GROUPS = 4
EXPERTS_PER_GROUP = N_EXPERTS // N_GROUPS
EXPERT_HIDDEN = 256
SHARED_HIDDEN = 512
ROUTED_SCALE = 2.5
MOE_BLOCK = 128

ALPHA = (2.0 * DEPTH) ** 0.25
BETA = (8.0 * DEPTH) ** -0.25

kernel_name = 'hybrid_mlstm_rwkv7_retention_moe_block'


def _split(z, sizes):
    return jnp.split(z, np.cumsum(sizes)[:-1].tolist(), axis=-1)


def layer_norm(x, g, b):
    xf = x.astype(F32)
    mu = xf.mean(-1, keepdims=True)
    var = jnp.mean(jnp.square(xf - mu), -1, keepdims=True)
    return ((xf - mu) * lax.rsqrt(var + LN_EPS)).astype(x.dtype) * g + b


def head_norm(h, w, b, eps):
    hf = h.astype(F32)
    mu = hf.mean(-1, keepdims=True)
    var = jnp.mean(jnp.square(hf - mu), -1, keepdims=True)
    y = ((hf - mu) * lax.rsqrt(var + eps)).reshape(*h.shape[:-2], -1) * w
    return y if b is None else y + b


def causal_conv(x, w, b):
    S = x.shape[1]
    K = w.shape[0]
    xp = jnp.pad(x, ((0, 0), (K - 1, 0), (0, 0)))
    y = b
    for j in range(K):
        y = y + w[j] * xp[:, j:j + S]
    return y


def token_shift(z):
    return jnp.pad(z, ((0, 0), (1, 0), (0, 0)))[:, :-1]


def mlstm_chunkwise(q, k, v, i_pre, f_pre):
    B, S, H, DK = q.shape
    DV = v.shape[-1]
    L = CHUNK
    NC = S // L

    def chunks(t):
        t = t.astype(F32).reshape(B, NC, L, H, *t.shape[3:])
        return jnp.moveaxis(t, 3, 1)

    q = chunks(q)
    k = chunks(k) * DK ** -0.5
    v = chunks(v)
    i_pre = chunks(i_pre)
    b = jnp.cumsum(jax.nn.log_sigmoid(chunks(f_pre)), -1)
    b_last = b[..., -1]

    w_end = b_last[..., None] - b + i_pre
    m_loc = w_end.max(-1)
    e_end = jnp.exp(w_end - m_loc[..., None])
    dC = jnp.einsum('bhcsv,bhcsk->bhcvk', v * e_end[..., None], k)
    dn = jnp.einsum('bhcs,bhcsk->bhck', e_end, k)

    def step(carry, inp):
        C, n, m = carry
        dC_c, dn_c, mloc_c, blast_c = inp
        m_new = jnp.maximum(blast_c + m, mloc_c)
        s_prev = jnp.exp(blast_c + m - m_new)
        s_new = jnp.exp(mloc_c - m_new)
        C_new = s_prev[..., None, None] * C + s_new[..., None, None] * dC_c
        n_new = s_prev[..., None] * n + s_new[..., None] * dn_c
        return (C_new, n_new, m_new), (C, n, m)

    init = (jnp.zeros((B, H, DV, DK), F32), jnp.zeros((B, H, DK), F32), jnp.full((B, H), NEG_BIG, F32))
    xs = tuple(jnp.moveaxis(t, 2, 0) for t in (dC, dn, m_loc, b_last))
    _, (C_prev, n_prev, m_prev) = lax.scan(step, init, xs)
    C_prev = jnp.moveaxis(C_prev, 0, 2)
    n_prev = jnp.moveaxis(n_prev, 0, 2)
    m_prev = jnp.moveaxis(m_prev, 0, 2)

    causal = jnp.tril(jnp.ones((L, L), bool))
    log_d = jnp.where(causal, b[..., :, None] - b[..., None, :] + i_pre[..., None, :], -jnp.inf)
    g_inter = b + m_prev[..., None]
    m_t = jnp.maximum(log_d.max(-1), g_inter)
    p = jnp.exp(log_d - m_t[..., None]) * jnp.einsum('bhctk,bhcsk->bhcts', q, k)
    e_inter = jnp.exp(g_inter - m_t)
    num = jnp.einsum('bhcts,bhcsv->bhctv', p, v) + e_inter[..., None] * jnp.einsum('bhcvk,bhctk->bhctv', C_prev, q)
    den = p.sum(-1) + e_inter * jnp.einsum('bhck,bhctk->bhct', n_prev, q)
    h = num / jnp.maximum(jnp.abs(den), jnp.exp(-m_t))[..., None]
    return jnp.moveaxis(h, 1, 3).reshape(B, S, H, DV)


def mlstm_group(z, conv_w, conv_b, gate_b, gn_w):
    B, S, _ = z.shape
    qk, v, o_pre, if_pre = _split(z, [2 * ML_QK, ML_WIDTH, ML_WIDTH, 2 * ML_HEADS])
    qk = jax.nn.silu(causal_conv(qk, conv_w, conv_b))
    q, k = jnp.split(qk, 2, axis=-1)
    gates = if_pre.astype(F32).reshape(B, S, 2, ML_HEADS) + gate_b
    h = mlstm_chunkwise(q.reshape(B, S, ML_HEADS, ML_DK), k.reshape(B, S, ML_HEADS, ML_DK),
                        v.reshape(B, S, ML_HEADS, ML_DV), gates[:, :, 0], gates[:, :, 1])
    return (head_norm(h, gn_w, None, 1e-6) * jax.nn.sigmoid(o_pre.astype(F32))).astype(z.dtype)


def rwkv7_scan(r, decay, k, v, kk, b):
    B, S, H, N = r.shape
    xs = tuple(jnp.moveaxis(t, 1, 0) for t in (r, decay, k, v, kk, b))

    def step(state, inp):
        r_t, w_t, k_t, v_t, kk_t, b_t = inp
        sa = jnp.einsum('bhvk,bhk->bhv', state, kk_t)
        state = (state * w_t[:, :, None, :] - sa[..., None] * b_t[:, :, None, :]
                 + v_t[..., None] * k_t[:, :, None, :])
        return state, jnp.einsum('bhvk,bhk->bhv', state, r_t)

    _, y = lax.scan(step, jnp.zeros((B, H, N, N), F32), xs)
    return jnp.moveaxis(y, 0, 1)


def rwkv7_group(z, mu, w0, w2, a0, a2, g2, k_k, k_a, r_k, gn_w, gn_b):
    B, S, _ = z.shape
    z = z + mu * (token_shift(z) - z)
    r, k, v, xw, xa, xg = _split(z, [RW_WIDTH] * 3 + [RW_LORA_W, RW_LORA_A, RW_LORA_G])
    w_log = -jnp.exp(-jax.nn.softplus(-(w0 + jnp.tanh(xw) @ w2).astype(F32)) - 0.5)
    a = jax.nn.sigmoid((a0 + xa @ a2).astype(F32))
    g = jax.nn.sigmoid(xg) @ g2

    def hs(t):
        return t.astype(F32).reshape(B, S, RW_HEADS, RW_HEAD)

    r, k, v, w_log, a = hs(r), hs(k), hs(v), hs(w_log), hs(a)
    kk = k * k_k.astype(F32).reshape(RW_HEADS, RW_HEAD)
    kk = kk * lax.rsqrt(jnp.maximum(jnp.sum(kk * kk, -1, keepdims=True), 1e-24))
    k = k * (1.0 + (a - 1.0) * k_a.astype(F32).reshape(RW_HEADS, RW_HEAD))
    y = rwkv7_scan(r, jnp.exp(w_log), k, v, kk, kk * a)
    y = head_norm(y, gn_w, gn_b, RW_GN_EPS)
    bonus = (jnp.sum(r * k * r_k.astype(F32), -1, keepdims=True) * v).reshape(B, S, RW_WIDTH)
    return ((y + bonus) * g.astype(F32)).astype(z.dtype)


def even_mixer(x, w_in, conv_w, conv_b, gate_b, ml_gn_w, rw_mu, rw_w0, rw_w2, rw_a0, rw_a2,
               rw_g2, rw_kk, rw_ka, rw_rk, rw_gn_w, rw_gn_b, w_out):
    z = x @ w_in
    h_ml = mlstm_group(z[..., :ML_COLS], conv_w, conv_b, gate_b, ml_gn_w)
    h_rw = rwkv7_group(z[..., ML_COLS:], rw_mu, rw_w0, rw_w2, rw_a0, rw_a2, rw_g2, rw_kk, rw_ka,
                       rw_rk, rw_gn_w, rw_gn_b)
    return jnp.concatenate([h_ml, h_rw], axis=-1) @ w_out


def rotary(t, positions):
    half = t.shape[-1] // 2
    inv = ROPE_BASE ** (-jnp.arange(half, dtype=F32) / half)
    ang = positions.astype(F32)[..., None] * inv
    cos = jnp.cos(ang)[:, :, None, :]
    sin = jnp.sin(ang)[:, :, None, :]
    t1, t2 = t[..., :half], t[..., half:]
    return jnp.concatenate([t1 * cos - t2 * sin, t1 * sin + t2 * cos], axis=-1)


def retention_chunkwise(q, k, v):
    B, S, H, DK = q.shape
    DV = v.shape[-1]
    L = CHUNK
    NC = S // L
    log_gamma = jnp.log1p(-jnp.exp2(-5.0 - jnp.arange(H, dtype=F32)))
    pos = jnp.arange(L, dtype=F32)
    rel = pos[:, None] - pos[None, :]
    decay_in = jnp.where(rel >= 0, jnp.exp(log_gamma[:, None, None] * jnp.maximum(rel, 0.0)), 0.0)
    q_dec = jnp.exp(log_gamma[:, None] * (pos + 1.0))
    k_dec = jnp.exp(log_gamma[:, None] * (L - 1.0 - pos))
    chunk_dec = jnp.exp(log_gamma * L)

    def chunks(t):
        return jnp.moveaxis(t.reshape(B, NC, L, H, t.shape[-1]), 3, 1)

    q, k, v = chunks(q), chunks(k), chunks(v)
    intra = jnp.einsum('bhcts,bhcsv->bhctv', jnp.einsum('bhctk,bhcsk->bhcts', q, k) * decay_in[:, None], v)

    def step(R, inp):
        q_c, k_c, v_c = inp
        y = jnp.einsum('bhtk,bhkv->bhtv', q_c, R) * q_dec[..., None]
        R = R * chunk_dec[:, None, None] + jnp.einsum('bhsk,bhsv->bhkv', k_c * k_dec[..., None], v_c)
        return R, y

    xs = tuple(jnp.moveaxis(t, 2, 0) for t in (q, k, v))
    _, inter = lax.scan(step, jnp.zeros((B, H, DK, DV), F32), xs)
    y = intra + jnp.moveaxis(inter, 0, 2)
    return jnp.moveaxis(y, 1, 3).reshape(B, S, H, DV)


def retention_mixer(x, positions, w_in, gn_w, gn_b, w_out):
    B, S, _ = x.shape
    q, k, v, g = _split(x @ w_in, [RET_HEADS * RET_DK, RET_HEADS * RET_DK, RET_WIDTH, RET_WIDTH])
    q = rotary(q.astype(F32).reshape(B, S, RET_HEADS, RET_DK), positions)
    k = rotary(k.astype(F32).reshape(B, S, RET_HEADS, RET_DK), positions) * RET_DK ** -0.5
    y = retention_chunkwise(q, k, v.astype(F32).reshape(B, S, RET_HEADS, RET_DV))
    y = head_norm(y, gn_w, gn_b, 1e-6)
    return (jax.nn.silu(g.astype(F32)) * y).astype(x.dtype) @ w_out


def cross_attention(x, mem, wq, wkv, wo):
    B, S, D = x.shape
    q = (x @ wq).reshape(B, S, XA_HEADS, XA_HD)
    k, v = jnp.split(mem @ wkv, 2, axis=-1)
    k = k.reshape(B, -1, XA_HEADS, XA_HD)
    v = v.reshape(B, -1, XA_HEADS, XA_HD)
    s = jnp.einsum('bshd,bmhd->bhsm', q, k).astype(F32) * XA_HD ** -0.5
    p = jax.nn.softmax(s, axis=-1).astype(v.dtype)
    return jnp.einsum('bhsm,bmhd->bshd', p, v).reshape(B, S, D) @ wo


def moe_ffn(x, router_w, router_bias, w1, w3, w2, sw1, sw3, sw2):
    B, S, D = x.shape
    t = x.reshape(B * S, D)
    scores = jax.nn.sigmoid((t @ router_w).astype(F32))
    biased = scores + router_bias.astype(F32)
    grp = biased.reshape(-1, N_GROUPS, EXPERTS_PER_GROUP)
    grp_score = lax.top_k(grp, 2)[0].sum(-1)
    _, top_grp = lax.top_k(grp_score, TOPK_GROUPS)
    grp_mask = jax.nn.one_hot(top_grp, N_GROUPS, dtype=F32).sum(-2) > 0
    cand = jnp.where(jnp.repeat(grp_mask, EXPERTS_PER_GROUP, axis=-1), biased, -jnp.inf)
    _, top_e = lax.top_k(cand, TOP_K)
    w_sel = jnp.take_along_axis(scores, top_e, axis=-1)
    w_sel = ROUTED_SCALE * w_sel / w_sel.sum(-1, keepdims=True)
    gates = jnp.einsum('tke,tk->te', jax.nn.one_hot(top_e, N_EXPERTS, dtype=F32), w_sel).astype(x.dtype)

    def expert_block(args):
        tb, gb = args
        h = jax.nn.silu(jnp.einsum('td,edh->teh', tb, w1)) * jnp.einsum('td,edh->teh', tb, w3)
        return jnp.einsum('teh,ehd->td', h * gb[..., None], w2)

    nb = S // MOE_BLOCK

    def blk(a):
        return jnp.moveaxis(a.reshape(B, nb, MOE_BLOCK, a.shape[-1]), 1, 0).reshape(nb, B * MOE_BLOCK, a.shape[-1])

    routed = lax.map(expert_block, (blk(t), blk(gates)))
    routed = jnp.moveaxis(routed.reshape(nb, B, MOE_BLOCK, D), 0, 1).reshape(B, S, D)
    shared = (jax.nn.silu(x @ sw1) * (x @ sw3)) @ sw2
    return routed + shared


def setup_inputs(seed: int = 0) -> dict:
    key = jax.random.key(seed)
    keys = iter(jax.random.split(key, 64))

    def nrm(shape, scale):
        return jax.random.normal(next(keys), shape, F32) * scale

    D = D_MODEL
    E, H, HS = N_EXPERTS, EXPERT_HIDDEN, SHARED_HIDDEN
    x = nrm((BATCH, SEQ, D), 1.0)
    mem = nrm((BATCH, MEM_LEN, D), 1.0)
    positions = (jax.random.randint(next(keys), (BATCH, 1), 0, MAX_POS_OFFSET, dtype=jnp.int32)
                 + jnp.arange(SEQ, dtype=jnp.int32)[None, :])
    rw_w0 = -6.5 + 5.0 * (jnp.arange(RW_WIDTH, dtype=F32) / (RW_WIDTH - 1)) ** 0.85 + nrm((N_EVEN, RW_WIDTH), 0.05)
    ml_gate_b = jnp.stack([nrm((N_EVEN, ML_HEADS), 0.1),
                           jnp.linspace(3.0, 6.0, ML_HEADS, dtype=F32) + nrm((N_EVEN, ML_HEADS), 0.1)], axis=1)
    return {
        'x': x,
        'mem': mem,
        'positions': positions,
        'ln_g': 1.0 + nrm((DEPTH, 3, D), 0.02),
        'ln_b': nrm((DEPTH, 3, D), 0.02),
        'xa_wq': nrm((DEPTH, D, D), D ** -0.5),
        'xa_wkv': nrm((DEPTH, D, 2 * D), D ** -0.5),
        'xa_wo': nrm((DEPTH, D, D), BETA * D ** -0.5),
        'router_w': nrm((DEPTH, D, E), D ** -0.5),
        'router_bias': nrm((DEPTH, E), 0.01),
        'moe_w1': nrm((DEPTH, E, D, H), D ** -0.5),
        'moe_w3': nrm((DEPTH, E, D, H), D ** -0.5),
        'moe_w2': nrm((DEPTH, E, H, D), BETA * H ** -0.5),
        'sh_w1': nrm((DEPTH, D, HS), D ** -0.5),
        'sh_w3': nrm((DEPTH, D, HS), D ** -0.5),
        'sh_w2': nrm((DEPTH, HS, D), BETA * HS ** -0.5),
        'ev_w_in': nrm((N_EVEN, D, EVEN_COLS), D ** -0.5),
        'ml_conv_w': nrm((N_EVEN, CONV_K, 2 * ML_QK), CONV_K ** -0.5),
        'ml_conv_b': nrm((N_EVEN, 2 * ML_QK), 0.02),
        'ml_gate_b': ml_gate_b,
        'ml_gn_w': 1.0 + nrm((N_EVEN, ML_WIDTH), 0.02),
        'rw_mu': jax.random.uniform(next(keys), (N_EVEN, RW_COLS), F32),
        'rw_w0': rw_w0,
        'rw_w2': nrm((N_EVEN, RW_LORA_W, RW_WIDTH), 0.5 * RW_LORA_W ** -0.5),
        'rw_a0': nrm((N_EVEN, RW_WIDTH), 0.1),
        'rw_a2': nrm((N_EVEN, RW_LORA_A, RW_WIDTH), RW_LORA_A ** -0.5),
        'rw_g2': nrm((N_EVEN, RW_LORA_G, RW_WIDTH), RW_LORA_G ** -0.5),
        'rw_kk': 0.85 + nrm((N_EVEN, RW_WIDTH), 0.02),
        'rw_ka': 1.0 + nrm((N_EVEN, RW_WIDTH), 0.02),
        'rw_rk': nrm((N_EVEN, RW_HEADS, RW_HEAD), 0.1),
        'rw_gn_w': 1.0 + nrm((N_EVEN, RW_WIDTH), 0.02),
        'rw_gn_b': nrm((N_EVEN, RW_WIDTH), 0.02),
        'ev_w_out': nrm((N_EVEN, ML_WIDTH + RW_WIDTH, D), BETA * (ML_WIDTH + RW_WIDTH) ** -0.5),
        'od_w_in': nrm((N_ODD, D, ODD_COLS), D ** -0.5),
        'ret_gn_w': 1.0 + nrm((N_ODD, RET_WIDTH), 0.02),
        'ret_gn_b': nrm((N_ODD, RET_WIDTH), 0.02),
        'od_w_out': nrm((N_ODD, RET_WIDTH, D), BETA * RET_WIDTH ** -0.5),
    }


def reference(x, mem, positions, ln_g, ln_b, xa_wq, xa_wkv, xa_wo, router_w, router_bias,
              moe_w1, moe_w3, moe_w2, sh_w1, sh_w3, sh_w2, ev_w_in, ml_conv_w, ml_conv_b,
              ml_gate_b, ml_gn_w, rw_mu, rw_w0, rw_w2, rw_a0, rw_a2, rw_g2, rw_kk, rw_ka, rw_rk,
              rw_gn_w, rw_gn_b, ev_w_out, od_w_in, ret_gn_w, ret_gn_b, od_w_out):
    for layer in range(DEPTH):
        j = layer // 2
        if layer % 2 == 0:
            mix = even_mixer(x, ev_w_in[j], ml_conv_w[j], ml_conv_b[j], ml_gate_b[j], ml_gn_w[j],
                             rw_mu[j], rw_w0[j], rw_w2[j], rw_a0[j], rw_a2[j], rw_g2[j], rw_kk[j],
                             rw_ka[j], rw_rk[j], rw_gn_w[j], rw_gn_b[j], ev_w_out[j])
        else:
            mix = retention_mixer(x, positions, od_w_in[j], ret_gn_w[j], ret_gn_b[j], od_w_out[j])
        x = layer_norm(ALPHA * x + mix, ln_g[layer, 0], ln_b[layer, 0])
        x = layer_norm(ALPHA * x + cross_attention(x, mem, xa_wq[layer], xa_wkv[layer], xa_wo[layer]),
                       ln_g[layer, 1], ln_b[layer, 1])
        x = layer_norm(ALPHA * x + moe_ffn(x, router_w[layer], router_bias[layer], moe_w1[layer],
                                           moe_w3[layer], moe_w2[layer], sh_w1[layer], sh_w3[layer],
                                           sh_w2[layer]),
                       ln_g[layer, 2], ln_b[layer, 2])
    return x
```

```python
import functools

import numpy as np
import jax
import jax.numpy as jnp
from jax import lax
from jax.experimental import pallas as pl
from jax.experimental.pallas import tpu as pltpu

F32 = jnp.float32
BF16 = jnp.bfloat16
HI = lax.Precision.HIGHEST

DEPTH = 4
ALPHA = (2.0 * DEPTH) ** 0.25
LN_EPS = 1e-5
NEG_BIG = -1e30
CONV_K = 4
ML_HEADS = 4
RW_HEAD = 64
RW_LORA_W = 64
RW_LORA_A = 64
RW_LORA_G = 128
RW_GN_EPS = 64e-5
RET_HEADS = 8
ROPE_BASE = 10000.0
XA_HEADS = 4
N_EXPERTS = 64
TOP_K = 8
N_GROUPS = 8
TOPK_GROUPS = 4
ROUTED_SCALE = 2.5

ML_CHUNK = 64
RW_CHUNK = 64
RET_CHUNK = 256

V7X_VMEM_BYTES = 64 * 2 ** 20
VMEM_LIMIT = (V7X_VMEM_BYTES * 3) // 4
LANES = 128


def _params(*sem):
    return pltpu.CompilerParams(dimension_semantics=sem, vmem_limit_bytes=VMEM_LIMIT)


def _bdot(a, b):
    return jnp.dot(a.astype(BF16), b.astype(BF16), preferred_element_type=F32)


def _bdot_nt(a, b):
    return lax.dot_general(a.astype(BF16), b.astype(BF16), (((1,), (1,)), ((), ())),
                           preferred_element_type=F32)


def _bdot_tn(a, b):
    return lax.dot_general(a.astype(BF16), b.astype(BF16), (((0,), (0,)), ((), ())),
                           preferred_element_type=F32)


def _sigmoid(x):
    return 1.0 / (1.0 + jnp.exp(-x))


def _softplus(x):
    return jnp.maximum(x, 0.0) + jnp.log1p(jnp.exp(-jnp.abs(x)))


def _iota2(shape, dim):
    return lax.broadcasted_iota(jnp.int32, shape, dim)


def _mm_kernel(x_ref, w_ref, o_ref):
    o_ref[...] = _bdot(x_ref[...], w_ref[...]).astype(o_ref.dtype)


def _mm(x, w, out_dtype, tm, tn):
    M, K = x.shape
    N = w.shape[1]
    assert M % tm == 0 and N % tn == 0
    return pl.pallas_call(
        _mm_kernel,
        grid=(M // tm, N // tn),
        in_specs=[pl.BlockSpec((tm, K), lambda i, j: (i, 0)),
                  pl.BlockSpec((K, tn), lambda i, j: (0, j))],
        out_specs=pl.BlockSpec((tm, tn), lambda i, j: (i, j)),
        out_shape=jax.ShapeDtypeStruct((M, N), out_dtype),
        compiler_params=_params("parallel", "parallel"),
        name="mm",
    )(x, w)


def _layer_norm_rows(y, g, b):
    mu = jnp.mean(y, axis=-1, keepdims=True)
    d = y - mu
    var = jnp.mean(d * d, axis=-1, keepdims=True)
    return d * lax.rsqrt(var + LN_EPS) * g + b


def _mm_res_ln_kernel(a_ref, w_ref, res_ref, g_ref, b_ref, o_ref):
    y = ALPHA * res_ref[...] + _bdot(a_ref[...], w_ref[...])
    o_ref[...] = _layer_norm_rows(y, g_ref[...], b_ref[...])


def _mm_res_add_ln_kernel(a_ref, w_ref, res_ref, add_ref, g_ref, b_ref, o_ref):
    y = ALPHA * res_ref[...] + (add_ref[...] + _bdot(a_ref[...], w_ref[...]))
    o_ref[...] = _layer_norm_rows(y, g_ref[...], b_ref[...])


def _mm_res_ln(a, w, res, g, b, add=None, tm=256):
    M, K = a.shape
    N = w.shape[1]
    row = lambda i: (i, 0)
    fixed = lambda i: (0, 0)
    in_specs = [pl.BlockSpec((tm, K), row), pl.BlockSpec((K, N), fixed), pl.BlockSpec((tm, N), row)]
    args = [a, w, res]
    if add is not None:
        in_specs.append(pl.BlockSpec((tm, N), row))
        args.append(add)
    in_specs += [pl.BlockSpec((1, N), fixed), pl.BlockSpec((1, N), fixed)]
    args += [g.reshape(1, N), b.reshape(1, N)]
    return pl.pallas_call(
        _mm_res_ln_kernel if add is None else _mm_res_add_ln_kernel,
        grid=(M // tm,),
        in_specs=in_specs,
        out_specs=pl.BlockSpec((tm, N), row),
        out_shape=jax.ShapeDtypeStruct((M, N), F32),
        compiler_params=_params("parallel"),
        name="mm_res_ln",
    )(*args)


def _glu_kernel(x_ref, w1_ref, w3_ref, o_ref):
    x = x_ref[...].astype(BF16)
    h1 = jnp.dot(x, w1_ref[...], preferred_element_type=F32)
    h3 = jnp.dot(x, w3_ref[...], preferred_element_type=F32)
    o_ref[...] = (h1 * _sigmoid(h1) * h3).astype(o_ref.dtype)


def _glu(x, w1, w3, tm=512):
    M, K = x.shape
    N = w1.shape[1]
    return pl.pallas_call(
        _glu_kernel,
        grid=(M // tm,),
        in_specs=[pl.BlockSpec((tm, K), lambda i: (i, 0)),
                  pl.BlockSpec((K, N), lambda i: (0, 0)),
                  pl.BlockSpec((K, N), lambda i: (0, 0))],
        out_specs=pl.BlockSpec((tm, N), lambda i: (i, 0)),
        out_shape=jax.ShapeDtypeStruct((M, N), BF16),
        compiler_params=_params("parallel"),
        name="glu",
    )(x, w1, w3)


def _log_sigmoid(x):
    return jnp.minimum(x, 0.0) - jnp.log1p(jnp.exp(-jnp.abs(x)))


def _mlstm_kernel(qk_ref, v_ref, o_ref, gc_ref, gr_ref, cw_ref, cb_ref, gbc_ref, gbr_ref, gn_ref,
                  out_ref, prev_ref, c_ref, n_ref, m_ref, *, heads, dk, dv):
    L = qk_ref.shape[1]

    @pl.when(pl.program_id(1) == 0)
    def _():
        prev_ref[...] = jnp.zeros_like(prev_ref)
        c_ref[...] = jnp.zeros_like(c_ref)
        n_ref[...] = jnp.zeros_like(n_ref)
        m_ref[...] = jnp.full_like(m_ref, NEG_BIG)

    cur = qk_ref[0]
    prev = prev_ref[...]
    row = _iota2((L, 1), 0)
    acc = cb_ref[...] + cw_ref[CONV_K - 1:CONV_K, :] * cur
    for j in range(CONV_K - 1):
        s = CONV_K - 1 - j
        shifted = jnp.where(row < s, pltpu.roll(prev, s, 0), pltpu.roll(cur, s, 0))
        acc = acc + cw_ref[j:j + 1, :] * shifted
    prev_ref[...] = cur
    qk = acc * _sigmoid(acc)
    qw = heads * dk

    tri = (_iota2((L, L), 0) >= _iota2((L, L), 1)).astype(F32)
    g_col = gc_ref[0] + gbc_ref[...]
    b_col = jnp.dot(tri, _log_sigmoid(g_col), precision=HI, preferred_element_type=F32)
    g_row = gr_ref[0, 0] + gbr_ref[...]
    b_row = lax.dot_general(_log_sigmoid(g_row), tri, (((1,), (1,)), ((), ())), precision=HI,
                            preferred_element_type=F32)
    causal = _iota2((L, L), 0) >= _iota2((L, L), 1)

    for h in range(heads):
        q = qk[:, h * dk:(h + 1) * dk]
        k = qk[:, qw + h * dk:qw + (h + 1) * dk] * dk ** -0.5
        v = v_ref[0, :, h * dv:(h + 1) * dv]
        bc = b_col[:, heads + h:heads + h + 1]
        ic = g_col[:, h:h + 1]
        br = b_row[heads + h:heads + h + 1, :]
        ir = g_row[h:h + 1, :]
        b_last = bc[L - 1:L, :]
        c_prev = c_ref[h]
        n_prev = n_ref[h]
        m_prev = m_ref[h][:, 0:1]

        log_d = jnp.where(causal, bc - br + ir, -jnp.inf)
        g_inter = bc + m_prev
        m_t = jnp.maximum(jnp.max(log_d, axis=-1, keepdims=True), g_inter)
        p = jnp.exp(log_d - m_t) * _bdot_nt(q, k)
        e_inter = jnp.exp(g_inter - m_t)
        num = _bdot(p, v) + e_inter * _bdot(q, c_prev)
        den = jnp.sum(p, axis=-1, keepdims=True) + e_inter * jnp.sum(q * n_prev, axis=-1, keepdims=True)
        hh = num / jnp.maximum(jnp.abs(den), jnp.exp(-m_t))

        w_end = b_last - bc + ic
        m_loc = jnp.max(w_end, axis=0, keepdims=True)
        e_end = jnp.exp(w_end - m_loc)
        ke = k * e_end
        m_new = jnp.maximum(b_last + m_prev, m_loc)
        s_prev = jnp.exp(b_last + m_prev - m_new)
        s_new = jnp.exp(m_loc - m_new)
        c_ref[h] = s_prev * c_prev + s_new * _bdot_tn(ke, v)
        n_ref[h] = s_prev * n_prev + s_new * jnp.sum(ke, axis=0, keepdims=True)
        m_ref[h] = jnp.broadcast_to(m_new, m_ref.shape[1:])

        mu = jnp.mean(hh, axis=-1, keepdims=True)
        d = hh - mu
        var = jnp.mean(d * d, axis=-1, keepdims=True)
        y = d * lax.rsqrt(var + 1e-6) * gn_ref[:, h * dv:(h + 1) * dv]
        out_ref[0, :, h * dv:(h + 1) * dv] = (y * _sigmoid(o_ref[0, :, h * dv:(h + 1) * dv])).astype(out_ref.dtype)


def _mlstm(z3, gates_t, conv_w, conv_b, gate_b, gn_w, *, qk_blk, v_blk, o_blk, gate_blk):
    B, S, _ = z3.shape
    H = ML_HEADS
    W = gn_w.shape[0]
    dv = W // H
    dk = conv_w.shape[1] // (2 * H)
    assert conv_w.shape[1] == W, "q|k block and v block share one column-block width"
    L = ML_CHUNK
    gb_col = jnp.zeros((1, LANES), F32).at[0, :2 * H].set(gate_b.reshape(-1))
    gb_row = gate_b.reshape(2 * H, 1)
    fixed = lambda b, c: (0, 0)
    return pl.pallas_call(
        functools.partial(_mlstm_kernel, heads=H, dk=dk, dv=dv),
        grid=(B, S // L),
        in_specs=[pl.BlockSpec((1, L, W), lambda b, c: (b, c, qk_blk)),
                  pl.BlockSpec((1, L, W), lambda b, c: (b, c, v_blk)),
                  pl.BlockSpec((1, L, W), lambda b, c: (b, c, o_blk)),
                  pl.BlockSpec((1, L, LANES), lambda b, c: (b, c, gate_blk)),
                  pl.BlockSpec((1, 1, 2 * H, L), lambda b, c: (b, c, 0, 0)),
                  pl.BlockSpec((CONV_K, W), fixed),
                  pl.BlockSpec((1, W), fixed),
                  pl.BlockSpec((1, LANES), fixed),
                  pl.BlockSpec((2 * H, 1), fixed),
                  pl.BlockSpec((1, W), fixed)],
        out_specs=pl.BlockSpec((1, L, W), lambda b, c: (b, c, 0)),
        out_shape=jax.ShapeDtypeStruct((B, S, W), BF16),
        scratch_shapes=[pltpu.VMEM((L, W), F32),
                        pltpu.VMEM((H, dk, dv), F32),
                        pltpu.VMEM((H, 1, dk), F32),
                        pltpu.VMEM((H, 1, LANES), F32)],
        compiler_params=_params("parallel", "arbitrary"),
        name="mlstm",
    )(z3, z3, z3, z3, gates_t, conv_w, conv_b.reshape(1, W), gb_col, gb_row, gn_w.reshape(1, W))


def _unit_lower_inverse(a, idx_r, idx_c):
    n = a.shape[0]
    eye = (idx_r == idx_c).astype(F32)
    t = None
    s = 1
    while s < n:
        pair = (idx_r // (2 * s)) == (idx_c // (2 * s))
        off = pair & ((idx_r // s) % 2 == 1) & ((idx_c // s) % 2 == 0)
        q = jnp.where(off, a, 0.0)
        t = eye - q if t is None else t - _bdot(_bdot(t, q), t)
        s *= 2
    return t


def _rwkv_kernel(r_ref, k_ref, v_ref, l_ref, mur_ref, muk_ref, muv_ref, mul_ref, w0_ref, w2_ref,
                 a0_ref, a2_ref, g2_ref, kk_ref, ka_ref, rk_ref, gnw_ref, gnb_ref, out_ref,
                 pr_ref, pk_ref, pv_ref, plo_ref, s_ref, y_ref, *, hd):
    L = r_ref.shape[1]
    W = r_ref.shape[2]
    nh = W // hd

    @pl.when(pl.program_id(2) == 0)
    def _():
        pr_ref[...] = jnp.zeros_like(pr_ref)
        pk_ref[...] = jnp.zeros_like(pk_ref)
        pv_ref[...] = jnp.zeros_like(pv_ref)
        plo_ref[...] = jnp.zeros_like(plo_ref)
        s_ref[...] = jnp.zeros_like(s_ref)

    first = _iota2((L, 1), 0) == 0

    def shift_lerp(x_ref, p_ref, mu_ref):
        cur = x_ref[0]
        shifted = jnp.where(first, p_ref[...], pltpu.roll(cur, 1, 0))
        p_ref[...] = cur[L - 1:L, :]
        return cur + mu_ref[...] * (shifted - cur)

    r = shift_lerp(r_ref, pr_ref, mur_ref)
    k = shift_lerp(k_ref, pk_ref, muk_ref)
    v = shift_lerp(v_ref, pv_ref, muv_ref)
    lo = shift_lerp(l_ref, plo_ref, mul_ref)
    xw = lo[:, :RW_LORA_W]
    xa = lo[:, RW_LORA_W:RW_LORA_W + RW_LORA_A]
    xg = lo[:, RW_LORA_W + RW_LORA_A:]

    w_log = -jnp.exp(-_softplus(-(w0_ref[...] + _bdot(jnp.tanh(xw), w2_ref[...]))) - 0.5)
    a = _sigmoid(a0_ref[...] + _bdot(xa, a2_ref[...]))
    g = _bdot(_sigmoid(xg), g2_ref[...])

    same_head = ((_iota2((W, W), 0) // hd) == (_iota2((W, W), 1) // hd)).astype(F32)

    def head_sum(x):
        return jnp.dot(x, same_head, precision=HI, preferred_element_type=F32)

    kk = k * kk_ref[...]
    kk = kk * lax.rsqrt(jnp.maximum(head_sum(kk * kk), 1e-24))
    k = k * (1.0 + (a - 1.0) * ka_ref[...])
    bv = kk * a

    idx_r = _iota2((L, L), 0)
    idx_c = _iota2((L, L), 1)
    tri = (idx_r >= idx_c).astype(F32)
    c = jnp.dot(tri, w_log, precision=HI, preferred_element_type=F32)
    c_last = c[L - 1:L, :]
    e_end = jnp.exp(c_last - c)
    e_neg = jnp.exp(-c)
    kt = kk * jnp.exp(c - w_log)
    rt = r * jnp.exp(c)
    kh = k * e_neg
    bh = bv * e_neg
    kp = k * e_end
    bp = bv * e_end
    p_last = jnp.exp(c_last)
    strict = idx_r > idx_c
    lower = idx_r >= idx_c

    for h in range(nh):
        sl = slice(h * hd, (h + 1) * hd)
        lhs = jnp.concatenate([kt[:, sl], rt[:, sl]], axis=0)
        rhs = jnp.concatenate([kh[:, sl], bh[:, sl]], axis=0)
        m = _bdot_nt(lhs, rhs)
        a_kk = jnp.where(strict, m[:L, :L], 0.0)
        a_kb = jnp.where(strict, m[:L, L:], 0.0)
        a_rk = jnp.where(lower, m[L:, :L], 0.0)
        a_rb = jnp.where(lower, m[L:, L:], 0.0)
        t_inv = _unit_lower_inverse(a_kb, idx_r, idx_c)
        st = s_ref[h]
        gs = _bdot_nt(lhs, st)
        vh = v[:, sl]
        u = _bdot(t_inv, gs[:L] + _bdot(a_kk, vh))
        y = gs[L:] + _bdot(jnp.concatenate([a_rk, -a_rb], axis=1), jnp.concatenate([vh, u], axis=0))
        s_ref[h] = st * p_last[:, sl] + _bdot_tn(jnp.concatenate([vh, -u], axis=0),
                                                  jnp.concatenate([kp[:, sl], bp[:, sl]], axis=0))
        y_ref[:, sl] = y

    y = y_ref[...]
    mu = head_sum(y) * (1.0 / hd)
    d = y - mu
    var = head_sum(d * d) * (1.0 / hd)
    yn = d * lax.rsqrt(var + RW_GN_EPS) * gnw_ref[...] + gnb_ref[...]
    bonus = head_sum(r * k * rk_ref[...]) * v
    out_ref[0] = ((yn + bonus) * g).astype(out_ref.dtype)


def _rwkv(z3, mu, w0, w2, a0, a2, g2, k_k, k_a, r_k, gn_w, gn_b, *, col0, lora_col0, heads_per_step):
    B, S, _ = z3.shape
    RW = w0.shape[0]
    W = heads_per_step * RW_HEAD
    G = RW // W
    L = RW_CHUNK
    LO = RW_LORA_W + RW_LORA_A + RW_LORA_G
    assert col0 % W == 0 and RW % W == 0 and lora_col0 % LO == 0
    zc = lambda off: (lambda b, g, c: (b, c, off // W + g))
    vec = lambda b, g, c: (0, g)
    row = lambda x: x.reshape(1, -1)
    return pl.pallas_call(
        functools.partial(_rwkv_kernel, hd=RW_HEAD),
        grid=(B, G, S // L),
        in_specs=[pl.BlockSpec((1, L, W), zc(col0)),
                  pl.BlockSpec((1, L, W), zc(col0 + RW)),
                  pl.BlockSpec((1, L, W), zc(col0 + 2 * RW)),
                  pl.BlockSpec((1, L, LO), lambda b, g, c: (b, c, lora_col0 // LO)),
                  pl.BlockSpec((1, W), vec), pl.BlockSpec((1, W), vec), pl.BlockSpec((1, W), vec),
                  pl.BlockSpec((1, LO), lambda b, g, c: (0, 0)),
                  pl.BlockSpec((1, W), vec),
                  pl.BlockSpec((RW_LORA_W, W), vec),
                  pl.BlockSpec((1, W), vec),
                  pl.BlockSpec((RW_LORA_A, W), vec),
                  pl.BlockSpec((RW_LORA_G, W), vec),
                  pl.BlockSpec((1, W), vec), pl.BlockSpec((1, W), vec), pl.BlockSpec((1, W), vec),
                  pl.BlockSpec((1, W), vec), pl.BlockSpec((1, W), vec)],
        out_specs=pl.BlockSpec((1, L, W), lambda b, g, c: (b, c, g)),
        out_shape=jax.ShapeDtypeStruct((B, S, RW), BF16),
        scratch_shapes=[pltpu.VMEM((1, W), F32), pltpu.VMEM((1, W), F32), pltpu.VMEM((1, W), F32),
                        pltpu.VMEM((1, LO), F32),
                        pltpu.VMEM((heads_per_step, RW_HEAD, RW_HEAD), F32),
                        pltpu.VMEM((L, W), F32)],
        compiler_params=_params("parallel", "parallel", "arbitrary"),
        name="rwkv7",
    )(z3, z3, z3, z3, row(mu[:RW]), row(mu[RW:2 * RW]), row(mu[2 * RW:3 * RW]), row(mu[3 * RW:]),
      row(w0), w2.astype(BF16), row(a0), a2.astype(BF16), g2.astype(BF16), row(k_k), row(k_a),
      row(r_k), row(gn_w), row(gn_b))


def _retention_kernel(q_ref, k_ref, v_ref, g_ref, pos_ref, inv_ref, dec_ref, gnw_ref, gnb_ref,
                      out_ref, r_ref, *, heads, log_gamma):
    L = q_ref.shape[1]
    dk = q_ref.shape[2] // heads
    dv = v_ref.shape[2] // heads
    half = dk // 2

    @pl.when(pl.program_id(1) == 0)
    def _():
        r_ref[...] = jnp.zeros_like(r_ref)

    ang = pos_ref[0] * inv_ref[...]
    cos = jnp.cos(ang)
    sin = jnp.sin(ang)
    t_in = _iota2((L, 1), 0).astype(F32)

    def rot(t):
        t1, t2 = t[:, :half], t[:, half:]
        return jnp.concatenate([t1 * cos - t2 * sin, t1 * sin + t2 * cos], axis=-1)

    for h in range(heads):
        lg = log_gamma[h]
        q = rot(q_ref[0, :, h * dk:(h + 1) * dk])
        k = rot(k_ref[0, :, h * dk:(h + 1) * dk]) * dk ** -0.5
        v = v_ref[0, :, h * dv:(h + 1) * dv]
        state = r_ref[h]
        inter = _bdot(q, state) * jnp.exp(lg * (t_in + 1.0))
        intra = _bdot(_bdot_nt(q, k) * dec_ref[h], v)
        r_ref[h] = state * float(np.exp(lg * L)) + _bdot_tn(k * jnp.exp(lg * (L - 1.0 - t_in)), v)
        y = intra + inter
        mu = jnp.mean(y, axis=-1, keepdims=True)
        d = y - mu
        var = jnp.mean(d * d, axis=-1, keepdims=True)
        yn = d * lax.rsqrt(var + 1e-6) * gnw_ref[:, h * dv:(h + 1) * dv] + gnb_ref[:, h * dv:(h + 1) * dv]
        gate = g_ref[0, :, h * dv:(h + 1) * dv]
        out_ref[0, :, h * dv:(h + 1) * dv] = (gate * _sigmoid(gate) * yn).astype(out_ref.dtype)


def _retention(z3, positions, gn_w, gn_b):
    B, S, _ = z3.shape
    W = gn_w.shape[0]
    H = RET_HEADS
    dk = W // H
    half = dk // 2
    L = RET_CHUNK
    log_gamma = np.log1p(-np.exp2(-5.0 - np.arange(H, dtype=np.float64)))
    rel = np.arange(L)[:, None] - np.arange(L)[None, :]
    decay = np.where(rel >= 0, np.exp(log_gamma[:, None, None] * np.maximum(rel, 0)), 0.0).astype(np.float32)
    inv = (ROPE_BASE ** (-jnp.arange(half, dtype=F32) / half)).reshape(1, half)
    pos = positions.astype(F32).reshape(B, S, 1)
    blk = lambda j: pl.BlockSpec((1, L, W), lambda b, c: (b, c, j))
    fixed2 = lambda b, c: (0, 0)
    return pl.pallas_call(
        functools.partial(_retention_kernel, heads=H, log_gamma=tuple(float(x) for x in log_gamma)),
        grid=(B, S // L),
        in_specs=[blk(0), blk(1), blk(2), blk(3),
                  pl.BlockSpec((1, L, 1), lambda b, c: (b, c, 0)),
                  pl.BlockSpec((1, half), fixed2),
                  pl.BlockSpec((H, L, L), lambda b, c: (0, 0, 0)),
                  pl.BlockSpec((1, W), fixed2), pl.BlockSpec((1, W), fixed2)],
        out_specs=pl.BlockSpec((1, L, W), lambda b, c: (b, c, 0)),
        out_shape=jax.ShapeDtypeStruct((B, S, W), BF16),
        scratch_shapes=[pltpu.VMEM((H, dk, W // H), F32)],
        compiler_params=_params("parallel", "arbitrary"),
        name="retention",
    )(z3, z3, z3, z3, pos, inv, jnp.asarray(decay), gn_w.reshape(1, W), gn_b.reshape(1, W))


def _xattn_kernel(q_ref, k_ref, v_ref, o_ref, *, heads):
    hd = q_ref.shape[2] // heads
    for h in range(heads):
        sl = slice(h * hd, (h + 1) * hd)
        s = _bdot_nt(q_ref[0, :, sl], k_ref[0, :, sl]) * hd ** -0.5
        e = jnp.exp(s - jnp.max(s, axis=-1, keepdims=True))
        p = e / jnp.sum(e, axis=-1, keepdims=True)
        o_ref[0, :, sl] = _bdot(p, v_ref[0, :, sl]).astype(o_ref.dtype)


def _xattn(q3, kv3, tq=512):
    B, S, D = q3.shape
    M = kv3.shape[1]
    return pl.pallas_call(
        functools.partial(_xattn_kernel, heads=XA_HEADS),
        grid=(B, S // tq),
        in_specs=[pl.BlockSpec((1, tq, D), lambda b, i: (b, i, 0)),
                  pl.BlockSpec((1, M, D), lambda b, i: (b, 0, 0)),
                  pl.BlockSpec((1, M, D), lambda b, i: (b, 0, 1))],
        out_specs=pl.BlockSpec((1, tq, D), lambda b, i: (b, i, 0)),
        out_shape=jax.ShapeDtypeStruct((B, S, D), BF16),
        compiler_params=_params("parallel", "parallel"),
        name="xattn",
    )(q3, kv3, kv3)


def _first_argmax(vals, lane, big):
    m = jnp.max(vals, axis=-1, keepdims=True)
    idx = jnp.min(jnp.where(vals == m, lane, big), axis=-1, keepdims=True)
    return m, idx


def _router_kernel(x_ref, w_ref, bias_ref, gates_ref):
    E = w_ref.shape[1]
    per = E // N_GROUPS
    scores = _sigmoid(jnp.dot(x_ref[...], w_ref[...], precision=HI, preferred_element_type=F32))
    biased = scores + bias_ref[...]
    lane = _iota2(biased.shape, 1)
    grp = lane // per
    neg = -jnp.inf

    gs = jnp.zeros_like(biased)
    for g in range(N_GROUPS):
        vals = jnp.where(grp == g, biased, neg)
        m1, i1 = _first_argmax(vals, lane, E)
        m2 = jnp.max(jnp.where(lane == i1, neg, vals), axis=-1, keepdims=True)
        gs = jnp.where(grp == g, m1 + m2, gs)

    keep = jnp.zeros(biased.shape, jnp.bool_)
    for _ in range(TOPK_GROUPS):
        _, gi = _first_argmax(gs, grp, N_GROUPS)
        hit = grp == gi
        keep = keep | hit
        gs = jnp.where(hit, neg, gs)

    cand = jnp.where(keep, biased, neg)
    sel = jnp.zeros(biased.shape, jnp.bool_)
    for _ in range(TOP_K):
        _, ei = _first_argmax(cand, lane, E)
        hit = lane == ei
        sel = sel | hit
        cand = jnp.where(hit, neg, cand)

    w_sel = jnp.where(sel, scores, 0.0)
    gates_ref[...] = ROUTED_SCALE * w_sel / jnp.sum(w_sel, axis=-1, keepdims=True)


def _router(x, router_w, router_bias, tm=256):
    T, D = x.shape
    E = router_w.shape[1]
    return pl.pallas_call(
        _router_kernel,
        grid=(T // tm,),
        in_specs=[pl.BlockSpec((tm, D), lambda i: (i, 0)),
                  pl.BlockSpec((D, E), lambda i: (0, 0)),
                  pl.BlockSpec((1, E), lambda i: (0, 0))],
        out_specs=pl.BlockSpec((tm, E), lambda i: (i, 0)),
        out_shape=jax.ShapeDtypeStruct((T, E), F32),
        compiler_params=_params("parallel"),
        name="router",
    )(x, router_w, router_bias.reshape(1, E))


def _experts_kernel(x_ref, gates_ref, w1_ref, w3_ref, w2_ref, o_ref):
    e = pl.program_id(1)

    @pl.when(e == 0)
    def _():
        o_ref[...] = jnp.zeros_like(o_ref)

    x = x_ref[...]
    h1 = jnp.dot(x, w1_ref[0], preferred_element_type=F32)
    h3 = jnp.dot(x, w3_ref[0], preferred_element_type=F32)
    gates = gates_ref[...]
    gate = jnp.sum(jnp.where(_iota2(gates.shape, 1) == e, gates, 0.0), axis=-1, keepdims=True)
    hidden = h1 * _sigmoid(h1) * h3 * gate
    o_ref[...] += jnp.dot(hidden.astype(BF16), w2_ref[0], preferred_element_type=F32)


def _experts(xb, gates, w1, w3, w2, tm=1024):
    T, D = xb.shape
    E, _, Hd = w1.shape
    return pl.pallas_call(
        _experts_kernel,
        grid=(T // tm, E),
        in_specs=[pl.BlockSpec((tm, D), lambda i, e: (i, 0)),
                  pl.BlockSpec((tm, E), lambda i, e: (i, 0)),
                  pl.BlockSpec((1, D, Hd), lambda i, e: (e, 0, 0)),
                  pl.BlockSpec((1, D, Hd), lambda i, e: (e, 0, 0)),
                  pl.BlockSpec((1, Hd, D), lambda i, e: (e, 0, 0))],
        out_specs=pl.BlockSpec((tm, D), lambda i, e: (i, 0)),
        out_shape=jax.ShapeDtypeStruct((T, D), F32),
        compiler_params=_params("parallel", "arbitrary"),
        name="experts",
    )(xb, gates, w1, w3, w2)


def _even_mixer(x2, B, S, ln_g, ln_b, w_in, conv_w, conv_b, gate_b, ml_gn_w, rw_mu, rw_w0, rw_w2,
                rw_a0, rw_a2, rw_g2, rw_kk, rw_ka, rw_rk, rw_gn_w, rw_gn_b, w_out):
    D = x2.shape[1]
    H = ML_HEADS
    ML = ml_gn_w.shape[0]
    RW = rw_w0.shape[0]
    LO = RW_LORA_W + RW_LORA_A + RW_LORA_G
    ml_main = 3 * ML
    ml_cols = ml_main + 2 * H
    pad_to = 512
    used = ml_main + 3 * RW + LO + LANES
    total = -(-used // pad_to) * pad_to
    w_perm = jnp.concatenate([
        w_in[:, :ml_main], w_in[:, ml_cols:ml_cols + 3 * RW + LO], w_in[:, ml_main:ml_cols],
        jnp.zeros((D, total - used + LANES - 2 * H), w_in.dtype)], axis=1).astype(BF16)
    z3 = _mm(x2, w_perm, F32, 512, pad_to).reshape(B, S, total)
    gate_col0 = ml_main + 3 * RW + LO
    L = ML_CHUNK
    gates_t = jnp.swapaxes(z3[:, :, gate_col0:gate_col0 + 2 * H].reshape(B, S // L, L, 2 * H), 2, 3)
    h_ml = _mlstm(z3, gates_t, conv_w, conv_b, gate_b, ml_gn_w,
                  qk_blk=0, v_blk=1, o_blk=2, gate_blk=gate_col0 // LANES)
    h_rw = _rwkv(z3, rw_mu, rw_w0, rw_w2, rw_a0, rw_a2, rw_g2, rw_kk, rw_ka, rw_rk, rw_gn_w, rw_gn_b,
                 col0=ml_main, lora_col0=ml_main + 3 * RW, heads_per_step=4)
    h = jnp.concatenate([h_ml, h_rw], axis=-1).reshape(B * S, ML + RW)
    return _mm_res_ln(h, w_out.astype(BF16), x2, ln_g, ln_b)


def _odd_mixer(x2, B, S, positions, ln_g, ln_b, w_in, gn_w, gn_b, w_out):
    z3 = _mm(x2, w_in.astype(BF16), F32, 512, 512).reshape(B, S, w_in.shape[1])
    h = _retention(z3, positions, gn_w, gn_b)
    return _mm_res_ln(h.reshape(B * S, -1), w_out.astype(BF16), x2, ln_g, ln_b)


def _cross_attention(x2, B, S, mem2, ln_g, ln_b, wq, wkv, wo):
    D = x2.shape[1]
    q = _mm(x2, wq.astype(BF16), BF16, 512, 512)
    kv = _mm(mem2, wkv.astype(BF16), BF16, mem2.shape[0], 512)
    o = _xattn(q.reshape(B, S, D), kv.reshape(B, -1, 2 * D))
    return _mm_res_ln(o.reshape(B * S, D), wo.astype(BF16), x2, ln_g, ln_b)


def _moe(x2, ln_g, ln_b, router_w, router_bias, w1, w3, w2, sw1, sw3, sw2):
    gates = _router(x2, router_w, router_bias)
    routed = _experts(x2.astype(BF16), gates, w1.astype(BF16), w3.astype(BF16), w2.astype(BF16))
    hs = _glu(x2, sw1.astype(BF16), sw3.astype(BF16))
    return _mm_res_ln(hs, sw2.astype(BF16), x2, ln_g, ln_b, add=routed)


def kernel(x, mem, positions, ln_g, ln_b, xa_wq, xa_wkv, xa_wo, router_w, router_bias, moe_w1, moe_w3, moe_w2, sh_w1, sh_w3, sh_w2, ev_w_in, ml_conv_w, ml_conv_b, ml_gate_b, ml_gn_w, rw_mu, rw_w0, rw_w2, rw_a0, rw_a2, rw_g2, rw_kk, rw_ka, rw_rk, rw_gn_w, rw_gn_b, ev_w_out, od_w_in, ret_gn_w, ret_gn_b, od_w_out):
    B, S, D = x.shape
    x2 = x.reshape(B * S, D)
    mem2 = mem.reshape(-1, D)
    for layer in range(ln_g.shape[0]):
        j = layer // 2
        if layer % 2 == 0:
            x2 = _even_mixer(x2, B, S, ln_g[layer, 0], ln_b[layer, 0], ev_w_in[j], ml_conv_w[j],
                             ml_conv_b[j], ml_gate_b[j], ml_gn_w[j], rw_mu[j], rw_w0[j], rw_w2[j],
                             rw_a0[j], rw_a2[j], rw_g2[j], rw_kk[j], rw_ka[j], rw_rk[j], rw_gn_w[j],
                             rw_gn_b[j], ev_w_out[j])
        else:
            x2 = _odd_mixer(x2, B, S, positions, ln_g[layer, 0], ln_b[layer, 0], od_w_in[j],
                            ret_gn_w[j], ret_gn_b[j], od_w_out[j])
        x2 = _cross_attention(x2, B, S, mem2, ln_g[layer, 1], ln_b[layer, 1], xa_wq[layer],
                              xa_wkv[layer], xa_wo[layer])
        x2 = _moe(x2, ln_g[layer, 2], ln_b[layer, 2], router_w[layer], router_bias[layer],
                  moe_w1[layer], moe_w3[layer], moe_w2[layer], sh_w1[layer], sh_w3[layer],
                  sh_w2[layer])
    return x2.reshape(B, S, D)
```

```python
import functools

import numpy as np
import jax
import jax.numpy as jnp
from jax import lax
from jax.experimental import pallas as pl
from jax.experimental.pallas import tpu as pltpu

F32 = jnp.float32
BF16 = jnp.bfloat16
HI = lax.Precision.HIGHEST

DEPTH = 4
ALPHA = (2.0 * DEPTH) ** 0.25
LN_EPS = 1e-5
NEG_BIG = -1e30
CONV_K = 4
ML_HEADS = 4
RW_HEAD = 64
RW_LORA_W = 64
RW_LORA_A = 64
RW_LORA_G = 128
RW_GN_EPS = 64e-5
RET_HEADS = 8
ROPE_BASE = 10000.0
XA_HEADS = 4
N_EXPERTS = 64
TOP_K = 8
N_GROUPS = 8
TOPK_GROUPS = 4
ROUTED_SCALE = 2.5

ML_CHUNK = 64
RW_CHUNK = 64
RET_CHUNK = 256

V7X_VMEM_BYTES = 64 * 2 ** 20
VMEM_LIMIT = (V7X_VMEM_BYTES * 3) // 4
LANES = 128
V7X_MXU_DIM = 256


def _params(*sem):
    return pltpu.CompilerParams(dimension_semantics=sem, vmem_limit_bytes=VMEM_LIMIT)


def _bdot(a, b):
    return jnp.dot(a.astype(BF16), b.astype(BF16), preferred_element_type=F32)


def _bdot_nt(a, b):
    return lax.dot_general(a.astype(BF16), b.astype(BF16), (((1,), (1,)), ((), ())),
                           preferred_element_type=F32)


def _bdot_tn(a, b):
    return lax.dot_general(a.astype(BF16), b.astype(BF16), (((0,), (0,)), ((), ())),
                           preferred_element_type=F32)


def _sigmoid(x):
    return 1.0 / (1.0 + jnp.exp(-x))


def _softplus(x):
    return jnp.maximum(x, 0.0) + jnp.log1p(jnp.exp(-jnp.abs(x)))


def _iota2(shape, dim):
    return lax.broadcasted_iota(jnp.int32, shape, dim)


def _mm_kernel(x_ref, w_ref, o_ref):
    o_ref[...] = _bdot(x_ref[...], w_ref[...]).astype(o_ref.dtype)


def _mm(x, w, layer, out_dtype, tm, tn):
    M, K = x.shape
    N = w.shape[2]
    assert M % tm == 0 and N % tn == 0
    return pl.pallas_call(
        _mm_kernel,
        grid=(M // tm, N // tn),
        in_specs=[pl.BlockSpec((tm, K), lambda i, j: (i, 0)),
                  pl.BlockSpec((None, K, tn), lambda i, j: (layer, 0, j))],
        out_specs=pl.BlockSpec((tm, tn), lambda i, j: (i, j)),
        out_shape=jax.ShapeDtypeStruct((M, N), out_dtype),
        compiler_params=_params("parallel", "parallel"),
        name="mm",
    )(x, w)


def _layer_norm_rows(y, g, b):
    mu = jnp.mean(y, axis=-1, keepdims=True)
    d = y - mu
    var = jnp.mean(d * d, axis=-1, keepdims=True)
    return d * lax.rsqrt(var + LN_EPS) * g + b


def _mm_res_ln_kernel(a_ref, w_ref, res_ref, g_ref, b_ref, o_ref, ob_ref):
    y = ALPHA * res_ref[...] + _bdot(a_ref[...], w_ref[...])
    out = _layer_norm_rows(y, g_ref[...], b_ref[...])
    o_ref[...] = out
    ob_ref[...] = out.astype(ob_ref.dtype)


def _mm_res_add_ln_kernel(a_ref, w_ref, res_ref, add_ref, g_ref, b_ref, o_ref, ob_ref):
    y = ALPHA * res_ref[...] + (add_ref[...] + _bdot(a_ref[...], w_ref[...]))
    out = _layer_norm_rows(y, g_ref[...], b_ref[...])
    o_ref[...] = out
    ob_ref[...] = out.astype(ob_ref.dtype)


def _mm_res_ln(a, w, layer, res, g, b, add=None, tm=256):
    M, K = a.shape
    N = w.shape[2]
    row = lambda i: (i, 0)
    fixed = lambda i: (0, 0)
    in_specs = [pl.BlockSpec((tm, K), row), pl.BlockSpec((None, K, N), lambda i: (layer, 0, 0)),
                pl.BlockSpec((tm, N), row)]
    args = [a, w, res]
    if add is not None:
        in_specs.append(pl.BlockSpec((tm, N), row))
        args.append(add)
    in_specs += [pl.BlockSpec((1, N), fixed), pl.BlockSpec((1, N), fixed)]
    args += [g.reshape(1, N), b.reshape(1, N)]
    return pl.pallas_call(
        _mm_res_ln_kernel if add is None else _mm_res_add_ln_kernel,
        grid=(M // tm,),
        in_specs=in_specs,
        out_specs=[pl.BlockSpec((tm, N), row), pl.BlockSpec((tm, N), row)],
        out_shape=[jax.ShapeDtypeStruct((M, N), F32), jax.ShapeDtypeStruct((M, N), BF16)],
        compiler_params=_params("parallel"),
        name="mm_res_ln",
    )(*args)


def _glu_kernel(x_ref, w1_ref, w3_ref, o_ref):
    x = x_ref[...]
    h1 = jnp.dot(x, w1_ref[...], preferred_element_type=F32)
    h3 = jnp.dot(x, w3_ref[...], preferred_element_type=F32)
    o_ref[...] = (h1 * _sigmoid(h1) * h3).astype(o_ref.dtype)


def _glu(xb, w1, w3, layer, tm=512):
    M, K = xb.shape
    N = w1.shape[2]
    wspec = pl.BlockSpec((None, K, N), lambda i: (layer, 0, 0))
    return pl.pallas_call(
        _glu_kernel,
        grid=(M // tm,),
        in_specs=[pl.BlockSpec((tm, K), lambda i: (i, 0)), wspec, wspec],
        out_specs=pl.BlockSpec((tm, N), lambda i: (i, 0)),
        out_shape=jax.ShapeDtypeStruct((M, N), BF16),
        compiler_params=_params("parallel"),
        name="glu",
    )(xb, w1, w3)


def _log_sigmoid(x):
    return jnp.minimum(x, 0.0) - jnp.log1p(jnp.exp(-jnp.abs(x)))


def _mlstm_kernel(qk_ref, v_ref, o_ref, gc_ref, gr_ref, cw_ref, cb_ref, gbc_ref, gbr_ref, gn_ref,
                  out_ref, prev_ref, c_ref, n_ref, m_ref, *, heads, dk, dv):
    L = qk_ref.shape[1]

    @pl.when(pl.program_id(1) == 0)
    def _():
        prev_ref[...] = jnp.zeros_like(prev_ref)
        c_ref[...] = jnp.zeros_like(c_ref)
        n_ref[...] = jnp.zeros_like(n_ref)
        m_ref[...] = jnp.full_like(m_ref, NEG_BIG)

    cur = qk_ref[0].astype(F32)
    prev = prev_ref[...]
    row = _iota2((L, 1), 0)
    acc = cb_ref[...] + cw_ref[CONV_K - 1:CONV_K, :] * cur
    for j in range(CONV_K - 1):
        s = CONV_K - 1 - j
        shifted = jnp.where(row < s, pltpu.roll(prev, s, 0), pltpu.roll(cur, s, 0))
        acc = acc + cw_ref[j:j + 1, :] * shifted
    prev_ref[...] = cur
    qk = acc * _sigmoid(acc)
    qw = heads * dk

    tri = (_iota2((L, L), 0) >= _iota2((L, L), 1)).astype(F32)
    g_col = gc_ref[0].astype(F32) + gbc_ref[...]
    b_col = jnp.dot(tri, _log_sigmoid(g_col), precision=HI, preferred_element_type=F32)
    g_row = gr_ref[0, 0] + gbr_ref[...]
    b_row = lax.dot_general(_log_sigmoid(g_row), tri, (((1,), (1,)), ((), ())), precision=HI,
                            preferred_element_type=F32)
    causal = _iota2((L, L), 0) >= _iota2((L, L), 1)

    for h in range(heads):
        q = qk[:, h * dk:(h + 1) * dk]
        k = qk[:, qw + h * dk:qw + (h + 1) * dk] * dk ** -0.5
        v = v_ref[0, :, h * dv:(h + 1) * dv]
        bc = b_col[:, heads + h:heads + h + 1]
        ic = g_col[:, h:h + 1]
        br = b_row[heads + h:heads + h + 1, :]
        ir = g_row[h:h + 1, :]
        b_last = bc[L - 1:L, :]
        c_prev = c_ref[h]
        n_prev = n_ref[h]
        m_prev = m_ref[h][:, 0:1]

        log_d = jnp.where(causal, bc - br + ir, -jnp.inf)
        g_inter = bc + m_prev
        m_t = jnp.maximum(jnp.max(log_d, axis=-1, keepdims=True), g_inter)
        p = jnp.exp(log_d - m_t) * _bdot_nt(q, k)
        e_inter = jnp.exp(g_inter - m_t)
        num = _bdot(p, v) + e_inter * _bdot(q, c_prev)
        den = jnp.sum(p, axis=-1, keepdims=True) + e_inter * jnp.sum(q * n_prev, axis=-1, keepdims=True)
        hh = num / jnp.maximum(jnp.abs(den), jnp.exp(-m_t))

        w_end = b_last - bc + ic
        m_loc = jnp.max(w_end, axis=0, keepdims=True)
        e_end = jnp.exp(w_end - m_loc)
        ke = k * e_end
        m_new = jnp.maximum(b_last + m_prev, m_loc)
        s_prev = jnp.exp(b_last + m_prev - m_new)
        s_new = jnp.exp(m_loc - m_new)
        c_ref[h] = s_prev * c_prev + s_new * _bdot_tn(ke, v)
        n_ref[h] = s_prev * n_prev + s_new * jnp.sum(ke, axis=0, keepdims=True)
        m_ref[h] = jnp.broadcast_to(m_new, m_ref.shape[1:])

        mu = jnp.mean(hh, axis=-1, keepdims=True)
        d = hh - mu
        var = jnp.mean(d * d, axis=-1, keepdims=True)
        y = d * lax.rsqrt(var + 1e-6) * gn_ref[:, h * dv:(h + 1) * dv]
        out_ref[0, :, h * dv:(h + 1) * dv] = (y * _sigmoid(o_ref[0, :, h * dv:(h + 1) * dv].astype(F32))).astype(out_ref.dtype)


def _mlstm(z3, gates_t, conv_w, conv_b, gate_b, gn_w, *, qk_blk, v_blk, o_blk, gate_blk):
    B, S, _ = z3.shape
    H = ML_HEADS
    W = gn_w.shape[0]
    dv = W // H
    dk = conv_w.shape[1] // (2 * H)
    assert conv_w.shape[1] == W, "q|k block and v block share one column-block width"
    L = ML_CHUNK
    gb_col = jnp.zeros((1, LANES), F32).at[0, :2 * H].set(gate_b.reshape(-1))
    gb_row = gate_b.reshape(2 * H, 1)
    fixed = lambda b, c: (0, 0)
    return pl.pallas_call(
        functools.partial(_mlstm_kernel, heads=H, dk=dk, dv=dv),
        grid=(B, S // L),
        in_specs=[pl.BlockSpec((1, L, W), lambda b, c: (b, c, qk_blk)),
                  pl.BlockSpec((1, L, W), lambda b, c: (b, c, v_blk)),
                  pl.BlockSpec((1, L, W), lambda b, c: (b, c, o_blk)),
                  pl.BlockSpec((1, L, LANES), lambda b, c: (b, c, gate_blk)),
                  pl.BlockSpec((1, 1, 2 * H, L), lambda b, c: (b, c, 0, 0)),
                  pl.BlockSpec((CONV_K, W), fixed),
                  pl.BlockSpec((1, W), fixed),
                  pl.BlockSpec((1, LANES), fixed),
                  pl.BlockSpec((2 * H, 1), fixed),
                  pl.BlockSpec((1, W), fixed)],
        out_specs=pl.BlockSpec((1, L, W), lambda b, c: (b, c, 0)),
        out_shape=jax.ShapeDtypeStruct((B, S, W), BF16),
        scratch_shapes=[pltpu.VMEM((L, W), F32),
                        pltpu.VMEM((H, dk, dv), F32),
                        pltpu.VMEM((H, 1, dk), F32),
                        pltpu.VMEM((H, 1, LANES), F32)],
        compiler_params=_params("parallel", "arbitrary"),
        name="mlstm",
    )(z3, z3, z3, z3, gates_t, conv_w, conv_b.reshape(1, W), gb_col, gb_row, gn_w.reshape(1, W))


def _unit_lower_inverse(a, idx_r, idx_c, n):
    eye = (idx_r == idx_c).astype(F32)
    t = None
    s = 1
    while s < n:
        pair = (idx_r // (2 * s)) == (idx_c // (2 * s))
        off = pair & ((idx_r // s) % 2 == 1) & ((idx_c // s) % 2 == 0)
        q = jnp.where(off, a, 0.0)
        t = eye - q if t is None else t - _bdot(_bdot(t, q), t)
        s *= 2
    return t


def _rwkv_kernel(r_ref, k_ref, v_ref, l_ref, mur_ref, muk_ref, muv_ref, mul_ref, w0_ref, w2_ref,
                 a0_ref, a2_ref, g2_ref, kk_ref, ka_ref, rk_ref, gnw_ref, gnb_ref, out_ref,
                 pr_ref, pk_ref, pv_ref, plo_ref, s_ref, *, hd, gw):
    L = r_ref.shape[1]
    W = r_ref.shape[2]
    ng = W // gw
    hpg = gw // hd
    assert hpg * L == gw

    @pl.when(pl.program_id(2) == 0)
    def _():
        pr_ref[...] = jnp.zeros_like(pr_ref)
        pk_ref[...] = jnp.zeros_like(pk_ref)
        pv_ref[...] = jnp.zeros_like(pv_ref)
        plo_ref[...] = jnp.zeros_like(plo_ref)
        s_ref[...] = jnp.zeros_like(s_ref)

    first = _iota2((L, 1), 0) == 0

    def shift_lerp(x_ref, p_ref, mu_ref):
        cur = x_ref[0].astype(F32)
        shifted = jnp.where(first, p_ref[...], pltpu.roll(cur, 1, 0))
        p_ref[...] = cur[L - 1:L, :]
        return cur + mu_ref[...] * (shifted - cur)

    r = shift_lerp(r_ref, pr_ref, mur_ref)
    k = shift_lerp(k_ref, pk_ref, muk_ref)
    v = shift_lerp(v_ref, pv_ref, muv_ref)
    lo = shift_lerp(l_ref, plo_ref, mul_ref)
    xw = lo[:, :RW_LORA_W]
    xa = lo[:, RW_LORA_W:RW_LORA_W + RW_LORA_A]
    xg = lo[:, RW_LORA_W + RW_LORA_A:]

    w_log = -jnp.exp(-_softplus(-(w0_ref[...] + _bdot(jnp.tanh(xw), w2_ref[...]))) - 0.5)
    a = _sigmoid(a0_ref[...] + _bdot(xa, a2_ref[...]))
    g = _bdot(_sigmoid(xg), g2_ref[...])

    idx_r = _iota2((gw, gw), 0)
    idx_c = _iota2((gw, gw), 1)
    same_head = (idx_r // hd) == (idx_c // hd)
    same_head_f = same_head.astype(F32)

    def head_sum(x):
        return jnp.concatenate(
            [jnp.dot(x[:, i * gw:(i + 1) * gw], same_head_f, precision=HI, preferred_element_type=F32)
             for i in range(ng)], axis=1)

    kk = k * kk_ref[...]
    kk = kk * lax.rsqrt(jnp.maximum(head_sum(kk * kk), 1e-24))
    k = k * (1.0 + (a - 1.0) * ka_ref[...])
    bv = kk * a

    tri = (_iota2((L, L), 0) >= _iota2((L, L), 1)).astype(F32)
    c = jnp.dot(tri, w_log, precision=HI, preferred_element_type=F32)
    c_last = c[L - 1:L, :]
    e_end = jnp.exp(c_last - c)
    e_neg = jnp.exp(-c)
    kt = (kk * jnp.exp(c - w_log)).astype(BF16)
    rt = (r * jnp.exp(c)).astype(BF16)
    kh = (k * e_neg).astype(BF16)
    bh = (bv * e_neg).astype(BF16)
    kp = (k * e_end).astype(BF16)
    bp = (bv * e_end).astype(BF16)
    vb = v.astype(BF16)
    p_last = jnp.exp(c_last)

    strict = same_head & (idx_r > idx_c)
    lower = same_head & (idx_r >= idx_c)
    zero = jnp.zeros((), BF16)

    def tiled(x):
        return jnp.concatenate([x] * hpg, axis=0)

    def block_diag(x):
        return jnp.where(same_head, tiled(x), zero)

    ys = []
    for i in range(ng):
        sl = slice(i * gw, (i + 1) * gw)
        lhs = jnp.concatenate([block_diag(kt[:, sl]), block_diag(rt[:, sl])], axis=0)
        rhs = jnp.concatenate([tiled(kh[:, sl]), tiled(bh[:, sl])], axis=0)
        m = _bdot_nt(lhs, rhs)
        a_kk = jnp.where(strict, m[:gw, :gw], 0.0)
        a_kb = jnp.where(strict, m[:gw, gw:], 0.0)
        a_rk = jnp.where(lower, m[gw:, :gw], 0.0)
        a_rb = jnp.where(lower, m[gw:, gw:], 0.0)
        t_inv = _unit_lower_inverse(a_kb, idx_r, idx_c, L)
        st = s_ref[i]
        gs = _bdot_nt(lhs, st)
        v_bd = block_diag(vb[:, sl])
        u = _bdot(t_inv, gs[:gw] + _bdot(a_kk, v_bd))
        y = gs[gw:] + _bdot(jnp.concatenate([a_rk, -a_rb], axis=1),
                            jnp.concatenate([v_bd, u.astype(BF16)], axis=0))
        s_ref[i] = st * p_last[:, sl] + _bdot_tn(
            jnp.concatenate([v_bd, (-u).astype(BF16)], axis=0),
            jnp.concatenate([block_diag(kp[:, sl]), block_diag(bp[:, sl])], axis=0))
        ys.append(sum(y[j * L:(j + 1) * L] for j in range(hpg)))

    y = jnp.concatenate(ys, axis=1)
    mu = head_sum(y) * (1.0 / hd)
    d = y - mu
    var = head_sum(d * d) * (1.0 / hd)
    yn = d * lax.rsqrt(var + RW_GN_EPS) * gnw_ref[...] + gnb_ref[...]
    bonus = head_sum(r * k * rk_ref[...]) * v
    out_ref[0] = ((yn + bonus) * g).astype(out_ref.dtype)


def _rwkv(z3, mu, w0, w2, a0, a2, g2, k_k, k_a, r_k, gn_w, gn_b, *, col0, lora_col0, width):
    B, S, _ = z3.shape
    RW = w0.shape[0]
    W = width
    G = RW // W
    L = RW_CHUNK
    LO = RW_LORA_W + RW_LORA_A + RW_LORA_G
    assert col0 % W == 0 and RW % W == 0 and lora_col0 % LO == 0 and W % V7X_MXU_DIM == 0
    zc = lambda off: (lambda b, g, c: (b, c, off // W + g))
    vec = lambda b, g, c: (0, g)
    row = lambda x: x.reshape(1, -1)
    return pl.pallas_call(
        functools.partial(_rwkv_kernel, hd=RW_HEAD, gw=V7X_MXU_DIM),
        grid=(B, G, S // L),
        in_specs=[pl.BlockSpec((1, L, W), zc(col0)),
                  pl.BlockSpec((1, L, W), zc(col0 + RW)),
                  pl.BlockSpec((1, L, W), zc(col0 + 2 * RW)),
                  pl.BlockSpec((1, L, LO), lambda b, g, c: (b, c, lora_col0 // LO)),
                  pl.BlockSpec((1, W), vec), pl.BlockSpec((1, W), vec), pl.BlockSpec((1, W), vec),
                  pl.BlockSpec((1, LO), lambda b, g, c: (0, 0)),
                  pl.BlockSpec((1, W), vec),
                  pl.BlockSpec((RW_LORA_W, W), vec),
                  pl.BlockSpec((1, W), vec),
                  pl.BlockSpec((RW_LORA_A, W), vec),
                  pl.BlockSpec((RW_LORA_G, W), vec),
                  pl.BlockSpec((1, W), vec), pl.BlockSpec((1, W), vec), pl.BlockSpec((1, W), vec),
                  pl.BlockSpec((1, W), vec), pl.BlockSpec((1, W), vec)],
        out_specs=pl.BlockSpec((1, L, W), lambda b, g, c: (b, c, g)),
        out_shape=jax.ShapeDtypeStruct((B, S, RW), BF16),
        scratch_shapes=[pltpu.VMEM((1, W), F32), pltpu.VMEM((1, W), F32), pltpu.VMEM((1, W), F32),
                        pltpu.VMEM((1, LO), F32),
                        pltpu.VMEM((W // V7X_MXU_DIM, V7X_MXU_DIM, V7X_MXU_DIM), F32)],
        compiler_params=_params("parallel", "parallel", "arbitrary"),
        name="rwkv7",
    )(z3, z3, z3, z3, row(mu[:RW]), row(mu[RW:2 * RW]), row(mu[2 * RW:3 * RW]), row(mu[3 * RW:]),
      row(w0), w2.astype(BF16), row(a0), a2.astype(BF16), g2.astype(BF16), row(k_k), row(k_a),
      row(r_k), row(gn_w), row(gn_b))


def _retention_kernel(q_ref, k_ref, v_ref, g_ref, pos_ref, inv_ref, dec_ref, gnw_ref, gnb_ref,
                      out_ref, r_ref, *, heads, log_gamma):
    L = q_ref.shape[1]
    dk = q_ref.shape[2] // heads
    dv = v_ref.shape[2] // heads
    half = dk // 2

    @pl.when(pl.program_id(1) == 0)
    def _():
        r_ref[...] = jnp.zeros_like(r_ref)

    ang = pos_ref[0] * inv_ref[...]
    cos = jnp.cos(ang)
    sin = jnp.sin(ang)
    t_in = _iota2((L, 1), 0).astype(F32)

    def rot(t):
        t1, t2 = t[:, :half], t[:, half:]
        return jnp.concatenate([t1 * cos - t2 * sin, t1 * sin + t2 * cos], axis=-1)

    for h in range(heads):
        lg = log_gamma[h]
        q = rot(q_ref[0, :, h * dk:(h + 1) * dk].astype(F32))
        k = rot(k_ref[0, :, h * dk:(h + 1) * dk].astype(F32)) * dk ** -0.5
        v = v_ref[0, :, h * dv:(h + 1) * dv]
        state = r_ref[h]
        inter = _bdot(q, state) * jnp.exp(lg * (t_in + 1.0))
        intra = _bdot(_bdot_nt(q, k) * dec_ref[h], v)
        r_ref[h] = state * float(np.exp(lg * L)) + _bdot_tn(k * jnp.exp(lg * (L - 1.0 - t_in)), v)
        y = intra + inter
        mu = jnp.mean(y, axis=-1, keepdims=True)
        d = y - mu
        var = jnp.mean(d * d, axis=-1, keepdims=True)
        yn = d * lax.rsqrt(var + 1e-6) * gnw_ref[:, h * dv:(h + 1) * dv] + gnb_ref[:, h * dv:(h + 1) * dv]
        gate = g_ref[0, :, h * dv:(h + 1) * dv].astype(F32)
        out_ref[0, :, h * dv:(h + 1) * dv] = (gate * _sigmoid(gate) * yn).astype(out_ref.dtype)


def _retention(z3, positions, gn_w, gn_b):
    B, S, _ = z3.shape
    W = gn_w.shape[0]
    H = RET_HEADS
    dk = W // H
    half = dk // 2
    L = RET_CHUNK
    log_gamma = np.log1p(-np.exp2(-5.0 - np.arange(H, dtype=np.float64)))
    rel = np.arange(L)[:, None] - np.arange(L)[None, :]
    decay = np.where(rel >= 0, np.exp(log_gamma[:, None, None] * np.maximum(rel, 0)), 0.0).astype(np.float32)
    inv = (ROPE_BASE ** (-jnp.arange(half, dtype=F32) / half)).reshape(1, half)
    pos = positions.astype(F32).reshape(B, S, 1)
    blk = lambda j: pl.BlockSpec((1, L, W), lambda b, c: (b, c, j))
    fixed2 = lambda b, c: (0, 0)
    return pl.pallas_call(
        functools.partial(_retention_kernel, heads=H, log_gamma=tuple(float(x) for x in log_gamma)),
        grid=(B, S // L),
        in_specs=[blk(0), blk(1), blk(2), blk(3),
                  pl.BlockSpec((1, L, 1), lambda b, c: (b, c, 0)),
                  pl.BlockSpec((1, half), fixed2),
                  pl.BlockSpec((H, L, L), lambda b, c: (0, 0, 0)),
                  pl.BlockSpec((1, W), fixed2), pl.BlockSpec((1, W), fixed2)],
        out_specs=pl.BlockSpec((1, L, W), lambda b, c: (b, c, 0)),
        out_shape=jax.ShapeDtypeStruct((B, S, W), BF16),
        scratch_shapes=[pltpu.VMEM((H, dk, W // H), F32)],
        compiler_params=_params("parallel", "arbitrary"),
        name="retention",
    )(z3, z3, z3, z3, pos, inv, jnp.asarray(decay), gn_w.reshape(1, W), gn_b.reshape(1, W))


def _xattn_kernel(q_ref, k_ref, v_ref, o_ref, *, heads):
    hd = q_ref.shape[2] // heads
    for h in range(heads):
        sl = slice(h * hd, (h + 1) * hd)
        s = _bdot_nt(q_ref[0, :, sl], k_ref[0, :, sl]) * hd ** -0.5
        e = jnp.exp(s - jnp.max(s, axis=-1, keepdims=True))
        p = e / jnp.sum(e, axis=-1, keepdims=True)
        o_ref[0, :, sl] = _bdot(p, v_ref[0, :, sl]).astype(o_ref.dtype)


def _xattn(q3, kv3, tq=512):
    B, S, D = q3.shape
    M = kv3.shape[1]
    return pl.pallas_call(
        functools.partial(_xattn_kernel, heads=XA_HEADS),
        grid=(B, S // tq),
        in_specs=[pl.BlockSpec((1, tq, D), lambda b, i: (b, i, 0)),
                  pl.BlockSpec((1, M, D), lambda b, i: (b, 0, 0)),
                  pl.BlockSpec((1, M, D), lambda b, i: (b, 0, 1))],
        out_specs=pl.BlockSpec((1, tq, D), lambda b, i: (b, i, 0)),
        out_shape=jax.ShapeDtypeStruct((B, S, D), BF16),
        compiler_params=_params("parallel", "parallel"),
        name="xattn",
    )(q3, kv3, kv3)


def _first_argmax(vals, lane, big):
    m = jnp.max(vals, axis=-1, keepdims=True)
    idx = jnp.min(jnp.where(vals == m, lane, big), axis=-1, keepdims=True)
    return m, idx


def _router_kernel(x_ref, w_ref, bias_ref, gates_ref):
    E = w_ref.shape[1]
    per = E // N_GROUPS
    scores = _sigmoid(jnp.dot(x_ref[...], w_ref[...], precision=HI, preferred_element_type=F32))
    biased = scores + bias_ref[...]
    lane = _iota2(biased.shape, 1)
    grp = lane // per
    neg = -jnp.inf

    gs = jnp.zeros_like(biased)
    for g in range(N_GROUPS):
        vals = jnp.where(grp == g, biased, neg)
        m1, i1 = _first_argmax(vals, lane, E)
        m2 = jnp.max(jnp.where(lane == i1, neg, vals), axis=-1, keepdims=True)
        gs = jnp.where(grp == g, m1 + m2, gs)

    keep = jnp.zeros(biased.shape, jnp.bool_)
    for _ in range(TOPK_GROUPS):
        _, gi = _first_argmax(gs, grp, N_GROUPS)
        hit = grp == gi
        keep = keep | hit
        gs = jnp.where(hit, neg, gs)

    cand = jnp.where(keep, biased, neg)
    sel = jnp.zeros(biased.shape, jnp.bool_)
    for _ in range(TOP_K):
        _, ei = _first_argmax(cand, lane, E)
        hit = lane == ei
        sel = sel | hit
        cand = jnp.where(hit, neg, cand)

    w_sel = jnp.where(sel, scores, 0.0)
    gates_ref[...] = ROUTED_SCALE * w_sel / jnp.sum(w_sel, axis=-1, keepdims=True)


def _router(x, router_w, router_bias, tm=256):
    T, D = x.shape
    E = router_w.shape[1]
    return pl.pallas_call(
        _router_kernel,
        grid=(T // tm,),
        in_specs=[pl.BlockSpec((tm, D), lambda i: (i, 0)),
                  pl.BlockSpec((D, E), lambda i: (0, 0)),
                  pl.BlockSpec((1, E), lambda i: (0, 0))],
        out_specs=pl.BlockSpec((tm, E), lambda i: (i, 0)),
        out_shape=jax.ShapeDtypeStruct((T, E), F32),
        compiler_params=_params("parallel"),
        name="router",
    )(x, router_w, router_bias.reshape(1, E))


def _experts_kernel(x_ref, gates_ref, w1_ref, w3_ref, w2_ref, o_ref):
    e = pl.program_id(1)

    @pl.when(e == 0)
    def _():
        o_ref[...] = jnp.zeros_like(o_ref)

    x = x_ref[...]
    h1 = jnp.dot(x, w1_ref[0], preferred_element_type=F32)
    h3 = jnp.dot(x, w3_ref[0], preferred_element_type=F32)
    gates = gates_ref[...]
    gate = jnp.sum(jnp.where(_iota2(gates.shape, 1) == e, gates, 0.0), axis=-1, keepdims=True)
    hidden = h1 * _sigmoid(h1) * h3 * gate
    o_ref[...] += jnp.dot(hidden.astype(BF16), w2_ref[0], preferred_element_type=F32)


def _experts(xb, gates, w1, w3, w2, layer, tm=1024):
    T, D = xb.shape
    _, E, _, Hd = w1.shape
    return pl.pallas_call(
        _experts_kernel,
        grid=(T // tm, E),
        in_specs=[pl.BlockSpec((tm, D), lambda i, e: (i, 0)),
                  pl.BlockSpec((tm, E), lambda i, e: (i, 0)),
                  pl.BlockSpec((None, 1, D, Hd), lambda i, e: (layer, e, 0, 0)),
                  pl.BlockSpec((None, 1, D, Hd), lambda i, e: (layer, e, 0, 0)),
                  pl.BlockSpec((None, 1, Hd, D), lambda i, e: (layer, e, 0, 0))],
        out_specs=pl.BlockSpec((tm, D), lambda i, e: (i, 0)),
        out_shape=jax.ShapeDtypeStruct((T, D), F32),
        compiler_params=_params("parallel", "arbitrary"),
        name="experts",
    )(xb, gates, w1, w3, w2)


def _even_mixer(x2, xb, B, S, ln_g, ln_b, w_in, conv_w, conv_b, gate_b, ml_gn_w, rw_mu, rw_w0, rw_w2,
                rw_a0, rw_a2, rw_g2, rw_kk, rw_ka, rw_rk, rw_gn_w, rw_gn_b, w_out, j):
    D = x2.shape[1]
    H = ML_HEADS
    ML = ml_gn_w.shape[0]
    RW = rw_w0.shape[0]
    LO = RW_LORA_W + RW_LORA_A + RW_LORA_G
    ml_main = 3 * ML
    ml_cols = ml_main + 2 * H
    pad_to = 512
    used = ml_main + 3 * RW + LO + LANES
    total = -(-used // pad_to) * pad_to
    w_perm = jnp.concatenate([
        w_in[:, :ml_main], w_in[:, ml_cols:ml_cols + 3 * RW + LO], w_in[:, ml_main:ml_cols],
        jnp.zeros((D, total - used + LANES - 2 * H), w_in.dtype)], axis=1).astype(BF16)
    z3 = _mm(xb, w_perm[None], 0, BF16, 1024, pad_to).reshape(B, S, total)
    gate_col0 = ml_main + 3 * RW + LO
    L = ML_CHUNK
    gates_t = jnp.swapaxes(
        z3[:, :, gate_col0:gate_col0 + 2 * H].astype(F32).reshape(B, S // L, L, 2 * H), 2, 3)
    h_ml = _mlstm(z3, gates_t, conv_w, conv_b, gate_b, ml_gn_w,
                  qk_blk=0, v_blk=1, o_blk=2, gate_blk=gate_col0 // LANES)
    h_rw = _rwkv(z3, rw_mu, rw_w0, rw_w2, rw_a0, rw_a2, rw_g2, rw_kk, rw_ka, rw_rk, rw_gn_w, rw_gn_b,
                 col0=ml_main, lora_col0=ml_main + 3 * RW, width=RW)
    h = jnp.concatenate([h_ml, h_rw], axis=-1).reshape(B * S, ML + RW)
    return _mm_res_ln(h, w_out, j, x2, ln_g, ln_b)


def _odd_mixer(x2, xb, B, S, positions, ln_g, ln_b, w_in, gn_w, gn_b, w_out, j):
    z3 = _mm(xb, w_in, j, BF16, 1024, 512).reshape(B, S, w_in.shape[2])
    h = _retention(z3, positions, gn_w, gn_b)
    return _mm_res_ln(h.reshape(B * S, -1), w_out, j, x2, ln_g, ln_b)


def _cross_attention(x2, xb, B, S, memb, ln_g, ln_b, wq, wkv, wo, layer):
    D = x2.shape[1]
    q = _mm(xb, wq, layer, BF16, 1024, 512)
    kv = _mm(memb, wkv, layer, BF16, memb.shape[0], 512)
    o = _xattn(q.reshape(B, S, D), kv.reshape(B, -1, 2 * D))
    return _mm_res_ln(o.reshape(B * S, D), wo, layer, x2, ln_g, ln_b)


def _moe(x2, xb, ln_g, ln_b, router_w, router_bias, w1, w3, w2, sw1, sw3, sw2, layer):
    gates = _router(x2, router_w, router_bias)
    routed = _experts(xb, gates, w1, w3, w2, layer)
    hs = _glu(xb, sw1, sw3, layer)
    return _mm_res_ln(hs, sw2, layer, x2, ln_g, ln_b, add=routed)


def kernel(x, mem, positions, ln_g, ln_b, xa_wq, xa_wkv, xa_wo, router_w, router_bias, moe_w1, moe_w3, moe_w2, sh_w1, sh_w3, sh_w2, ev_w_in, ml_conv_w, ml_conv_b, ml_gate_b, ml_gn_w, rw_mu, rw_w0, rw_w2, rw_a0, rw_a2, rw_g2, rw_kk, rw_ka, rw_rk, rw_gn_w, rw_gn_b, ev_w_out, od_w_in, ret_gn_w, ret_gn_b, od_w_out):
    B, S, D = x.shape
    x2 = x.reshape(B * S, D)
    xb = x2.astype(BF16)
    memb = mem.reshape(-1, D).astype(BF16)
    (xa_wq, xa_wkv, xa_wo, moe_w1, moe_w3, moe_w2, sh_w1, sh_w3, sh_w2, ev_w_out, od_w_in,
     od_w_out) = (w.astype(BF16) for w in (xa_wq, xa_wkv, xa_wo, moe_w1, moe_w3, moe_w2, sh_w1, sh_w3,
                                          sh_w2, ev_w_out, od_w_in, od_w_out))
    for layer in range(ln_g.shape[0]):
        j = layer // 2
        if layer % 2 == 0:
            x2, xb = _even_mixer(x2, xb, B, S, ln_g[layer, 0], ln_b[layer, 0], ev_w_in[j], ml_conv_w[j],
                                 ml_conv_b[j], ml_gate_b[j], ml_gn_w[j], rw_mu[j], rw_w0[j], rw_w2[j],
                                 rw_a0[j], rw_a2[j], rw_g2[j], rw_kk[j], rw_ka[j], rw_rk[j],
                                 rw_gn_w[j], rw_gn_b[j], ev_w_out, j)
        else:
            x2, xb = _odd_mixer(x2, xb, B, S, positions, ln_g[layer, 0], ln_b[layer, 0], od_w_in,
                                ret_gn_w[j], ret_gn_b[j], od_w_out, j)
        x2, xb = _cross_attention(x2, xb, B, S, memb, ln_g[layer, 1], ln_b[layer, 1], xa_wq, xa_wkv,
                                  xa_wo, layer)
        x2, xb = _moe(x2, xb, ln_g[layer, 2], ln_b[layer, 2], router_w[layer], router_bias[layer],
                      moe_w1, moe_w3, moe_w2, sh_w1, sh_w3, sh_w2, layer)
    return x2.reshape(B, S, D)
```

```python
import functools

import numpy as np
import jax
import jax.numpy as jnp
from jax import lax
from jax.experimental import pallas as pl
from jax.experimental.pallas import tpu as pltpu

F32 = jnp.float32
BF16 = jnp.bfloat16
HI = lax.Precision.HIGHEST

DEPTH = 4
ALPHA = (2.0 * DEPTH) ** 0.25
LN_EPS = 1e-5
NEG_BIG = -1e30
CONV_K = 4
ML_HEADS = 4
RW_HEAD = 64
RW_LORA_W = 64
RW_LORA_A = 64
RW_LORA_G = 128
RW_GN_EPS = 64e-5
RET_HEADS = 8
ROPE_BASE = 10000.0
XA_HEADS = 4
N_EXPERTS = 64
TOP_K = 8
N_GROUPS = 8
TOPK_GROUPS = 4
ROUTED_SCALE = 2.5

ML_CHUNK = 64
RW_CHUNK = 64
RET_CHUNK = 256
MOE_ROWS_PER_TILE = 256

V7X_VMEM_BYTES = 64 * 2 ** 20
VMEM_LIMIT = (V7X_VMEM_BYTES * 3) // 4
LANES = 128
V7X_MXU_DIM = 256


def _params(*sem):
    return pltpu.CompilerParams(dimension_semantics=sem, vmem_limit_bytes=VMEM_LIMIT)


def _bdot(a, b):
    return jnp.dot(a.astype(BF16), b.astype(BF16), preferred_element_type=F32)


def _bdot_nt(a, b):
    return lax.dot_general(a.astype(BF16), b.astype(BF16), (((1,), (1,)), ((), ())),
                           preferred_element_type=F32)


def _bdot_tn(a, b):
    return lax.dot_general(a.astype(BF16), b.astype(BF16), (((0,), (0,)), ((), ())),
                           preferred_element_type=F32)


def _sigmoid(x):
    return 1.0 / (1.0 + jnp.exp(-x))


def _softplus(x):
    return jnp.maximum(x, 0.0) + jnp.log1p(jnp.exp(-jnp.abs(x)))


def _iota2(shape, dim):
    return lax.broadcasted_iota(jnp.int32, shape, dim)


def _mm_kernel(x_ref, w_ref, o_ref):
    o_ref[...] = _bdot(x_ref[...], w_ref[...]).astype(o_ref.dtype)


def _mm(x, w, layer, out_dtype, tm, tn):
    M, K = x.shape
    N = w.shape[2]
    assert M % tm == 0 and N % tn == 0
    return pl.pallas_call(
        _mm_kernel,
        grid=(M // tm, N // tn),
        in_specs=[pl.BlockSpec((tm, K), lambda i, j: (i, 0)),
                  pl.BlockSpec((None, K, tn), lambda i, j: (layer, 0, j))],
        out_specs=pl.BlockSpec((tm, tn), lambda i, j: (i, j)),
        out_shape=jax.ShapeDtypeStruct((M, N), out_dtype),
        compiler_params=_params("parallel", "parallel"),
        name="mm",
    )(x, w)


def _layer_norm_rows(y, g, b):
    mu = jnp.mean(y, axis=-1, keepdims=True)
    d = y - mu
    var = jnp.mean(d * d, axis=-1, keepdims=True)
    return d * lax.rsqrt(var + LN_EPS) * g + b


def _mm_res_ln_kernel(a_ref, w_ref, res_ref, g_ref, b_ref, o_ref, ob_ref):
    y = ALPHA * res_ref[...] + _bdot(a_ref[...], w_ref[...])
    out = _layer_norm_rows(y, g_ref[...], b_ref[...])
    o_ref[...] = out
    ob_ref[...] = out.astype(ob_ref.dtype)


def _mm_res_add_ln_kernel(a_ref, w_ref, res_ref, add_ref, g_ref, b_ref, o_ref, ob_ref):
    y = ALPHA * res_ref[...] + (add_ref[...] + _bdot(a_ref[...], w_ref[...]))
    out = _layer_norm_rows(y, g_ref[...], b_ref[...])
    o_ref[...] = out
    ob_ref[...] = out.astype(ob_ref.dtype)


def _mm_res_ln(a, w, layer, res, g, b, add=None, tm=256):
    M, K = a.shape
    N = w.shape[2]
    row = lambda i: (i, 0)
    fixed = lambda i: (0, 0)
    in_specs = [pl.BlockSpec((tm, K), row), pl.BlockSpec((None, K, N), lambda i: (layer, 0, 0)),
                pl.BlockSpec((tm, N), row)]
    args = [a, w, res]
    if add is not None:
        in_specs.append(pl.BlockSpec((tm, N), row))
        args.append(add)
    in_specs += [pl.BlockSpec((1, N), fixed), pl.BlockSpec((1, N), fixed)]
    args += [g.reshape(1, N), b.reshape(1, N)]
    return pl.pallas_call(
        _mm_res_ln_kernel if add is None else _mm_res_add_ln_kernel,
        grid=(M // tm,),
        in_specs=in_specs,
        out_specs=[pl.BlockSpec((tm, N), row), pl.BlockSpec((tm, N), row)],
        out_shape=[jax.ShapeDtypeStruct((M, N), F32), jax.ShapeDtypeStruct((M, N), BF16)],
        compiler_params=_params("parallel"),
        name="mm_res_ln",
    )(*args)


def _glu_kernel(x_ref, w1_ref, w3_ref, o_ref):
    x = x_ref[...]
    h1 = jnp.dot(x, w1_ref[...], preferred_element_type=F32)
    h3 = jnp.dot(x, w3_ref[...], preferred_element_type=F32)
    o_ref[...] = (h1 * _sigmoid(h1) * h3).astype(o_ref.dtype)


def _glu(xb, w1, w3, layer, tm=512):
    M, K = xb.shape
    N = w1.shape[2]
    wspec = pl.BlockSpec((None, K, N), lambda i: (layer, 0, 0))
    return pl.pallas_call(
        _glu_kernel,
        grid=(M // tm,),
        in_specs=[pl.BlockSpec((tm, K), lambda i: (i, 0)), wspec, wspec],
        out_specs=pl.BlockSpec((tm, N), lambda i: (i, 0)),
        out_shape=jax.ShapeDtypeStruct((M, N), BF16),
        compiler_params=_params("parallel"),
        name="glu",
    )(xb, w1, w3)


def _log_sigmoid(x):
    return jnp.minimum(x, 0.0) - jnp.log1p(jnp.exp(-jnp.abs(x)))


def _mlstm_kernel(qk_ref, v_ref, o_ref, gc_ref, gr_ref, cw_ref, cb_ref, gbc_ref, gbr_ref, gn_ref,
                  out_ref, prev_ref, c_ref, n_ref, m_ref, *, heads, dk, dv):
    L = qk_ref.shape[1]

    @pl.when(pl.program_id(1) == 0)
    def _():
        prev_ref[...] = jnp.zeros_like(prev_ref)
        c_ref[...] = jnp.zeros_like(c_ref)
        n_ref[...] = jnp.zeros_like(n_ref)
        m_ref[...] = jnp.full_like(m_ref, NEG_BIG)

    cur = qk_ref[0].astype(F32)
    prev = prev_ref[...]
    row = _iota2((L, 1), 0)
    acc = cb_ref[...] + cw_ref[CONV_K - 1:CONV_K, :] * cur
    for j in range(CONV_K - 1):
        s = CONV_K - 1 - j
        shifted = jnp.where(row < s, pltpu.roll(prev, s, 0), pltpu.roll(cur, s, 0))
        acc = acc + cw_ref[j:j + 1, :] * shifted
    prev_ref[...] = cur
    qk = acc * _sigmoid(acc)
    qw = heads * dk

    tri = (_iota2((L, L), 0) >= _iota2((L, L), 1)).astype(F32)
    g_col = gc_ref[0].astype(F32) + gbc_ref[...]
    b_col = jnp.dot(tri, _log_sigmoid(g_col), precision=HI, preferred_element_type=F32)
    g_row = gr_ref[0, 0] + gbr_ref[...]
    b_row = lax.dot_general(_log_sigmoid(g_row), tri, (((1,), (1,)), ((), ())), precision=HI,
                            preferred_element_type=F32)
    causal = _iota2((L, L), 0) >= _iota2((L, L), 1)

    for h in range(heads):
        q = qk[:, h * dk:(h + 1) * dk]
        k = qk[:, qw + h * dk:qw + (h + 1) * dk] * dk ** -0.5
        v = v_ref[0, :, h * dv:(h + 1) * dv]
        bc = b_col[:, heads + h:heads + h + 1]
        ic = g_col[:, h:h + 1]
        br = b_row[heads + h:heads + h + 1, :]
        ir = g_row[h:h + 1, :]
        b_last = bc[L - 1:L, :]
        c_prev = c_ref[h]
        n_prev = n_ref[h]
        m_prev = m_ref[h][:, 0:1]

        log_d = jnp.where(causal, bc - br + ir, -jnp.inf)
        g_inter = bc + m_prev
        m_t = jnp.maximum(jnp.max(log_d, axis=-1, keepdims=True), g_inter)
        p = jnp.exp(log_d - m_t) * _bdot_nt(q, k)
        e_inter = jnp.exp(g_inter - m_t)
        num = _bdot(p, v) + e_inter * _bdot(q, c_prev)
        den = jnp.sum(p, axis=-1, keepdims=True) + e_inter * jnp.sum(q * n_prev, axis=-1, keepdims=True)
        hh = num / jnp.maximum(jnp.abs(den), jnp.exp(-m_t))

        w_end = b_last - bc + ic
        m_loc = jnp.max(w_end, axis=0, keepdims=True)
        e_end = jnp.exp(w_end - m_loc)
        ke = k * e_end
        m_new = jnp.maximum(b_last + m_prev, m_loc)
        s_prev = jnp.exp(b_last + m_prev - m_new)
        s_new = jnp.exp(m_loc - m_new)
        c_ref[h] = s_prev * c_prev + s_new * _bdot_tn(ke, v)
        n_ref[h] = s_prev * n_prev + s_new * jnp.sum(ke, axis=0, keepdims=True)
        m_ref[h] = jnp.broadcast_to(m_new, m_ref.shape[1:])

        mu = jnp.mean(hh, axis=-1, keepdims=True)
        d = hh - mu
        var = jnp.mean(d * d, axis=-1, keepdims=True)
        y = d * lax.rsqrt(var + 1e-6) * gn_ref[:, h * dv:(h + 1) * dv]
        out_ref[0, :, h * dv:(h + 1) * dv] = (y * _sigmoid(o_ref[0, :, h * dv:(h + 1) * dv].astype(F32))).astype(out_ref.dtype)


def _mlstm(z3, gates_t, conv_w, conv_b, gate_b, gn_w, *, qk_blk, v_blk, o_blk, gate_blk):
    B, S, _ = z3.shape
    H = ML_HEADS
    W = gn_w.shape[0]
    dv = W // H
    dk = conv_w.shape[1] // (2 * H)
    assert conv_w.shape[1] == W, "q|k block and v block share one column-block width"
    L = ML_CHUNK
    gb_col = jnp.zeros((1, LANES), F32).at[0, :2 * H].set(gate_b.reshape(-1))
    gb_row = gate_b.reshape(2 * H, 1)
    fixed = lambda b, c: (0, 0)
    return pl.pallas_call(
        functools.partial(_mlstm_kernel, heads=H, dk=dk, dv=dv),
        grid=(B, S // L),
        in_specs=[pl.BlockSpec((1, L, W), lambda b, c: (b, c, qk_blk)),
                  pl.BlockSpec((1, L, W), lambda b, c: (b, c, v_blk)),
                  pl.BlockSpec((1, L, W), lambda b, c: (b, c, o_blk)),
                  pl.BlockSpec((1, L, LANES), lambda b, c: (b, c, gate_blk)),
                  pl.BlockSpec((1, 1, 2 * H, L), lambda b, c: (b, c, 0, 0)),
                  pl.BlockSpec((CONV_K, W), fixed),
                  pl.BlockSpec((1, W), fixed),
                  pl.BlockSpec((1, LANES), fixed),
                  pl.BlockSpec((2 * H, 1), fixed),
                  pl.BlockSpec((1, W), fixed)],
        out_specs=pl.BlockSpec((1, L, W), lambda b, c: (b, c, 0)),
        out_shape=jax.ShapeDtypeStruct((B, S, W), BF16),
        scratch_shapes=[pltpu.VMEM((L, W), F32),
                        pltpu.VMEM((H, dk, dv), F32),
                        pltpu.VMEM((H, 1, dk), F32),
                        pltpu.VMEM((H, 1, LANES), F32)],
        compiler_params=_params("parallel", "arbitrary"),
        name="mlstm",
    )(z3, z3, z3, z3, gates_t, conv_w, conv_b.reshape(1, W), gb_col, gb_row, gn_w.reshape(1, W))


def _unit_lower_inverse(a, idx_r, idx_c, n):
    eye = (idx_r == idx_c).astype(F32)
    t = None
    s = 1
    while s < n:
        pair = (idx_r // (2 * s)) == (idx_c // (2 * s))
        off = pair & ((idx_r // s) % 2 == 1) & ((idx_c // s) % 2 == 0)
        q = jnp.where(off, a, 0.0)
        t = eye - q if t is None else t - _bdot(_bdot(t, q), t)
        s *= 2
    return t


def _rwkv_kernel(r_ref, k_ref, v_ref, l_ref, mur_ref, muk_ref, muv_ref, mul_ref, w0_ref, w2_ref,
                 a0_ref, a2_ref, g2_ref, kk_ref, ka_ref, rk_ref, gnw_ref, gnb_ref, out_ref,
                 pr_ref, pk_ref, pv_ref, plo_ref, s_ref, *, hd, gw):
    L = r_ref.shape[1]
    W = r_ref.shape[2]
    ng = W // gw
    hpg = gw // hd
    assert hpg * L == gw

    @pl.when(pl.program_id(2) == 0)
    def _():
        pr_ref[...] = jnp.zeros_like(pr_ref)
        pk_ref[...] = jnp.zeros_like(pk_ref)
        pv_ref[...] = jnp.zeros_like(pv_ref)
        plo_ref[...] = jnp.zeros_like(plo_ref)
        s_ref[...] = jnp.zeros_like(s_ref)

    first = _iota2((L, 1), 0) == 0

    def shift_lerp(x_ref, p_ref, mu_ref):
        cur = x_ref[0].astype(F32)
        shifted = jnp.where(first, p_ref[...], pltpu.roll(cur, 1, 0))
        p_ref[...] = cur[L - 1:L, :]
        return cur + mu_ref[...] * (shifted - cur)

    r = shift_lerp(r_ref, pr_ref, mur_ref)
    k = shift_lerp(k_ref, pk_ref, muk_ref)
    v = shift_lerp(v_ref, pv_ref, muv_ref)
    lo = shift_lerp(l_ref, plo_ref, mul_ref)
    xw = lo[:, :RW_LORA_W]
    xa = lo[:, RW_LORA_W:RW_LORA_W + RW_LORA_A]
    xg = lo[:, RW_LORA_W + RW_LORA_A:]

    w_log = -jnp.exp(-_softplus(-(w0_ref[...] + _bdot(jnp.tanh(xw), w2_ref[...]))) - 0.5)
    a = _sigmoid(a0_ref[...] + _bdot(xa, a2_ref[...]))
    g = _bdot(_sigmoid(xg), g2_ref[...])

    idx_r = _iota2((gw, gw), 0)
    idx_c = _iota2((gw, gw), 1)
    same_head = (idx_r // hd) == (idx_c // hd)
    same_head_f = same_head.astype(F32)

    def head_sum(x):
        return jnp.concatenate(
            [jnp.dot(x[:, i * gw:(i + 1) * gw], same_head_f, precision=HI, preferred_element_type=F32)
             for i in range(ng)], axis=1)

    kk = k * kk_ref[...]
    kk = kk * lax.rsqrt(jnp.maximum(head_sum(kk * kk), 1e-24))
    k = k * (1.0 + (a - 1.0) * ka_ref[...])
    bv = kk * a

    tri = (_iota2((L, L), 0) >= _iota2((L, L), 1)).astype(F32)
    c = jnp.dot(tri, w_log, precision=HI, preferred_element_type=F32)
    c_last = c[L - 1:L, :]
    e_end = jnp.exp(c_last - c)
    e_neg = jnp.exp(-c)
    kt = (kk * jnp.exp(c - w_log)).astype(BF16)
    rt = (r * jnp.exp(c)).astype(BF16)
    kh = (k * e_neg).astype(BF16)
    bh = (bv * e_neg).astype(BF16)
    kp = (k * e_end).astype(BF16)
    bp = (bv * e_end).astype(BF16)
    vb = v.astype(BF16)
    p_last = jnp.exp(c_last)

    strict = same_head & (idx_r > idx_c)
    lower = same_head & (idx_r >= idx_c)
    zero = jnp.zeros((), BF16)

    def tiled(x):
        return jnp.concatenate([x] * hpg, axis=0)

    def block_diag(x):
        return jnp.where(same_head, tiled(x), zero)

    ys = []
    for i in range(ng):
        sl = slice(i * gw, (i + 1) * gw)
        lhs = jnp.concatenate([block_diag(kt[:, sl]), block_diag(rt[:, sl])], axis=0)
        rhs = jnp.concatenate([tiled(kh[:, sl]), tiled(bh[:, sl])], axis=0)
        m = _bdot_nt(lhs, rhs)
        a_kk = jnp.where(strict, m[:gw, :gw], 0.0)
        a_kb = jnp.where(strict, m[:gw, gw:], 0.0)
        a_rk = jnp.where(lower, m[gw:, :gw], 0.0)
        a_rb = jnp.where(lower, m[gw:, gw:], 0.0)
        t_inv = _unit_lower_inverse(a_kb, idx_r, idx_c, L)
        st = s_ref[i]
        gs = _bdot_nt(lhs, st)
        v_bd = block_diag(vb[:, sl])
        u = _bdot(t_inv, gs[:gw] + _bdot(a_kk, v_bd))
        y = gs[gw:] + _bdot(jnp.concatenate([a_rk, -a_rb], axis=1),
                            jnp.concatenate([v_bd, u.astype(BF16)], axis=0))
        s_ref[i] = st * p_last[:, sl] + _bdot_tn(
            jnp.concatenate([v_bd, (-u).astype(BF16)], axis=0),
            jnp.concatenate([block_diag(kp[:, sl]), block_diag(bp[:, sl])], axis=0))
        ys.append(sum(y[j * L:(j + 1) * L] for j in range(hpg)))

    y = jnp.concatenate(ys, axis=1)
    mu = head_sum(y) * (1.0 / hd)
    d = y - mu
    var = head_sum(d * d) * (1.0 / hd)
    yn = d * lax.rsqrt(var + RW_GN_EPS) * gnw_ref[...] + gnb_ref[...]
    bonus = head_sum(r * k * rk_ref[...]) * v
    out_ref[0] = ((yn + bonus) * g).astype(out_ref.dtype)


def _rwkv(z3, mu, w0, w2, a0, a2, g2, k_k, k_a, r_k, gn_w, gn_b, *, col0, lora_col0, width):
    B, S, _ = z3.shape
    RW = w0.shape[0]
    W = width
    G = RW // W
    L = RW_CHUNK
    LO = RW_LORA_W + RW_LORA_A + RW_LORA_G
    assert col0 % W == 0 and RW % W == 0 and lora_col0 % LO == 0 and W % V7X_MXU_DIM == 0
    zc = lambda off: (lambda b, g, c: (b, c, off // W + g))
    vec = lambda b, g, c: (0, g)
    row = lambda x: x.reshape(1, -1)
    return pl.pallas_call(
        functools.partial(_rwkv_kernel, hd=RW_HEAD, gw=V7X_MXU_DIM),
        grid=(B, G, S // L),
        in_specs=[pl.BlockSpec((1, L, W), zc(col0)),
                  pl.BlockSpec((1, L, W), zc(col0 + RW)),
                  pl.BlockSpec((1, L, W), zc(col0 + 2 * RW)),
                  pl.BlockSpec((1, L, LO), lambda b, g, c: (b, c, lora_col0 // LO)),
                  pl.BlockSpec((1, W), vec), pl.BlockSpec((1, W), vec), pl.BlockSpec((1, W), vec),
                  pl.BlockSpec((1, LO), lambda b, g, c: (0, 0)),
                  pl.BlockSpec((1, W), vec),
                  pl.BlockSpec((RW_LORA_W, W), vec),
                  pl.BlockSpec((1, W), vec),
                  pl.BlockSpec((RW_LORA_A, W), vec),
                  pl.BlockSpec((RW_LORA_G, W), vec),
                  pl.BlockSpec((1, W), vec), pl.BlockSpec((1, W), vec), pl.BlockSpec((1, W), vec),
                  pl.BlockSpec((1, W), vec), pl.BlockSpec((1, W), vec)],
        out_specs=pl.BlockSpec((1, L, W), lambda b, g, c: (b, c, g)),
        out_shape=jax.ShapeDtypeStruct((B, S, RW), BF16),
        scratch_shapes=[pltpu.VMEM((1, W), F32), pltpu.VMEM((1, W), F32), pltpu.VMEM((1, W), F32),
                        pltpu.VMEM((1, LO), F32),
                        pltpu.VMEM((W // V7X_MXU_DIM, V7X_MXU_DIM, V7X_MXU_DIM), F32)],
        compiler_params=_params("parallel", "parallel", "arbitrary"),
        name="rwkv7",
    )(z3, z3, z3, z3, row(mu[:RW]), row(mu[RW:2 * RW]), row(mu[2 * RW:3 * RW]), row(mu[3 * RW:]),
      row(w0), w2.astype(BF16), row(a0), a2.astype(BF16), g2.astype(BF16), row(k_k), row(k_a),
      row(r_k), row(gn_w), row(gn_b))


def _retention_kernel(q_ref, k_ref, v_ref, g_ref, pos_ref, inv_ref, dec_ref, gnw_ref, gnb_ref,
                      out_ref, r_ref, *, heads, log_gamma):
    L = q_ref.shape[1]
    dk = q_ref.shape[2] // heads
    dv = v_ref.shape[2] // heads
    half = dk // 2

    @pl.when(pl.program_id(1) == 0)
    def _():
        r_ref[...] = jnp.zeros_like(r_ref)

    ang = pos_ref[0] * inv_ref[...]
    cos = jnp.cos(ang)
    sin = jnp.sin(ang)
    t_in = _iota2((L, 1), 0).astype(F32)

    def rot(t):
        t1, t2 = t[:, :half], t[:, half:]
        return jnp.concatenate([t1 * cos - t2 * sin, t1 * sin + t2 * cos], axis=-1)

    for h in range(heads):
        lg = log_gamma[h]
        q = rot(q_ref[0, :, h * dk:(h + 1) * dk].astype(F32))
        k = rot(k_ref[0, :, h * dk:(h + 1) * dk].astype(F32)) * dk ** -0.5
        v = v_ref[0, :, h * dv:(h + 1) * dv]
        state = r_ref[h]
        inter = _bdot(q, state) * jnp.exp(lg * (t_in + 1.0))
        intra = _bdot(_bdot_nt(q, k) * dec_ref[h], v)
        r_ref[h] = state * float(np.exp(lg * L)) + _bdot_tn(k * jnp.exp(lg * (L - 1.0 - t_in)), v)
        y = intra + inter
        mu = jnp.mean(y, axis=-1, keepdims=True)
        d = y - mu
        var = jnp.mean(d * d, axis=-1, keepdims=True)
        yn = d * lax.rsqrt(var + 1e-6) * gnw_ref[:, h * dv:(h + 1) * dv] + gnb_ref[:, h * dv:(h + 1) * dv]
        gate = g_ref[0, :, h * dv:(h + 1) * dv].astype(F32)
        out_ref[0, :, h * dv:(h + 1) * dv] = (gate * _sigmoid(gate) * yn).astype(out_ref.dtype)


def _retention(z3, positions, gn_w, gn_b):
    B, S, _ = z3.shape
    W = gn_w.shape[0]
    H = RET_HEADS
    dk = W // H
    half = dk // 2
    L = RET_CHUNK
    log_gamma = np.log1p(-np.exp2(-5.0 - np.arange(H, dtype=np.float64)))
    rel = np.arange(L)[:, None] - np.arange(L)[None, :]
    decay = np.where(rel >= 0, np.exp(log_gamma[:, None, None] * np.maximum(rel, 0)), 0.0).astype(np.float32)
    inv = (ROPE_BASE ** (-jnp.arange(half, dtype=F32) / half)).reshape(1, half)
    pos = positions.astype(F32).reshape(B, S, 1)
    blk = lambda j: pl.BlockSpec((1, L, W), lambda b, c: (b, c, j))
    fixed2 = lambda b, c: (0, 0)
    return pl.pallas_call(
        functools.partial(_retention_kernel, heads=H, log_gamma=tuple(float(x) for x in log_gamma)),
        grid=(B, S // L),
        in_specs=[blk(0), blk(1), blk(2), blk(3),
                  pl.BlockSpec((1, L, 1), lambda b, c: (b, c, 0)),
                  pl.BlockSpec((1, half), fixed2),
                  pl.BlockSpec((H, L, L), lambda b, c: (0, 0, 0)),
                  pl.BlockSpec((1, W), fixed2), pl.BlockSpec((1, W), fixed2)],
        out_specs=pl.BlockSpec((1, L, W), lambda b, c: (b, c, 0)),
        out_shape=jax.ShapeDtypeStruct((B, S, W), BF16),
        scratch_shapes=[pltpu.VMEM((H, dk, W // H), F32)],
        compiler_params=_params("parallel", "arbitrary"),
        name="retention",
    )(z3, z3, z3, z3, pos, inv, jnp.asarray(decay), gn_w.reshape(1, W), gn_b.reshape(1, W))


def _xattn_kernel(q_ref, k_ref, v_ref, o_ref, *, heads):
    hd = q_ref.shape[2] // heads
    for h in range(heads):
        sl = slice(h * hd, (h + 1) * hd)
        s = _bdot_nt(q_ref[0, :, sl], k_ref[0, :, sl]) * hd ** -0.5
        e = jnp.exp(s - jnp.max(s, axis=-1, keepdims=True))
        p = e / jnp.sum(e, axis=-1, keepdims=True)
        o_ref[0, :, sl] = _bdot(p, v_ref[0, :, sl]).astype(o_ref.dtype)


def _xattn(q3, kv3, tq=512):
    B, S, D = q3.shape
    M = kv3.shape[1]
    return pl.pallas_call(
        functools.partial(_xattn_kernel, heads=XA_HEADS),
        grid=(B, S // tq),
        in_specs=[pl.BlockSpec((1, tq, D), lambda b, i: (b, i, 0)),
                  pl.BlockSpec((1, M, D), lambda b, i: (b, 0, 0)),
                  pl.BlockSpec((1, M, D), lambda b, i: (b, 0, 1))],
        out_specs=pl.BlockSpec((1, tq, D), lambda b, i: (b, i, 0)),
        out_shape=jax.ShapeDtypeStruct((B, S, D), BF16),
        compiler_params=_params("parallel", "parallel"),
        name="xattn",
    )(q3, kv3, kv3)


def _first_argmax(vals, lane, big):
    m = jnp.max(vals, axis=-1, keepdims=True)
    idx = jnp.min(jnp.where(vals == m, lane, big), axis=-1, keepdims=True)
    return m, idx


def _router_kernel(x_ref, w_ref, bias_ref, idx_ref, gate_ref):
    E = w_ref.shape[1]
    per = E // N_GROUPS
    scores = _sigmoid(jnp.dot(x_ref[...], w_ref[...], precision=HI, preferred_element_type=F32))
    biased = scores + bias_ref[...]
    lane = _iota2(biased.shape, 1)
    grp = lane // per
    neg = -jnp.inf

    gs = jnp.zeros_like(biased)
    for g in range(N_GROUPS):
        vals = jnp.where(grp == g, biased, neg)
        m1, i1 = _first_argmax(vals, lane, E)
        m2 = jnp.max(jnp.where(lane == i1, neg, vals), axis=-1, keepdims=True)
        gs = jnp.where(grp == g, m1 + m2, gs)

    keep = jnp.zeros(biased.shape, jnp.bool_)
    for _ in range(TOPK_GROUPS):
        _, gi = _first_argmax(gs, grp, N_GROUPS)
        hit = grp == gi
        keep = keep | hit
        gs = jnp.where(hit, neg, gs)

    cand = jnp.where(keep, biased, neg)
    slot = _iota2(idx_ref.shape, 1)
    idx = jnp.zeros(idx_ref.shape, jnp.int32)
    aff = jnp.zeros(gate_ref.shape, F32)
    for k in range(TOP_K):
        _, ei = _first_argmax(cand, lane, E)
        hit = lane == ei
        idx = jnp.where(slot == k, ei, idx)
        aff = jnp.where(slot == k, jnp.sum(jnp.where(hit, scores, 0.0), axis=-1, keepdims=True), aff)
        cand = jnp.where(hit, neg, cand)
    idx_ref[...] = idx
    gate_ref[...] = ROUTED_SCALE * aff / jnp.sum(aff, axis=-1, keepdims=True)


def _router(x, router_w, router_bias, tm=256):
    T, D = x.shape
    E = router_w.shape[1]
    idx, gate = pl.pallas_call(
        _router_kernel,
        grid=(T // tm,),
        in_specs=[pl.BlockSpec((tm, D), lambda i: (i, 0)),
                  pl.BlockSpec((D, E), lambda i: (0, 0)),
                  pl.BlockSpec((1, E), lambda i: (0, 0))],
        out_specs=[pl.BlockSpec((tm, LANES), lambda i: (i, 0)), pl.BlockSpec((tm, LANES), lambda i: (i, 0))],
        out_shape=[jax.ShapeDtypeStruct((T, LANES), jnp.int32), jax.ShapeDtypeStruct((T, LANES), F32)],
        compiler_params=_params("parallel"),
        name="router",
    )(x, router_w, router_bias.reshape(1, E))
    return idx[:, :TOP_K], gate[:, :TOP_K]


def _dispatch_plan(top_e, n_experts, rows_per_tile):
    T, K = top_e.shape
    n_tiles = (T * K) // rows_per_tile + n_experts
    sel = jnp.any(top_e[:, :, None] == jnp.arange(n_experts, dtype=jnp.int32)[None, None, :], axis=1)
    sel = sel.astype(jnp.int32)
    csum = jnp.cumsum(sel, axis=0)
    counts = csum[-1]
    tiles = (counts + rows_per_tile - 1) // rows_per_tile
    tile_end = jnp.cumsum(tiles)
    n_used = tile_end[-1]
    row_start = (tile_end - tiles) * rows_per_tile
    dest = jnp.take_along_axis(row_start[None, :] + csum - sel, top_e, axis=1)
    tile_id = jnp.arange(n_tiles, dtype=jnp.int32)
    tile_expert = jnp.searchsorted(tile_end, jnp.minimum(tile_id, n_used - 1), side="right")
    tile_expert = jnp.minimum(tile_expert, n_experts - 1).astype(jnp.int32)
    token = jnp.broadcast_to(jnp.arange(T, dtype=jnp.int32)[:, None], (T, K))
    row_token = jnp.zeros((n_tiles * rows_per_tile,), jnp.int32).at[dest.reshape(-1)].set(
        token.reshape(-1), unique_indices=True)
    return tile_expert, n_used.reshape(1).astype(jnp.int32), row_token.reshape(n_tiles, 1, rows_per_tile), dest


def _gather_rows(idx_ref, src_hbm, dst, sem, n):
    def body(i, carry):
        pltpu.make_async_copy(src_hbm.at[pl.ds(idx_ref[0, i], 1)], dst.at[pl.ds(i, 1)], sem).start()
        return carry
    lax.fori_loop(0, n, body, 0, unroll=8)


def _sparse_experts_kernel(te_ref, nu_ref, tok_ref, tok_next_ref, x_hbm, w1_ref, w3_ref, w2_ref, y_ref,
                           buf, sem, wb1, wb3, wb2):
    n = pl.program_id(0)
    n_used = nu_ref[0]
    rows = buf.shape[1]
    slot = n % 2

    @pl.when((n == 0) & (n_used > 0))
    def _():
        _gather_rows(tok_ref, x_hbm, buf.at[0], sem.at[0], rows)

    @pl.when(n + 1 < n_used)
    def _():
        _gather_rows(tok_next_ref, x_hbm, buf.at[1 - slot], sem.at[1 - slot], rows)

    @pl.when(n < n_used)
    def _():
        @pl.when((n == 0) | (te_ref[n] != te_ref[jnp.maximum(n - 1, 0)]))
        def _():
            wb1[...] = w1_ref[0].astype(BF16)
            wb3[...] = w3_ref[0].astype(BF16)
            wb2[...] = w2_ref[0].astype(BF16)

        pltpu.make_async_copy(x_hbm.at[pl.ds(0, rows)], buf.at[slot], sem.at[slot]).wait()
        x = buf[slot].astype(BF16)
        h1 = jnp.dot(x, wb1[...], preferred_element_type=F32)
        h3 = jnp.dot(x, wb3[...], preferred_element_type=F32)
        hidden = (h1 * _sigmoid(h1) * h3).astype(BF16)
        y_ref[...] = jnp.dot(hidden, wb2[...], preferred_element_type=F32)

    @pl.when(n >= n_used)
    def _():
        y_ref[...] = jnp.zeros_like(y_ref)


def _sparse_experts(x2, tile_expert, n_used, row_token, w1, w3, w2, layer):
    T, D = x2.shape
    Hd = w1.shape[3]
    n_tiles, _, rows = row_token.shape
    wmap = lambda n, te, nu: (layer, te[n], 0, 0)
    return pl.pallas_call(
        _sparse_experts_kernel,
        grid_spec=pltpu.PrefetchScalarGridSpec(
            num_scalar_prefetch=2,
            grid=(n_tiles,),
            in_specs=[pl.BlockSpec((None, 1, rows), lambda n, te, nu: (n, 0, 0), memory_space=pltpu.SMEM),
                      pl.BlockSpec((None, 1, rows), lambda n, te, nu: (jnp.minimum(n + 1, n_tiles - 1), 0, 0),
                                   memory_space=pltpu.SMEM),
                      pl.BlockSpec(memory_space=pl.ANY),
                      pl.BlockSpec((None, 1, D, Hd), wmap),
                      pl.BlockSpec((None, 1, D, Hd), wmap),
                      pl.BlockSpec((None, 1, Hd, D), wmap)],
            out_specs=pl.BlockSpec((rows, D), lambda n, te, nu: (n, 0)),
            scratch_shapes=[pltpu.VMEM((2, rows, D), F32),
                            pltpu.SemaphoreType.DMA((2,)),
                            pltpu.VMEM((D, Hd), BF16), pltpu.VMEM((D, Hd), BF16), pltpu.VMEM((Hd, D), BF16)]),
        out_shape=jax.ShapeDtypeStruct((n_tiles * rows, D), F32),
        compiler_params=_params("arbitrary"),
        name="sparse_experts",
    )(tile_expert, n_used, row_token, row_token, x2, w1, w3, w2)


def _combine_kernel(dest_ref, dest_next_ref, gate_ref, y_hbm, o_ref, buf, sem):
    i = pl.program_id(0)
    steps = pl.num_programs(0)
    K, tm, _ = buf.shape[1:]
    slot = i % 2

    def fetch(d_ref, s):
        for k in range(K):
            def body(t, carry):
                pltpu.make_async_copy(y_hbm.at[pl.ds(d_ref[0, k * tm + t], 1)], buf.at[s, k, pl.ds(t, 1)],
                                      sem.at[s]).start()
                return carry
            lax.fori_loop(0, tm, body, 0, unroll=8)

    @pl.when(i == 0)
    def _():
        fetch(dest_ref, 0)

    @pl.when(i + 1 < steps)
    def _():
        fetch(dest_next_ref, 1 - slot)

    acc = jnp.zeros(o_ref.shape, F32)
    gate = gate_ref[...]
    for k in range(K):
        pltpu.make_async_copy(y_hbm.at[pl.ds(0, tm)], buf.at[slot, k], sem.at[slot]).wait()
    for k in range(K):
        acc = acc + gate[:, k:k + 1] * buf[slot, k]
    o_ref[...] = acc


def _combine(y, dest, gate, tm=128):
    T, K = dest.shape
    D = y.shape[1]
    steps = T // tm
    d3 = jnp.swapaxes(dest.reshape(steps, tm, K), 1, 2).reshape(steps, 1, K * tm)
    return pl.pallas_call(
        _combine_kernel,
        grid=(steps,),
        in_specs=[pl.BlockSpec((None, 1, K * tm), lambda i: (i, 0, 0), memory_space=pltpu.SMEM),
                  pl.BlockSpec((None, 1, K * tm), lambda i: (jnp.minimum(i + 1, steps - 1), 0, 0),
                               memory_space=pltpu.SMEM),
                  pl.BlockSpec((tm, K), lambda i: (i, 0)),
                  pl.BlockSpec(memory_space=pl.ANY)],
        out_specs=pl.BlockSpec((tm, D), lambda i: (i, 0)),
        out_shape=jax.ShapeDtypeStruct((T, D), F32),
        scratch_shapes=[pltpu.VMEM((2, K, tm, D), F32), pltpu.SemaphoreType.DMA((2,))],
        compiler_params=_params("arbitrary"),
        name="combine",
    )(d3, d3, gate, y)


def _even_mixer(x2, xb, B, S, ln_g, ln_b, w_in, conv_w, conv_b, gate_b, ml_gn_w, rw_mu, rw_w0, rw_w2,
                rw_a0, rw_a2, rw_g2, rw_kk, rw_ka, rw_rk, rw_gn_w, rw_gn_b, w_out, j):
    D = x2.shape[1]
    H = ML_HEADS
    ML = ml_gn_w.shape[0]
    RW = rw_w0.shape[0]
    LO = RW_LORA_W + RW_LORA_A + RW_LORA_G
    ml_main = 3 * ML
    ml_cols = ml_main + 2 * H
    pad_to = 512
    used = ml_main + 3 * RW + LO + LANES
    total = -(-used // pad_to) * pad_to
    w_perm = jnp.concatenate([
        w_in[:, :ml_main], w_in[:, ml_cols:ml_cols + 3 * RW + LO], w_in[:, ml_main:ml_cols],
        jnp.zeros((D, total - used + LANES - 2 * H), w_in.dtype)], axis=1).astype(BF16)
    z3 = _mm(xb, w_perm[None], 0, BF16, 1024, pad_to).reshape(B, S, total)
    gate_col0 = ml_main + 3 * RW + LO
    L = ML_CHUNK
    gates_t = jnp.swapaxes(
        z3[:, :, gate_col0:gate_col0 + 2 * H].astype(F32).reshape(B, S // L, L, 2 * H), 2, 3)
    h_ml = _mlstm(z3, gates_t, conv_w, conv_b, gate_b, ml_gn_w,
                  qk_blk=0, v_blk=1, o_blk=2, gate_blk=gate_col0 // LANES)
    h_rw = _rwkv(z3, rw_mu, rw_w0, rw_w2, rw_a0, rw_a2, rw_g2, rw_kk, rw_ka, rw_rk, rw_gn_w, rw_gn_b,
                 col0=ml_main, lora_col0=ml_main + 3 * RW, width=RW)
    h = jnp.concatenate([h_ml, h_rw], axis=-1).reshape(B * S, ML + RW)
    return _mm_res_ln(h, w_out, j, x2, ln_g, ln_b)


def _odd_mixer(x2, xb, B, S, positions, ln_g, ln_b, w_in, gn_w, gn_b, w_out, j):
    z3 = _mm(xb, w_in, j, BF16, 1024, 512).reshape(B, S, w_in.shape[2])
    h = _retention(z3, positions, gn_w, gn_b)
    return _mm_res_ln(h.reshape(B * S, -1), w_out, j, x2, ln_g, ln_b)


def _cross_attention(x2, xb, B, S, memb, ln_g, ln_b, wq, wkv, wo, layer):
    D = x2.shape[1]
    q = _mm(xb, wq, layer, BF16, 1024, 512)
    kv = _mm(memb, wkv, layer, BF16, memb.shape[0], 512)
    o = _xattn(q.reshape(B, S, D), kv.reshape(B, -1, 2 * D))
    return _mm_res_ln(o.reshape(B * S, D), wo, layer, x2, ln_g, ln_b)


def _moe(x2, xb, ln_g, ln_b, router_w, router_bias, w1, w3, w2, sw1, sw3, sw2, layer):
    top_e, gate = _router(x2, router_w, router_bias)
    tile_expert, n_used, row_token, dest = _dispatch_plan(top_e, w1.shape[1], MOE_ROWS_PER_TILE)
    routed = _combine(_sparse_experts(x2, tile_expert, n_used, row_token, w1, w3, w2, layer), dest, gate)
    hs = _glu(xb, sw1, sw3, layer)
    return _mm_res_ln(hs, sw2, layer, x2, ln_g, ln_b, add=routed)


def kernel(x, mem, positions, ln_g, ln_b, xa_wq, xa_wkv, xa_wo, router_w, router_bias, moe_w1, moe_w3, moe_w2, sh_w1, sh_w3, sh_w2, ev_w_in, ml_conv_w, ml_conv_b, ml_gate_b, ml_gn_w, rw_mu, rw_w0, rw_w2, rw_a0, rw_a2, rw_g2, rw_kk, rw_ka, rw_rk, rw_gn_w, rw_gn_b, ev_w_out, od_w_in, ret_gn_w, ret_gn_b, od_w_out):
    B, S, D = x.shape
    x2 = x.reshape(B * S, D)
    xb = x2.astype(BF16)
    memb = mem.reshape(-1, D).astype(BF16)
    (xa_wq, xa_wkv, xa_wo, sh_w1, sh_w3, sh_w2, ev_w_out, od_w_in, od_w_out) = (
        w.astype(BF16) for w in (xa_wq, xa_wkv, xa_wo, sh_w1, sh_w3, sh_w2, ev_w_out, od_w_in, od_w_out))
    for layer in range(ln_g.shape[0]):
        j = layer // 2
        if layer % 2 == 0:
            x2, xb = _even_mixer(x2, xb, B, S, ln_g[layer, 0], ln_b[layer, 0], ev_w_in[j], ml_conv_w[j],
                                 ml_conv_b[j], ml_gate_b[j], ml_gn_w[j], rw_mu[j], rw_w0[j], rw_w2[j],
                                 rw_a0[j], rw_a2[j], rw_g2[j], rw_kk[j], rw_ka[j], rw_rk[j],
                                 rw_gn_w[j], rw_gn_b[j], ev_w_out, j)
        else:
            x2, xb = _odd_mixer(x2, xb, B, S, positions, ln_g[layer, 0], ln_b[layer, 0], od_w_in,
                                ret_gn_w[j], ret_gn_b[j], od_w_out, j)
        x2, xb = _cross_attention(x2, xb, B, S, memb, ln_g[layer, 1], ln_b[layer, 1], xa_wq, xa_wkv,
                                  xa_wo, layer)
        x2, xb = _moe(x2, xb, ln_g[layer, 2], ln_b[layer, 2], router_w[layer], router_bias[layer],
                      moe_w1, moe_w3, moe_w2, sh_w1, sh_w3, sh_w2, layer)
    return x2.reshape(B, S, D)
```

```python
import functools

import numpy as np
import jax
import jax.numpy as jnp
from jax import lax
from jax.experimental import pallas as pl
from jax.experimental.pallas import tpu as pltpu

F32 = jnp.float32
BF16 = jnp.bfloat16
HI = lax.Precision.HIGHEST

DEPTH = 4
ALPHA = (2.0 * DEPTH) ** 0.25
LN_EPS = 1e-5
NEG_BIG = -1e30
CONV_K = 4
ML_HEADS = 4
RW_HEAD = 64
RW_LORA_W = 64
RW_LORA_A = 64
RW_LORA_G = 128
RW_GN_EPS = 64e-5
RET_HEADS = 8
ROPE_BASE = 10000.0
XA_HEADS = 4
N_EXPERTS = 64
TOP_K = 8
N_GROUPS = 8
TOPK_GROUPS = 4
ROUTED_SCALE = 2.5

ML_CHUNK = 64
RW_CHUNK = 64
RET_CHUNK = 256

V7X_VMEM_BYTES = 64 * 2 ** 20
VMEM_LIMIT = (V7X_VMEM_BYTES * 3) // 4
LANES = 128
V7X_MXU_DIM = 256


def _params(*sem):
    return pltpu.CompilerParams(dimension_semantics=sem, vmem_limit_bytes=VMEM_LIMIT)


def _bdot(a, b):
    return jnp.dot(a.astype(BF16), b.astype(BF16), preferred_element_type=F32)


def _bdot_nt(a, b):
    return lax.dot_general(a.astype(BF16), b.astype(BF16), (((1,), (1,)), ((), ())),
                           preferred_element_type=F32)


def _bdot_tn(a, b):
    return lax.dot_general(a.astype(BF16), b.astype(BF16), (((0,), (0,)), ((), ())),
                           preferred_element_type=F32)


def _sigmoid(x):
    return 1.0 / (1.0 + jnp.exp(-x))


def _softplus(x):
    return jnp.maximum(x, 0.0) + jnp.log1p(jnp.exp(-jnp.abs(x)))


def _iota2(shape, dim):
    return lax.broadcasted_iota(jnp.int32, shape, dim)


def _mm_kernel(x_ref, w_ref, o_ref):
    o_ref[...] = _bdot(x_ref[...], w_ref[...]).astype(o_ref.dtype)


def _mm(x, w, layer, out_dtype, tm, tn):
    M, K = x.shape
    N = w.shape[2]
    assert M % tm == 0 and N % tn == 0
    return pl.pallas_call(
        _mm_kernel,
        grid=(M // tm, N // tn),
        in_specs=[pl.BlockSpec((tm, K), lambda i, j: (i, 0)),
                  pl.BlockSpec((None, K, tn), lambda i, j: (layer, 0, j))],
        out_specs=pl.BlockSpec((tm, tn), lambda i, j: (i, j)),
        out_shape=jax.ShapeDtypeStruct((M, N), out_dtype),
        compiler_params=_params("parallel", "parallel"),
        name="mm",
    )(x, w)


def _layer_norm_rows(y, g, b):
    mu = jnp.mean(y, axis=-1, keepdims=True)
    d = y - mu
    var = jnp.mean(d * d, axis=-1, keepdims=True)
    return d * lax.rsqrt(var + LN_EPS) * g + b


def _mm_res_ln_kernel(a_ref, w_ref, res_ref, g_ref, b_ref, o_ref, ob_ref):
    y = ALPHA * res_ref[...] + _bdot(a_ref[...], w_ref[...])
    out = _layer_norm_rows(y, g_ref[...], b_ref[...])
    o_ref[...] = out
    ob_ref[...] = out.astype(ob_ref.dtype)


def _mm_res_add_ln_kernel(a_ref, w_ref, res_ref, add_ref, g_ref, b_ref, o_ref, ob_ref):
    y = ALPHA * res_ref[...] + (add_ref[...] + _bdot(a_ref[...], w_ref[...]))
    out = _layer_norm_rows(y, g_ref[...], b_ref[...])
    o_ref[...] = out
    ob_ref[...] = out.astype(ob_ref.dtype)


def _mm_res_ln(a, w, layer, res, g, b, add=None, tm=256):
    M, K = a.shape
    N = w.shape[2]
    row = lambda i: (i, 0)
    fixed = lambda i: (0, 0)
    in_specs = [pl.BlockSpec((tm, K), row), pl.BlockSpec((None, K, N), lambda i: (layer, 0, 0)),
                pl.BlockSpec((tm, N), row)]
    args = [a, w, res]
    if add is not None:
        in_specs.append(pl.BlockSpec((tm, N), row))
        args.append(add)
    in_specs += [pl.BlockSpec((1, N), fixed), pl.BlockSpec((1, N), fixed)]
    args += [g.reshape(1, N), b.reshape(1, N)]
    return pl.pallas_call(
        _mm_res_ln_kernel if add is None else _mm_res_add_ln_kernel,
        grid=(M // tm,),
        in_specs=in_specs,
        out_specs=[pl.BlockSpec((tm, N), row), pl.BlockSpec((tm, N), row)],
        out_shape=[jax.ShapeDtypeStruct((M, N), F32), jax.ShapeDtypeStruct((M, N), BF16)],
        compiler_params=_params("parallel"),
        name="mm_res_ln",
    )(*args)


def _glu_kernel(x_ref, w1_ref, w3_ref, o_ref):
    x = x_ref[...]
    h1 = jnp.dot(x, w1_ref[...], preferred_element_type=F32)
    h3 = jnp.dot(x, w3_ref[...], preferred_element_type=F32)
    o_ref[...] = (h1 * _sigmoid(h1) * h3).astype(o_ref.dtype)


def _glu(xb, w1, w3, layer, tm=512):
    M, K = xb.shape
    N = w1.shape[2]
    wspec = pl.BlockSpec((None, K, N), lambda i: (layer, 0, 0))
    return pl.pallas_call(
        _glu_kernel,
        grid=(M // tm,),
        in_specs=[pl.BlockSpec((tm, K), lambda i: (i, 0)), wspec, wspec],
        out_specs=pl.BlockSpec((tm, N), lambda i: (i, 0)),
        out_shape=jax.ShapeDtypeStruct((M, N), BF16),
        compiler_params=_params("parallel"),
        name="glu",
    )(xb, w1, w3)


def _log_sigmoid(x):
    return jnp.minimum(x, 0.0) - jnp.log1p(jnp.exp(-jnp.abs(x)))


def _mlstm_kernel(qk_ref, v_ref, o_ref, gc_ref, gr_ref, cw_ref, cb_ref, gbc_ref, gbr_ref, gn_ref,
                  out_ref, prev_ref, c_ref, n_ref, m_ref, *, heads, dk, dv):
    L = qk_ref.shape[1]

    @pl.when(pl.program_id(1) == 0)
    def _():
        prev_ref[...] = jnp.zeros_like(prev_ref)
        c_ref[...] = jnp.zeros_like(c_ref)
        n_ref[...] = jnp.zeros_like(n_ref)
        m_ref[...] = jnp.full_like(m_ref, NEG_BIG)

    cur = qk_ref[0].astype(F32)
    prev = prev_ref[...]
    row = _iota2((L, 1), 0)
    acc = cb_ref[...] + cw_ref[CONV_K - 1:CONV_K, :] * cur
    for j in range(CONV_K - 1):
        s = CONV_K - 1 - j
        shifted = jnp.where(row < s, pltpu.roll(prev, s, 0), pltpu.roll(cur, s, 0))
        acc = acc + cw_ref[j:j + 1, :] * shifted
    prev_ref[...] = cur
    qk = acc * _sigmoid(acc)
    qw = heads * dk

    tri = (_iota2((L, L), 0) >= _iota2((L, L), 1)).astype(F32)
    g_col = gc_ref[0].astype(F32) + gbc_ref[...]
    b_col = jnp.dot(tri, _log_sigmoid(g_col), precision=HI, preferred_element_type=F32)
    g_row = gr_ref[0, 0] + gbr_ref[...]
    b_row = lax.dot_general(_log_sigmoid(g_row), tri, (((1,), (1,)), ((), ())), precision=HI,
                            preferred_element_type=F32)
    causal = _iota2((L, L), 0) >= _iota2((L, L), 1)

    for h in range(heads):
        q = qk[:, h * dk:(h + 1) * dk]
        k = qk[:, qw + h * dk:qw + (h + 1) * dk] * dk ** -0.5
        v = v_ref[0, :, h * dv:(h + 1) * dv]
        bc = b_col[:, heads + h:heads + h + 1]
        ic = g_col[:, h:h + 1]
        br = b_row[heads + h:heads + h + 1, :]
        ir = g_row[h:h + 1, :]
        b_last = bc[L - 1:L, :]
        c_prev = c_ref[h]
        n_prev = n_ref[h]
        m_prev = m_ref[h][:, 0:1]

        log_d = jnp.where(causal, bc - br + ir, -jnp.inf)
        g_inter = bc + m_prev
        m_t = jnp.maximum(jnp.max(log_d, axis=-1, keepdims=True), g_inter)
        p = jnp.exp(log_d - m_t) * _bdot_nt(q, k)
        e_inter = jnp.exp(g_inter - m_t)
        num = _bdot(p, v) + e_inter * _bdot(q, c_prev)
        den = jnp.sum(p, axis=-1, keepdims=True) + e_inter * jnp.sum(q * n_prev, axis=-1, keepdims=True)
        hh = num / jnp.maximum(jnp.abs(den), jnp.exp(-m_t))

        w_end = b_last - bc + ic
        m_loc = jnp.max(w_end, axis=0, keepdims=True)
        e_end = jnp.exp(w_end - m_loc)
        ke = k * e_end
        m_new = jnp.maximum(b_last + m_prev, m_loc)
        s_prev = jnp.exp(b_last + m_prev - m_new)
        s_new = jnp.exp(m_loc - m_new)
        c_ref[h] = s_prev * c_prev + s_new * _bdot_tn(ke, v)
        n_ref[h] = s_prev * n_prev + s_new * jnp.sum(ke, axis=0, keepdims=True)
        m_ref[h] = jnp.broadcast_to(m_new, m_ref.shape[1:])

        mu = jnp.mean(hh, axis=-1, keepdims=True)
        d = hh - mu
        var = jnp.mean(d * d, axis=-1, keepdims=True)
        y = d * lax.rsqrt(var + 1e-6) * gn_ref[:, h * dv:(h + 1) * dv]
        out_ref[0, :, h * dv:(h + 1) * dv] = (y * _sigmoid(o_ref[0, :, h * dv:(h + 1) * dv].astype(F32))).astype(out_ref.dtype)


def _mlstm(z3, gates_t, conv_w, conv_b, gate_b, gn_w, *, qk_blk, v_blk, o_blk, gate_blk):
    B, S, _ = z3.shape
    H = ML_HEADS
    W = gn_w.shape[0]
    dv = W // H
    dk = conv_w.shape[1] // (2 * H)
    assert conv_w.shape[1] == W, "q|k block and v block share one column-block width"
    L = ML_CHUNK
    gb_col = jnp.zeros((1, LANES), F32).at[0, :2 * H].set(gate_b.reshape(-1))
    gb_row = gate_b.reshape(2 * H, 1)
    fixed = lambda b, c: (0, 0)
    return pl.pallas_call(
        functools.partial(_mlstm_kernel, heads=H, dk=dk, dv=dv),
        grid=(B, S // L),
        in_specs=[pl.BlockSpec((1, L, W), lambda b, c: (b, c, qk_blk)),
                  pl.BlockSpec((1, L, W), lambda b, c: (b, c, v_blk)),
                  pl.BlockSpec((1, L, W), lambda b, c: (b, c, o_blk)),
                  pl.BlockSpec((1, L, LANES), lambda b, c: (b, c, gate_blk)),
                  pl.BlockSpec((1, 1, 2 * H, L), lambda b, c: (b, c, 0, 0)),
                  pl.BlockSpec((CONV_K, W), fixed),
                  pl.BlockSpec((1, W), fixed),
                  pl.BlockSpec((1, LANES), fixed),
                  pl.BlockSpec((2 * H, 1), fixed),
                  pl.BlockSpec((1, W), fixed)],
        out_specs=pl.BlockSpec((1, L, W), lambda b, c: (b, c, 0)),
        out_shape=jax.ShapeDtypeStruct((B, S, W), BF16),
        scratch_shapes=[pltpu.VMEM((L, W), F32),
                        pltpu.VMEM((H, dk, dv), F32),
                        pltpu.VMEM((H, 1, dk), F32),
                        pltpu.VMEM((H, 1, LANES), F32)],
        compiler_params=_params("parallel", "arbitrary"),
        name="mlstm",
    )(z3, z3, z3, z3, gates_t, conv_w, conv_b.reshape(1, W), gb_col, gb_row, gn_w.reshape(1, W))


def _unit_lower_inverse(a, t_idx, i_idx, block_diag, matmul):
    n = a.shape[-2]
    eye = (t_idx == i_idx).astype(F32)
    t = None
    s = 1
    while s < n:
        pair = (t_idx // (2 * s)) == (i_idx // (2 * s))
        off = pair & ((t_idx // s) % 2 == 1) & ((i_idx // s) % 2 == 0)
        q = jnp.where(off, a, 0.0)
        t = eye - q if t is None else t - matmul(matmul(t, block_diag(q)), block_diag(t))
        s *= 2
    return t


def _rwkv_kernel(r_ref, k_ref, v_ref, l_ref, mur_ref, muk_ref, muv_ref, mul_ref, w0_ref, w2_ref,
                 a0_ref, a2_ref, g2_ref, kk_ref, ka_ref, rk_ref, gnw_ref, gnb_ref, out_ref,
                 pr_ref, pk_ref, pv_ref, plo_ref, s_ref, *, hd, gw):
    nb, L, W = r_ref.shape
    ng = W // gw
    hpg = gw // hd
    assert hpg * L == gw

    @pl.when(pl.program_id(1) == 0)
    def _():
        pr_ref[...] = jnp.zeros_like(pr_ref)
        pk_ref[...] = jnp.zeros_like(pk_ref)
        pv_ref[...] = jnp.zeros_like(pv_ref)
        plo_ref[...] = jnp.zeros_like(plo_ref)
        s_ref[...] = jnp.zeros_like(s_ref)

    R = nb * L
    first = _iota2((L, 1), 0) == 0
    same_head = (_iota2((gw, gw), 0) // hd) == (_iota2((gw, gw), 1) // hd)
    zero = jnp.zeros((), BF16)
    same_head_b = same_head.astype(F32).astype(BF16)
    row, col = _iota2((R, R), 0), _iota2((R, R), 1)
    tri = ((row // L == col // L) & (row >= col)).astype(F32).astype(BF16)
    t_idx = _iota2((L, gw), 0)
    i_idx = _iota2((L, gw), 1) % L
    strict = t_idx > i_idx
    lower = t_idx >= i_idx

    def shift_lerp(x_ref, p_ref, mu_ref):
        parts = []
        for b in range(nb):
            cur = x_ref[b].astype(F32)
            shifted = jnp.where(first, p_ref[b], pltpu.roll(cur, 1, 0))
            p_ref[b] = cur[L - 1:L, :]
            parts.append(cur + mu_ref[...] * (shifted - cur))
        return jnp.concatenate(parts, axis=0)

    def split(x, terms):
        pieces = []
        for _ in range(terms):
            p = x.astype(BF16)
            pieces.append(p)
            x = x - p.astype(F32)
        return pieces

    def head_sums(xs):
        stacked = jnp.concatenate([x[:, i * gw:(i + 1) * gw] for x in xs for i in range(ng)], axis=0)
        n = stacked.shape[0]
        s = jnp.dot(jnp.concatenate(split(stacked, 2), axis=0), same_head_b, preferred_element_type=F32)
        s = s[:n] + s[n:]
        return [jnp.concatenate([s[(j * ng + i) * R:(j * ng + i + 1) * R] for i in range(ng)], axis=1)
                for j in range(len(xs))]

    r = shift_lerp(r_ref, pr_ref, mur_ref)
    k = shift_lerp(k_ref, pk_ref, muk_ref)
    v = shift_lerp(v_ref, pv_ref, muv_ref)
    lo = shift_lerp(l_ref, plo_ref, mul_ref)
    xw = lo[:, :RW_LORA_W]
    xa = lo[:, RW_LORA_W:RW_LORA_W + RW_LORA_A]
    xg = lo[:, RW_LORA_W + RW_LORA_A:]

    w_log = -jnp.exp(-_softplus(-(w0_ref[...] + _bdot(jnp.tanh(xw), w2_ref[...]))) - 0.5)
    a = _sigmoid(a0_ref[...] + _bdot(xa, a2_ref[...]))
    g = _bdot(_sigmoid(xg), g2_ref[...])

    kk = k * kk_ref[...]
    k = k * (1.0 + (a - 1.0) * ka_ref[...])
    kk_sq, rk_sum = head_sums([kk * kk, r * k * rk_ref[...]])
    kk = kk * lax.rsqrt(jnp.maximum(kk_sq, 1e-24))
    bv = kk * a

    c3 = jnp.dot(tri, jnp.concatenate(split(w_log, 3), axis=1), preferred_element_type=F32)
    c = c3[:, :W] + c3[:, W:2 * W] + c3[:, 2 * W:]
    c_last = jnp.concatenate([jnp.broadcast_to(c[(b + 1) * L - 1:(b + 1) * L], (L, W)) for b in range(nb)],
                             axis=0)
    e_end = jnp.exp(c_last - c)
    e_neg = jnp.exp(-c)
    kt = (kk * jnp.exp(c - w_log)).astype(BF16)
    rt = (r * jnp.exp(c)).astype(BF16)
    kh = (k * e_neg).astype(BF16)
    bh = (bv * e_neg).astype(BF16)
    kp = (k * e_end).astype(BF16)
    bp = (bv * e_end).astype(BF16)
    vb = v.astype(BF16)
    p_last = jnp.exp(c_last)

    def groups(x):
        return jnp.stack([x[b * L:(b + 1) * L, i * gw:(i + 1) * gw] for b in range(nb) for i in range(ng)])

    def block_diag(x):
        return jnp.where(same_head, jnp.concatenate([x.astype(BF16)] * hpg, axis=1), zero)

    def bmm(x, y):
        return jnp.einsum("gmk,gkn->gmn", x.astype(BF16), y.astype(BF16), preferred_element_type=F32)

    def bmm_nt(x, y):
        return jnp.einsum("gmk,gnk->gmn", x.astype(BF16), y.astype(BF16), preferred_element_type=F32)

    lhs = jnp.concatenate([groups(kt), groups(rt)], axis=1)
    m_k = bmm_nt(lhs, block_diag(groups(kh)))
    m_b = bmm_nt(lhs, block_diag(groups(bh)))
    a_kk = jnp.where(strict, m_k[:, :L], 0.0)
    a_kb = jnp.where(strict, m_b[:, :L], 0.0)
    a_rk = jnp.where(lower, m_k[:, L:], 0.0)
    a_rb = jnp.where(lower, m_b[:, L:], 0.0)
    t_inv = _unit_lower_inverse(a_kb, t_idx, i_idx, block_diag, bmm)
    st = s_ref[...]
    gs = bmm_nt(lhs, st)
    vg = groups(vb)
    v_bd = block_diag(vg)
    u = bmm(t_inv, block_diag(gs[:, :L] + bmm(a_kk, v_bd)))
    yg = gs[:, L:] + bmm(jnp.concatenate([a_rk, -a_rb], axis=2), jnp.concatenate([v_bd, block_diag(u)], axis=1))
    vu = jnp.concatenate([vg, (-u).astype(BF16)], axis=1)
    kb = jnp.concatenate([groups(kp), groups(bp)], axis=1)
    pg = groups(p_last)[:, :1]
    for j in range(nb * ng):
        s_ref[j] = st[j] * pg[j] + jnp.where(same_head, _bdot_tn(vu[j], kb[j]), 0.0)
    ys = [jnp.concatenate([yg[b * ng + i] for i in range(ng)], axis=1) for b in range(nb)]

    y = jnp.concatenate(ys, axis=0)
    mu = head_sums([y])[0] * (1.0 / hd)
    d = y - mu
    var = head_sums([d * d])[0] * (1.0 / hd)
    yn = d * lax.rsqrt(var + RW_GN_EPS) * gnw_ref[...] + gnb_ref[...]
    out = ((yn + rk_sum * v) * g).astype(out_ref.dtype)
    for b in range(nb):
        out_ref[b] = out[b * L:(b + 1) * L]


def _rwkv(z3, mu, w0, w2, a0, a2, g2, k_k, k_a, r_k, gn_w, gn_b, *, col0, lora_col0, width):
    B, S, _ = z3.shape
    RW = w0.shape[0]
    W = width
    G = RW // W
    L = RW_CHUNK
    LO = RW_LORA_W + RW_LORA_A + RW_LORA_G
    assert col0 % W == 0 and RW % W == 0 and lora_col0 % LO == 0 and W % V7X_MXU_DIM == 0
    zc = lambda off: (lambda g, c: (0, c, off // W + g))
    vec = lambda g, c: (0, g)
    row = lambda x: x.reshape(1, -1)
    return pl.pallas_call(
        functools.partial(_rwkv_kernel, hd=RW_HEAD, gw=V7X_MXU_DIM),
        grid=(G, S // L),
        in_specs=[pl.BlockSpec((B, L, W), zc(col0)),
                  pl.BlockSpec((B, L, W), zc(col0 + RW)),
                  pl.BlockSpec((B, L, W), zc(col0 + 2 * RW)),
                  pl.BlockSpec((B, L, LO), lambda g, c: (0, c, lora_col0 // LO)),
                  pl.BlockSpec((1, W), vec), pl.BlockSpec((1, W), vec), pl.BlockSpec((1, W), vec),
                  pl.BlockSpec((1, LO), lambda g, c: (0, 0)),
                  pl.BlockSpec((1, W), vec),
                  pl.BlockSpec((RW_LORA_W, W), vec),
                  pl.BlockSpec((1, W), vec),
                  pl.BlockSpec((RW_LORA_A, W), vec),
                  pl.BlockSpec((RW_LORA_G, W), vec),
                  pl.BlockSpec((1, W), vec), pl.BlockSpec((1, W), vec), pl.BlockSpec((1, W), vec),
                  pl.BlockSpec((1, W), vec), pl.BlockSpec((1, W), vec)],
        out_specs=pl.BlockSpec((B, L, W), lambda g, c: (0, c, g)),
        out_shape=jax.ShapeDtypeStruct((B, S, RW), BF16),
        scratch_shapes=[pltpu.VMEM((B, 1, W), F32), pltpu.VMEM((B, 1, W), F32), pltpu.VMEM((B, 1, W), F32),
                        pltpu.VMEM((B, 1, LO), F32),
                        pltpu.VMEM((B * (W // V7X_MXU_DIM), V7X_MXU_DIM, V7X_MXU_DIM), F32)],
        compiler_params=_params("parallel", "arbitrary"),
        name="rwkv7",
    )(z3, z3, z3, z3, row(mu[:RW]), row(mu[RW:2 * RW]), row(mu[2 * RW:3 * RW]), row(mu[3 * RW:]),
      row(w0), w2.astype(BF16), row(a0), a2.astype(BF16), g2.astype(BF16), row(k_k), row(k_a),
      row(r_k), row(gn_w), row(gn_b))


def _retention_kernel(q_ref, k_ref, v_ref, g_ref, pos_ref, inv_ref, dec_ref, gnw_ref, gnb_ref,
                      out_ref, r_ref, *, heads, log_gamma):
    L = q_ref.shape[1]
    dk = q_ref.shape[2] // heads
    dv = v_ref.shape[2] // heads
    half = dk // 2

    @pl.when(pl.program_id(1) == 0)
    def _():
        r_ref[...] = jnp.zeros_like(r_ref)

    ang = pos_ref[0] * inv_ref[...]
    cos = jnp.cos(ang)
    sin = jnp.sin(ang)
    t_in = _iota2((L, 1), 0).astype(F32)

    def rot(t):
        t1, t2 = t[:, :half], t[:, half:]
        return jnp.concatenate([t1 * cos - t2 * sin, t1 * sin + t2 * cos], axis=-1)

    for h in range(heads):
        lg = log_gamma[h]
        q = rot(q_ref[0, :, h * dk:(h + 1) * dk].astype(F32))
        k = rot(k_ref[0, :, h * dk:(h + 1) * dk].astype(F32)) * dk ** -0.5
        v = v_ref[0, :, h * dv:(h + 1) * dv]
        state = r_ref[h]
        inter = _bdot(q, state) * jnp.exp(lg * (t_in + 1.0))
        intra = _bdot(_bdot_nt(q, k) * dec_ref[h], v)
        r_ref[h] = state * float(np.exp(lg * L)) + _bdot_tn(k * jnp.exp(lg * (L - 1.0 - t_in)), v)
        y = intra + inter
        mu = jnp.mean(y, axis=-1, keepdims=True)
        d = y - mu
        var = jnp.mean(d * d, axis=-1, keepdims=True)
        yn = d * lax.rsqrt(var + 1e-6) * gnw_ref[:, h * dv:(h + 1) * dv] + gnb_ref[:, h * dv:(h + 1) * dv]
        gate = g_ref[0, :, h * dv:(h + 1) * dv].astype(F32)
        out_ref[0, :, h * dv:(h + 1) * dv] = (gate * _sigmoid(gate) * yn).astype(out_ref.dtype)


def _retention(z3, positions, gn_w, gn_b):
    B, S, _ = z3.shape
    W = gn_w.shape[0]
    H = RET_HEADS
    dk = W // H
    half = dk // 2
    L = RET_CHUNK
    log_gamma = np.log1p(-np.exp2(-5.0 - np.arange(H, dtype=np.float64)))
    rel = np.arange(L)[:, None] - np.arange(L)[None, :]
    decay = np.where(rel >= 0, np.exp(log_gamma[:, None, None] * np.maximum(rel, 0)), 0.0).astype(np.float32)
    inv = (ROPE_BASE ** (-jnp.arange(half, dtype=F32) / half)).reshape(1, half)
    pos = positions.astype(F32).reshape(B, S, 1)
    blk = lambda j: pl.BlockSpec((1, L, W), lambda b, c: (b, c, j))
    fixed2 = lambda b, c: (0, 0)
    return pl.pallas_call(
        functools.partial(_retention_kernel, heads=H, log_gamma=tuple(float(x) for x in log_gamma)),
        grid=(B, S // L),
        in_specs=[blk(0), blk(1), blk(2), blk(3),
                  pl.BlockSpec((1, L, 1), lambda b, c: (b, c, 0)),
                  pl.BlockSpec((1, half), fixed2),
                  pl.BlockSpec((H, L, L), lambda b, c: (0, 0, 0)),
                  pl.BlockSpec((1, W), fixed2), pl.BlockSpec((1, W), fixed2)],
        out_specs=pl.BlockSpec((1, L, W), lambda b, c: (b, c, 0)),
        out_shape=jax.ShapeDtypeStruct((B, S, W), BF16),
        scratch_shapes=[pltpu.VMEM((H, dk, W // H), F32)],
        compiler_params=_params("parallel", "arbitrary"),
        name="retention",
    )(z3, z3, z3, z3, pos, inv, jnp.asarray(decay), gn_w.reshape(1, W), gn_b.reshape(1, W))


def _xattn_kernel(q_ref, k_ref, v_ref, o_ref, *, heads):
    hd = q_ref.shape[2] // heads
    for h in range(heads):
        sl = slice(h * hd, (h + 1) * hd)
        s = _bdot_nt(q_ref[0, :, sl], k_ref[0, :, sl]) * hd ** -0.5
        e = jnp.exp(s - jnp.max(s, axis=-1, keepdims=True))
        p = e / jnp.sum(e, axis=-1, keepdims=True)
        o_ref[0, :, sl] = _bdot(p, v_ref[0, :, sl]).astype(o_ref.dtype)


def _xattn(q3, kv3, tq=512):
    B, S, D = q3.shape
    M = kv3.shape[1]
    return pl.pallas_call(
        functools.partial(_xattn_kernel, heads=XA_HEADS),
        grid=(B, S // tq),
        in_specs=[pl.BlockSpec((1, tq, D), lambda b, i: (b, i, 0)),
                  pl.BlockSpec((1, M, D), lambda b, i: (b, 0, 0)),
                  pl.BlockSpec((1, M, D), lambda b, i: (b, 0, 1))],
        out_specs=pl.BlockSpec((1, tq, D), lambda b, i: (b, i, 0)),
        out_shape=jax.ShapeDtypeStruct((B, S, D), BF16),
        compiler_params=_params("parallel", "parallel"),
        name="xattn",
    )(q3, kv3, kv3)


def _first_argmax(vals, lane, big):
    m = jnp.max(vals, axis=-1, keepdims=True)
    idx = jnp.min(jnp.where(vals == m, lane, big), axis=-1, keepdims=True)
    return m, idx


def _router_kernel(x_ref, w_ref, bias_ref, idx_ref, gate_ref):
    E = w_ref.shape[1]
    per = E // N_GROUPS
    scores = _sigmoid(jnp.dot(x_ref[...], w_ref[...], precision=HI, preferred_element_type=F32))
    biased = scores + bias_ref[...]
    lane = _iota2(biased.shape, 1)
    grp = lane // per
    neg = -jnp.inf

    gs = jnp.zeros_like(biased)
    for g in range(N_GROUPS):
        vals = jnp.where(grp == g, biased, neg)
        m1, i1 = _first_argmax(vals, lane, E)
        m2 = jnp.max(jnp.where(lane == i1, neg, vals), axis=-1, keepdims=True)
        gs = jnp.where(grp == g, m1 + m2, gs)

    keep = jnp.zeros(biased.shape, jnp.bool_)
    for _ in range(TOPK_GROUPS):
        _, gi = _first_argmax(gs, grp, N_GROUPS)
        hit = grp == gi
        keep = keep | hit
        gs = jnp.where(hit, neg, gs)

    cand = jnp.where(keep, biased, neg)
    slot = _iota2(idx_ref.shape, 1)
    idx = jnp.zeros(idx_ref.shape, jnp.int32)
    aff = jnp.zeros(gate_ref.shape, F32)
    for k in range(TOP_K):
        _, ei = _first_argmax(cand, lane, E)
        hit = lane == ei
        idx = jnp.where(slot == k, ei, idx)
        aff = jnp.where(slot == k, jnp.sum(jnp.where(hit, scores, 0.0), axis=-1, keepdims=True), aff)
        cand = jnp.where(hit, neg, cand)
    idx_ref[...] = idx
    gate_ref[...] = ROUTED_SCALE * aff / jnp.sum(aff, axis=-1, keepdims=True)


def _router(x, router_w, router_bias, tm=256):
    T, D = x.shape
    E = router_w.shape[1]
    return pl.pallas_call(
        _router_kernel,
        grid=(T // tm,),
        in_specs=[pl.BlockSpec((tm, D), lambda i: (i, 0)),
                  pl.BlockSpec((D, E), lambda i: (0, 0)),
                  pl.BlockSpec((1, E), lambda i: (0, 0))],
        out_specs=[pl.BlockSpec((tm, LANES), lambda i: (i, 0)), pl.BlockSpec((tm, LANES), lambda i: (i, 0))],
        out_shape=[jax.ShapeDtypeStruct((T, LANES), jnp.int32), jax.ShapeDtypeStruct((T, LANES), F32)],
        compiler_params=_params("parallel"),
        name="router",
    )(x, router_w, router_bias.reshape(1, E))


def _experts_kernel(x_ref, idx_ref, gate_ref, w1_ref, w3_ref, w2_ref, o_ref):
    e = pl.program_id(1)

    @pl.when(e == 0)
    def _():
        o_ref[...] = jnp.zeros_like(o_ref)

    x = x_ref[...]
    h1 = jnp.dot(x, w1_ref[0].astype(BF16), preferred_element_type=F32)
    h3 = jnp.dot(x, w3_ref[0].astype(BF16), preferred_element_type=F32)
    gate = jnp.sum(jnp.where(idx_ref[...] == e, gate_ref[...], 0.0), axis=-1, keepdims=True)
    hidden = h1 * _sigmoid(h1) * h3 * gate
    o_ref[...] += jnp.dot(hidden.astype(BF16), w2_ref[0].astype(BF16), preferred_element_type=F32)


def _experts(xb, idx, gate, w1, w3, w2, layer, tm=1024):
    T, D = xb.shape
    _, E, _, Hd = w1.shape
    slots = idx.shape[1]
    return pl.pallas_call(
        _experts_kernel,
        grid=(T // tm, E),
        in_specs=[pl.BlockSpec((tm, D), lambda i, e: (i, 0)),
                  pl.BlockSpec((tm, slots), lambda i, e: (i, 0)),
                  pl.BlockSpec((tm, slots), lambda i, e: (i, 0)),
                  pl.BlockSpec((None, 1, D, Hd), lambda i, e: (layer, e, 0, 0)),
                  pl.BlockSpec((None, 1, D, Hd), lambda i, e: (layer, e, 0, 0)),
                  pl.BlockSpec((None, 1, Hd, D), lambda i, e: (layer, e, 0, 0))],
        out_specs=pl.BlockSpec((tm, D), lambda i, e: (i, 0)),
        out_shape=jax.ShapeDtypeStruct((T, D), F32),
        compiler_params=_params("parallel", "arbitrary"),
        name="experts",
    )(xb, idx, gate, w1, w3, w2)


def _even_mixer(x2, xb, B, S, ln_g, ln_b, w_in, conv_w, conv_b, gate_b, ml_gn_w, rw_mu, rw_w0, rw_w2,
                rw_a0, rw_a2, rw_g2, rw_kk, rw_ka, rw_rk, rw_gn_w, rw_gn_b, w_out, j):
    D = x2.shape[1]
    H = ML_HEADS
    ML = ml_gn_w.shape[0]
    RW = rw_w0.shape[0]
    LO = RW_LORA_W + RW_LORA_A + RW_LORA_G
    ml_main = 3 * ML
    ml_cols = ml_main + 2 * H
    pad_to = 512
    used = ml_main + 3 * RW + LO + LANES
    total = -(-used // pad_to) * pad_to
    w_perm = jnp.concatenate([
        w_in[:, :ml_main], w_in[:, ml_cols:ml_cols + 3 * RW + LO], w_in[:, ml_main:ml_cols],
        jnp.zeros((D, total - used + LANES - 2 * H), w_in.dtype)], axis=1).astype(BF16)
    z3 = _mm(xb, w_perm[None], 0, BF16, 1024, pad_to).reshape(B, S, total)
    gate_col0 = ml_main + 3 * RW + LO
    L = ML_CHUNK
    gates_t = jnp.swapaxes(
        z3[:, :, gate_col0:gate_col0 + 2 * H].astype(F32).reshape(B, S // L, L, 2 * H), 2, 3)
    h_ml = _mlstm(z3, gates_t, conv_w, conv_b, gate_b, ml_gn_w,
                  qk_blk=0, v_blk=1, o_blk=2, gate_blk=gate_col0 // LANES)
    h_rw = _rwkv(z3, rw_mu, rw_w0, rw_w2, rw_a0, rw_a2, rw_g2, rw_kk, rw_ka, rw_rk, rw_gn_w, rw_gn_b,
                 col0=ml_main, lora_col0=ml_main + 3 * RW, width=RW)
    h = jnp.concatenate([h_ml, h_rw], axis=-1).reshape(B * S, ML + RW)
    return _mm_res_ln(h, w_out, j, x2, ln_g, ln_b)


def _odd_mixer(x2, xb, B, S, positions, ln_g, ln_b, w_in, gn_w, gn_b, w_out, j):
    z3 = _mm(xb, w_in, j, BF16, 1024, 512).reshape(B, S, w_in.shape[2])
    h = _retention(z3, positions, gn_w, gn_b)
    return _mm_res_ln(h.reshape(B * S, -1), w_out, j, x2, ln_g, ln_b)


def _cross_attention(x2, xb, B, S, memb, ln_g, ln_b, wq, wkv, wo, layer):
    D = x2.shape[1]
    q = _mm(xb, wq, layer, BF16, 1024, 512)
    kv = _mm(memb, wkv, layer, BF16, memb.shape[0], 512)
    o = _xattn(q.reshape(B, S, D), kv.reshape(B, -1, 2 * D))
    return _mm_res_ln(o.reshape(B * S, D), wo, layer, x2, ln_g, ln_b)


def _moe(x2, xb, ln_g, ln_b, router_w, router_bias, w1, w3, w2, sw1, sw3, sw2, layer):
    idx, gate = _router(x2, router_w, router_bias)
    routed = _experts(xb, idx, gate, w1, w3, w2, layer)
    hs = _glu(xb, sw1, sw3, layer)
    return _mm_res_ln(hs, sw2, layer, x2, ln_g, ln_b, add=routed)


def kernel(x, mem, positions, ln_g, ln_b, xa_wq, xa_wkv, xa_wo, router_w, router_bias, moe_w1, moe_w3, moe_w2, sh_w1, sh_w3, sh_w2, ev_w_in, ml_conv_w, ml_conv_b, ml_gate_b, ml_gn_w, rw_mu, rw_w0, rw_w2, rw_a0, rw_a2, rw_g2, rw_kk, rw_ka, rw_rk, rw_gn_w, rw_gn_b, ev_w_out, od_w_in, ret_gn_w, ret_gn_b, od_w_out):
    B, S, D = x.shape
    x2 = x.reshape(B * S, D)
    xb = x2.astype(BF16)
    memb = mem.reshape(-1, D).astype(BF16)
    (xa_wq, xa_wkv, xa_wo, sh_w1, sh_w3, sh_w2, ev_w_out, od_w_in, od_w_out) = (
        w.astype(BF16) for w in (xa_wq, xa_wkv, xa_wo, sh_w1, sh_w3, sh_w2, ev_w_out, od_w_in, od_w_out))
    for layer in range(ln_g.shape[0]):
        j = layer // 2
        if layer % 2 == 0:
            x2, xb = _even_mixer(x2, xb, B, S, ln_g[layer, 0], ln_b[layer, 0], ev_w_in[j], ml_conv_w[j],
                                 ml_conv_b[j], ml_gate_b[j], ml_gn_w[j], rw_mu[j], rw_w0[j], rw_w2[j],
                                 rw_a0[j], rw_a2[j], rw_g2[j], rw_kk[j], rw_ka[j], rw_rk[j],
                                 rw_gn_w[j], rw_gn_b[j], ev_w_out, j)
        else:
            x2, xb = _odd_mixer(x2, xb, B, S, positions, ln_g[layer, 0], ln_b[layer, 0], od_w_in,
                                ret_gn_w[j], ret_gn_b[j], od_w_out, j)
        x2, xb = _cross_attention(x2, xb, B, S, memb, ln_g[layer, 1], ln_b[layer, 1], xa_wq, xa_wkv,
                                  xa_wo, layer)
        x2, xb = _moe(x2, xb, ln_g[layer, 2], ln_b[layer, 2], router_w[layer], router_bias[layer],
                      moe_w1, moe_w3, moe_w2, sh_w1, sh_w3, sh_w2, layer)
    return x2.reshape(B, S, D)
```

```python
import functools

import numpy as np
import jax
import jax.numpy as jnp
from jax import lax
from jax.experimental import pallas as pl
from jax.experimental.pallas import tpu as pltpu

F32 = jnp.float32
BF16 = jnp.bfloat16
HI = lax.Precision.HIGHEST

DEPTH = 4
ALPHA = (2.0 * DEPTH) ** 0.25
LN_EPS = 1e-5
NEG_BIG = -1e30
CONV_K = 4
ML_HEADS = 4
RW_HEAD = 64
RW_LORA_W = 64
RW_LORA_A = 64
RW_LORA_G = 128
RW_GN_EPS = 64e-5
RET_HEADS = 8
ROPE_BASE = 10000.0
XA_HEADS = 4
N_EXPERTS = 64
TOP_K = 8
N_GROUPS = 8
TOPK_GROUPS = 4
ROUTED_SCALE = 2.5

ML_CHUNK = 64
RW_CHUNK = 64
RET_CHUNK = 256

V7X_VMEM_BYTES = 64 * 2 ** 20
VMEM_LIMIT = (V7X_VMEM_BYTES * 3) // 4
LANES = 128
V7X_MXU_DIM = 256


def _params(*sem):
    return pltpu.CompilerParams(dimension_semantics=sem, vmem_limit_bytes=VMEM_LIMIT)


def _bdot(a, b):
    return jnp.dot(a.astype(BF16), b.astype(BF16), preferred_element_type=F32)


def _bdot_nt(a, b):
    return lax.dot_general(a.astype(BF16), b.astype(BF16), (((1,), (1,)), ((), ())),
                           preferred_element_type=F32)


def _bdot_tn(a, b):
    return lax.dot_general(a.astype(BF16), b.astype(BF16), (((0,), (0,)), ((), ())),
                           preferred_element_type=F32)


def _sigmoid(x):
    return 1.0 / (1.0 + jnp.exp(-x))


def _softplus(x):
    return jnp.maximum(x, 0.0) + jnp.log1p(jnp.exp(-jnp.abs(x)))


def _iota2(shape, dim):
    return lax.broadcasted_iota(jnp.int32, shape, dim)


def _mm_kernel(x_ref, w_ref, o_ref):
    o_ref[...] = _bdot(x_ref[...], w_ref[...]).astype(o_ref.dtype)


def _mm(x, w, layer, out_dtype, tm, tn):
    M, K = x.shape
    N = w.shape[2]
    assert M % tm == 0 and N % tn == 0
    return pl.pallas_call(
        _mm_kernel,
        grid=(M // tm, N // tn),
        in_specs=[pl.BlockSpec((tm, K), lambda i, j: (i, 0)),
                  pl.BlockSpec((None, K, tn), lambda i, j: (layer, 0, j))],
        out_specs=pl.BlockSpec((tm, tn), lambda i, j: (i, j)),
        out_shape=jax.ShapeDtypeStruct((M, N), out_dtype),
        compiler_params=_params("parallel", "parallel"),
        name="mm",
    )(x, w)


def _layer_norm_rows(y, g, b):
    mu = jnp.mean(y, axis=-1, keepdims=True)
    d = y - mu
    var = jnp.mean(d * d, axis=-1, keepdims=True)
    return d * lax.rsqrt(var + LN_EPS) * g + b


def _mm_res_ln_kernel(*refs, n_a, has_add):
    a_refs, (w_ref, res_ref), rest = refs[:n_a], refs[n_a:n_a + 2], refs[n_a + 2:]
    add_ref = rest[0] if has_add else None
    g_ref, b_ref, o_ref, ob_ref = rest[-4:]
    acc, k0 = None, 0
    for a_ref in a_refs:
        part = _bdot(a_ref[...], w_ref[k0:k0 + a_ref.shape[1], :])
        acc = part if acc is None else acc + part
        k0 += a_ref.shape[1]
    if has_add:
        acc = add_ref[...] + acc
    out = _layer_norm_rows(ALPHA * res_ref[...] + acc, g_ref[...], b_ref[...])
    o_ref[...] = out
    ob_ref[...] = out.astype(ob_ref.dtype)


def _mm_res_ln(a_parts, w, layer, res, g, b, add=None, tm=256):
    M = res.shape[0]
    K, N = w.shape[1:]
    assert sum(a.shape[1] for a in a_parts) == K
    row = lambda i: (i, 0)
    fixed = lambda i: (0, 0)
    in_specs = [pl.BlockSpec((tm, a.shape[1]), row) for a in a_parts]
    in_specs += [pl.BlockSpec((None, K, N), lambda i: (layer, 0, 0)), pl.BlockSpec((tm, N), row)]
    args = [*a_parts, w, res]
    if add is not None:
        in_specs.append(pl.BlockSpec((tm, N), row))
        args.append(add)
    in_specs += [pl.BlockSpec((1, N), fixed), pl.BlockSpec((1, N), fixed)]
    args += [g.reshape(1, N), b.reshape(1, N)]
    return pl.pallas_call(
        functools.partial(_mm_res_ln_kernel, n_a=len(a_parts), has_add=add is not None),
        grid=(M // tm,),
        in_specs=in_specs,
        out_specs=[pl.BlockSpec((tm, N), row), pl.BlockSpec((tm, N), row)],
        out_shape=[jax.ShapeDtypeStruct((M, N), F32), jax.ShapeDtypeStruct((M, N), BF16)],
        compiler_params=_params("parallel"),
        name="mm_res_ln",
    )(*args)


def _glu_kernel(x_ref, w1_ref, w3_ref, o_ref):
    x = x_ref[...]
    h1 = jnp.dot(x, w1_ref[...], preferred_element_type=F32)
    h3 = jnp.dot(x, w3_ref[...], preferred_element_type=F32)
    o_ref[...] = (h1 * _sigmoid(h1) * h3).astype(o_ref.dtype)


def _glu(xb, w1, w3, layer, tm=512):
    M, K = xb.shape
    N = w1.shape[2]
    wspec = pl.BlockSpec((None, K, N), lambda i: (layer, 0, 0))
    return pl.pallas_call(
        _glu_kernel,
        grid=(M // tm,),
        in_specs=[pl.BlockSpec((tm, K), lambda i: (i, 0)), wspec, wspec],
        out_specs=pl.BlockSpec((tm, N), lambda i: (i, 0)),
        out_shape=jax.ShapeDtypeStruct((M, N), BF16),
        compiler_params=_params("parallel"),
        name="glu",
    )(xb, w1, w3)


def _log_sigmoid(x):
    return jnp.minimum(x, 0.0) - jnp.log1p(jnp.exp(-jnp.abs(x)))


def _split_bf16(x, terms):
    pieces = []
    for _ in range(terms):
        p = x.astype(BF16)
        pieces.append(p)
        x = x - p.astype(F32)
    return pieces


def _mlstm_kernel(qk_ref, v_ref, o_ref, gc_ref, gr_ref, cw_ref, cb_ref, gbc_ref, gbr_ref, gn_ref,
                  out_ref, prev_ref, c_ref, n_ref, m_ref, *, heads, dk, dv):
    nb, L, _ = qk_ref.shape
    qw = heads * dk

    @pl.when(pl.program_id(0) == 0)
    def _():
        prev_ref[...] = jnp.zeros_like(prev_ref)
        c_ref[...] = jnp.zeros_like(c_ref)
        n_ref[...] = jnp.zeros_like(n_ref)
        m_ref[...] = jnp.full_like(m_ref, NEG_BIG)

    row = _iota2((L, 1), 0)
    qk = []
    for b in range(nb):
        cur = qk_ref[b].astype(F32)
        prev = prev_ref[b]
        acc = cb_ref[...] + cw_ref[CONV_K - 1:CONV_K, :] * cur
        for j in range(CONV_K - 1):
            s = CONV_K - 1 - j
            shifted = jnp.where(row < s, pltpu.roll(prev, s, 0), pltpu.roll(cur, s, 0))
            acc = acc + cw_ref[j:j + 1, :] * shifted
        prev_ref[b] = cur
        qk.append(acc * _sigmoid(acc))

    causal = _iota2((L, L), 0) >= _iota2((L, L), 1)
    tri = causal.astype(F32).astype(BF16)
    g_col = [gc_ref[b].astype(F32) + gbc_ref[...] for b in range(nb)]
    pieces = _split_bf16(jnp.concatenate([_log_sigmoid(g) for g in g_col], axis=1), 3)
    lanes = nb * g_col[0].shape[1]
    b_col = jnp.dot(tri, jnp.concatenate(pieces, axis=1), preferred_element_type=F32)
    b_col = b_col[:, :lanes] + b_col[:, lanes:2 * lanes] + b_col[:, 2 * lanes:]
    g_row = [gr_ref[b, 0] + gbr_ref[...] for b in range(nb)]
    pieces = _split_bf16(jnp.concatenate([_log_sigmoid(g) for g in g_row], axis=0), 3)
    rows = nb * 2 * heads
    b_row = lax.dot_general(jnp.concatenate(pieces, axis=0), tri, (((1,), (1,)), ((), ())),
                            preferred_element_type=F32)
    b_row = b_row[:rows] + b_row[rows:2 * rows] + b_row[2 * rows:]

    def per_problem(f):
        return jnp.stack([f(b, h) for b in range(nb) for h in range(heads)])

    q = per_problem(lambda b, h: qk[b][:, h * dk:(h + 1) * dk])
    k = per_problem(lambda b, h: qk[b][:, qw + h * dk:qw + (h + 1) * dk]) * dk ** -0.5
    v = per_problem(lambda b, h: v_ref[b, :, h * dv:(h + 1) * dv])
    gcw = g_col[0].shape[1]
    bc = per_problem(lambda b, h: b_col[:, b * gcw + heads + h:b * gcw + heads + h + 1])
    ic = per_problem(lambda b, h: g_col[b][:, h:h + 1])
    br = per_problem(lambda b, h: b_row[b * 2 * heads + heads + h:b * 2 * heads + heads + h + 1])
    ir = per_problem(lambda b, h: g_row[b][h:h + 1])
    b_last = bc[:, L - 1:L]
    c_prev = c_ref[...]
    n_prev = n_ref[...]
    m_prev = m_ref[...][:, :, 0:1]

    def bmm(x, y, spec):
        return jnp.einsum(spec, x.astype(BF16), y.astype(BF16), preferred_element_type=F32)

    log_d = jnp.where(causal, bc - br + ir, -jnp.inf)
    g_inter = bc + m_prev
    m_t = jnp.maximum(jnp.max(log_d, axis=-1, keepdims=True), g_inter)
    p = jnp.exp(log_d - m_t) * bmm(q, k, "gtk,gsk->gts")
    e_inter = jnp.exp(g_inter - m_t)
    num = bmm(p, v, "gts,gsv->gtv") + e_inter * bmm(q, c_prev, "gtk,gkv->gtv")
    den = jnp.sum(p, axis=-1, keepdims=True) + e_inter * jnp.sum(q * n_prev, axis=-1, keepdims=True)
    hh = num / jnp.maximum(jnp.abs(den), jnp.exp(-m_t))

    w_end = b_last - bc + ic
    m_loc = jnp.max(w_end, axis=1, keepdims=True)
    e_end = jnp.exp(w_end - m_loc)
    ke = k * e_end
    m_new = jnp.maximum(b_last + m_prev, m_loc)
    s_prev = jnp.exp(b_last + m_prev - m_new)
    s_new = jnp.exp(m_loc - m_new)
    c_ref[...] = s_prev * c_prev + s_new * bmm(ke, v, "gsk,gsv->gkv")
    n_ref[...] = s_prev * n_prev + s_new * jnp.sum(ke, axis=1, keepdims=True)
    m_ref[...] = jnp.broadcast_to(m_new, m_ref.shape)

    mu = jnp.mean(hh, axis=-1, keepdims=True)
    d = hh - mu
    var = jnp.mean(d * d, axis=-1, keepdims=True)
    y = d * lax.rsqrt(var + 1e-6)
    for b in range(nb):
        for h in range(heads):
            sl = slice(h * dv, (h + 1) * dv)
            gate = _sigmoid(o_ref[b, :, sl].astype(F32))
            out_ref[b, :, sl] = (y[b * heads + h] * gn_ref[:, sl] * gate).astype(out_ref.dtype)


def _mlstm(z3, gates_t, conv_w, conv_b, gate_b, gn_w, *, qk_blk, v_blk, o_blk, gate_blk):
    B, S, _ = z3.shape
    H = ML_HEADS
    W = gn_w.shape[0]
    dv = W // H
    dk = conv_w.shape[1] // (2 * H)
    assert conv_w.shape[1] == W, "q|k block and v block share one column-block width"
    L = ML_CHUNK
    gb_col = jnp.zeros((1, LANES), F32).at[0, :2 * H].set(gate_b.reshape(-1))
    gb_row = gate_b.reshape(2 * H, 1)
    fixed = lambda c: (0, 0)
    return pl.pallas_call(
        functools.partial(_mlstm_kernel, heads=H, dk=dk, dv=dv),
        grid=(S // L,),
        in_specs=[pl.BlockSpec((B, L, W), lambda c: (0, c, qk_blk)),
                  pl.BlockSpec((B, L, W), lambda c: (0, c, v_blk)),
                  pl.BlockSpec((B, L, W), lambda c: (0, c, o_blk)),
                  pl.BlockSpec((B, L, LANES), lambda c: (0, c, gate_blk)),
                  pl.BlockSpec((B, 1, 2 * H, L), lambda c: (0, c, 0, 0)),
                  pl.BlockSpec((CONV_K, W), fixed),
                  pl.BlockSpec((1, W), fixed),
                  pl.BlockSpec((1, LANES), fixed),
                  pl.BlockSpec((2 * H, 1), fixed),
                  pl.BlockSpec((1, W), fixed)],
        out_specs=pl.BlockSpec((B, L, W), lambda c: (0, c, 0)),
        out_shape=jax.ShapeDtypeStruct((B, S, W), BF16),
        scratch_shapes=[pltpu.VMEM((B, L, W), F32),
                        pltpu.VMEM((B * H, dk, dv), F32),
                        pltpu.VMEM((B * H, 1, dk), F32),
                        pltpu.VMEM((B * H, 1, LANES), F32)],
        compiler_params=_params("arbitrary"),
        name="mlstm",
    )(z3, z3, z3, z3, gates_t, conv_w, conv_b.reshape(1, W), gb_col, gb_row, gn_w.reshape(1, W))


def _unit_lower_inverse(a, t_idx, i_idx, block_diag, matmul):
    n = a.shape[-2]
    eye = (t_idx == i_idx).astype(F32)
    t = None
    s = 1
    while s < n:
        pair = (t_idx // (2 * s)) == (i_idx // (2 * s))
        off = pair & ((t_idx // s) % 2 == 1) & ((i_idx // s) % 2 == 0)
        q = jnp.where(off, a, 0.0)
        t = eye - q if t is None else t - matmul(matmul(t, block_diag(q)), block_diag(t))
        s *= 2
    return t


def _rwkv_kernel(r_ref, k_ref, v_ref, l_ref, mur_ref, muk_ref, muv_ref, mul_ref, w0_ref, w2_ref,
                 a0_ref, a2_ref, g2_ref, kk_ref, ka_ref, rk_ref, gnw_ref, gnb_ref, out_ref,
                 pr_ref, pk_ref, pv_ref, plo_ref, s_ref, *, hd, gw):
    nb, L, W = r_ref.shape
    ng = W // gw
    hpg = gw // hd
    assert hpg * L == gw

    @pl.when(pl.program_id(1) == 0)
    def _():
        pr_ref[...] = jnp.zeros_like(pr_ref)
        pk_ref[...] = jnp.zeros_like(pk_ref)
        pv_ref[...] = jnp.zeros_like(pv_ref)
        plo_ref[...] = jnp.zeros_like(plo_ref)
        s_ref[...] = jnp.zeros_like(s_ref)

    R = nb * L
    first = _iota2((L, 1), 0) == 0
    same_head = (_iota2((gw, gw), 0) // hd) == (_iota2((gw, gw), 1) // hd)
    zero = jnp.zeros((), BF16)
    same_head_b = same_head.astype(F32).astype(BF16)
    row, col = _iota2((R, R), 0), _iota2((R, R), 1)
    tri = ((row // L == col // L) & (row >= col)).astype(F32).astype(BF16)
    t_idx = _iota2((L, gw), 0)
    i_idx = _iota2((L, gw), 1) % L
    strict = t_idx > i_idx
    lower = t_idx >= i_idx

    def shift_lerp(x_ref, p_ref, mu_ref):
        parts = []
        for b in range(nb):
            cur = x_ref[b].astype(F32)
            shifted = jnp.where(first, p_ref[b], pltpu.roll(cur, 1, 0))
            p_ref[b] = cur[L - 1:L, :]
            parts.append(cur + mu_ref[...] * (shifted - cur))
        return jnp.concatenate(parts, axis=0)

    def head_sums(xs):
        stacked = jnp.concatenate([x[:, i * gw:(i + 1) * gw] for x in xs for i in range(ng)], axis=0)
        n = stacked.shape[0]
        s = jnp.dot(jnp.concatenate(_split_bf16(stacked, 2), axis=0), same_head_b, preferred_element_type=F32)
        s = s[:n] + s[n:]
        return [jnp.concatenate([s[(j * ng + i) * R:(j * ng + i + 1) * R] for i in range(ng)], axis=1)
                for j in range(len(xs))]

    r = shift_lerp(r_ref, pr_ref, mur_ref)
    k = shift_lerp(k_ref, pk_ref, muk_ref)
    v = shift_lerp(v_ref, pv_ref, muv_ref)
    lo = shift_lerp(l_ref, plo_ref, mul_ref)
    xw = lo[:, :RW_LORA_W]
    xa = lo[:, RW_LORA_W:RW_LORA_W + RW_LORA_A]
    xg = lo[:, RW_LORA_W + RW_LORA_A:]

    w_log = -jnp.exp(-_softplus(-(w0_ref[...] + _bdot(jnp.tanh(xw), w2_ref[...]))) - 0.5)
    a = _sigmoid(a0_ref[...] + _bdot(xa, a2_ref[...]))
    g = _bdot(_sigmoid(xg), g2_ref[...])

    kk = k * kk_ref[...]
    k = k * (1.0 + (a - 1.0) * ka_ref[...])
    kk_sq, rk_sum = head_sums([kk * kk, r * k * rk_ref[...]])
    kk = kk * lax.rsqrt(jnp.maximum(kk_sq, 1e-24))
    bv = kk * a

    c3 = jnp.dot(tri, jnp.concatenate(_split_bf16(w_log, 3), axis=1), preferred_element_type=F32)
    c = c3[:, :W] + c3[:, W:2 * W] + c3[:, 2 * W:]
    c_last = jnp.concatenate([jnp.broadcast_to(c[(b + 1) * L - 1:(b + 1) * L], (L, W)) for b in range(nb)],
                             axis=0)
    e_end = jnp.exp(c_last - c)
    e_neg = jnp.exp(-c)
    kt = (kk * jnp.exp(c - w_log)).astype(BF16)
    rt = (r * jnp.exp(c)).astype(BF16)
    kh = (k * e_neg).astype(BF16)
    bh = (bv * e_neg).astype(BF16)
    kp = (k * e_end).astype(BF16)
    bp = (bv * e_end).astype(BF16)
    vb = v.astype(BF16)
    p_last = jnp.exp(c_last)

    def groups(x):
        return jnp.stack([x[b * L:(b + 1) * L, i * gw:(i + 1) * gw] for b in range(nb) for i in range(ng)])

    def block_diag(x):
        return jnp.where(same_head, jnp.concatenate([x.astype(BF16)] * hpg, axis=1), zero)

    def bmm(x, y):
        return jnp.einsum("gmk,gkn->gmn", x.astype(BF16), y.astype(BF16), preferred_element_type=F32)

    def bmm_nt(x, y):
        return jnp.einsum("gmk,gnk->gmn", x.astype(BF16), y.astype(BF16), preferred_element_type=F32)

    lhs = jnp.concatenate([groups(kt), groups(rt)], axis=1)
    m_k = bmm_nt(lhs, block_diag(groups(kh)))
    m_b = bmm_nt(lhs, block_diag(groups(bh)))
    a_kk = jnp.where(strict, m_k[:, :L], 0.0)
    a_kb = jnp.where(strict, m_b[:, :L], 0.0)
    a_rk = jnp.where(lower, m_k[:, L:], 0.0)
    a_rb = jnp.where(lower, m_b[:, L:], 0.0)
    t_inv = _unit_lower_inverse(a_kb, t_idx, i_idx, block_diag, bmm)
    st = s_ref[...]
    gs = bmm_nt(lhs, st)
    vg = groups(vb)
    v_bd = block_diag(vg)
    u = bmm(t_inv, block_diag(gs[:, :L] + bmm(a_kk, v_bd)))
    yg = gs[:, L:] + bmm(jnp.concatenate([a_rk, -a_rb], axis=2), jnp.concatenate([v_bd, block_diag(u)], axis=1))
    vu = jnp.concatenate([vg, (-u).astype(BF16)], axis=1)
    kb = jnp.concatenate([groups(kp), groups(bp)], axis=1)
    pg = groups(p_last)[:, :1]
    outer = jnp.einsum("gtv,gtk->gvk", vu, kb, preferred_element_type=F32)
    s_ref[...] = st * pg + jnp.where(same_head, outer, 0.0)
    ys = [jnp.concatenate([yg[b * ng + i] for i in range(ng)], axis=1) for b in range(nb)]

    y = jnp.concatenate(ys, axis=0)
    mu = head_sums([y])[0] * (1.0 / hd)
    d = y - mu
    var = head_sums([d * d])[0] * (1.0 / hd)
    yn = d * lax.rsqrt(var + RW_GN_EPS) * gnw_ref[...] + gnb_ref[...]
    out = ((yn + rk_sum * v) * g).astype(out_ref.dtype)
    for b in range(nb):
        out_ref[b] = out[b * L:(b + 1) * L]


def _rwkv(z3, mu, w0, w2, a0, a2, g2, k_k, k_a, r_k, gn_w, gn_b, *, col0, lora_col0, width):
    B, S, _ = z3.shape
    RW = w0.shape[0]
    W = width
    G = RW // W
    L = RW_CHUNK
    LO = RW_LORA_W + RW_LORA_A + RW_LORA_G
    assert col0 % W == 0 and RW % W == 0 and lora_col0 % LO == 0 and W % V7X_MXU_DIM == 0
    zc = lambda off: (lambda g, c: (0, c, off // W + g))
    vec = lambda g, c: (0, g)
    row = lambda x: x.reshape(1, -1)
    return pl.pallas_call(
        functools.partial(_rwkv_kernel, hd=RW_HEAD, gw=V7X_MXU_DIM),
        grid=(G, S // L),
        in_specs=[pl.BlockSpec((B, L, W), zc(col0)),
                  pl.BlockSpec((B, L, W), zc(col0 + RW)),
                  pl.BlockSpec((B, L, W), zc(col0 + 2 * RW)),
                  pl.BlockSpec((B, L, LO), lambda g, c: (0, c, lora_col0 // LO)),
                  pl.BlockSpec((1, W), vec), pl.BlockSpec((1, W), vec), pl.BlockSpec((1, W), vec),
                  pl.BlockSpec((1, LO), lambda g, c: (0, 0)),
                  pl.BlockSpec((1, W), vec),
                  pl.BlockSpec((RW_LORA_W, W), vec),
                  pl.BlockSpec((1, W), vec),
                  pl.BlockSpec((RW_LORA_A, W), vec),
                  pl.BlockSpec((RW_LORA_G, W), vec),
                  pl.BlockSpec((1, W), vec), pl.BlockSpec((1, W), vec), pl.BlockSpec((1, W), vec),
                  pl.BlockSpec((1, W), vec), pl.BlockSpec((1, W), vec)],
        out_specs=pl.BlockSpec((B, L, W), lambda g, c: (0, c, g)),
        out_shape=jax.ShapeDtypeStruct((B, S, RW), BF16),
        scratch_shapes=[pltpu.VMEM((B, 1, W), F32), pltpu.VMEM((B, 1, W), F32), pltpu.VMEM((B, 1, W), F32),
                        pltpu.VMEM((B, 1, LO), F32),
                        pltpu.VMEM((B * (W // V7X_MXU_DIM), V7X_MXU_DIM, V7X_MXU_DIM), F32)],
        compiler_params=_params("parallel", "arbitrary"),
        name="rwkv7",
    )(z3, z3, z3, z3, row(mu[:RW]), row(mu[RW:2 * RW]), row(mu[2 * RW:3 * RW]), row(mu[3 * RW:]),
      row(w0), w2.astype(BF16), row(a0), a2.astype(BF16), g2.astype(BF16), row(k_k), row(k_a),
      row(r_k), row(gn_w), row(gn_b))


def _retention_kernel(q_ref, k_ref, v_ref, g_ref, pos_ref, inv_ref, dec_ref, gnw_ref, gnb_ref,
                      out_ref, r_ref, *, heads, log_gamma):
    L = q_ref.shape[1]
    dk = q_ref.shape[2] // heads
    dv = v_ref.shape[2] // heads
    half = dk // 2

    @pl.when(pl.program_id(1) == 0)
    def _():
        r_ref[...] = jnp.zeros_like(r_ref)

    ang = pos_ref[0] * inv_ref[...]
    cos = jnp.cos(ang)
    sin = jnp.sin(ang)
    t_in = _iota2((L, 1), 0).astype(F32)

    def rot(t):
        t1, t2 = t[:, :half], t[:, half:]
        return jnp.concatenate([t1 * cos - t2 * sin, t1 * sin + t2 * cos], axis=-1)

    for h in range(heads):
        lg = log_gamma[h]
        q = rot(q_ref[0, :, h * dk:(h + 1) * dk].astype(F32))
        k = rot(k_ref[0, :, h * dk:(h + 1) * dk].astype(F32)) * dk ** -0.5
        v = v_ref[0, :, h * dv:(h + 1) * dv]
        state = r_ref[h]
        inter = _bdot(q, state) * jnp.exp(lg * (t_in + 1.0))
        intra = _bdot(_bdot_nt(q, k) * dec_ref[h], v)
        r_ref[h] = state * float(np.exp(lg * L)) + _bdot_tn(k * jnp.exp(lg * (L - 1.0 - t_in)), v)
        y = intra + inter
        mu = jnp.mean(y, axis=-1, keepdims=True)
        d = y - mu
        var = jnp.mean(d * d, axis=-1, keepdims=True)
        yn = d * lax.rsqrt(var + 1e-6) * gnw_ref[:, h * dv:(h + 1) * dv] + gnb_ref[:, h * dv:(h + 1) * dv]
        gate = g_ref[0, :, h * dv:(h + 1) * dv].astype(F32)
        out_ref[0, :, h * dv:(h + 1) * dv] = (gate * _sigmoid(gate) * yn).astype(out_ref.dtype)


def _retention(z3, positions, gn_w, gn_b):
    B, S, _ = z3.shape
    W = gn_w.shape[0]
    H = RET_HEADS
    dk = W // H
    half = dk // 2
    L = RET_CHUNK
    log_gamma = np.log1p(-np.exp2(-5.0 - np.arange(H, dtype=np.float64)))
    rel = np.arange(L)[:, None] - np.arange(L)[None, :]
    decay = np.where(rel >= 0, np.exp(log_gamma[:, None, None] * np.maximum(rel, 0)), 0.0).astype(np.float32)
    inv = (ROPE_BASE ** (-jnp.arange(half, dtype=F32) / half)).reshape(1, half)
    pos = positions.astype(F32).reshape(B, S, 1)
    blk = lambda j: pl.BlockSpec((1, L, W), lambda b, c: (b, c, j))
    fixed2 = lambda b, c: (0, 0)
    return pl.pallas_call(
        functools.partial(_retention_kernel, heads=H, log_gamma=tuple(float(x) for x in log_gamma)),
        grid=(B, S // L),
        in_specs=[blk(0), blk(1), blk(2), blk(3),
                  pl.BlockSpec((1, L, 1), lambda b, c: (b, c, 0)),
                  pl.BlockSpec((1, half), fixed2),
                  pl.BlockSpec((H, L, L), lambda b, c: (0, 0, 0)),
                  pl.BlockSpec((1, W), fixed2), pl.BlockSpec((1, W), fixed2)],
        out_specs=pl.BlockSpec((1, L, W), lambda b, c: (b, c, 0)),
        out_shape=jax.ShapeDtypeStruct((B, S, W), BF16),
        scratch_shapes=[pltpu.VMEM((H, dk, W // H), F32)],
        compiler_params=_params("parallel", "arbitrary"),
        name="retention",
    )(z3, z3, z3, z3, pos, inv, jnp.asarray(decay), gn_w.reshape(1, W), gn_b.reshape(1, W))


def _xattn_kernel(q_ref, k_ref, v_ref, o_ref, *, heads):
    hd = q_ref.shape[2] // heads
    q, k, v = (jnp.stack([r[0, :, h * hd:(h + 1) * hd] for h in range(heads)]) for r in (q_ref, k_ref, v_ref))
    s = jnp.einsum("hqd,hmd->hqm", q, k, preferred_element_type=F32) * hd ** -0.5
    e = jnp.exp(s - jnp.max(s, axis=-1, keepdims=True))
    p = e / jnp.sum(e, axis=-1, keepdims=True)
    o = jnp.einsum("hqm,hmd->hqd", p.astype(BF16), v, preferred_element_type=F32)
    for h in range(heads):
        o_ref[0, :, h * hd:(h + 1) * hd] = o[h].astype(o_ref.dtype)


def _xattn(q3, kv3, tq=512):
    B, S, D = q3.shape
    M = kv3.shape[1]
    return pl.pallas_call(
        functools.partial(_xattn_kernel, heads=XA_HEADS),
        grid=(B, S // tq),
        in_specs=[pl.BlockSpec((1, tq, D), lambda b, i: (b, i, 0)),
                  pl.BlockSpec((1, M, D), lambda b, i: (b, 0, 0)),
                  pl.BlockSpec((1, M, D), lambda b, i: (b, 0, 1))],
        out_specs=pl.BlockSpec((1, tq, D), lambda b, i: (b, i, 0)),
        out_shape=jax.ShapeDtypeStruct((B, S, D), BF16),
        compiler_params=_params("parallel", "parallel"),
        name="xattn",
    )(q3, kv3, kv3)


def _first_argmax(vals, index, big):
    m = jnp.max(vals, axis=0, keepdims=True)
    idx = jnp.min(jnp.where(vals == m, index, big), axis=0, keepdims=True)
    return m, idx


def _router_kernel(x_ref, wt_ref, bias_ref, idx_ref, gate_ref):
    E = wt_ref.shape[0]
    tm = x_ref.shape[0]
    per = E // N_GROUPS
    neg = -jnp.inf
    logits = lax.dot_general(wt_ref[...], x_ref[...], (((1,), (1,)), ((), ())), precision=HI,
                             preferred_element_type=F32)
    scores = _sigmoid(logits)
    biased = scores + bias_ref[...]
    eidx = _iota2((E, tm), 0)

    within = _iota2((per, tm), 0)
    group_scores = []
    for g in range(N_GROUPS):
        vals = biased[g * per:(g + 1) * per]
        m1, i1 = _first_argmax(vals, within, per)
        m2 = jnp.max(jnp.where(within == i1, neg, vals), axis=0, keepdims=True)
        group_scores.append(m1 + m2)
    gs = jnp.concatenate(group_scores, axis=0)

    gidx = _iota2((N_GROUPS, tm), 0)
    keep = jnp.zeros((N_GROUPS, tm), F32)
    for _ in range(TOPK_GROUPS):
        _, gi = _first_argmax(gs, gidx, N_GROUPS)
        hit = gidx == gi
        keep = jnp.where(hit, 1.0, keep)
        gs = jnp.where(hit, neg, gs)
    keep_e = jnp.concatenate([jnp.broadcast_to(keep[g:g + 1], (per, tm)) for g in range(N_GROUPS)], axis=0)

    cand = jnp.where(keep_e > 0.0, biased, neg)
    picks, affs = [], []
    for _ in range(TOP_K):
        _, ei = _first_argmax(cand, eidx, E)
        hit = eidx == ei
        picks.append(ei)
        affs.append(jnp.sum(jnp.where(hit, scores, 0.0), axis=0, keepdims=True))
        cand = jnp.where(hit, neg, cand)
    aff = jnp.concatenate(affs, axis=0)
    idx_ref[...] = jnp.concatenate(picks, axis=0)
    gate_ref[...] = ROUTED_SCALE * aff / jnp.sum(aff, axis=0, keepdims=True)


def _router(x, router_w, router_bias, tm=512):
    T, D = x.shape
    E = router_w.shape[1]
    idx, gate = pl.pallas_call(
        _router_kernel,
        grid=(T // tm,),
        in_specs=[pl.BlockSpec((tm, D), lambda i: (i, 0)),
                  pl.BlockSpec((E, D), lambda i: (0, 0)),
                  pl.BlockSpec((E, 1), lambda i: (0, 0))],
        out_specs=[pl.BlockSpec((TOP_K, tm), lambda i: (0, i)), pl.BlockSpec((TOP_K, tm), lambda i: (0, i))],
        out_shape=[jax.ShapeDtypeStruct((TOP_K, T), jnp.int32), jax.ShapeDtypeStruct((TOP_K, T), F32)],
        compiler_params=_params("parallel"),
        name="router",
    )(x, router_w.T, router_bias.reshape(E, 1))
    return idx.T, gate.T


def _experts_kernel(x_ref, idx_ref, gate_ref, w1_ref, w3_ref, w2_ref, o_ref):
    e = pl.program_id(1)

    @pl.when(e == 0)
    def _():
        o_ref[...] = jnp.zeros_like(o_ref)

    x = x_ref[...]
    h1 = jnp.dot(x, w1_ref[0].astype(BF16), preferred_element_type=F32)
    h3 = jnp.dot(x, w3_ref[0].astype(BF16), preferred_element_type=F32)
    gate = jnp.sum(jnp.where(idx_ref[...] == e, gate_ref[...], 0.0), axis=-1, keepdims=True)
    hidden = h1 * _sigmoid(h1) * h3 * gate
    o_ref[...] += jnp.dot(hidden.astype(BF16), w2_ref[0].astype(BF16), preferred_element_type=F32)


def _experts(xb, idx, gate, w1, w3, w2, layer, tm=1024):
    T, D = xb.shape
    _, E, _, Hd = w1.shape
    slots = idx.shape[1]
    return pl.pallas_call(
        _experts_kernel,
        grid=(T // tm, E),
        in_specs=[pl.BlockSpec((tm, D), lambda i, e: (i, 0)),
                  pl.BlockSpec((tm, slots), lambda i, e: (i, 0)),
                  pl.BlockSpec((tm, slots), lambda i, e: (i, 0)),
                  pl.BlockSpec((None, 1, D, Hd), lambda i, e: (layer, e, 0, 0)),
                  pl.BlockSpec((None, 1, D, Hd), lambda i, e: (layer, e, 0, 0)),
                  pl.BlockSpec((None, 1, Hd, D), lambda i, e: (layer, e, 0, 0))],
        out_specs=pl.BlockSpec((tm, D), lambda i, e: (i, 0)),
        out_shape=jax.ShapeDtypeStruct((T, D), F32),
        compiler_params=_params("parallel", "arbitrary"),
        name="experts",
    )(xb, idx, gate, w1, w3, w2)


def _even_mixer(x2, xb, B, S, ln_g, ln_b, w_in, conv_w, conv_b, gate_b, ml_gn_w, rw_mu, rw_w0, rw_w2,
                rw_a0, rw_a2, rw_g2, rw_kk, rw_ka, rw_rk, rw_gn_w, rw_gn_b, w_out, j):
    D = x2.shape[1]
    H = ML_HEADS
    ML = ml_gn_w.shape[0]
    RW = rw_w0.shape[0]
    LO = RW_LORA_W + RW_LORA_A + RW_LORA_G
    ml_main = 3 * ML
    ml_cols = ml_main + 2 * H
    pad_to = 512
    used = ml_main + 3 * RW + LO + LANES
    total = -(-used // pad_to) * pad_to
    w_perm = jnp.concatenate([
        w_in[:, :ml_main], w_in[:, ml_cols:ml_cols + 3 * RW + LO], w_in[:, ml_main:ml_cols],
        jnp.zeros((D, total - used + LANES - 2 * H), w_in.dtype)], axis=1).astype(BF16)
    z3 = _mm(xb, w_perm[None], 0, BF16, 1024, pad_to).reshape(B, S, total)
    gate_col0 = ml_main + 3 * RW + LO
    L = ML_CHUNK
    gates_t = jnp.swapaxes(
        z3[:, :, gate_col0:gate_col0 + 2 * H].astype(F32).reshape(B, S // L, L, 2 * H), 2, 3)
    h_ml = _mlstm(z3, gates_t, conv_w, conv_b, gate_b, ml_gn_w,
                  qk_blk=0, v_blk=1, o_blk=2, gate_blk=gate_col0 // LANES)
    h_rw = _rwkv(z3, rw_mu, rw_w0, rw_w2, rw_a0, rw_a2, rw_g2, rw_kk, rw_ka, rw_rk, rw_gn_w, rw_gn_b,
                 col0=ml_main, lora_col0=ml_main + 3 * RW, width=RW)
    return _mm_res_ln([h_ml.reshape(B * S, ML), h_rw.reshape(B * S, RW)], w_out, j, x2, ln_g, ln_b)


def _odd_mixer(x2, xb, B, S, positions, ln_g, ln_b, w_in, gn_w, gn_b, w_out, j):
    z3 = _mm(xb, w_in, j, BF16, 1024, 512).reshape(B, S, w_in.shape[2])
    h = _retention(z3, positions, gn_w, gn_b)
    return _mm_res_ln([h.reshape(B * S, -1)], w_out, j, x2, ln_g, ln_b)


def _cross_attention(x2, xb, B, S, memb, ln_g, ln_b, wq, wkv, wo, layer):
    D = x2.shape[1]
    q = _mm(xb, wq, layer, BF16, 1024, 512)
    kv = _mm(memb, wkv, layer, BF16, memb.shape[0], 512)
    o = _xattn(q.reshape(B, S, D), kv.reshape(B, -1, 2 * D))
    return _mm_res_ln([o.reshape(B * S, D)], wo, layer, x2, ln_g, ln_b)


def _moe(x2, xb, ln_g, ln_b, router_w, router_bias, w1, w3, w2, sw1, sw3, sw2, layer):
    idx, gate = _router(x2, router_w, router_bias)
    routed = _experts(xb, idx, gate, w1, w3, w2, layer)
    hs = _glu(xb, sw1, sw3, layer)
    return _mm_res_ln([hs], sw2, layer, x2, ln_g, ln_b, add=routed)


def kernel(x, mem, positions, ln_g, ln_b, xa_wq, xa_wkv, xa_wo, router_w, router_bias, moe_w1, moe_w3, moe_w2, sh_w1, sh_w3, sh_w2, ev_w_in, ml_conv_w, ml_conv_b, ml_gate_b, ml_gn_w, rw_mu, rw_w0, rw_w2, rw_a0, rw_a2, rw_g2, rw_kk, rw_ka, rw_rk, rw_gn_w, rw_gn_b, ev_w_out, od_w_in, ret_gn_w, ret_gn_b, od_w_out):
    B, S, D = x.shape
    x2 = x.reshape(B * S, D)
    xb = x2.astype(BF16)
    memb = mem.reshape(-1, D).astype(BF16)
    (xa_wq, xa_wkv, xa_wo, sh_w1, sh_w3, sh_w2, ev_w_out, od_w_in, od_w_out) = (
        w.astype(BF16) for w in (xa_wq, xa_wkv, xa_wo, sh_w1, sh_w3, sh_w2, ev_w_out, od_w_in, od_w_out))
    for layer in range(ln_g.shape[0]):
        j = layer // 2
        if layer % 2 == 0:
            x2, xb = _even_mixer(x2, xb, B, S, ln_g[layer, 0], ln_b[layer, 0], ev_w_in[j], ml_conv_w[j],
                                 ml_conv_b[j], ml_gate_b[j], ml_gn_w[j], rw_mu[j], rw_w0[j], rw_w2[j],
                                 rw_a0[j], rw_a2[j], rw_g2[j], rw_kk[j], rw_ka[j], rw_rk[j],
                                 rw_gn_w[j], rw_gn_b[j], ev_w_out, j)
        else:
            x2, xb = _odd_mixer(x2, xb, B, S, positions, ln_g[layer, 0], ln_b[layer, 0], od_w_in,
                                ret_gn_w[j], ret_gn_b[j], od_w_out, j)
        x2, xb = _cross_attention(x2, xb, B, S, memb, ln_g[layer, 1], ln_b[layer, 1], xa_wq, xa_wkv,
                                  xa_wo, layer)
        x2, xb = _moe(x2, xb, ln_g[layer, 2], ln_b[layer, 2], router_w[layer], router_bias[layer],
                      moe_w1, moe_w3, moe_w2, sh_w1, sh_w3, sh_w2, layer)
    return x2.reshape(B, S, D)
```

```python
import functools

import numpy as np
import jax
import jax.numpy as jnp
from jax import lax
from jax.experimental import pallas as pl
from jax.experimental.pallas import tpu as pltpu

F32 = jnp.float32
BF16 = jnp.bfloat16
HI = lax.Precision.HIGHEST

DEPTH = 4
ALPHA = (2.0 * DEPTH) ** 0.25
LN_EPS = 1e-5
NEG_BIG = -1e30
CONV_K = 4
ML_HEADS = 4
RW_HEAD = 64
RW_LORA_W = 64
RW_LORA_A = 64
RW_LORA_G = 128
RW_GN_EPS = 64e-5
RET_HEADS = 8
ROPE_BASE = 10000.0
XA_HEADS = 4
N_EXPERTS = 64
TOP_K = 8
N_GROUPS = 8
TOPK_GROUPS = 4
ROUTED_SCALE = 2.5

ML_CHUNK = 64
RW_CHUNK = 64
RET_CHUNK = 256
MOE_ROWS_PER_TILE = 256

V7X_VMEM_BYTES = 64 * 2 ** 20
VMEM_LIMIT = (V7X_VMEM_BYTES * 3) // 4
LANES = 128
V7X_MXU_DIM = 256


def _params(*sem):
    return pltpu.CompilerParams(dimension_semantics=sem, vmem_limit_bytes=VMEM_LIMIT)


def _bdot(a, b):
    return jnp.dot(a.astype(BF16), b.astype(BF16), preferred_element_type=F32)


def _bdot_nt(a, b):
    return lax.dot_general(a.astype(BF16), b.astype(BF16), (((1,), (1,)), ((), ())),
                           preferred_element_type=F32)


def _bdot_tn(a, b):
    return lax.dot_general(a.astype(BF16), b.astype(BF16), (((0,), (0,)), ((), ())),
                           preferred_element_type=F32)


def _sigmoid(x):
    return 1.0 / (1.0 + jnp.exp(-x))


def _softplus(x):
    return jnp.maximum(x, 0.0) + jnp.log1p(jnp.exp(-jnp.abs(x)))


def _iota2(shape, dim):
    return lax.broadcasted_iota(jnp.int32, shape, dim)


def _mm_kernel(x_ref, w_ref, o_ref):
    o_ref[...] = _bdot(x_ref[...], w_ref[...]).astype(o_ref.dtype)


def _mm(x, w, layer, out_dtype, tm, tn):
    M, K = x.shape
    N = w.shape[2]
    assert M % tm == 0 and N % tn == 0
    return pl.pallas_call(
        _mm_kernel,
        grid=(M // tm, N // tn),
        in_specs=[pl.BlockSpec((tm, K), lambda i, j: (i, 0)),
                  pl.BlockSpec((None, K, tn), lambda i, j: (layer, 0, j))],
        out_specs=pl.BlockSpec((tm, tn), lambda i, j: (i, j)),
        out_shape=jax.ShapeDtypeStruct((M, N), out_dtype),
        compiler_params=_params("parallel", "parallel"),
        name="mm",
    )(x, w)


def _layer_norm_rows(y, g, b):
    mu = jnp.mean(y, axis=-1, keepdims=True)
    d = y - mu
    var = jnp.mean(d * d, axis=-1, keepdims=True)
    return d * lax.rsqrt(var + LN_EPS) * g + b


def _pack_rows(v, ref):
    J = ref.shape[1]
    bits = lax.bitcast_convert_type(v.astype(BF16).astype(F32), jnp.int32)
    for j in range(J):
        lo = lax.shift_right_logical(bits[:, j * LANES:(j + 1) * LANES], 16)
        hi = bits[:, (J + j) * LANES:(J + j + 1) * LANES]
        ref[:, j, :] = hi | lo


def _unpack_rows(ref):
    J = ref.shape[1]
    lo, hi = [], []
    for j in range(J):
        w = ref[:, j, :]
        lo.append(lax.bitcast_convert_type(lax.shift_left(w, 16), F32))
        hi.append(lax.bitcast_convert_type(w & jnp.int32(-65536), F32))
    return jnp.concatenate(lo + hi, axis=-1)


def _mm_res_ln_kernel(*refs, n_a, has_add, packed):
    a_refs, (w_ref, res_ref), rest = refs[:n_a], refs[n_a:n_a + 2], refs[n_a + 2:]
    add_ref = rest[0] if has_add else None
    n_out = 3 if packed else 2
    g_ref, b_ref = rest[-n_out - 2:-n_out]
    o_ref, ob_ref = rest[-n_out:][:2]
    acc, k0 = None, 0
    for a_ref in a_refs:
        part = _bdot(a_ref[...], w_ref[k0:k0 + a_ref.shape[1], :])
        acc = part if acc is None else acc + part
        k0 += a_ref.shape[1]
    if has_add:
        acc = add_ref[...] + acc
    out = _layer_norm_rows(ALPHA * res_ref[...] + acc, g_ref[...], b_ref[...])
    o_ref[...] = out
    ob_ref[...] = out.astype(ob_ref.dtype)
    if packed:
        _pack_rows(out, rest[-1])


def _mm_res_ln(a_parts, w, layer, res, g, b, add=None, packed=False, tm=256):
    M = res.shape[0]
    K, N = w.shape[1:]
    assert sum(a.shape[1] for a in a_parts) == K
    row = lambda i: (i, 0)
    fixed = lambda i: (0, 0)
    in_specs = [pl.BlockSpec((tm, a.shape[1]), row) for a in a_parts]
    in_specs += [pl.BlockSpec((None, K, N), lambda i: (layer, 0, 0)), pl.BlockSpec((tm, N), row)]
    args = [*a_parts, w, res]
    if add is not None:
        in_specs.append(pl.BlockSpec((tm, N), row))
        args.append(add)
    in_specs += [pl.BlockSpec((1, N), fixed), pl.BlockSpec((1, N), fixed)]
    args += [g.reshape(1, N), b.reshape(1, N)]
    out_specs = [pl.BlockSpec((tm, N), row), pl.BlockSpec((tm, N), row)]
    out_shape = [jax.ShapeDtypeStruct((M, N), F32), jax.ShapeDtypeStruct((M, N), BF16)]
    if packed:
        J = N // (2 * LANES)
        out_specs.append(pl.BlockSpec((tm, J, LANES), lambda i: (i, 0, 0)))
        out_shape.append(jax.ShapeDtypeStruct((M, J, LANES), jnp.int32))
    return pl.pallas_call(
        functools.partial(_mm_res_ln_kernel, n_a=len(a_parts), has_add=add is not None, packed=packed),
        grid=(M // tm,),
        in_specs=in_specs,
        out_specs=out_specs,
        out_shape=out_shape,
        compiler_params=_params("parallel"),
        name="mm_res_ln",
    )(*args)


def _glu_kernel(x_ref, w1_ref, w3_ref, o_ref):
    x = x_ref[...]
    h1 = jnp.dot(x, w1_ref[...], preferred_element_type=F32)
    h3 = jnp.dot(x, w3_ref[...], preferred_element_type=F32)
    o_ref[...] = (h1 * _sigmoid(h1) * h3).astype(o_ref.dtype)


def _glu(xb, w1, w3, layer, tm=512):
    M, K = xb.shape
    N = w1.shape[2]
    wspec = pl.BlockSpec((None, K, N), lambda i: (layer, 0, 0))
    return pl.pallas_call(
        _glu_kernel,
        grid=(M // tm,),
        in_specs=[pl.BlockSpec((tm, K), lambda i: (i, 0)), wspec, wspec],
        out_specs=pl.BlockSpec((tm, N), lambda i: (i, 0)),
        out_shape=jax.ShapeDtypeStruct((M, N), BF16),
        compiler_params=_params("parallel"),
        name="glu",
    )(xb, w1, w3)


def _log_sigmoid(x):
    return jnp.minimum(x, 0.0) - jnp.log1p(jnp.exp(-jnp.abs(x)))


def _split_bf16(x, terms):
    pieces = []
    for _ in range(terms):
        p = x.astype(BF16)
        pieces.append(p)
        x = x - p.astype(F32)
    return pieces


def _mlstm_kernel(qk_ref, v_ref, o_ref, gc_ref, gr_ref, cw_ref, cb_ref, gbc_ref, gbr_ref, gn_ref,
                  out_ref, prev_ref, c_ref, n_ref, m_ref, *, heads, dk, dv):
    nb, L, _ = qk_ref.shape
    qw = heads * dk

    @pl.when(pl.program_id(0) == 0)
    def _():
        prev_ref[...] = jnp.zeros_like(prev_ref)
        c_ref[...] = jnp.zeros_like(c_ref)
        n_ref[...] = jnp.zeros_like(n_ref)
        m_ref[...] = jnp.full_like(m_ref, NEG_BIG)

    row = _iota2((L, 1), 0)
    qk = []
    for b in range(nb):
        cur = qk_ref[b].astype(F32)
        prev = prev_ref[b]
        acc = cb_ref[...] + cw_ref[CONV_K - 1:CONV_K, :] * cur
        for j in range(CONV_K - 1):
            s = CONV_K - 1 - j
            shifted = jnp.where(row < s, pltpu.roll(prev, s, 0), pltpu.roll(cur, s, 0))
            acc = acc + cw_ref[j:j + 1, :] * shifted
        prev_ref[b] = cur
        qk.append(acc * _sigmoid(acc))

    causal = _iota2((L, L), 0) >= _iota2((L, L), 1)
    tri = causal.astype(F32).astype(BF16)
    g_col = [gc_ref[b].astype(F32) + gbc_ref[...] for b in range(nb)]
    pieces = _split_bf16(jnp.concatenate([_log_sigmoid(g) for g in g_col], axis=1), 3)
    lanes = nb * g_col[0].shape[1]
    b_col = jnp.dot(tri, jnp.concatenate(pieces, axis=1), preferred_element_type=F32)
    b_col = b_col[:, :lanes] + b_col[:, lanes:2 * lanes] + b_col[:, 2 * lanes:]
    g_row = [gr_ref[b, 0] + gbr_ref[...] for b in range(nb)]
    pieces = _split_bf16(jnp.concatenate([_log_sigmoid(g) for g in g_row], axis=0), 3)
    rows = nb * 2 * heads
    b_row = lax.dot_general(jnp.concatenate(pieces, axis=0), tri, (((1,), (1,)), ((), ())),
                            preferred_element_type=F32)
    b_row = b_row[:rows] + b_row[rows:2 * rows] + b_row[2 * rows:]

    def per_problem(f):
        return jnp.stack([f(b, h) for b in range(nb) for h in range(heads)])

    q = per_problem(lambda b, h: qk[b][:, h * dk:(h + 1) * dk])
    k = per_problem(lambda b, h: qk[b][:, qw + h * dk:qw + (h + 1) * dk]) * dk ** -0.5
    v = per_problem(lambda b, h: v_ref[b, :, h * dv:(h + 1) * dv])
    gcw = g_col[0].shape[1]
    bc = per_problem(lambda b, h: b_col[:, b * gcw + heads + h:b * gcw + heads + h + 1])
    ic = per_problem(lambda b, h: g_col[b][:, h:h + 1])
    br = per_problem(lambda b, h: b_row[b * 2 * heads + heads + h:b * 2 * heads + heads + h + 1])
    ir = per_problem(lambda b, h: g_row[b][h:h + 1])
    b_last = bc[:, L - 1:L]
    c_prev = c_ref[...]
    n_prev = n_ref[...]
    m_prev = m_ref[...][:, :, 0:1]

    def bmm(x, y, spec):
        return jnp.einsum(spec, x.astype(BF16), y.astype(BF16), preferred_element_type=F32)

    log_d = jnp.where(causal, bc - br + ir, -jnp.inf)
    g_inter = bc + m_prev
    m_t = jnp.maximum(jnp.max(log_d, axis=-1, keepdims=True), g_inter)
    p = jnp.exp(log_d - m_t) * bmm(q, k, "gtk,gsk->gts")
    e_inter = jnp.exp(g_inter - m_t)
    num = bmm(p, v, "gts,gsv->gtv") + e_inter * bmm(q, c_prev, "gtk,gkv->gtv")
    den = jnp.sum(p, axis=-1, keepdims=True) + e_inter * jnp.sum(q * n_prev, axis=-1, keepdims=True)
    hh = num / jnp.maximum(jnp.abs(den), jnp.exp(-m_t))

    w_end = b_last - bc + ic
    m_loc = jnp.max(w_end, axis=1, keepdims=True)
    e_end = jnp.exp(w_end - m_loc)
    ke = k * e_end
    m_new = jnp.maximum(b_last + m_prev, m_loc)
    s_prev = jnp.exp(b_last + m_prev - m_new)
    s_new = jnp.exp(m_loc - m_new)
    c_ref[...] = s_prev * c_prev + s_new * bmm(ke, v, "gsk,gsv->gkv")
    n_ref[...] = s_prev * n_prev + s_new * jnp.sum(ke, axis=1, keepdims=True)
    m_ref[...] = jnp.broadcast_to(m_new, m_ref.shape)

    mu = jnp.mean(hh, axis=-1, keepdims=True)
    d = hh - mu
    var = jnp.mean(d * d, axis=-1, keepdims=True)
    y = d * lax.rsqrt(var + 1e-6)
    for b in range(nb):
        for h in range(heads):
            sl = slice(h * dv, (h + 1) * dv)
            gate = _sigmoid(o_ref[b, :, sl].astype(F32))
            out_ref[b, :, sl] = (y[b * heads + h] * gn_ref[:, sl] * gate).astype(out_ref.dtype)


def _mlstm(z3, gates_t, conv_w, conv_b, gate_b, gn_w, *, qk_blk, v_blk, o_blk, gate_blk):
    B, S, _ = z3.shape
    H = ML_HEADS
    W = gn_w.shape[0]
    dv = W // H
    dk = conv_w.shape[1] // (2 * H)
    assert conv_w.shape[1] == W, "q|k block and v block share one column-block width"
    L = ML_CHUNK
    gb_col = jnp.zeros((1, LANES), F32).at[0, :2 * H].set(gate_b.reshape(-1))
    gb_row = gate_b.reshape(2 * H, 1)
    fixed = lambda c: (0, 0)
    return pl.pallas_call(
        functools.partial(_mlstm_kernel, heads=H, dk=dk, dv=dv),
        grid=(S // L,),
        in_specs=[pl.BlockSpec((B, L, W), lambda c: (0, c, qk_blk)),
                  pl.BlockSpec((B, L, W), lambda c: (0, c, v_blk)),
                  pl.BlockSpec((B, L, W), lambda c: (0, c, o_blk)),
                  pl.BlockSpec((B, L, LANES), lambda c: (0, c, gate_blk)),
                  pl.BlockSpec((B, 1, 2 * H, L), lambda c: (0, c, 0, 0)),
                  pl.BlockSpec((CONV_K, W), fixed),
                  pl.BlockSpec((1, W), fixed),
                  pl.BlockSpec((1, LANES), fixed),
                  pl.BlockSpec((2 * H, 1), fixed),
                  pl.BlockSpec((1, W), fixed)],
        out_specs=pl.BlockSpec((B, L, W), lambda c: (0, c, 0)),
        out_shape=jax.ShapeDtypeStruct((B, S, W), BF16),
        scratch_shapes=[pltpu.VMEM((B, L, W), F32),
                        pltpu.VMEM((B * H, dk, dv), F32),
                        pltpu.VMEM((B * H, 1, dk), F32),
                        pltpu.VMEM((B * H, 1, LANES), F32)],
        compiler_params=_params("arbitrary"),
        name="mlstm",
    )(z3, z3, z3, z3, gates_t, conv_w, conv_b.reshape(1, W), gb_col, gb_row, gn_w.reshape(1, W))


def _unit_lower_inverse(a, t_idx, i_idx, block_diag, matmul):
    n = a.shape[-2]
    eye = (t_idx == i_idx).astype(F32)
    t = None
    s = 1
    while s < n:
        pair = (t_idx // (2 * s)) == (i_idx // (2 * s))
        off = pair & ((t_idx // s) % 2 == 1) & ((i_idx // s) % 2 == 0)
        q = jnp.where(off, a, 0.0)
        t = eye - q if t is None else t - matmul(matmul(t, block_diag(q)), block_diag(t))
        s *= 2
    return t


def _rwkv_kernel(r_ref, k_ref, v_ref, l_ref, mur_ref, muk_ref, muv_ref, mul_ref, w0_ref, w2_ref,
                 a0_ref, a2_ref, g2_ref, kk_ref, ka_ref, rk_ref, gnw_ref, gnb_ref, out_ref,
                 pr_ref, pk_ref, pv_ref, plo_ref, s_ref, *, hd, gw):
    nb, L, W = r_ref.shape
    ng = W // gw
    hpg = gw // hd
    assert hpg * L == gw

    @pl.when(pl.program_id(1) == 0)
    def _():
        pr_ref[...] = jnp.zeros_like(pr_ref)
        pk_ref[...] = jnp.zeros_like(pk_ref)
        pv_ref[...] = jnp.zeros_like(pv_ref)
        plo_ref[...] = jnp.zeros_like(plo_ref)
        s_ref[...] = jnp.zeros_like(s_ref)

    R = nb * L
    first = _iota2((L, 1), 0) == 0
    same_head = (_iota2((gw, gw), 0) // hd) == (_iota2((gw, gw), 1) // hd)
    zero = jnp.zeros((), BF16)
    same_head_b = same_head.astype(F32).astype(BF16)
    row, col = _iota2((R, R), 0), _iota2((R, R), 1)
    tri = ((row // L == col // L) & (row >= col)).astype(F32).astype(BF16)
    t_idx = _iota2((L, gw), 0)
    i_idx = _iota2((L, gw), 1) % L
    strict = t_idx > i_idx
    lower = t_idx >= i_idx

    def shift_lerp(x_ref, p_ref, mu_ref):
        parts = []
        for b in range(nb):
            cur = x_ref[b].astype(F32)
            shifted = jnp.where(first, p_ref[b], pltpu.roll(cur, 1, 0))
            p_ref[b] = cur[L - 1:L, :]
            parts.append(cur + mu_ref[...] * (shifted - cur))
        return jnp.concatenate(parts, axis=0)

    def head_sums(xs):
        stacked = jnp.concatenate([x[:, i * gw:(i + 1) * gw] for x in xs for i in range(ng)], axis=0)
        n = stacked.shape[0]
        s = jnp.dot(jnp.concatenate(_split_bf16(stacked, 2), axis=0), same_head_b, preferred_element_type=F32)
        s = s[:n] + s[n:]
        return [jnp.concatenate([s[(j * ng + i) * R:(j * ng + i + 1) * R] for i in range(ng)], axis=1)
                for j in range(len(xs))]

    r = shift_lerp(r_ref, pr_ref, mur_ref)
    k = shift_lerp(k_ref, pk_ref, muk_ref)
    v = shift_lerp(v_ref, pv_ref, muv_ref)
    lo = shift_lerp(l_ref, plo_ref, mul_ref)
    xw = lo[:, :RW_LORA_W]
    xa = lo[:, RW_LORA_W:RW_LORA_W + RW_LORA_A]
    xg = lo[:, RW_LORA_W + RW_LORA_A:]

    w_log = -jnp.exp(-_softplus(-(w0_ref[...] + _bdot(jnp.tanh(xw), w2_ref[...]))) - 0.5)
    a = _sigmoid(a0_ref[...] + _bdot(xa, a2_ref[...]))
    g = _bdot(_sigmoid(xg), g2_ref[...])

    kk = k * kk_ref[...]
    k = k * (1.0 + (a - 1.0) * ka_ref[...])
    kk_sq, rk_sum = head_sums([kk * kk, r * k * rk_ref[...]])
    kk = kk * lax.rsqrt(jnp.maximum(kk_sq, 1e-24))
    bv = kk * a

    c3 = jnp.dot(tri, jnp.concatenate(_split_bf16(w_log, 3), axis=1), preferred_element_type=F32)
    c = c3[:, :W] + c3[:, W:2 * W] + c3[:, 2 * W:]
    c_last = jnp.concatenate([jnp.broadcast_to(c[(b + 1) * L - 1:(b + 1) * L], (L, W)) for b in range(nb)],
                             axis=0)
    e_end = jnp.exp(c_last - c)
    e_neg = jnp.exp(-c)
    kt = (kk * jnp.exp(c - w_log)).astype(BF16)
    rt = (r * jnp.exp(c)).astype(BF16)
    kh = (k * e_neg).astype(BF16)
    bh = (bv * e_neg).astype(BF16)
    kp = (k * e_end).astype(BF16)
    bp = (bv * e_end).astype(BF16)
    vb = v.astype(BF16)
    p_last = jnp.exp(c_last)

    def groups(x):
        return jnp.stack([x[b * L:(b + 1) * L, i * gw:(i + 1) * gw] for b in range(nb) for i in range(ng)])

    def block_diag(x):
        return jnp.where(same_head, jnp.concatenate([x.astype(BF16)] * hpg, axis=1), zero)

    def bmm(x, y):
        return jnp.einsum("gmk,gkn->gmn", x.astype(BF16), y.astype(BF16), preferred_element_type=F32)

    def bmm_nt(x, y):
        return jnp.einsum("gmk,gnk->gmn", x.astype(BF16), y.astype(BF16), preferred_element_type=F32)

    lhs = jnp.concatenate([groups(kt), groups(rt)], axis=1)
    m_k = bmm_nt(lhs, block_diag(groups(kh)))
    m_b = bmm_nt(lhs, block_diag(groups(bh)))
    a_kk = jnp.where(strict, m_k[:, :L], 0.0)
    a_kb = jnp.where(strict, m_b[:, :L], 0.0)
    a_rk = jnp.where(lower, m_k[:, L:], 0.0)
    a_rb = jnp.where(lower, m_b[:, L:], 0.0)
    t_inv = _unit_lower_inverse(a_kb, t_idx, i_idx, block_diag, bmm)
    st = s_ref[...]
    gs = bmm_nt(lhs, st)
    vg = groups(vb)
    v_bd = block_diag(vg)
    u = bmm(t_inv, block_diag(gs[:, :L] + bmm(a_kk, v_bd)))
    yg = gs[:, L:] + bmm(jnp.concatenate([a_rk, -a_rb], axis=2), jnp.concatenate([v_bd, block_diag(u)], axis=1))
    vu = jnp.concatenate([vg, (-u).astype(BF16)], axis=1)
    kb = jnp.concatenate([groups(kp), groups(bp)], axis=1)
    pg = groups(p_last)[:, :1]
    outer = jnp.einsum("gtv,gtk->gvk", vu, kb, preferred_element_type=F32)
    s_ref[...] = st * pg + jnp.where(same_head, outer, 0.0)
    ys = [jnp.concatenate([yg[b * ng + i] for i in range(ng)], axis=1) for b in range(nb)]

    y = jnp.concatenate(ys, axis=0)
    mu = head_sums([y])[0] * (1.0 / hd)
    d = y - mu
    var = head_sums([d * d])[0] * (1.0 / hd)
    yn = d * lax.rsqrt(var + RW_GN_EPS) * gnw_ref[...] + gnb_ref[...]
    out = ((yn + rk_sum * v) * g).astype(out_ref.dtype)
    for b in range(nb):
        out_ref[b] = out[b * L:(b + 1) * L]


def _rwkv(z3, mu, w0, w2, a0, a2, g2, k_k, k_a, r_k, gn_w, gn_b, *, col0, lora_col0, width):
    B, S, _ = z3.shape
    RW = w0.shape[0]
    W = width
    G = RW // W
    L = RW_CHUNK
    LO = RW_LORA_W + RW_LORA_A + RW_LORA_G
    assert col0 % W == 0 and RW % W == 0 and lora_col0 % LO == 0 and W % V7X_MXU_DIM == 0
    zc = lambda off: (lambda g, c: (0, c, off // W + g))
    vec = lambda g, c: (0, g)
    row = lambda x: x.reshape(1, -1)
    return pl.pallas_call(
        functools.partial(_rwkv_kernel, hd=RW_HEAD, gw=V7X_MXU_DIM),
        grid=(G, S // L),
        in_specs=[pl.BlockSpec((B, L, W), zc(col0)),
                  pl.BlockSpec((B, L, W), zc(col0 + RW)),
                  pl.BlockSpec((B, L, W), zc(col0 + 2 * RW)),
                  pl.BlockSpec((B, L, LO), lambda g, c: (0, c, lora_col0 // LO)),
                  pl.BlockSpec((1, W), vec), pl.BlockSpec((1, W), vec), pl.BlockSpec((1, W), vec),
                  pl.BlockSpec((1, LO), lambda g, c: (0, 0)),
                  pl.BlockSpec((1, W), vec),
                  pl.BlockSpec((RW_LORA_W, W), vec),
                  pl.BlockSpec((1, W), vec),
                  pl.BlockSpec((RW_LORA_A, W), vec),
                  pl.BlockSpec((RW_LORA_G, W), vec),
                  pl.BlockSpec((1, W), vec), pl.BlockSpec((1, W), vec), pl.BlockSpec((1, W), vec),
                  pl.BlockSpec((1, W), vec), pl.BlockSpec((1, W), vec)],
        out_specs=pl.BlockSpec((B, L, W), lambda g, c: (0, c, g)),
        out_shape=jax.ShapeDtypeStruct((B, S, RW), BF16),
        scratch_shapes=[pltpu.VMEM((B, 1, W), F32), pltpu.VMEM((B, 1, W), F32), pltpu.VMEM((B, 1, W), F32),
                        pltpu.VMEM((B, 1, LO), F32),
                        pltpu.VMEM((B * (W // V7X_MXU_DIM), V7X_MXU_DIM, V7X_MXU_DIM), F32)],
        compiler_params=_params("parallel", "arbitrary"),
        name="rwkv7",
    )(z3, z3, z3, z3, row(mu[:RW]), row(mu[RW:2 * RW]), row(mu[2 * RW:3 * RW]), row(mu[3 * RW:]),
      row(w0), w2.astype(BF16), row(a0), a2.astype(BF16), g2.astype(BF16), row(k_k), row(k_a),
      row(r_k), row(gn_w), row(gn_b))


def _retention_kernel(q_ref, k_ref, v_ref, g_ref, pos_ref, inv_ref, dec_ref, gnw_ref, gnb_ref,
                      out_ref, r_ref, *, heads, log_gamma):
    L = q_ref.shape[1]
    dk = q_ref.shape[2] // heads
    dv = v_ref.shape[2] // heads
    half = dk // 2

    @pl.when(pl.program_id(1) == 0)
    def _():
        r_ref[...] = jnp.zeros_like(r_ref)

    ang = pos_ref[0] * inv_ref[...]
    cos = jnp.cos(ang)
    sin = jnp.sin(ang)
    t_in = _iota2((L, 1), 0).astype(F32)

    def rot(t):
        t1, t2 = t[:, :half], t[:, half:]
        return jnp.concatenate([t1 * cos - t2 * sin, t1 * sin + t2 * cos], axis=-1)

    for h in range(heads):
        lg = log_gamma[h]
        q = rot(q_ref[0, :, h * dk:(h + 1) * dk].astype(F32))
        k = rot(k_ref[0, :, h * dk:(h + 1) * dk].astype(F32)) * dk ** -0.5
        v = v_ref[0, :, h * dv:(h + 1) * dv]
        state = r_ref[h]
        inter = _bdot(q, state) * jnp.exp(lg * (t_in + 1.0))
        intra = _bdot(_bdot_nt(q, k) * dec_ref[h], v)
        r_ref[h] = state * float(np.exp(lg * L)) + _bdot_tn(k * jnp.exp(lg * (L - 1.0 - t_in)), v)
        y = intra + inter
        mu = jnp.mean(y, axis=-1, keepdims=True)
        d = y - mu
        var = jnp.mean(d * d, axis=-1, keepdims=True)
        yn = d * lax.rsqrt(var + 1e-6) * gnw_ref[:, h * dv:(h + 1) * dv] + gnb_ref[:, h * dv:(h + 1) * dv]
        gate = g_ref[0, :, h * dv:(h + 1) * dv].astype(F32)
        out_ref[0, :, h * dv:(h + 1) * dv] = (gate * _sigmoid(gate) * yn).astype(out_ref.dtype)


def _retention(z3, positions, gn_w, gn_b):
    B, S, _ = z3.shape
    W = gn_w.shape[0]
    H = RET_HEADS
    dk = W // H
    half = dk // 2
    L = RET_CHUNK
    log_gamma = np.log1p(-np.exp2(-5.0 - np.arange(H, dtype=np.float64)))
    rel = np.arange(L)[:, None] - np.arange(L)[None, :]
    decay = np.where(rel >= 0, np.exp(log_gamma[:, None, None] * np.maximum(rel, 0)), 0.0).astype(np.float32)
    inv = (ROPE_BASE ** (-jnp.arange(half, dtype=F32) / half)).reshape(1, half)
    pos = positions.astype(F32).reshape(B, S, 1)
    blk = lambda j: pl.BlockSpec((1, L, W), lambda b, c: (b, c, j))
    fixed2 = lambda b, c: (0, 0)
    return pl.pallas_call(
        functools.partial(_retention_kernel, heads=H, log_gamma=tuple(float(x) for x in log_gamma)),
        grid=(B, S // L),
        in_specs=[blk(0), blk(1), blk(2), blk(3),
                  pl.BlockSpec((1, L, 1), lambda b, c: (b, c, 0)),
                  pl.BlockSpec((1, half), fixed2),
                  pl.BlockSpec((H, L, L), lambda b, c: (0, 0, 0)),
                  pl.BlockSpec((1, W), fixed2), pl.BlockSpec((1, W), fixed2)],
        out_specs=pl.BlockSpec((1, L, W), lambda b, c: (b, c, 0)),
        out_shape=jax.ShapeDtypeStruct((B, S, W), BF16),
        scratch_shapes=[pltpu.VMEM((H, dk, W // H), F32)],
        compiler_params=_params("parallel", "arbitrary"),
        name="retention",
    )(z3, z3, z3, z3, pos, inv, jnp.asarray(decay), gn_w.reshape(1, W), gn_b.reshape(1, W))


def _xattn_kernel(q_ref, k_ref, v_ref, o_ref, *, heads):
    hd = q_ref.shape[2] // heads
    q, k, v = (jnp.stack([r[0, :, h * hd:(h + 1) * hd] for h in range(heads)]) for r in (q_ref, k_ref, v_ref))
    s = jnp.einsum("hqd,hmd->hqm", q, k, preferred_element_type=F32) * hd ** -0.5
    e = jnp.exp(s - jnp.max(s, axis=-1, keepdims=True))
    p = e / jnp.sum(e, axis=-1, keepdims=True)
    o = jnp.einsum("hqm,hmd->hqd", p.astype(BF16), v, preferred_element_type=F32)
    for h in range(heads):
        o_ref[0, :, h * hd:(h + 1) * hd] = o[h].astype(o_ref.dtype)


def _xattn(q3, kv3, tq=512):
    B, S, D = q3.shape
    M = kv3.shape[1]
    return pl.pallas_call(
        functools.partial(_xattn_kernel, heads=XA_HEADS),
        grid=(B, S // tq),
        in_specs=[pl.BlockSpec((1, tq, D), lambda b, i: (b, i, 0)),
                  pl.BlockSpec((1, M, D), lambda b, i: (b, 0, 0)),
                  pl.BlockSpec((1, M, D), lambda b, i: (b, 0, 1))],
        out_specs=pl.BlockSpec((1, tq, D), lambda b, i: (b, i, 0)),
        out_shape=jax.ShapeDtypeStruct((B, S, D), BF16),
        compiler_params=_params("parallel", "parallel"),
        name="xattn",
    )(q3, kv3, kv3)


def _first_argmax(vals, index, big):
    m = jnp.max(vals, axis=0, keepdims=True)
    idx = jnp.min(jnp.where(vals == m, index, big), axis=0, keepdims=True)
    return m, idx


def _router_kernel(x_ref, wt_ref, bias_ref, idx_ref, gate_ref, pos_ref, cnt_ref, carry_ref):
    E = wt_ref.shape[0]
    tm = x_ref.shape[0]
    per = E // N_GROUPS
    neg = -jnp.inf
    logits = lax.dot_general(wt_ref[...], x_ref[...], (((1,), (1,)), ((), ())), precision=HI,
                             preferred_element_type=F32)
    scores = _sigmoid(logits)
    biased = scores + bias_ref[...]
    eidx = _iota2((E, tm), 0)

    within = _iota2((per, tm), 0)
    group_scores = []
    for g in range(N_GROUPS):
        vals = biased[g * per:(g + 1) * per]
        m1, i1 = _first_argmax(vals, within, per)
        m2 = jnp.max(jnp.where(within == i1, neg, vals), axis=0, keepdims=True)
        group_scores.append(m1 + m2)
    gs = jnp.concatenate(group_scores, axis=0)

    gidx = _iota2((N_GROUPS, tm), 0)
    keep = jnp.zeros((N_GROUPS, tm), F32)
    for _ in range(TOPK_GROUPS):
        _, gi = _first_argmax(gs, gidx, N_GROUPS)
        hit = gidx == gi
        keep = jnp.where(hit, 1.0, keep)
        gs = jnp.where(hit, neg, gs)
    keep_e = jnp.concatenate([jnp.broadcast_to(keep[g:g + 1], (per, tm)) for g in range(N_GROUPS)], axis=0)

    cand = jnp.where(keep_e > 0.0, biased, neg)
    picks, affs = [], []
    for _ in range(TOP_K):
        _, ei = _first_argmax(cand, eidx, E)
        hit = eidx == ei
        picks.append(ei)
        affs.append(jnp.sum(jnp.where(hit, scores, 0.0), axis=0, keepdims=True))
        cand = jnp.where(hit, neg, cand)
    aff = jnp.concatenate(affs, axis=0)
    idx_ref[...] = jnp.concatenate(picks, axis=0)
    gate_ref[...] = ROUTED_SCALE * aff / jnp.sum(aff, axis=0, keepdims=True)

    @pl.when(pl.program_id(0) == 0)
    def _():
        carry_ref[...] = jnp.zeros_like(carry_ref)

    sel = jnp.where(cand == neg, jnp.where(keep_e > 0.0, 1.0, 0.0), 0.0)
    upper = (_iota2((tm, tm), 0) < _iota2((tm, tm), 1)).astype(F32).astype(BF16)
    before = carry_ref[:, :1] + jnp.dot(sel.astype(BF16), upper, preferred_element_type=F32)
    pos_ref[...] = jnp.concatenate(
        [jnp.sum(jnp.where(eidx == ei, before, 0.0), axis=0, keepdims=True) for ei in picks], axis=0
    ).astype(jnp.int32)
    total = carry_ref[:, :1] + jnp.sum(sel, axis=1, keepdims=True)
    carry_ref[...] = jnp.broadcast_to(total, carry_ref.shape)
    cnt_ref[...] = jnp.broadcast_to(total, cnt_ref.shape).astype(jnp.int32)


def _router(x, router_w, router_bias, tm=512):
    T, D = x.shape
    E = router_w.shape[1]
    idx, gate, pos, cnt = pl.pallas_call(
        _router_kernel,
        grid=(T // tm,),
        in_specs=[pl.BlockSpec((tm, D), lambda i: (i, 0)),
                  pl.BlockSpec((E, D), lambda i: (0, 0)),
                  pl.BlockSpec((E, 1), lambda i: (0, 0))],
        out_specs=[pl.BlockSpec((TOP_K, tm), lambda i: (0, i)), pl.BlockSpec((TOP_K, tm), lambda i: (0, i)),
                   pl.BlockSpec((TOP_K, tm), lambda i: (0, i)), pl.BlockSpec((E, LANES), lambda i: (0, 0))],
        out_shape=[jax.ShapeDtypeStruct((TOP_K, T), jnp.int32), jax.ShapeDtypeStruct((TOP_K, T), F32),
                   jax.ShapeDtypeStruct((TOP_K, T), jnp.int32), jax.ShapeDtypeStruct((E, LANES), jnp.int32)],
        scratch_shapes=[pltpu.VMEM((E, LANES), F32)],
        compiler_params=_params("arbitrary"),
        name="router",
    )(x, router_w.T, router_bias.reshape(E, 1))
    return idx.T, gate.T, pos.T, cnt[:, 0]


def _dispatch_plan(top_e, pos, counts, rows_per_tile):
    T, K = top_e.shape
    E = counts.shape[0]
    n_tiles = (T * K) // rows_per_tile + E
    tiles = (counts + rows_per_tile - 1) // rows_per_tile
    tile_end = jnp.cumsum(tiles)
    n_used = tile_end[-1]
    row_start = (tile_end - tiles) * rows_per_tile
    dest = jnp.take(row_start, top_e) + pos
    tile_id = jnp.minimum(jnp.arange(n_tiles, dtype=jnp.int32), n_used - 1)
    tile_expert = jnp.minimum(jnp.searchsorted(tile_end, tile_id, side="right"), E - 1)
    return (tile_expert.astype(jnp.int32), tile_end.astype(jnp.int32), n_used.reshape(1).astype(jnp.int32),
            dest.astype(jnp.int32))


def _dispatch_kernel(tend_ref, dest_ref, x_ref, xs_hbm, zero_ref, zsem, sem, *, slots):
    tm = x_ref.shape[0]
    rows = zero_ref.shape[0]
    n_experts = tend_ref.shape[0]

    @pl.when(pl.program_id(0) == 0)
    def _():
        zero_ref[...] = jnp.zeros_like(zero_ref)

        def last_tile(e):
            end = tend_ref[e]
            start = tend_ref[jnp.maximum(e - 1, 0)]
            start = jnp.where(e == 0, 0, start)
            return end > start, pltpu.make_async_copy(
                zero_ref, xs_hbm.at[pl.ds(jnp.maximum(end - 1, 0) * rows, rows)], zsem)

        def start(e, carry):
            owns, copy = last_tile(e)

            @pl.when(owns)
            def _():
                copy.start()
            return carry

        def wait(e, carry):
            owns, copy = last_tile(e)

            @pl.when(owns)
            def _():
                copy.wait()
            return carry

        def unused_tile(n):
            return pltpu.make_async_copy(zero_ref, xs_hbm.at[pl.ds(n * rows, rows)], zsem)

        def start_unused(n, carry):
            unused_tile(n).start()
            return carry

        def wait_unused(n, carry):
            unused_tile(n).wait()
            return carry

        n_used, n_tiles = tend_ref[n_experts - 1], xs_hbm.shape[0] // rows
        lax.fori_loop(0, n_experts, start, 0)
        lax.fori_loop(n_used, n_tiles, start_unused, 0)
        lax.fori_loop(0, n_experts, wait, 0)
        lax.fori_loop(n_used, n_tiles, wait_unused, 0)

    for k in range(slots):
        def body(t, carry):
            pltpu.make_async_copy(x_ref.at[t], xs_hbm.at[dest_ref[0, k * tm + t]], sem).start()
            return carry
        lax.fori_loop(0, tm, body, 0, unroll=8)
    for k in range(slots):
        pltpu.make_async_copy(x_ref, xs_hbm.at[pl.ds(0, tm)], sem).wait()


def _dispatch(x3p, dest, tile_end, n_tiles, rows_per_tile, tm=256):
    T, J, _ = x3p.shape
    K = dest.shape[1]
    steps = T // tm
    d3 = jnp.swapaxes(dest.reshape(steps, tm, K), 1, 2).reshape(steps, 1, K * tm)
    return pl.pallas_call(
        functools.partial(_dispatch_kernel, slots=K),
        grid_spec=pltpu.PrefetchScalarGridSpec(
            num_scalar_prefetch=1,
            grid=(steps,),
            in_specs=[pl.BlockSpec((None, 1, K * tm), lambda i, te: (i, 0, 0), memory_space=pltpu.SMEM),
                      pl.BlockSpec((tm, J, LANES), lambda i, te: (i, 0, 0))],
            out_specs=pl.BlockSpec(memory_space=pl.ANY),
            scratch_shapes=[pltpu.VMEM((rows_per_tile, J, LANES), jnp.int32),
                            pltpu.SemaphoreType.DMA(()), pltpu.SemaphoreType.DMA(())]),
        out_shape=jax.ShapeDtypeStruct((n_tiles * rows_per_tile, J, LANES), jnp.int32),
        compiler_params=_params("arbitrary"),
        name="dispatch",
    )(tile_end, d3, x3p)


def _sparse_experts_kernel(te_ref, nu_ref, xs_ref, w1_ref, w3_ref, w2_ref, y_ref, wb1, wb3, wb2):
    n = pl.program_id(0)

    @pl.when(n < nu_ref[0])
    def _():
        @pl.when((n == 0) | (te_ref[n] != te_ref[jnp.maximum(n - 1, 0)]))
        def _():
            wb1[...] = w1_ref[0].astype(BF16)
            wb3[...] = w3_ref[0].astype(BF16)
            wb2[...] = w2_ref[0].astype(BF16)

        x = _unpack_rows(xs_ref).astype(BF16)
        h1 = jnp.dot(x, wb1[...], preferred_element_type=F32)
        h3 = jnp.dot(x, wb3[...], preferred_element_type=F32)
        hidden = (h1 * _sigmoid(h1) * h3).astype(BF16)
        _pack_rows(jnp.dot(hidden, wb2[...], preferred_element_type=F32), y_ref)

    @pl.when(n >= nu_ref[0])
    def _():
        y_ref[...] = jnp.zeros_like(y_ref)


def _sparse_experts(xs, tile_expert, n_used, w1, w3, w2, layer, rows):
    _, J, _ = xs.shape
    _, _, D, Hd = w1.shape
    n_tiles = tile_expert.shape[0]
    wmap = lambda n, te, nu: (layer, te[n], 0, 0)
    return pl.pallas_call(
        _sparse_experts_kernel,
        grid_spec=pltpu.PrefetchScalarGridSpec(
            num_scalar_prefetch=2,
            grid=(n_tiles,),
            in_specs=[pl.BlockSpec((rows, J, LANES), lambda n, te, nu: (jnp.minimum(n, nu[0] - 1), 0, 0)),
                      pl.BlockSpec((None, 1, D, Hd), wmap),
                      pl.BlockSpec((None, 1, D, Hd), wmap),
                      pl.BlockSpec((None, 1, Hd, D), wmap)],
            out_specs=pl.BlockSpec((rows, J, LANES), lambda n, te, nu: (n, 0, 0)),
            scratch_shapes=[pltpu.VMEM((D, Hd), BF16), pltpu.VMEM((D, Hd), BF16), pltpu.VMEM((Hd, D), BF16)]),
        out_shape=jax.ShapeDtypeStruct(xs.shape, jnp.int32),
        compiler_params=_params("arbitrary"),
        name="sparse_experts",
    )(tile_expert, n_used, xs, w1, w3, w2)


def _combine_kernel(dest_ref, dest_next_ref, gate_ref, y_hbm, o_ref, buf, sem):
    i = pl.program_id(0)
    steps = pl.num_programs(0)
    K, tm = buf.shape[1:3]
    slot = i % 2

    def fetch(d_ref, s):
        for k in range(K):
            def body(t, carry):
                pltpu.make_async_copy(y_hbm.at[d_ref[0, k * tm + t]], buf.at[s, k, t], sem.at[s]).start()
                return carry
            lax.fori_loop(0, tm, body, 0, unroll=8)

    @pl.when(i == 0)
    def _():
        fetch(dest_ref, 0)

    @pl.when(i + 1 < steps)
    def _():
        fetch(dest_next_ref, 1 - slot)

    for k in range(K):
        pltpu.make_async_copy(y_hbm.at[pl.ds(0, tm)], buf.at[slot, k], sem.at[slot]).wait()
    gate = gate_ref[...]
    acc = jnp.zeros(o_ref.shape, F32)
    for k in range(K):
        acc = acc + gate[:, k:k + 1] * _unpack_rows(buf.at[slot, k])
    o_ref[...] = acc


def _combine(y, dest, gate, tm=128):
    T, K = dest.shape
    _, J, _ = y.shape
    D = 2 * J * LANES
    steps = T // tm
    d3 = jnp.swapaxes(dest.reshape(steps, tm, K), 1, 2).reshape(steps, 1, K * tm)
    return pl.pallas_call(
        _combine_kernel,
        grid=(steps,),
        in_specs=[pl.BlockSpec((None, 1, K * tm), lambda i: (i, 0, 0), memory_space=pltpu.SMEM),
                  pl.BlockSpec((None, 1, K * tm), lambda i: (jnp.minimum(i + 1, steps - 1), 0, 0),
                               memory_space=pltpu.SMEM),
                  pl.BlockSpec((tm, K), lambda i: (i, 0)),
                  pl.BlockSpec(memory_space=pl.ANY)],
        out_specs=pl.BlockSpec((tm, D), lambda i: (i, 0)),
        out_shape=jax.ShapeDtypeStruct((T, D), F32),
        scratch_shapes=[pltpu.VMEM((2, K, tm, J, LANES), jnp.int32), pltpu.SemaphoreType.DMA((2,))],
        compiler_params=_params("arbitrary"),
        name="combine",
    )(d3, d3, gate, y)


def _even_mixer(x2, xb, B, S, ln_g, ln_b, w_in, conv_w, conv_b, gate_b, ml_gn_w, rw_mu, rw_w0, rw_w2,
                rw_a0, rw_a2, rw_g2, rw_kk, rw_ka, rw_rk, rw_gn_w, rw_gn_b, w_out, j):
    D = x2.shape[1]
    H = ML_HEADS
    ML = ml_gn_w.shape[0]
    RW = rw_w0.shape[0]
    LO = RW_LORA_W + RW_LORA_A + RW_LORA_G
    ml_main = 3 * ML
    ml_cols = ml_main + 2 * H
    pad_to = 512
    used = ml_main + 3 * RW + LO + LANES
    total = -(-used // pad_to) * pad_to
    w_perm = jnp.concatenate([
        w_in[:, :ml_main], w_in[:, ml_cols:ml_cols + 3 * RW + LO], w_in[:, ml_main:ml_cols],
        jnp.zeros((D, total - used + LANES - 2 * H), w_in.dtype)], axis=1).astype(BF16)
    z3 = _mm(xb, w_perm[None], 0, BF16, 1024, pad_to).reshape(B, S, total)
    gate_col0 = ml_main + 3 * RW + LO
    L = ML_CHUNK
    gates_t = jnp.swapaxes(
        z3[:, :, gate_col0:gate_col0 + 2 * H].astype(F32).reshape(B, S // L, L, 2 * H), 2, 3)
    h_ml = _mlstm(z3, gates_t, conv_w, conv_b, gate_b, ml_gn_w,
                  qk_blk=0, v_blk=1, o_blk=2, gate_blk=gate_col0 // LANES)
    h_rw = _rwkv(z3, rw_mu, rw_w0, rw_w2, rw_a0, rw_a2, rw_g2, rw_kk, rw_ka, rw_rk, rw_gn_w, rw_gn_b,
                 col0=ml_main, lora_col0=ml_main + 3 * RW, width=RW)
    return _mm_res_ln([h_ml.reshape(B * S, ML), h_rw.reshape(B * S, RW)], w_out, j, x2, ln_g, ln_b)


def _odd_mixer(x2, xb, B, S, positions, ln_g, ln_b, w_in, gn_w, gn_b, w_out, j):
    z3 = _mm(xb, w_in, j, BF16, 1024, 512).reshape(B, S, w_in.shape[2])
    h = _retention(z3, positions, gn_w, gn_b)
    return _mm_res_ln([h.reshape(B * S, -1)], w_out, j, x2, ln_g, ln_b)


def _cross_attention(x2, xb, B, S, memb, ln_g, ln_b, wq, wkv, wo, layer):
    D = x2.shape[1]
    q = _mm(xb, wq, layer, BF16, 1024, 512)
    kv = _mm(memb, wkv, layer, BF16, memb.shape[0], 512)
    o = _xattn(q.reshape(B, S, D), kv.reshape(B, -1, 2 * D))
    return _mm_res_ln([o.reshape(B * S, D)], wo, layer, x2, ln_g, ln_b, packed=True)


def _moe(x2, xb, x3p, ln_g, ln_b, router_w, router_bias, w1, w3, w2, sw1, sw3, sw2, layer):
    rows = MOE_ROWS_PER_TILE
    top_e, gate, pos, counts = _router(x2, router_w, router_bias)
    tile_expert, tile_end, n_used, dest = _dispatch_plan(top_e, pos, counts, rows)
    xs = _dispatch(x3p, dest, tile_end, tile_expert.shape[0], rows)
    routed = _combine(_sparse_experts(xs, tile_expert, n_used, w1, w3, w2, layer, rows), dest, gate)
    hs = _glu(xb, sw1, sw3, layer)
    return _mm_res_ln([hs], sw2, layer, x2, ln_g, ln_b, add=routed)


def kernel(x, mem, positions, ln_g, ln_b, xa_wq, xa_wkv, xa_wo, router_w, router_bias, moe_w1, moe_w3, moe_w2, sh_w1, sh_w3, sh_w2, ev_w_in, ml_conv_w, ml_conv_b, ml_gate_b, ml_gn_w, rw_mu, rw_w0, rw_w2, rw_a0, rw_a2, rw_g2, rw_kk, rw_ka, rw_rk, rw_gn_w, rw_gn_b, ev_w_out, od_w_in, ret_gn_w, ret_gn_b, od_w_out):
    B, S, D = x.shape
    x2 = x.reshape(B * S, D)
    xb = x2.astype(BF16)
    memb = mem.reshape(-1, D).astype(BF16)
    (xa_wq, xa_wkv, xa_wo, sh_w1, sh_w3, sh_w2, ev_w_out, od_w_in, od_w_out) = (
        w.astype(BF16) for w in (xa_wq, xa_wkv, xa_wo, sh_w1, sh_w3, sh_w2, ev_w_out, od_w_in, od_w_out))
    for layer in range(ln_g.shape[0]):
        j = layer // 2
        if layer % 2 == 0:
            x2, xb = _even_mixer(x2, xb, B, S, ln_g[layer, 0], ln_b[layer, 0], ev_w_in[j], ml_conv_w[j],
                                 ml_conv_b[j], ml_gate_b[j], ml_gn_w[j], rw_mu[j], rw_w0[j], rw_w2[j],
                                 rw_a0[j], rw_a2[j], rw_g2[j], rw_kk[j], rw_ka[j], rw_rk[j],
                                 rw_gn_w[j], rw_gn_b[j], ev_w_out, j)
        else:
            x2, xb = _odd_mixer(x2, xb, B, S, positions, ln_g[layer, 0], ln_b[layer, 0], od_w_in,
                                ret_gn_w[j], ret_gn_b[j], od_w_out, j)
        x2, xb, x3p = _cross_attention(x2, xb, B, S, memb, ln_g[layer, 1], ln_b[layer, 1], xa_wq, xa_wkv,
                                       xa_wo, layer)
        x2, xb = _moe(x2, xb, x3p, ln_g[layer, 2], ln_b[layer, 2], router_w[layer], router_bias[layer],
                      moe_w1, moe_w3, moe_w2, sh_w1, sh_w3, sh_w2, layer)
    return x2.reshape(B, S, D)
```

```python
import functools

import numpy as np
import jax
import jax.numpy as jnp
from jax import lax
from jax.experimental import pallas as pl
from jax.experimental.pallas import tpu as pltpu

F32 = jnp.float32
BF16 = jnp.bfloat16
HI = lax.Precision.HIGHEST

DEPTH = 4
ALPHA = (2.0 * DEPTH) ** 0.25
LN_EPS = 1e-5
NEG_BIG = -1e30
CONV_K = 4
ML_HEADS = 4
RW_HEAD = 64
RW_LORA_W = 64
RW_LORA_A = 64
RW_LORA_G = 128
RW_GN_EPS = 64e-5
RET_HEADS = 8
ROPE_BASE = 10000.0
XA_HEADS = 4
N_EXPERTS = 64
TOP_K = 8
N_GROUPS = 8
TOPK_GROUPS = 4
ROUTED_SCALE = 2.5

ML_CHUNK = 64
RW_CHUNK = 64
RET_CHUNK = 256
MOE_ROWS_PER_TILE = 256

V7X_VMEM_BYTES = 64 * 2 ** 20
VMEM_LIMIT = (V7X_VMEM_BYTES * 3) // 4
LANES = 128
V7X_MXU_DIM = 256


def _params(*sem):
    return pltpu.CompilerParams(dimension_semantics=sem, vmem_limit_bytes=VMEM_LIMIT)


def _bdot(a, b):
    return jnp.dot(a.astype(BF16), b.astype(BF16), preferred_element_type=F32)


def _bdot_nt(a, b):
    return lax.dot_general(a.astype(BF16), b.astype(BF16), (((1,), (1,)), ((), ())),
                           preferred_element_type=F32)


def _bdot_tn(a, b):
    return lax.dot_general(a.astype(BF16), b.astype(BF16), (((0,), (0,)), ((), ())),
                           preferred_element_type=F32)


def _sigmoid(x):
    return 1.0 / (1.0 + jnp.exp(-x))


def _softplus(x):
    return jnp.maximum(x, 0.0) + jnp.log1p(jnp.exp(-jnp.abs(x)))


def _iota2(shape, dim):
    return lax.broadcasted_iota(jnp.int32, shape, dim)


def _mm_kernel(x_ref, w_ref, o_ref):
    o_ref[...] = _bdot(x_ref[...], w_ref[...]).astype(o_ref.dtype)


def _mm(x, w, layer, out_dtype, tm, tn):
    M, K = x.shape
    N = w.shape[2]
    assert M % tm == 0 and N % tn == 0
    return pl.pallas_call(
        _mm_kernel,
        grid=(M // tm, N // tn),
        in_specs=[pl.BlockSpec((tm, K), lambda i, j: (i, 0)),
                  pl.BlockSpec((None, K, tn), lambda i, j: (layer, 0, j))],
        out_specs=pl.BlockSpec((tm, tn), lambda i, j: (i, j)),
        out_shape=jax.ShapeDtypeStruct((M, N), out_dtype),
        compiler_params=_params("parallel", "parallel"),
        name="mm",
    )(x, w)


def _layer_norm_rows(y, g, b):
    mu = jnp.mean(y, axis=-1, keepdims=True)
    d = y - mu
    var = jnp.mean(d * d, axis=-1, keepdims=True)
    return d * lax.rsqrt(var + LN_EPS) * g + b


def _pack_rows(v):
    half = v.shape[1] // 2
    bits = lax.bitcast_convert_type(v.astype(BF16).astype(F32), jnp.int32)
    return bits[:, half:] | lax.shift_right_logical(bits[:, :half], 16)


def _unpack_rows(w):
    return jnp.concatenate([lax.bitcast_convert_type(lax.shift_left(w, 16), F32),
                            lax.bitcast_convert_type(w & jnp.int32(-65536), F32)], axis=-1)


def _mm_res_ln_kernel(*refs, n_a, has_add, packed):
    a_refs, (w_ref, res_ref), rest = refs[:n_a], refs[n_a:n_a + 2], refs[n_a + 2:]
    add_ref = rest[0] if has_add else None
    n_out = 3 if packed else 2
    g_ref, b_ref = rest[-n_out - 2:-n_out]
    o_ref, ob_ref = rest[-n_out:][:2]
    acc, k0 = None, 0
    for a_ref in a_refs:
        part = _bdot(a_ref[...], w_ref[k0:k0 + a_ref.shape[1], :])
        acc = part if acc is None else acc + part
        k0 += a_ref.shape[1]
    if has_add:
        acc = add_ref[...] + acc
    out = _layer_norm_rows(ALPHA * res_ref[...] + acc, g_ref[...], b_ref[...])
    o_ref[...] = out
    ob_ref[...] = out.astype(ob_ref.dtype)
    if packed:
        rest[-1][...] = _pack_rows(out)


def _mm_res_ln(a_parts, w, layer, res, g, b, add=None, packed=False, tm=256):
    M = res.shape[0]
    K, N = w.shape[1:]
    assert sum(a.shape[1] for a in a_parts) == K
    row = lambda i: (i, 0)
    fixed = lambda i: (0, 0)
    in_specs = [pl.BlockSpec((tm, a.shape[1]), row) for a in a_parts]
    in_specs += [pl.BlockSpec((None, K, N), lambda i: (layer, 0, 0)), pl.BlockSpec((tm, N), row)]
    args = [*a_parts, w, res]
    if add is not None:
        in_specs.append(pl.BlockSpec((tm, N), row))
        args.append(add)
    in_specs += [pl.BlockSpec((1, N), fixed), pl.BlockSpec((1, N), fixed)]
    args += [g.reshape(1, N), b.reshape(1, N)]
    out_specs = [pl.BlockSpec((tm, N), row), pl.BlockSpec((tm, N), row)]
    out_shape = [jax.ShapeDtypeStruct((M, N), F32), jax.ShapeDtypeStruct((M, N), BF16)]
    if packed:
        out_specs.append(pl.BlockSpec((tm, N // 2), row))
        out_shape.append(jax.ShapeDtypeStruct((M, N // 2), jnp.int32))
    return pl.pallas_call(
        functools.partial(_mm_res_ln_kernel, n_a=len(a_parts), has_add=add is not None, packed=packed),
        grid=(M // tm,),
        in_specs=in_specs,
        out_specs=out_specs,
        out_shape=out_shape,
        compiler_params=_params("parallel"),
        name="mm_res_ln",
    )(*args)


def _glu_kernel(x_ref, w1_ref, w3_ref, o_ref):
    x = x_ref[...]
    h1 = jnp.dot(x, w1_ref[...], preferred_element_type=F32)
    h3 = jnp.dot(x, w3_ref[...], preferred_element_type=F32)
    o_ref[...] = (h1 * _sigmoid(h1) * h3).astype(o_ref.dtype)


def _glu(xb, w1, w3, layer, tm=512):
    M, K = xb.shape
    N = w1.shape[2]
    wspec = pl.BlockSpec((None, K, N), lambda i: (layer, 0, 0))
    return pl.pallas_call(
        _glu_kernel,
        grid=(M // tm,),
        in_specs=[pl.BlockSpec((tm, K), lambda i: (i, 0)), wspec, wspec],
        out_specs=pl.BlockSpec((tm, N), lambda i: (i, 0)),
        out_shape=jax.ShapeDtypeStruct((M, N), BF16),
        compiler_params=_params("parallel"),
        name="glu",
    )(xb, w1, w3)


def _log_sigmoid(x):
    return jnp.minimum(x, 0.0) - jnp.log1p(jnp.exp(-jnp.abs(x)))


def _split_bf16(x, terms):
    pieces = []
    for _ in range(terms):
        p = x.astype(BF16)
        pieces.append(p)
        x = x - p.astype(F32)
    return pieces


def _mlstm_kernel(qk_ref, v_ref, o_ref, gc_ref, gr_ref, cw_ref, cb_ref, gbc_ref, gbr_ref, gn_ref,
                  out_ref, prev_ref, c_ref, n_ref, m_ref, *, heads, dk, dv):
    nb, L, _ = qk_ref.shape
    qw = heads * dk

    @pl.when(pl.program_id(0) == 0)
    def _():
        prev_ref[...] = jnp.zeros_like(prev_ref)
        c_ref[...] = jnp.zeros_like(c_ref)
        n_ref[...] = jnp.zeros_like(n_ref)
        m_ref[...] = jnp.full_like(m_ref, NEG_BIG)

    row = _iota2((L, 1), 0)
    qk = []
    for b in range(nb):
        cur = qk_ref[b].astype(F32)
        prev = prev_ref[b]
        acc = cb_ref[...] + cw_ref[CONV_K - 1:CONV_K, :] * cur
        for j in range(CONV_K - 1):
            s = CONV_K - 1 - j
            shifted = jnp.where(row < s, pltpu.roll(prev, s, 0), pltpu.roll(cur, s, 0))
            acc = acc + cw_ref[j:j + 1, :] * shifted
        prev_ref[b] = cur
        qk.append(acc * _sigmoid(acc))

    causal = _iota2((L, L), 0) >= _iota2((L, L), 1)
    tri = causal.astype(F32).astype(BF16)
    g_col = [gc_ref[b].astype(F32) + gbc_ref[...] for b in range(nb)]
    pieces = _split_bf16(jnp.concatenate([_log_sigmoid(g) for g in g_col], axis=1), 3)
    lanes = nb * g_col[0].shape[1]
    b_col = jnp.dot(tri, jnp.concatenate(pieces, axis=1), preferred_element_type=F32)
    b_col = b_col[:, :lanes] + b_col[:, lanes:2 * lanes] + b_col[:, 2 * lanes:]
    g_row = [gr_ref[b, 0] + gbr_ref[...] for b in range(nb)]
    pieces = _split_bf16(jnp.concatenate([_log_sigmoid(g) for g in g_row], axis=0), 3)
    rows = nb * 2 * heads
    b_row = lax.dot_general(jnp.concatenate(pieces, axis=0), tri, (((1,), (1,)), ((), ())),
                            preferred_element_type=F32)
    b_row = b_row[:rows] + b_row[rows:2 * rows] + b_row[2 * rows:]

    def per_problem(f):
        return jnp.stack([f(b, h) for b in range(nb) for h in range(heads)])

    q = per_problem(lambda b, h: qk[b][:, h * dk:(h + 1) * dk])
    k = per_problem(lambda b, h: qk[b][:, qw + h * dk:qw + (h + 1) * dk]) * dk ** -0.5
    v = per_problem(lambda b, h: v_ref[b, :, h * dv:(h + 1) * dv])
    gcw = g_col[0].shape[1]
    bc = per_problem(lambda b, h: b_col[:, b * gcw + heads + h:b * gcw + heads + h + 1])
    ic = per_problem(lambda b, h: g_col[b][:, h:h + 1])
    br = per_problem(lambda b, h: b_row[b * 2 * heads + heads + h:b * 2 * heads + heads + h + 1])
    ir = per_problem(lambda b, h: g_row[b][h:h + 1])
    b_last = bc[:, L - 1:L]
    c_prev = c_ref[...]
    n_prev = n_ref[...]
    m_prev = m_ref[...][:, :, 0:1]

    def bmm(x, y, spec):
        return jnp.einsum(spec, x.astype(BF16), y.astype(BF16), preferred_element_type=F32)

    log_d = jnp.where(causal, bc - br + ir, -jnp.inf)
    g_inter = bc + m_prev
    m_t = jnp.maximum(jnp.max(log_d, axis=-1, keepdims=True), g_inter)
    p = jnp.exp(log_d - m_t) * bmm(q, k, "gtk,gsk->gts")
    e_inter = jnp.exp(g_inter - m_t)
    num = bmm(p, v, "gts,gsv->gtv") + e_inter * bmm(q, c_prev, "gtk,gkv->gtv")
    den = jnp.sum(p, axis=-1, keepdims=True) + e_inter * jnp.sum(q * n_prev, axis=-1, keepdims=True)
    hh = num / jnp.maximum(jnp.abs(den), jnp.exp(-m_t))

    w_end = b_last - bc + ic
    m_loc = jnp.max(w_end, axis=1, keepdims=True)
    e_end = jnp.exp(w_end - m_loc)
    ke = k * e_end
    m_new = jnp.maximum(b_last + m_prev, m_loc)
    s_prev = jnp.exp(b_last + m_prev - m_new)
    s_new = jnp.exp(m_loc - m_new)
    c_ref[...] = s_prev * c_prev + s_new * bmm(ke, v, "gsk,gsv->gkv")
    n_ref[...] = s_prev * n_prev + s_new * jnp.sum(ke, axis=1, keepdims=True)
    m_ref[...] = jnp.broadcast_to(m_new, m_ref.shape)

    mu = jnp.mean(hh, axis=-1, keepdims=True)
    d = hh - mu
    var = jnp.mean(d * d, axis=-1, keepdims=True)
    y = d * lax.rsqrt(var + 1e-6)
    for b in range(nb):
        for h in range(heads):
            sl = slice(h * dv, (h + 1) * dv)
            gate = _sigmoid(o_ref[b, :, sl].astype(F32))
            out_ref[b, :, sl] = (y[b * heads + h] * gn_ref[:, sl] * gate).astype(out_ref.dtype)


def _mlstm(z3, gates_t, conv_w, conv_b, gate_b, gn_w, *, qk_blk, v_blk, o_blk, gate_blk):
    B, S, _ = z3.shape
    H = ML_HEADS
    W = gn_w.shape[0]
    dv = W // H
    dk = conv_w.shape[1] // (2 * H)
    assert conv_w.shape[1] == W, "q|k block and v block share one column-block width"
    L = ML_CHUNK
    gb_col = jnp.zeros((1, LANES), F32).at[0, :2 * H].set(gate_b.reshape(-1))
    gb_row = gate_b.reshape(2 * H, 1)
    fixed = lambda c: (0, 0)
    return pl.pallas_call(
        functools.partial(_mlstm_kernel, heads=H, dk=dk, dv=dv),
        grid=(S // L,),
        in_specs=[pl.BlockSpec((B, L, W), lambda c: (0, c, qk_blk)),
                  pl.BlockSpec((B, L, W), lambda c: (0, c, v_blk)),
                  pl.BlockSpec((B, L, W), lambda c: (0, c, o_blk)),
                  pl.BlockSpec((B, L, LANES), lambda c: (0, c, gate_blk)),
                  pl.BlockSpec((B, 1, 2 * H, L), lambda c: (0, c, 0, 0)),
                  pl.BlockSpec((CONV_K, W), fixed),
                  pl.BlockSpec((1, W), fixed),
                  pl.BlockSpec((1, LANES), fixed),
                  pl.BlockSpec((2 * H, 1), fixed),
                  pl.BlockSpec((1, W), fixed)],
        out_specs=pl.BlockSpec((B, L, W), lambda c: (0, c, 0)),
        out_shape=jax.ShapeDtypeStruct((B, S, W), BF16),
        scratch_shapes=[pltpu.VMEM((B, L, W), F32),
                        pltpu.VMEM((B * H, dk, dv), F32),
                        pltpu.VMEM((B * H, 1, dk), F32),
                        pltpu.VMEM((B * H, 1, LANES), F32)],
        compiler_params=_params("arbitrary"),
        name="mlstm",
    )(z3, z3, z3, z3, gates_t, conv_w, conv_b.reshape(1, W), gb_col, gb_row, gn_w.reshape(1, W))


def _unit_lower_inverse(a, t_idx, i_idx, block_diag, matmul):
    n = a.shape[-2]
    eye = (t_idx == i_idx).astype(F32)
    t = None
    s = 1
    while s < n:
        pair = (t_idx // (2 * s)) == (i_idx // (2 * s))
        off = pair & ((t_idx // s) % 2 == 1) & ((i_idx // s) % 2 == 0)
        q = jnp.where(off, a, 0.0)
        t = eye - q if t is None else t - matmul(matmul(t, block_diag(q)), block_diag(t))
        s *= 2
    return t


def _rwkv_kernel(r_ref, k_ref, v_ref, l_ref, mur_ref, muk_ref, muv_ref, mul_ref, w0_ref, w2_ref,
                 a0_ref, a2_ref, g2_ref, kk_ref, ka_ref, rk_ref, gnw_ref, gnb_ref, out_ref,
                 pr_ref, pk_ref, pv_ref, plo_ref, s_ref, *, hd, gw):
    nb, L, W = r_ref.shape
    ng = W // gw
    hpg = gw // hd
    assert hpg * L == gw

    @pl.when(pl.program_id(1) == 0)
    def _():
        pr_ref[...] = jnp.zeros_like(pr_ref)
        pk_ref[...] = jnp.zeros_like(pk_ref)
        pv_ref[...] = jnp.zeros_like(pv_ref)
        plo_ref[...] = jnp.zeros_like(plo_ref)
        s_ref[...] = jnp.zeros_like(s_ref)

    R = nb * L
    first = _iota2((L, 1), 0) == 0
    same_head = (_iota2((gw, gw), 0) // hd) == (_iota2((gw, gw), 1) // hd)
    zero = jnp.zeros((), BF16)
    same_head_b = same_head.astype(F32).astype(BF16)
    row, col = _iota2((R, R), 0), _iota2((R, R), 1)
    tri = ((row // L == col // L) & (row >= col)).astype(F32).astype(BF16)
    t_idx = _iota2((L, gw), 0)
    i_idx = _iota2((L, gw), 1) % L
    strict = t_idx > i_idx
    lower = t_idx >= i_idx

    def shift_lerp(x_ref, p_ref, mu_ref):
        parts = []
        for b in range(nb):
            cur = x_ref[b].astype(F32)
            shifted = jnp.where(first, p_ref[b], pltpu.roll(cur, 1, 0))
            p_ref[b] = cur[L - 1:L, :]
            parts.append(cur + mu_ref[...] * (shifted - cur))
        return jnp.concatenate(parts, axis=0)

    def head_sums(xs):
        stacked = jnp.concatenate([x[:, i * gw:(i + 1) * gw] for x in xs for i in range(ng)], axis=0)
        n = stacked.shape[0]
        s = jnp.dot(jnp.concatenate(_split_bf16(stacked, 2), axis=0), same_head_b, preferred_element_type=F32)
        s = s[:n] + s[n:]
        return [jnp.concatenate([s[(j * ng + i) * R:(j * ng + i + 1) * R] for i in range(ng)], axis=1)
                for j in range(len(xs))]

    r = shift_lerp(r_ref, pr_ref, mur_ref)
    k = shift_lerp(k_ref, pk_ref, muk_ref)
    v = shift_lerp(v_ref, pv_ref, muv_ref)
    lo = shift_lerp(l_ref, plo_ref, mul_ref)
    xw = lo[:, :RW_LORA_W]
    xa = lo[:, RW_LORA_W:RW_LORA_W + RW_LORA_A]
    xg = lo[:, RW_LORA_W + RW_LORA_A:]

    w_log = -jnp.exp(-_softplus(-(w0_ref[...] + _bdot(jnp.tanh(xw), w2_ref[...]))) - 0.5)
    a = _sigmoid(a0_ref[...] + _bdot(xa, a2_ref[...]))
    g = _bdot(_sigmoid(xg), g2_ref[...])

    kk = k * kk_ref[...]
    k = k * (1.0 + (a - 1.0) * ka_ref[...])
    kk_sq, rk_sum = head_sums([kk * kk, r * k * rk_ref[...]])
    kk = kk * lax.rsqrt(jnp.maximum(kk_sq, 1e-24))
    bv = kk * a

    c3 = jnp.dot(tri, jnp.concatenate(_split_bf16(w_log, 3), axis=1), preferred_element_type=F32)
    c = c3[:, :W] + c3[:, W:2 * W] + c3[:, 2 * W:]
    c_last = jnp.concatenate([jnp.broadcast_to(c[(b + 1) * L - 1:(b + 1) * L], (L, W)) for b in range(nb)],
                             axis=0)
    e_end = jnp.exp(c_last - c)
    e_neg = jnp.exp(-c)
    kt = (kk * jnp.exp(c - w_log)).astype(BF16)
    rt = (r * jnp.exp(c)).astype(BF16)
    kh = (k * e_neg).astype(BF16)
    bh = (bv * e_neg).astype(BF16)
    kp = (k * e_end).astype(BF16)
    bp = (bv * e_end).astype(BF16)
    vb = v.astype(BF16)
    p_last = jnp.exp(c_last)

    def groups(x):
        return jnp.stack([x[b * L:(b + 1) * L, i * gw:(i + 1) * gw] for b in range(nb) for i in range(ng)])

    def block_diag(x):
        return jnp.where(same_head, jnp.concatenate([x.astype(BF16)] * hpg, axis=1), zero)

    def bmm(x, y):
        return jnp.einsum("gmk,gkn->gmn", x.astype(BF16), y.astype(BF16), preferred_element_type=F32)

    def bmm_nt(x, y):
        return jnp.einsum("gmk,gnk->gmn", x.astype(BF16), y.astype(BF16), preferred_element_type=F32)

    lhs = jnp.concatenate([groups(kt), groups(rt)], axis=1)
    m_k = bmm_nt(lhs, block_diag(groups(kh)))
    m_b = bmm_nt(lhs, block_diag(groups(bh)))
    a_kk = jnp.where(strict, m_k[:, :L], 0.0)
    a_kb = jnp.where(strict, m_b[:, :L], 0.0)
    a_rk = jnp.where(lower, m_k[:, L:], 0.0)
    a_rb = jnp.where(lower, m_b[:, L:], 0.0)
    t_inv = _unit_lower_inverse(a_kb, t_idx, i_idx, block_diag, bmm)
    st = s_ref[...]
    gs = bmm_nt(lhs, st)
    vg = groups(vb)
    v_bd = block_diag(vg)
    u = bmm(t_inv, block_diag(gs[:, :L] + bmm(a_kk, v_bd)))
    yg = gs[:, L:] + bmm(jnp.concatenate([a_rk, -a_rb], axis=2), jnp.concatenate([v_bd, block_diag(u)], axis=1))
    vu = jnp.concatenate([vg, (-u).astype(BF16)], axis=1)
    kb = jnp.concatenate([groups(kp), groups(bp)], axis=1)
    pg = groups(p_last)[:, :1]
    outer = jnp.einsum("gtv,gtk->gvk", vu, kb, preferred_element_type=F32)
    s_ref[...] = st * pg + jnp.where(same_head, outer, 0.0)
    ys = [jnp.concatenate([yg[b * ng + i] for i in range(ng)], axis=1) for b in range(nb)]

    y = jnp.concatenate(ys, axis=0)
    mu = head_sums([y])[0] * (1.0 / hd)
    d = y - mu
    var = head_sums([d * d])[0] * (1.0 / hd)
    yn = d * lax.rsqrt(var + RW_GN_EPS) * gnw_ref[...] + gnb_ref[...]
    out = ((yn + rk_sum * v) * g).astype(out_ref.dtype)
    for b in range(nb):
        out_ref[b] = out[b * L:(b + 1) * L]


def _rwkv(z3, mu, w0, w2, a0, a2, g2, k_k, k_a, r_k, gn_w, gn_b, *, col0, lora_col0, width):
    B, S, _ = z3.shape
    RW = w0.shape[0]
    W = width
    G = RW // W
    L = RW_CHUNK
    LO = RW_LORA_W + RW_LORA_A + RW_LORA_G
    assert col0 % W == 0 and RW % W == 0 and lora_col0 % LO == 0 and W % V7X_MXU_DIM == 0
    zc = lambda off: (lambda g, c: (0, c, off // W + g))
    vec = lambda g, c: (0, g)
    row = lambda x: x.reshape(1, -1)
    return pl.pallas_call(
        functools.partial(_rwkv_kernel, hd=RW_HEAD, gw=V7X_MXU_DIM),
        grid=(G, S // L),
        in_specs=[pl.BlockSpec((B, L, W), zc(col0)),
                  pl.BlockSpec((B, L, W), zc(col0 + RW)),
                  pl.BlockSpec((B, L, W), zc(col0 + 2 * RW)),
                  pl.BlockSpec((B, L, LO), lambda g, c: (0, c, lora_col0 // LO)),
                  pl.BlockSpec((1, W), vec), pl.BlockSpec((1, W), vec), pl.BlockSpec((1, W), vec),
                  pl.BlockSpec((1, LO), lambda g, c: (0, 0)),
                  pl.BlockSpec((1, W), vec),
                  pl.BlockSpec((RW_LORA_W, W), vec),
                  pl.BlockSpec((1, W), vec),
                  pl.BlockSpec((RW_LORA_A, W), vec),
                  pl.BlockSpec((RW_LORA_G, W), vec),
                  pl.BlockSpec((1, W), vec), pl.BlockSpec((1, W), vec), pl.BlockSpec((1, W), vec),
                  pl.BlockSpec((1, W), vec), pl.BlockSpec((1, W), vec)],
        out_specs=pl.BlockSpec((B, L, W), lambda g, c: (0, c, g)),
        out_shape=jax.ShapeDtypeStruct((B, S, RW), BF16),
        scratch_shapes=[pltpu.VMEM((B, 1, W), F32), pltpu.VMEM((B, 1, W), F32), pltpu.VMEM((B, 1, W), F32),
                        pltpu.VMEM((B, 1, LO), F32),
                        pltpu.VMEM((B * (W // V7X_MXU_DIM), V7X_MXU_DIM, V7X_MXU_DIM), F32)],
        compiler_params=_params("parallel", "arbitrary"),
        name="rwkv7",
    )(z3, z3, z3, z3, row(mu[:RW]), row(mu[RW:2 * RW]), row(mu[2 * RW:3 * RW]), row(mu[3 * RW:]),
      row(w0), w2.astype(BF16), row(a0), a2.astype(BF16), g2.astype(BF16), row(k_k), row(k_a),
      row(r_k), row(gn_w), row(gn_b))


def _retention_kernel(q_ref, k_ref, v_ref, g_ref, pos_ref, inv_ref, dec_ref, gnw_ref, gnb_ref,
                      out_ref, r_ref, *, heads, log_gamma):
    L = q_ref.shape[1]
    dk = q_ref.shape[2] // heads
    dv = v_ref.shape[2] // heads
    half = dk // 2

    @pl.when(pl.program_id(1) == 0)
    def _():
        r_ref[...] = jnp.zeros_like(r_ref)

    ang = pos_ref[0] * inv_ref[...]
    cos = jnp.cos(ang)
    sin = jnp.sin(ang)
    t_in = _iota2((L, 1), 0).astype(F32)

    def rot(t):
        t1, t2 = t[:, :half], t[:, half:]
        return jnp.concatenate([t1 * cos - t2 * sin, t1 * sin + t2 * cos], axis=-1)

    for h in range(heads):
        lg = log_gamma[h]
        q = rot(q_ref[0, :, h * dk:(h + 1) * dk].astype(F32))
        k = rot(k_ref[0, :, h * dk:(h + 1) * dk].astype(F32)) * dk ** -0.5
        v = v_ref[0, :, h * dv:(h + 1) * dv]
        state = r_ref[h]
        inter = _bdot(q, state) * jnp.exp(lg * (t_in + 1.0))
        intra = _bdot(_bdot_nt(q, k) * dec_ref[h], v)
        r_ref[h] = state * float(np.exp(lg * L)) + _bdot_tn(k * jnp.exp(lg * (L - 1.0 - t_in)), v)
        y = intra + inter
        mu = jnp.mean(y, axis=-1, keepdims=True)
        d = y - mu
        var = jnp.mean(d * d, axis=-1, keepdims=True)
        yn = d * lax.rsqrt(var + 1e-6) * gnw_ref[:, h * dv:(h + 1) * dv] + gnb_ref[:, h * dv:(h + 1) * dv]
        gate = g_ref[0, :, h * dv:(h + 1) * dv].astype(F32)
        out_ref[0, :, h * dv:(h + 1) * dv] = (gate * _sigmoid(gate) * yn).astype(out_ref.dtype)


def _retention(z3, positions, gn_w, gn_b):
    B, S, _ = z3.shape
    W = gn_w.shape[0]
    H = RET_HEADS
    dk = W // H
    half = dk // 2
    L = RET_CHUNK
    log_gamma = np.log1p(-np.exp2(-5.0 - np.arange(H, dtype=np.float64)))
    rel = np.arange(L)[:, None] - np.arange(L)[None, :]
    decay = np.where(rel >= 0, np.exp(log_gamma[:, None, None] * np.maximum(rel, 0)), 0.0).astype(np.float32)
    inv = (ROPE_BASE ** (-jnp.arange(half, dtype=F32) / half)).reshape(1, half)
    pos = positions.astype(F32).reshape(B, S, 1)
    blk = lambda j: pl.BlockSpec((1, L, W), lambda b, c: (b, c, j))
    fixed2 = lambda b, c: (0, 0)
    return pl.pallas_call(
        functools.partial(_retention_kernel, heads=H, log_gamma=tuple(float(x) for x in log_gamma)),
        grid=(B, S // L),
        in_specs=[blk(0), blk(1), blk(2), blk(3),
                  pl.BlockSpec((1, L, 1), lambda b, c: (b, c, 0)),
                  pl.BlockSpec((1, half), fixed2),
                  pl.BlockSpec((H, L, L), lambda b, c: (0, 0, 0)),
                  pl.BlockSpec((1, W), fixed2), pl.BlockSpec((1, W), fixed2)],
        out_specs=pl.BlockSpec((1, L, W), lambda b, c: (b, c, 0)),
        out_shape=jax.ShapeDtypeStruct((B, S, W), BF16),
        scratch_shapes=[pltpu.VMEM((H, dk, W // H), F32)],
        compiler_params=_params("parallel", "arbitrary"),
        name="retention",
    )(z3, z3, z3, z3, pos, inv, jnp.asarray(decay), gn_w.reshape(1, W), gn_b.reshape(1, W))


def _xattn_kernel(q_ref, k_ref, v_ref, o_ref, *, heads):
    hd = q_ref.shape[2] // heads
    q, k, v = (jnp.stack([r[0, :, h * hd:(h + 1) * hd] for h in range(heads)]) for r in (q_ref, k_ref, v_ref))
    s = jnp.einsum("hqd,hmd->hqm", q, k, preferred_element_type=F32) * hd ** -0.5
    e = jnp.exp(s - jnp.max(s, axis=-1, keepdims=True))
    p = e / jnp.sum(e, axis=-1, keepdims=True)
    o = jnp.einsum("hqm,hmd->hqd", p.astype(BF16), v, preferred_element_type=F32)
    for h in range(heads):
        o_ref[0, :, h * hd:(h + 1) * hd] = o[h].astype(o_ref.dtype)


def _xattn(q3, kv3, tq=512):
    B, S, D = q3.shape
    M = kv3.shape[1]
    return pl.pallas_call(
        functools.partial(_xattn_kernel, heads=XA_HEADS),
        grid=(B, S // tq),
        in_specs=[pl.BlockSpec((1, tq, D), lambda b, i: (b, i, 0)),
                  pl.BlockSpec((1, M, D), lambda b, i: (b, 0, 0)),
                  pl.BlockSpec((1, M, D), lambda b, i: (b, 0, 1))],
        out_specs=pl.BlockSpec((1, tq, D), lambda b, i: (b, i, 0)),
        out_shape=jax.ShapeDtypeStruct((B, S, D), BF16),
        compiler_params=_params("parallel", "parallel"),
        name="xattn",
    )(q3, kv3, kv3)


def _first_argmax(vals, index, big):
    m = jnp.max(vals, axis=0, keepdims=True)
    idx = jnp.min(jnp.where(vals == m, index, big), axis=0, keepdims=True)
    return m, idx


def _router_kernel(x_ref, wt_ref, bias_ref, idx_ref, gate_ref, pos_ref, cnt_ref, carry_ref):
    E = wt_ref.shape[0]
    tm = x_ref.shape[0]
    per = E // N_GROUPS
    neg = -jnp.inf
    logits = lax.dot_general(wt_ref[...], x_ref[...], (((1,), (1,)), ((), ())), precision=HI,
                             preferred_element_type=F32)
    scores = _sigmoid(logits)
    biased = scores + bias_ref[...]
    eidx = _iota2((E, tm), 0)

    within = _iota2((per, tm), 0)
    group_scores = []
    for g in range(N_GROUPS):
        vals = biased[g * per:(g + 1) * per]
        m1, i1 = _first_argmax(vals, within, per)
        m2 = jnp.max(jnp.where(within == i1, neg, vals), axis=0, keepdims=True)
        group_scores.append(m1 + m2)
    gs = jnp.concatenate(group_scores, axis=0)

    gidx = _iota2((N_GROUPS, tm), 0)
    keep = jnp.zeros((N_GROUPS, tm), F32)
    for _ in range(TOPK_GROUPS):
        _, gi = _first_argmax(gs, gidx, N_GROUPS)
        hit = gidx == gi
        keep = jnp.where(hit, 1.0, keep)
        gs = jnp.where(hit, neg, gs)
    keep_e = jnp.concatenate([jnp.broadcast_to(keep[g:g + 1], (per, tm)) for g in range(N_GROUPS)], axis=0)

    cand = jnp.where(keep_e > 0.0, biased, neg)
    picks, affs = [], []
    for _ in range(TOP_K):
        _, ei = _first_argmax(cand, eidx, E)
        hit = eidx == ei
        picks.append(ei)
        affs.append(jnp.sum(jnp.where(hit, scores, 0.0), axis=0, keepdims=True))
        cand = jnp.where(hit, neg, cand)
    aff = jnp.concatenate(affs, axis=0)
    idx_ref[...] = jnp.concatenate(picks, axis=0)
    gate_ref[...] = ROUTED_SCALE * aff / jnp.sum(aff, axis=0, keepdims=True)

    @pl.when(pl.program_id(0) == 0)
    def _():
        carry_ref[...] = jnp.zeros_like(carry_ref)

    sel = jnp.where(cand == neg, jnp.where(keep_e > 0.0, 1.0, 0.0), 0.0)
    upper = (_iota2((tm, tm), 0) < _iota2((tm, tm), 1)).astype(F32).astype(BF16)
    before = carry_ref[:, :1] + jnp.dot(sel.astype(BF16), upper, preferred_element_type=F32)
    pos_ref[...] = jnp.concatenate(
        [jnp.sum(jnp.where(eidx == ei, before, 0.0), axis=0, keepdims=True) for ei in picks], axis=0
    ).astype(jnp.int32)
    total = carry_ref[:, :1] + jnp.sum(sel, axis=1, keepdims=True)
    carry_ref[...] = jnp.broadcast_to(total, carry_ref.shape)
    cnt_ref[...] = jnp.broadcast_to(total, cnt_ref.shape).astype(jnp.int32)


def _router(x, router_w, router_bias, tm=512):
    T, D = x.shape
    E = router_w.shape[1]
    idx, gate, pos, cnt = pl.pallas_call(
        _router_kernel,
        grid=(T // tm,),
        in_specs=[pl.BlockSpec((tm, D), lambda i: (i, 0)),
                  pl.BlockSpec((E, D), lambda i: (0, 0)),
                  pl.BlockSpec((E, 1), lambda i: (0, 0))],
        out_specs=[pl.BlockSpec((TOP_K, tm), lambda i: (0, i)), pl.BlockSpec((TOP_K, tm), lambda i: (0, i)),
                   pl.BlockSpec((TOP_K, tm), lambda i: (0, i)), pl.BlockSpec((E, LANES), lambda i: (0, 0))],
        out_shape=[jax.ShapeDtypeStruct((TOP_K, T), jnp.int32), jax.ShapeDtypeStruct((TOP_K, T), F32),
                   jax.ShapeDtypeStruct((TOP_K, T), jnp.int32), jax.ShapeDtypeStruct((E, LANES), jnp.int32)],
        scratch_shapes=[pltpu.VMEM((E, LANES), F32)],
        compiler_params=_params("arbitrary"),
        name="router",
    )(x, router_w.T, router_bias.reshape(E, 1))
    return idx.T, gate.T, pos.T, cnt[:, 0]


def _dispatch_plan(top_e, pos, counts, rows_per_tile):
    T, K = top_e.shape
    E = counts.shape[0]
    n_tiles = (T * K) // rows_per_tile + E
    tiles = (counts + rows_per_tile - 1) // rows_per_tile
    tile_end = jnp.cumsum(tiles)
    n_used = tile_end[-1]
    row_start = (tile_end - tiles) * rows_per_tile
    dest = jnp.take(row_start, top_e) + pos
    tile_id = jnp.minimum(jnp.arange(n_tiles, dtype=jnp.int32), n_used - 1)
    tile_expert = jnp.minimum(jnp.sum(tile_end[None, :] <= tile_id[:, None], axis=1), E - 1)
    return (tile_expert.astype(jnp.int32), tile_end.astype(jnp.int32), n_used.reshape(1).astype(jnp.int32),
            dest.astype(jnp.int32))


def _dispatch_kernel(tend_ref, dest_ref, x_ref, xs_hbm, zero_ref, zsem, sem, *, slots):
    tm = x_ref.shape[0]
    rows = zero_ref.shape[0]
    n_experts = tend_ref.shape[0]

    @pl.when(pl.program_id(0) == 0)
    def _():
        zero_ref[...] = jnp.zeros_like(zero_ref)

        def last_tile(e):
            end = tend_ref[e]
            start = tend_ref[jnp.maximum(e - 1, 0)]
            start = jnp.where(e == 0, 0, start)
            return end > start, pltpu.make_async_copy(
                zero_ref, xs_hbm.at[pl.ds(jnp.maximum(end - 1, 0) * rows, rows)], zsem)

        def start(e, carry):
            owns, copy = last_tile(e)

            @pl.when(owns)
            def _():
                copy.start()
            return carry

        def wait(e, carry):
            owns, copy = last_tile(e)

            @pl.when(owns)
            def _():
                copy.wait()
            return carry

        def unused_tile(n):
            return pltpu.make_async_copy(zero_ref, xs_hbm.at[pl.ds(n * rows, rows)], zsem)

        def start_unused(n, carry):
            unused_tile(n).start()
            return carry

        def wait_unused(n, carry):
            unused_tile(n).wait()
            return carry

        n_used, n_tiles = tend_ref[n_experts - 1], xs_hbm.shape[0] // rows
        lax.fori_loop(0, n_experts, start, 0)
        lax.fori_loop(n_used, n_tiles, start_unused, 0)
        lax.fori_loop(0, n_experts, wait, 0)
        lax.fori_loop(n_used, n_tiles, wait_unused, 0)

    for k in range(slots):
        def body(t, carry):
            pltpu.make_async_copy(x_ref.at[pl.ds(t, 1)], xs_hbm.at[pl.ds(dest_ref[0, k * tm + t], 1)], sem).start()
            return carry
        lax.fori_loop(0, tm, body, 0, unroll=8)
    for k in range(slots):
        pltpu.make_async_copy(x_ref, xs_hbm.at[pl.ds(0, tm)], sem).wait()


def _dispatch(x3p, dest, tile_end, n_tiles, rows_per_tile, tm=256):
    T, half = x3p.shape
    K = dest.shape[1]
    steps = T // tm
    d3 = jnp.swapaxes(dest.reshape(steps, tm, K), 1, 2).reshape(steps, 1, K * tm)
    return pl.pallas_call(
        functools.partial(_dispatch_kernel, slots=K),
        grid_spec=pltpu.PrefetchScalarGridSpec(
            num_scalar_prefetch=1,
            grid=(steps,),
            in_specs=[pl.BlockSpec((None, 1, K * tm), lambda i, te: (i, 0, 0), memory_space=pltpu.SMEM),
                      pl.BlockSpec((tm, half), lambda i, te: (i, 0))],
            out_specs=pl.BlockSpec(memory_space=pl.ANY),
            scratch_shapes=[pltpu.VMEM((rows_per_tile, half), jnp.int32),
                            pltpu.SemaphoreType.DMA(()), pltpu.SemaphoreType.DMA(())]),
        out_shape=jax.ShapeDtypeStruct((n_tiles * rows_per_tile, half), jnp.int32),
        compiler_params=_params("arbitrary"),
        name="dispatch",
    )(tile_end, d3, x3p)


def _sparse_experts_kernel(te_ref, nu_ref, xs_ref, w1_ref, w3_ref, w2_ref, y_ref, wb1, wb3, wb2):
    n = pl.program_id(0)

    @pl.when(n < nu_ref[0])
    def _():
        @pl.when((n == 0) | (te_ref[n] != te_ref[jnp.maximum(n - 1, 0)]))
        def _():
            wb1[...] = w1_ref[0].astype(BF16)
            wb3[...] = w3_ref[0].astype(BF16)
            wb2[...] = w2_ref[0].astype(BF16)

        x = _unpack_rows(xs_ref[...]).astype(BF16)
        h1 = jnp.dot(x, wb1[...], preferred_element_type=F32)
        h3 = jnp.dot(x, wb3[...], preferred_element_type=F32)
        hidden = (h1 * _sigmoid(h1) * h3).astype(BF16)
        y_ref[...] = _pack_rows(jnp.dot(hidden, wb2[...], preferred_element_type=F32))

    @pl.when(n >= nu_ref[0])
    def _():
        y_ref[...] = jnp.zeros_like(y_ref)


def _sparse_experts(xs, tile_expert, n_used, w1, w3, w2, layer, rows):
    _, _, D, Hd = w1.shape
    n_tiles = tile_expert.shape[0]
    wmap = lambda n, te, nu: (layer, te[n], 0, 0)
    return pl.pallas_call(
        _sparse_experts_kernel,
        grid_spec=pltpu.PrefetchScalarGridSpec(
            num_scalar_prefetch=2,
            grid=(n_tiles,),
            in_specs=[pl.BlockSpec((rows, D // 2), lambda n, te, nu: (jnp.minimum(n, nu[0] - 1), 0)),
                      pl.BlockSpec((None, 1, D, Hd), wmap),
                      pl.BlockSpec((None, 1, D, Hd), wmap),
                      pl.BlockSpec((None, 1, Hd, D), wmap)],
            out_specs=pl.BlockSpec((rows, D // 2), lambda n, te, nu: (n, 0)),
            scratch_shapes=[pltpu.VMEM((D, Hd), BF16), pltpu.VMEM((D, Hd), BF16), pltpu.VMEM((Hd, D), BF16)]),
        out_shape=jax.ShapeDtypeStruct(xs.shape, jnp.int32),
        compiler_params=_params("arbitrary"),
        name="sparse_experts",
    )(tile_expert, n_used, xs, w1, w3, w2)


def _combine_kernel(dest_ref, dest_next_ref, gate_ref, y_hbm, o_ref, buf, sem):
    i = pl.program_id(0)
    steps = pl.num_programs(0)
    K, tm = buf.shape[1:3]
    slot = i % 2

    def fetch(d_ref, s):
        for k in range(K):
            def body(t, carry):
                pltpu.make_async_copy(y_hbm.at[pl.ds(d_ref[0, k * tm + t], 1)], buf.at[s, k, pl.ds(t, 1)],
                                      sem.at[s]).start()
                return carry
            lax.fori_loop(0, tm, body, 0, unroll=8)

    @pl.when(i == 0)
    def _():
        fetch(dest_ref, 0)

    @pl.when(i + 1 < steps)
    def _():
        fetch(dest_next_ref, 1 - slot)

    for k in range(K):
        pltpu.make_async_copy(y_hbm.at[pl.ds(0, tm)], buf.at[slot, k], sem.at[slot]).wait()
    gate = gate_ref[...]
    acc = jnp.zeros(o_ref.shape, F32)
    for k in range(K):
        acc = acc + gate[:, k:k + 1] * _unpack_rows(buf[slot, k])
    o_ref[...] = acc


def _combine(y, dest, gate, tm=128):
    T, K = dest.shape
    D = 2 * y.shape[1]
    steps = T // tm
    d3 = jnp.swapaxes(dest.reshape(steps, tm, K), 1, 2).reshape(steps, 1, K * tm)
    return pl.pallas_call(
        _combine_kernel,
        grid=(steps,),
        in_specs=[pl.BlockSpec((None, 1, K * tm), lambda i: (i, 0, 0), memory_space=pltpu.SMEM),
                  pl.BlockSpec((None, 1, K * tm), lambda i: (jnp.minimum(i + 1, steps - 1), 0, 0),
                               memory_space=pltpu.SMEM),
                  pl.BlockSpec((tm, K), lambda i: (i, 0)),
                  pl.BlockSpec(memory_space=pl.ANY)],
        out_specs=pl.BlockSpec((tm, D), lambda i: (i, 0)),
        out_shape=jax.ShapeDtypeStruct((T, D), F32),
        scratch_shapes=[pltpu.VMEM((2, K, tm, D // 2), jnp.int32), pltpu.SemaphoreType.DMA((2,))],
        compiler_params=_params("arbitrary"),
        name="combine",
    )(d3, d3, gate, y)


def _even_mixer(x2, xb, B, S, ln_g, ln_b, w_in, conv_w, conv_b, gate_b, ml_gn_w, rw_mu, rw_w0, rw_w2,
                rw_a0, rw_a2, rw_g2, rw_kk, rw_ka, rw_rk, rw_gn_w, rw_gn_b, w_out, j):
    D = x2.shape[1]
    H = ML_HEADS
    ML = ml_gn_w.shape[0]
    RW = rw_w0.shape[0]
    LO = RW_LORA_W + RW_LORA_A + RW_LORA_G
    ml_main = 3 * ML
    ml_cols = ml_main + 2 * H
    pad_to = 512
    used = ml_main + 3 * RW + LO + LANES
    total = -(-used // pad_to) * pad_to
    w_perm = jnp.concatenate([
        w_in[:, :ml_main], w_in[:, ml_cols:ml_cols + 3 * RW + LO], w_in[:, ml_main:ml_cols],
        jnp.zeros((D, total - used + LANES - 2 * H), w_in.dtype)], axis=1).astype(BF16)
    z3 = _mm(xb, w_perm[None], 0, BF16, 1024, pad_to).reshape(B, S, total)
    gate_col0 = ml_main + 3 * RW + LO
    L = ML_CHUNK
    gates_t = jnp.swapaxes(
        z3[:, :, gate_col0:gate_col0 + 2 * H].astype(F32).reshape(B, S // L, L, 2 * H), 2, 3)
    h_ml = _mlstm(z3, gates_t, conv_w, conv_b, gate_b, ml_gn_w,
                  qk_blk=0, v_blk=1, o_blk=2, gate_blk=gate_col0 // LANES)
    h_rw = _rwkv(z3, rw_mu, rw_w0, rw_w2, rw_a0, rw_a2, rw_g2, rw_kk, rw_ka, rw_rk, rw_gn_w, rw_gn_b,
                 col0=ml_main, lora_col0=ml_main + 3 * RW, width=RW)
    return _mm_res_ln([h_ml.reshape(B * S, ML), h_rw.reshape(B * S, RW)], w_out, j, x2, ln_g, ln_b)


def _odd_mixer(x2, xb, B, S, positions, ln_g, ln_b, w_in, gn_w, gn_b, w_out, j):
    z3 = _mm(xb, w_in, j, BF16, 1024, 512).reshape(B, S, w_in.shape[2])
    h = _retention(z3, positions, gn_w, gn_b)
    return _mm_res_ln([h.reshape(B * S, -1)], w_out, j, x2, ln_g, ln_b)


def _cross_attention(x2, xb, B, S, memb, ln_g, ln_b, wq, wkv, wo, layer):
    D = x2.shape[1]
    q = _mm(xb, wq, layer, BF16, 1024, 512)
    kv = _mm(memb, wkv, layer, BF16, memb.shape[0], 512)
    o = _xattn(q.reshape(B, S, D), kv.reshape(B, -1, 2 * D))
    return _mm_res_ln([o.reshape(B * S, D)], wo, layer, x2, ln_g, ln_b, packed=True)


def _moe(x2, xb, x3p, ln_g, ln_b, router_w, router_bias, w1, w3, w2, sw1, sw3, sw2, layer):
    rows = MOE_ROWS_PER_TILE
    top_e, gate, pos, counts = _router(x2, router_w, router_bias)
    tile_expert, tile_end, n_used, dest = _dispatch_plan(top_e, pos, counts, rows)
    xs = _dispatch(x3p, dest, tile_end, tile_expert.shape[0], rows)
    routed = _combine(_sparse_experts(xs, tile_expert, n_used, w1, w3, w2, layer, rows), dest, gate)
    hs = _glu(xb, sw1, sw3, layer)
    return _mm_res_ln([hs], sw2, layer, x2, ln_g, ln_b, add=routed)


def kernel(x, mem, positions, ln_g, ln_b, xa_wq, xa_wkv, xa_wo, router_w, router_bias, moe_w1, moe_w3, moe_w2, sh_w1, sh_w3, sh_w2, ev_w_in, ml_conv_w, ml_conv_b, ml_gate_b, ml_gn_w, rw_mu, rw_w0, rw_w2, rw_a0, rw_a2, rw_g2, rw_kk, rw_ka, rw_rk, rw_gn_w, rw_gn_b, ev_w_out, od_w_in, ret_gn_w, ret_gn_b, od_w_out):
    B, S, D = x.shape
    x2 = x.reshape(B * S, D)
    xb = x2.astype(BF16)
    memb = mem.reshape(-1, D).astype(BF16)
    (xa_wq, xa_wkv, xa_wo, sh_w1, sh_w3, sh_w2, ev_w_out, od_w_in, od_w_out) = (
        w.astype(BF16) for w in (xa_wq, xa_wkv, xa_wo, sh_w1, sh_w3, sh_w2, ev_w_out, od_w_in, od_w_out))
    for layer in range(ln_g.shape[0]):
        j = layer // 2
        if layer % 2 == 0:
            x2, xb = _even_mixer(x2, xb, B, S, ln_g[layer, 0], ln_b[layer, 0], ev_w_in[j], ml_conv_w[j],
                                 ml_conv_b[j], ml_gate_b[j], ml_gn_w[j], rw_mu[j], rw_w0[j], rw_w2[j],
                                 rw_a0[j], rw_a2[j], rw_g2[j], rw_kk[j], rw_ka[j], rw_rk[j],
                                 rw_gn_w[j], rw_gn_b[j], ev_w_out, j)
        else:
            x2, xb = _odd_mixer(x2, xb, B, S, positions, ln_g[layer, 0], ln_b[layer, 0], od_w_in,
                                ret_gn_w[j], ret_gn_b[j], od_w_out, j)
        x2, xb, x3p = _cross_attention(x2, xb, B, S, memb, ln_g[layer, 1], ln_b[layer, 1], xa_wq, xa_wkv,
                                       xa_wo, layer)
        x2, xb = _moe(x2, xb, x3p, ln_g[layer, 2], ln_b[layer, 2], router_w[layer], router_bias[layer],
                      moe_w1, moe_w3, moe_w2, sh_w1, sh_w3, sh_w2, layer)
    return x2.reshape(B, S, D)
```

```python
import functools

import numpy as np
import jax
import jax.numpy as jnp
from jax import lax
from jax.experimental import pallas as pl
from jax.experimental.pallas import tpu as pltpu

F32 = jnp.float32
BF16 = jnp.bfloat16
HI = lax.Precision.HIGHEST

DEPTH = 4
ALPHA = (2.0 * DEPTH) ** 0.25
LN_EPS = 1e-5
NEG_BIG = -1e30
CONV_K = 4
ML_HEADS = 4
RW_HEAD = 64
RW_LORA_W = 64
RW_LORA_A = 64
RW_LORA_G = 128
RW_GN_EPS = 64e-5
RET_HEADS = 8
ROPE_BASE = 10000.0
XA_HEADS = 4
N_EXPERTS = 64
TOP_K = 8
N_GROUPS = 8
TOPK_GROUPS = 4
ROUTED_SCALE = 2.5

ML_CHUNK = 64
RW_CHUNK = 64
RET_CHUNK = 256
MOE_ROWS_PER_TILE = 512

V7X_VMEM_BYTES = 64 * 2 ** 20
VMEM_LIMIT = (V7X_VMEM_BYTES * 3) // 4
LANES = 128
V7X_MXU_DIM = 256


def _params(*sem):
    return pltpu.CompilerParams(dimension_semantics=sem, vmem_limit_bytes=VMEM_LIMIT)


def _bdot(a, b):
    return jnp.dot(a.astype(BF16), b.astype(BF16), preferred_element_type=F32)


def _bdot_nt(a, b):
    return lax.dot_general(a.astype(BF16), b.astype(BF16), (((1,), (1,)), ((), ())),
                           preferred_element_type=F32)


def _bdot_tn(a, b):
    return lax.dot_general(a.astype(BF16), b.astype(BF16), (((0,), (0,)), ((), ())),
                           preferred_element_type=F32)


def _sigmoid(x):
    return 1.0 / (1.0 + jnp.exp(-x))


def _softplus(x):
    return jnp.maximum(x, 0.0) + jnp.log1p(jnp.exp(-jnp.abs(x)))


def _iota2(shape, dim):
    return lax.broadcasted_iota(jnp.int32, shape, dim)


def _mm_kernel(x_ref, w_ref, o_ref):
    o_ref[...] = _bdot(x_ref[...], w_ref[...]).astype(o_ref.dtype)


def _mm(x, w, layer, out_dtype, tm, tn):
    M, K = x.shape
    N = w.shape[2]
    assert M % tm == 0 and N % tn == 0
    return pl.pallas_call(
        _mm_kernel,
        grid=(M // tm, N // tn),
        in_specs=[pl.BlockSpec((tm, K), lambda i, j: (i, 0)),
                  pl.BlockSpec((None, K, tn), lambda i, j: (layer, 0, j))],
        out_specs=pl.BlockSpec((tm, tn), lambda i, j: (i, j)),
        out_shape=jax.ShapeDtypeStruct((M, N), out_dtype),
        compiler_params=_params("parallel", "parallel"),
        name="mm",
    )(x, w)


def _layer_norm_rows(y, g, b):
    mu = jnp.mean(y, axis=-1, keepdims=True)
    d = y - mu
    var = jnp.mean(d * d, axis=-1, keepdims=True)
    return d * lax.rsqrt(var + LN_EPS) * g + b


def _pack_rows(v):
    half = v.shape[1] // 2
    bits = lax.bitcast_convert_type(v.astype(BF16).astype(F32), jnp.int32)
    return bits[:, half:] | lax.shift_right_logical(bits[:, :half], 16)


def _unpack_rows(w):
    return jnp.concatenate([lax.bitcast_convert_type(lax.shift_left(w, 16), F32),
                            lax.bitcast_convert_type(w & jnp.int32(-65536), F32)], axis=-1)


def _mm_res_ln_kernel(*refs, n_a, has_add, packed):
    a_refs, (w_ref, res_ref), rest = refs[:n_a], refs[n_a:n_a + 2], refs[n_a + 2:]
    add_ref = rest[0] if has_add else None
    n_out = 3 if packed else 2
    g_ref, b_ref = rest[-n_out - 2:-n_out]
    o_ref, ob_ref = rest[-n_out:][:2]
    acc, k0 = None, 0
    for a_ref in a_refs:
        part = _bdot(a_ref[...], w_ref[k0:k0 + a_ref.shape[1], :])
        acc = part if acc is None else acc + part
        k0 += a_ref.shape[1]
    if has_add:
        acc = add_ref[...] + acc
    out = _layer_norm_rows(ALPHA * res_ref[...] + acc, g_ref[...], b_ref[...])
    o_ref[...] = out
    ob_ref[...] = out.astype(ob_ref.dtype)
    if packed:
        rest[-1][...] = _pack_rows(out)


def _mm_res_ln(a_parts, w, layer, res, g, b, add=None, packed=False, tm=256):
    M = res.shape[0]
    K, N = w.shape[1:]
    assert sum(a.shape[1] for a in a_parts) == K
    row = lambda i: (i, 0)
    fixed = lambda i: (0, 0)
    in_specs = [pl.BlockSpec((tm, a.shape[1]), row) for a in a_parts]
    in_specs += [pl.BlockSpec((None, K, N), lambda i: (layer, 0, 0)), pl.BlockSpec((tm, N), row)]
    args = [*a_parts, w, res]
    if add is not None:
        in_specs.append(pl.BlockSpec((tm, N), row))
        args.append(add)
    in_specs += [pl.BlockSpec((1, N), fixed), pl.BlockSpec((1, N), fixed)]
    args += [g.reshape(1, N), b.reshape(1, N)]
    out_specs = [pl.BlockSpec((tm, N), row), pl.BlockSpec((tm, N), row)]
    out_shape = [jax.ShapeDtypeStruct((M, N), F32), jax.ShapeDtypeStruct((M, N), BF16)]
    if packed:
        out_specs.append(pl.BlockSpec((tm, N // 2), row))
        out_shape.append(jax.ShapeDtypeStruct((M, N // 2), jnp.int32))
    return pl.pallas_call(
        functools.partial(_mm_res_ln_kernel, n_a=len(a_parts), has_add=add is not None, packed=packed),
        grid=(M // tm,),
        in_specs=in_specs,
        out_specs=out_specs,
        out_shape=out_shape,
        compiler_params=_params("parallel"),
        name="mm_res_ln",
    )(*args)


def _glu_kernel(x_ref, w1_ref, w3_ref, o_ref):
    x = x_ref[...]
    h1 = jnp.dot(x, w1_ref[...], preferred_element_type=F32)
    h3 = jnp.dot(x, w3_ref[...], preferred_element_type=F32)
    o_ref[...] = (h1 * _sigmoid(h1) * h3).astype(o_ref.dtype)


def _glu(xb, w1, w3, layer, tm=512):
    M, K = xb.shape
    N = w1.shape[2]
    wspec = pl.BlockSpec((None, K, N), lambda i: (layer, 0, 0))
    return pl.pallas_call(
        _glu_kernel,
        grid=(M // tm,),
        in_specs=[pl.BlockSpec((tm, K), lambda i: (i, 0)), wspec, wspec],
        out_specs=pl.BlockSpec((tm, N), lambda i: (i, 0)),
        out_shape=jax.ShapeDtypeStruct((M, N), BF16),
        compiler_params=_params("parallel"),
        name="glu",
    )(xb, w1, w3)


def _log_sigmoid(x):
    return jnp.minimum(x, 0.0) - jnp.log1p(jnp.exp(-jnp.abs(x)))


def _split_bf16(x, terms):
    pieces = []
    for _ in range(terms):
        p = x.astype(BF16)
        pieces.append(p)
        x = x - p.astype(F32)
    return pieces


def _mlstm_kernel(qk_ref, v_ref, o_ref, gc_ref, gr_ref, cw_ref, cb_ref, gbc_ref, gbr_ref, gn_ref,
                  out_ref, prev_ref, c_ref, n_ref, m_ref, *, heads, dk, dv):
    nb, L, _ = qk_ref.shape
    qw = heads * dk

    @pl.when(pl.program_id(0) == 0)
    def _():
        prev_ref[...] = jnp.zeros_like(prev_ref)
        c_ref[...] = jnp.zeros_like(c_ref)
        n_ref[...] = jnp.zeros_like(n_ref)
        m_ref[...] = jnp.full_like(m_ref, NEG_BIG)

    row = _iota2((L, 1), 0)
    qk = []
    for b in range(nb):
        cur = qk_ref[b].astype(F32)
        prev = prev_ref[b]
        acc = cb_ref[...] + cw_ref[CONV_K - 1:CONV_K, :] * cur
        for j in range(CONV_K - 1):
            s = CONV_K - 1 - j
            shifted = jnp.where(row < s, pltpu.roll(prev, s, 0), pltpu.roll(cur, s, 0))
            acc = acc + cw_ref[j:j + 1, :] * shifted
        prev_ref[b] = cur
        qk.append(acc * _sigmoid(acc))

    causal = _iota2((L, L), 0) >= _iota2((L, L), 1)
    tri = causal.astype(F32).astype(BF16)
    g_col = [gc_ref[b].astype(F32) + gbc_ref[...] for b in range(nb)]
    pieces = _split_bf16(jnp.concatenate([_log_sigmoid(g) for g in g_col], axis=1), 3)
    lanes = nb * g_col[0].shape[1]
    b_col = jnp.dot(tri, jnp.concatenate(pieces, axis=1), preferred_element_type=F32)
    b_col = b_col[:, :lanes] + b_col[:, lanes:2 * lanes] + b_col[:, 2 * lanes:]
    g_row = [gr_ref[b, 0] + gbr_ref[...] for b in range(nb)]
    pieces = _split_bf16(jnp.concatenate([_log_sigmoid(g) for g in g_row], axis=0), 3)
    rows = nb * 2 * heads
    b_row = lax.dot_general(jnp.concatenate(pieces, axis=0), tri, (((1,), (1,)), ((), ())),
                            preferred_element_type=F32)
    b_row = b_row[:rows] + b_row[rows:2 * rows] + b_row[2 * rows:]

    def per_problem(f):
        return jnp.stack([f(b, h) for b in range(nb) for h in range(heads)])

    q = per_problem(lambda b, h: qk[b][:, h * dk:(h + 1) * dk])
    k = per_problem(lambda b, h: qk[b][:, qw + h * dk:qw + (h + 1) * dk]) * dk ** -0.5
    v = per_problem(lambda b, h: v_ref[b, :, h * dv:(h + 1) * dv])
    gcw = g_col[0].shape[1]
    bc = per_problem(lambda b, h: b_col[:, b * gcw + heads + h:b * gcw + heads + h + 1])
    ic = per_problem(lambda b, h: g_col[b][:, h:h + 1])
    br = per_problem(lambda b, h: b_row[b * 2 * heads + heads + h:b * 2 * heads + heads + h + 1])
    ir = per_problem(lambda b, h: g_row[b][h:h + 1])
    b_last = bc[:, L - 1:L]
    c_prev = c_ref[...]
    n_prev = n_ref[...]
    m_prev = m_ref[...][:, :, 0:1]

    def bmm(x, y, spec):
        return jnp.einsum(spec, x.astype(BF16), y.astype(BF16), preferred_element_type=F32)

    log_d = jnp.where(causal, bc - br + ir, -jnp.inf)
    g_inter = bc + m_prev
    m_t = jnp.maximum(jnp.max(log_d, axis=-1, keepdims=True), g_inter)
    p = jnp.exp(log_d - m_t) * bmm(q, k, "gtk,gsk->gts")
    e_inter = jnp.exp(g_inter - m_t)
    num = bmm(p, v, "gts,gsv->gtv") + e_inter * bmm(q, c_prev, "gtk,gkv->gtv")
    den = jnp.sum(p, axis=-1, keepdims=True) + e_inter * jnp.sum(q * n_prev, axis=-1, keepdims=True)
    hh = num / jnp.maximum(jnp.abs(den), jnp.exp(-m_t))

    w_end = b_last - bc + ic
    m_loc = jnp.max(w_end, axis=1, keepdims=True)
    e_end = jnp.exp(w_end - m_loc)
    ke = k * e_end
    m_new = jnp.maximum(b_last + m_prev, m_loc)
    s_prev = jnp.exp(b_last + m_prev - m_new)
    s_new = jnp.exp(m_loc - m_new)
    c_ref[...] = s_prev * c_prev + s_new * bmm(ke, v, "gsk,gsv->gkv")
    n_ref[...] = s_prev * n_prev + s_new * jnp.sum(ke, axis=1, keepdims=True)
    m_ref[...] = jnp.broadcast_to(m_new, m_ref.shape)

    mu = jnp.mean(hh, axis=-1, keepdims=True)
    d = hh - mu
    var = jnp.mean(d * d, axis=-1, keepdims=True)
    y = d * lax.rsqrt(var + 1e-6)
    for b in range(nb):
        for h in range(heads):
            sl = slice(h * dv, (h + 1) * dv)
            gate = _sigmoid(o_ref[b, :, sl].astype(F32))
            out_ref[b, :, sl] = (y[b * heads + h] * gn_ref[:, sl] * gate).astype(out_ref.dtype)


def _mlstm(z3, gates_t, conv_w, conv_b, gate_b, gn_w, *, qk_blk, v_blk, o_blk, gate_blk):
    B, S, _ = z3.shape
    H = ML_HEADS
    W = gn_w.shape[0]
    dv = W // H
    dk = conv_w.shape[1] // (2 * H)
    assert conv_w.shape[1] == W, "q|k block and v block share one column-block width"
    L = ML_CHUNK
    gb_col = jnp.zeros((1, LANES), F32).at[0, :2 * H].set(gate_b.reshape(-1))
    gb_row = gate_b.reshape(2 * H, 1)
    fixed = lambda c: (0, 0)
    return pl.pallas_call(
        functools.partial(_mlstm_kernel, heads=H, dk=dk, dv=dv),
        grid=(S // L,),
        in_specs=[pl.BlockSpec((B, L, W), lambda c: (0, c, qk_blk)),
                  pl.BlockSpec((B, L, W), lambda c: (0, c, v_blk)),
                  pl.BlockSpec((B, L, W), lambda c: (0, c, o_blk)),
                  pl.BlockSpec((B, L, LANES), lambda c: (0, c, gate_blk)),
                  pl.BlockSpec((B, 1, 2 * H, L), lambda c: (0, c, 0, 0)),
                  pl.BlockSpec((CONV_K, W), fixed),
                  pl.BlockSpec((1, W), fixed),
                  pl.BlockSpec((1, LANES), fixed),
                  pl.BlockSpec((2 * H, 1), fixed),
                  pl.BlockSpec((1, W), fixed)],
        out_specs=pl.BlockSpec((B, L, W), lambda c: (0, c, 0)),
        out_shape=jax.ShapeDtypeStruct((B, S, W), BF16),
        scratch_shapes=[pltpu.VMEM((B, L, W), F32),
                        pltpu.VMEM((B * H, dk, dv), F32),
                        pltpu.VMEM((B * H, 1, dk), F32),
                        pltpu.VMEM((B * H, 1, LANES), F32)],
        compiler_params=_params("arbitrary"),
        name="mlstm",
    )(z3, z3, z3, z3, gates_t, conv_w, conv_b.reshape(1, W), gb_col, gb_row, gn_w.reshape(1, W))


def _unit_lower_inverse(a, t_idx, i_idx, block_diag, matmul):
    n = a.shape[-2]
    eye = (t_idx == i_idx).astype(F32)
    t = None
    s = 1
    while s < n:
        pair = (t_idx // (2 * s)) == (i_idx // (2 * s))
        off = pair & ((t_idx // s) % 2 == 1) & ((i_idx // s) % 2 == 0)
        q = jnp.where(off, a, 0.0)
        t = eye - q if t is None else t - matmul(matmul(t, block_diag(q)), block_diag(t))
        s *= 2
    return t


def _rwkv_kernel(r_ref, k_ref, v_ref, l_ref, mur_ref, muk_ref, muv_ref, mul_ref, w0_ref, w2_ref,
                 a0_ref, a2_ref, g2_ref, kk_ref, ka_ref, rk_ref, gnw_ref, gnb_ref, out_ref,
                 pr_ref, pk_ref, pv_ref, plo_ref, s_ref, *, hd, gw):
    nb, L, W = r_ref.shape
    ng = W // gw
    hpg = gw // hd
    assert hpg * L == gw

    @pl.when(pl.program_id(1) == 0)
    def _():
        pr_ref[...] = jnp.zeros_like(pr_ref)
        pk_ref[...] = jnp.zeros_like(pk_ref)
        pv_ref[...] = jnp.zeros_like(pv_ref)
        plo_ref[...] = jnp.zeros_like(plo_ref)
        s_ref[...] = jnp.zeros_like(s_ref)

    R = nb * L
    first = _iota2((L, 1), 0) == 0
    same_head = (_iota2((gw, gw), 0) // hd) == (_iota2((gw, gw), 1) // hd)
    zero = jnp.zeros((), BF16)
    same_head_b = same_head.astype(F32).astype(BF16)
    row, col = _iota2((R, R), 0), _iota2((R, R), 1)
    tri = ((row // L == col // L) & (row >= col)).astype(F32).astype(BF16)
    t_idx = _iota2((L, gw), 0)
    i_idx = _iota2((L, gw), 1) % L
    strict = t_idx > i_idx
    lower = t_idx >= i_idx

    def shift_lerp(x_ref, p_ref, mu_ref):
        parts = []
        for b in range(nb):
            cur = x_ref[b].astype(F32)
            shifted = jnp.where(first, p_ref[b], pltpu.roll(cur, 1, 0))
            p_ref[b] = cur[L - 1:L, :]
            parts.append(cur + mu_ref[...] * (shifted - cur))
        return jnp.concatenate(parts, axis=0)

    def head_sums(xs):
        stacked = jnp.concatenate([x[:, i * gw:(i + 1) * gw] for x in xs for i in range(ng)], axis=0)
        n = stacked.shape[0]
        s = jnp.dot(jnp.concatenate(_split_bf16(stacked, 2), axis=0), same_head_b, preferred_element_type=F32)
        s = s[:n] + s[n:]
        return [jnp.concatenate([s[(j * ng + i) * R:(j * ng + i + 1) * R] for i in range(ng)], axis=1)
                for j in range(len(xs))]

    r = shift_lerp(r_ref, pr_ref, mur_ref)
    k = shift_lerp(k_ref, pk_ref, muk_ref)
    v = shift_lerp(v_ref, pv_ref, muv_ref)
    lo = shift_lerp(l_ref, plo_ref, mul_ref)
    xw = lo[:, :RW_LORA_W]
    xa = lo[:, RW_LORA_W:RW_LORA_W + RW_LORA_A]
    xg = lo[:, RW_LORA_W + RW_LORA_A:]

    w_log = -jnp.exp(-_softplus(-(w0_ref[...] + _bdot(jnp.tanh(xw), w2_ref[...]))) - 0.5)
    a = _sigmoid(a0_ref[...] + _bdot(xa, a2_ref[...]))
    g = _bdot(_sigmoid(xg), g2_ref[...])

    kk = k * kk_ref[...]
    k = k * (1.0 + (a - 1.0) * ka_ref[...])
    kk_sq, rk_sum = head_sums([kk * kk, r * k * rk_ref[...]])
    kk = kk * lax.rsqrt(jnp.maximum(kk_sq, 1e-24))
    bv = kk * a

    c3 = jnp.dot(tri, jnp.concatenate(_split_bf16(w_log, 3), axis=1), preferred_element_type=F32)
    c = c3[:, :W] + c3[:, W:2 * W] + c3[:, 2 * W:]
    c_last = jnp.concatenate([jnp.broadcast_to(c[(b + 1) * L - 1:(b + 1) * L], (L, W)) for b in range(nb)],
                             axis=0)
    e_end = jnp.exp(c_last - c)
    e_neg = jnp.exp(-c)
    kt = (kk * jnp.exp(c - w_log)).astype(BF16)
    rt = (r * jnp.exp(c)).astype(BF16)
    kh = (k * e_neg).astype(BF16)
    bh = (bv * e_neg).astype(BF16)
    kp = (k * e_end).astype(BF16)
    bp = (bv * e_end).astype(BF16)
    vb = v.astype(BF16)
    p_last = jnp.exp(c_last)

    def groups(x):
        return jnp.stack([x[b * L:(b + 1) * L, i * gw:(i + 1) * gw] for b in range(nb) for i in range(ng)])

    def block_diag(x):
        return jnp.where(same_head, jnp.concatenate([x.astype(BF16)] * hpg, axis=1), zero)

    def bmm(x, y):
        return jnp.einsum("gmk,gkn->gmn", x.astype(BF16), y.astype(BF16), preferred_element_type=F32)

    def bmm_nt(x, y):
        return jnp.einsum("gmk,gnk->gmn", x.astype(BF16), y.astype(BF16), preferred_element_type=F32)

    lhs = jnp.concatenate([groups(kt), groups(rt)], axis=1)
    m_k = bmm_nt(lhs, block_diag(groups(kh)))
    m_b = bmm_nt(lhs, block_diag(groups(bh)))
    a_kk = jnp.where(strict, m_k[:, :L], 0.0)
    a_kb = jnp.where(strict, m_b[:, :L], 0.0)
    a_rk = jnp.where(lower, m_k[:, L:], 0.0)
    a_rb = jnp.where(lower, m_b[:, L:], 0.0)
    t_inv = _unit_lower_inverse(a_kb, t_idx, i_idx, block_diag, bmm)
    st = s_ref[...]
    gs = bmm_nt(lhs, st)
    vg = groups(vb)
    v_bd = block_diag(vg)
    u = bmm(t_inv, block_diag(gs[:, :L] + bmm(a_kk, v_bd)))
    yg = gs[:, L:] + bmm(jnp.concatenate([a_rk, -a_rb], axis=2), jnp.concatenate([v_bd, block_diag(u)], axis=1))
    vu = jnp.concatenate([vg, (-u).astype(BF16)], axis=1)
    kb = jnp.concatenate([groups(kp), groups(bp)], axis=1)
    pg = groups(p_last)[:, :1]
    outer = jnp.einsum("gtv,gtk->gvk", vu, kb, preferred_element_type=F32)
    s_ref[...] = st * pg + jnp.where(same_head, outer, 0.0)
    ys = [jnp.concatenate([yg[b * ng + i] for i in range(ng)], axis=1) for b in range(nb)]

    y = jnp.concatenate(ys, axis=0)
    mu = head_sums([y])[0] * (1.0 / hd)
    d = y - mu
    var = head_sums([d * d])[0] * (1.0 / hd)
    yn = d * lax.rsqrt(var + RW_GN_EPS) * gnw_ref[...] + gnb_ref[...]
    out = ((yn + rk_sum * v) * g).astype(out_ref.dtype)
    for b in range(nb):
        out_ref[b] = out[b * L:(b + 1) * L]


def _rwkv(z3, mu, w0, w2, a0, a2, g2, k_k, k_a, r_k, gn_w, gn_b, *, col0, lora_col0, width):
    B, S, _ = z3.shape
    RW = w0.shape[0]
    W = width
    G = RW // W
    L = RW_CHUNK
    LO = RW_LORA_W + RW_LORA_A + RW_LORA_G
    assert col0 % W == 0 and RW % W == 0 and lora_col0 % LO == 0 and W % V7X_MXU_DIM == 0
    zc = lambda off: (lambda g, c: (0, c, off // W + g))
    vec = lambda g, c: (0, g)
    row = lambda x: x.reshape(1, -1)
    return pl.pallas_call(
        functools.partial(_rwkv_kernel, hd=RW_HEAD, gw=V7X_MXU_DIM),
        grid=(G, S // L),
        in_specs=[pl.BlockSpec((B, L, W), zc(col0)),
                  pl.BlockSpec((B, L, W), zc(col0 + RW)),
                  pl.BlockSpec((B, L, W), zc(col0 + 2 * RW)),
                  pl.BlockSpec((B, L, LO), lambda g, c: (0, c, lora_col0 // LO)),
                  pl.BlockSpec((1, W), vec), pl.BlockSpec((1, W), vec), pl.BlockSpec((1, W), vec),
                  pl.BlockSpec((1, LO), lambda g, c: (0, 0)),
                  pl.BlockSpec((1, W), vec),
                  pl.BlockSpec((RW_LORA_W, W), vec),
                  pl.BlockSpec((1, W), vec),
                  pl.BlockSpec((RW_LORA_A, W), vec),
                  pl.BlockSpec((RW_LORA_G, W), vec),
                  pl.BlockSpec((1, W), vec), pl.BlockSpec((1, W), vec), pl.BlockSpec((1, W), vec),
                  pl.BlockSpec((1, W), vec), pl.BlockSpec((1, W), vec)],
        out_specs=pl.BlockSpec((B, L, W), lambda g, c: (0, c, g)),
        out_shape=jax.ShapeDtypeStruct((B, S, RW), BF16),
        scratch_shapes=[pltpu.VMEM((B, 1, W), F32), pltpu.VMEM((B, 1, W), F32), pltpu.VMEM((B, 1, W), F32),
                        pltpu.VMEM((B, 1, LO), F32),
                        pltpu.VMEM((B * (W // V7X_MXU_DIM), V7X_MXU_DIM, V7X_MXU_DIM), F32)],
        compiler_params=_params("parallel", "arbitrary"),
        name="rwkv7",
    )(z3, z3, z3, z3, row(mu[:RW]), row(mu[RW:2 * RW]), row(mu[2 * RW:3 * RW]), row(mu[3 * RW:]),
      row(w0), w2.astype(BF16), row(a0), a2.astype(BF16), g2.astype(BF16), row(k_k), row(k_a),
      row(r_k), row(gn_w), row(gn_b))


def _retention_kernel(q_ref, k_ref, v_ref, g_ref, pos_ref, inv_ref, dec_ref, gnw_ref, gnb_ref,
                      out_ref, r_ref, *, heads, log_gamma):
    L = q_ref.shape[1]
    dk = q_ref.shape[2] // heads
    dv = v_ref.shape[2] // heads
    half = dk // 2

    @pl.when(pl.program_id(1) == 0)
    def _():
        r_ref[...] = jnp.zeros_like(r_ref)

    ang = pos_ref[0] * inv_ref[...]
    cos = jnp.cos(ang)
    sin = jnp.sin(ang)
    t_in = _iota2((L, 1), 0).astype(F32)

    def rot(t):
        t1, t2 = t[:, :half], t[:, half:]
        return jnp.concatenate([t1 * cos - t2 * sin, t1 * sin + t2 * cos], axis=-1)

    for h in range(heads):
        lg = log_gamma[h]
        q = rot(q_ref[0, :, h * dk:(h + 1) * dk].astype(F32))
        k = rot(k_ref[0, :, h * dk:(h + 1) * dk].astype(F32)) * dk ** -0.5
        v = v_ref[0, :, h * dv:(h + 1) * dv]
        state = r_ref[h]
        inter = _bdot(q, state) * jnp.exp(lg * (t_in + 1.0))
        intra = _bdot(_bdot_nt(q, k) * dec_ref[h], v)
        r_ref[h] = state * float(np.exp(lg * L)) + _bdot_tn(k * jnp.exp(lg * (L - 1.0 - t_in)), v)
        y = intra + inter
        mu = jnp.mean(y, axis=-1, keepdims=True)
        d = y - mu
        var = jnp.mean(d * d, axis=-1, keepdims=True)
        yn = d * lax.rsqrt(var + 1e-6) * gnw_ref[:, h * dv:(h + 1) * dv] + gnb_ref[:, h * dv:(h + 1) * dv]
        gate = g_ref[0, :, h * dv:(h + 1) * dv].astype(F32)
        out_ref[0, :, h * dv:(h + 1) * dv] = (gate * _sigmoid(gate) * yn).astype(out_ref.dtype)


def _retention(z3, positions, gn_w, gn_b):
    B, S, _ = z3.shape
    W = gn_w.shape[0]
    H = RET_HEADS
    dk = W // H
    half = dk // 2
    L = RET_CHUNK
    log_gamma = np.log1p(-np.exp2(-5.0 - np.arange(H, dtype=np.float64)))
    rel = np.arange(L)[:, None] - np.arange(L)[None, :]
    decay = np.where(rel >= 0, np.exp(log_gamma[:, None, None] * np.maximum(rel, 0)), 0.0).astype(np.float32)
    inv = (ROPE_BASE ** (-jnp.arange(half, dtype=F32) / half)).reshape(1, half)
    pos = positions.astype(F32).reshape(B, S, 1)
    blk = lambda j: pl.BlockSpec((1, L, W), lambda b, c: (b, c, j))
    fixed2 = lambda b, c: (0, 0)
    return pl.pallas_call(
        functools.partial(_retention_kernel, heads=H, log_gamma=tuple(float(x) for x in log_gamma)),
        grid=(B, S // L),
        in_specs=[blk(0), blk(1), blk(2), blk(3),
                  pl.BlockSpec((1, L, 1), lambda b, c: (b, c, 0)),
                  pl.BlockSpec((1, half), fixed2),
                  pl.BlockSpec((H, L, L), lambda b, c: (0, 0, 0)),
                  pl.BlockSpec((1, W), fixed2), pl.BlockSpec((1, W), fixed2)],
        out_specs=pl.BlockSpec((1, L, W), lambda b, c: (b, c, 0)),
        out_shape=jax.ShapeDtypeStruct((B, S, W), BF16),
        scratch_shapes=[pltpu.VMEM((H, dk, W // H), F32)],
        compiler_params=_params("parallel", "arbitrary"),
        name="retention",
    )(z3, z3, z3, z3, pos, inv, jnp.asarray(decay), gn_w.reshape(1, W), gn_b.reshape(1, W))


def _xattn_kernel(q_ref, k_ref, v_ref, o_ref, *, heads):
    hd = q_ref.shape[2] // heads
    q, k, v = (jnp.stack([r[0, :, h * hd:(h + 1) * hd] for h in range(heads)]) for r in (q_ref, k_ref, v_ref))
    s = jnp.einsum("hqd,hmd->hqm", q, k, preferred_element_type=F32) * hd ** -0.5
    e = jnp.exp(s - jnp.max(s, axis=-1, keepdims=True))
    p = e / jnp.sum(e, axis=-1, keepdims=True)
    o = jnp.einsum("hqm,hmd->hqd", p.astype(BF16), v, preferred_element_type=F32)
    for h in range(heads):
        o_ref[0, :, h * hd:(h + 1) * hd] = o[h].astype(o_ref.dtype)


def _xattn(q3, kv3, tq=512):
    B, S, D = q3.shape
    M = kv3.shape[1]
    return pl.pallas_call(
        functools.partial(_xattn_kernel, heads=XA_HEADS),
        grid=(B, S // tq),
        in_specs=[pl.BlockSpec((1, tq, D), lambda b, i: (b, i, 0)),
                  pl.BlockSpec((1, M, D), lambda b, i: (b, 0, 0)),
                  pl.BlockSpec((1, M, D), lambda b, i: (b, 0, 1))],
        out_specs=pl.BlockSpec((1, tq, D), lambda b, i: (b, i, 0)),
        out_shape=jax.ShapeDtypeStruct((B, S, D), BF16),
        compiler_params=_params("parallel", "parallel"),
        name="xattn",
    )(q3, kv3, kv3)


def _first_argmax(vals, index, big):
    m = jnp.max(vals, axis=0, keepdims=True)
    idx = jnp.min(jnp.where(vals == m, index, big), axis=0, keepdims=True)
    return m, idx


def _router_kernel(x_ref, wt_ref, bias_ref, idx_ref, gate_ref, pos_ref, cnt_ref, carry_ref):
    E = wt_ref.shape[0]
    tm = x_ref.shape[0]
    per = E // N_GROUPS
    neg = -jnp.inf
    logits = lax.dot_general(wt_ref[...], x_ref[...], (((1,), (1,)), ((), ())), precision=HI,
                             preferred_element_type=F32)
    scores = _sigmoid(logits)
    biased = scores + bias_ref[...]
    eidx = _iota2((E, tm), 0)

    within = _iota2((per, tm), 0)
    group_scores = []
    for g in range(N_GROUPS):
        vals = biased[g * per:(g + 1) * per]
        m1, i1 = _first_argmax(vals, within, per)
        m2 = jnp.max(jnp.where(within == i1, neg, vals), axis=0, keepdims=True)
        group_scores.append(m1 + m2)
    gs = jnp.concatenate(group_scores, axis=0)

    gidx = _iota2((N_GROUPS, tm), 0)
    keep = jnp.zeros((N_GROUPS, tm), F32)
    for _ in range(TOPK_GROUPS):
        _, gi = _first_argmax(gs, gidx, N_GROUPS)
        hit = gidx == gi
        keep = jnp.where(hit, 1.0, keep)
        gs = jnp.where(hit, neg, gs)
    keep_e = jnp.concatenate([jnp.broadcast_to(keep[g:g + 1], (per, tm)) for g in range(N_GROUPS)], axis=0)

    cand = jnp.where(keep_e > 0.0, biased, neg)
    picks, affs = [], []
    for _ in range(TOP_K):
        _, ei = _first_argmax(cand, eidx, E)
        hit = eidx == ei
        picks.append(ei)
        affs.append(jnp.sum(jnp.where(hit, scores, 0.0), axis=0, keepdims=True))
        cand = jnp.where(hit, neg, cand)
    aff = jnp.concatenate(affs, axis=0)
    idx_ref[...] = jnp.concatenate(picks, axis=0)
    gate_ref[...] = ROUTED_SCALE * aff / jnp.sum(aff, axis=0, keepdims=True)

    @pl.when(pl.program_id(0) == 0)
    def _():
        carry_ref[...] = jnp.zeros_like(carry_ref)

    sel = jnp.where(cand == neg, jnp.where(keep_e > 0.0, 1.0, 0.0), 0.0)
    upper = (_iota2((tm, tm), 0) < _iota2((tm, tm), 1)).astype(F32).astype(BF16)
    before = carry_ref[:, :1] + jnp.dot(sel.astype(BF16), upper, preferred_element_type=F32)
    pos_ref[...] = jnp.concatenate(
        [jnp.sum(jnp.where(eidx == ei, before, 0.0), axis=0, keepdims=True) for ei in picks], axis=0
    ).astype(jnp.int32)
    total = carry_ref[:, :1] + jnp.sum(sel, axis=1, keepdims=True)
    carry_ref[...] = jnp.broadcast_to(total, carry_ref.shape)
    cnt_ref[...] = jnp.broadcast_to(total, cnt_ref.shape).astype(jnp.int32)


def _router(x, router_w, router_bias, tm=512):
    T, D = x.shape
    E = router_w.shape[1]
    idx, gate, pos, cnt = pl.pallas_call(
        _router_kernel,
        grid=(T // tm,),
        in_specs=[pl.BlockSpec((tm, D), lambda i: (i, 0)),
                  pl.BlockSpec((E, D), lambda i: (0, 0)),
                  pl.BlockSpec((E, 1), lambda i: (0, 0))],
        out_specs=[pl.BlockSpec((TOP_K, tm), lambda i: (0, i)), pl.BlockSpec((TOP_K, tm), lambda i: (0, i)),
                   pl.BlockSpec((TOP_K, tm), lambda i: (0, i)), pl.BlockSpec((E, LANES), lambda i: (0, 0))],
        out_shape=[jax.ShapeDtypeStruct((TOP_K, T), jnp.int32), jax.ShapeDtypeStruct((TOP_K, T), F32),
                   jax.ShapeDtypeStruct((TOP_K, T), jnp.int32), jax.ShapeDtypeStruct((E, LANES), jnp.int32)],
        scratch_shapes=[pltpu.VMEM((E, LANES), F32)],
        compiler_params=_params("arbitrary"),
        name="router",
    )(x, router_w.T, router_bias.reshape(E, 1))
    return idx.T, gate.T, pos.T, cnt[:, 0]


def _dispatch_plan(top_e, pos, counts, rows_per_tile):
    T, K = top_e.shape
    E = counts.shape[0]
    n_tiles = (T * K) // rows_per_tile + E
    tiles = (counts + rows_per_tile - 1) // rows_per_tile
    tile_end = jnp.cumsum(tiles)
    n_used = tile_end[-1]
    row_start = (tile_end - tiles) * rows_per_tile
    dest = jnp.take(row_start, top_e) + pos
    tile_id = jnp.minimum(jnp.arange(n_tiles, dtype=jnp.int32), n_used - 1)
    tile_expert = jnp.minimum(jnp.sum(tile_end[None, :] <= tile_id[:, None], axis=1), E - 1)
    return (tile_expert.astype(jnp.int32), tile_end.astype(jnp.int32), n_used.reshape(1).astype(jnp.int32),
            dest.astype(jnp.int32))


def _dispatch_kernel(tend_ref, dest_ref, x_ref, xs_hbm, zero_ref, zsem, sem, *, slots):
    tm = x_ref.shape[0]
    rows = zero_ref.shape[0]
    n_experts = tend_ref.shape[0]

    @pl.when(pl.program_id(0) == 0)
    def _():
        zero_ref[...] = jnp.zeros_like(zero_ref)

        def last_tile(e):
            end = tend_ref[e]
            start = tend_ref[jnp.maximum(e - 1, 0)]
            start = jnp.where(e == 0, 0, start)
            return end > start, pltpu.make_async_copy(
                zero_ref, xs_hbm.at[pl.ds(jnp.maximum(end - 1, 0) * rows, rows)], zsem)

        def start(e, carry):
            owns, copy = last_tile(e)

            @pl.when(owns)
            def _():
                copy.start()
            return carry

        def wait(e, carry):
            owns, copy = last_tile(e)

            @pl.when(owns)
            def _():
                copy.wait()
            return carry

        def unused_tile(n):
            return pltpu.make_async_copy(zero_ref, xs_hbm.at[pl.ds(n * rows, rows)], zsem)

        def start_unused(n, carry):
            unused_tile(n).start()
            return carry

        def wait_unused(n, carry):
            unused_tile(n).wait()
            return carry

        n_used, n_tiles = tend_ref[n_experts - 1], xs_hbm.shape[0] // rows
        lax.fori_loop(0, n_experts, start, 0)
        lax.fori_loop(n_used, n_tiles, start_unused, 0)
        lax.fori_loop(0, n_experts, wait, 0)
        lax.fori_loop(n_used, n_tiles, wait_unused, 0)

    for k in range(slots):
        def body(t, carry):
            pltpu.make_async_copy(x_ref.at[pl.ds(t, 1)], xs_hbm.at[pl.ds(dest_ref[0, k * tm + t], 1)], sem).start()
            return carry
        lax.fori_loop(0, tm, body, 0, unroll=8)
    for k in range(slots):
        pltpu.make_async_copy(x_ref, xs_hbm.at[pl.ds(0, tm)], sem).wait()


def _dispatch(x3p, dest, tile_end, n_tiles, rows_per_tile, tm=512):
    T, half = x3p.shape
    K = dest.shape[1]
    steps = T // tm
    d3 = jnp.swapaxes(dest.reshape(steps, tm, K), 1, 2).reshape(steps, 1, K * tm)
    return pl.pallas_call(
        functools.partial(_dispatch_kernel, slots=K),
        grid_spec=pltpu.PrefetchScalarGridSpec(
            num_scalar_prefetch=1,
            grid=(steps,),
            in_specs=[pl.BlockSpec((None, 1, K * tm), lambda i, te: (i, 0, 0), memory_space=pltpu.SMEM),
                      pl.BlockSpec((tm, half), lambda i, te: (i, 0))],
            out_specs=pl.BlockSpec(memory_space=pl.ANY),
            scratch_shapes=[pltpu.VMEM((rows_per_tile, half), jnp.int32),
                            pltpu.SemaphoreType.DMA(()), pltpu.SemaphoreType.DMA(())]),
        out_shape=jax.ShapeDtypeStruct((n_tiles * rows_per_tile, half), jnp.int32),
        compiler_params=_params("arbitrary"),
        name="dispatch",
    )(tile_end, d3, x3p)


def _sparse_experts_kernel(te_ref, nu_ref, xs_ref, w1_ref, w3_ref, w2_ref, y_ref, wb1, wb3, wb2):
    n = pl.program_id(0)

    @pl.when(n < nu_ref[0])
    def _():
        @pl.when((n == 0) | (te_ref[n] != te_ref[jnp.maximum(n - 1, 0)]))
        def _():
            wb1[...] = w1_ref[0].astype(BF16)
            wb3[...] = w3_ref[0].astype(BF16)
            wb2[...] = w2_ref[0].astype(BF16)

        x = _unpack_rows(xs_ref[...]).astype(BF16)
        h1 = jnp.dot(x, wb1[...], preferred_element_type=F32)
        h3 = jnp.dot(x, wb3[...], preferred_element_type=F32)
        hidden = (h1 * _sigmoid(h1) * h3).astype(BF16)
        y_ref[...] = _pack_rows(jnp.dot(hidden, wb2[...], preferred_element_type=F32))

    @pl.when(n >= nu_ref[0])
    def _():
        y_ref[...] = jnp.zeros_like(y_ref)


def _sparse_experts(xs, tile_expert, n_used, w1, w3, w2, layer, rows):
    _, _, D, Hd = w1.shape
    n_tiles = tile_expert.shape[0]
    wmap = lambda n, te, nu: (layer, te[n], 0, 0)
    return pl.pallas_call(
        _sparse_experts_kernel,
        grid_spec=pltpu.PrefetchScalarGridSpec(
            num_scalar_prefetch=2,
            grid=(n_tiles,),
            in_specs=[pl.BlockSpec((rows, D // 2), lambda n, te, nu: (jnp.minimum(n, nu[0] - 1), 0)),
                      pl.BlockSpec((None, 1, D, Hd), wmap),
                      pl.BlockSpec((None, 1, D, Hd), wmap),
                      pl.BlockSpec((None, 1, Hd, D), wmap)],
            out_specs=pl.BlockSpec((rows, D // 2), lambda n, te, nu: (n, 0)),
            scratch_shapes=[pltpu.VMEM((D, Hd), BF16), pltpu.VMEM((D, Hd), BF16), pltpu.VMEM((Hd, D), BF16)]),
        out_shape=jax.ShapeDtypeStruct(xs.shape, jnp.int32),
        compiler_params=_params("arbitrary"),
        name="sparse_experts",
    )(tile_expert, n_used, xs, w1, w3, w2)


def _combine_kernel(dest_ref, dest_next_ref, gate_ref, y_hbm, o_ref, buf, sem):
    i = pl.program_id(0)
    steps = pl.num_programs(0)
    K, tm = buf.shape[1:3]
    slot = i % 2

    def fetch(d_ref, s):
        for k in range(K):
            def body(t, carry):
                pltpu.make_async_copy(y_hbm.at[pl.ds(d_ref[0, k * tm + t], 1)], buf.at[s, k, pl.ds(t, 1)],
                                      sem.at[s]).start()
                return carry
            lax.fori_loop(0, tm, body, 0, unroll=8)

    @pl.when(i == 0)
    def _():
        fetch(dest_ref, 0)

    @pl.when(i + 1 < steps)
    def _():
        fetch(dest_next_ref, 1 - slot)

    for k in range(K):
        pltpu.make_async_copy(y_hbm.at[pl.ds(0, tm)], buf.at[slot, k], sem.at[slot]).wait()
    gate = gate_ref[...]
    acc = jnp.zeros(o_ref.shape, F32)
    for k in range(K):
        acc = acc + gate[:, k:k + 1] * _unpack_rows(buf[slot, k])
    o_ref[...] = acc


def _combine(y, dest, gate, tm=256):
    T, K = dest.shape
    D = 2 * y.shape[1]
    steps = T // tm
    d3 = jnp.swapaxes(dest.reshape(steps, tm, K), 1, 2).reshape(steps, 1, K * tm)
    return pl.pallas_call(
        _combine_kernel,
        grid=(steps,),
        in_specs=[pl.BlockSpec((None, 1, K * tm), lambda i: (i, 0, 0), memory_space=pltpu.SMEM),
                  pl.BlockSpec((None, 1, K * tm), lambda i: (jnp.minimum(i + 1, steps - 1), 0, 0),
                               memory_space=pltpu.SMEM),
                  pl.BlockSpec((tm, K), lambda i: (i, 0)),
                  pl.BlockSpec(memory_space=pl.ANY)],
        out_specs=pl.BlockSpec((tm, D), lambda i: (i, 0)),
        out_shape=jax.ShapeDtypeStruct((T, D), F32),
        scratch_shapes=[pltpu.VMEM((2, K, tm, D // 2), jnp.int32), pltpu.SemaphoreType.DMA((2,))],
        compiler_params=_params("arbitrary"),
        name="combine",
    )(d3, d3, gate, y)


def _even_mixer(x2, xb, B, S, ln_g, ln_b, w_in, conv_w, conv_b, gate_b, ml_gn_w, rw_mu, rw_w0, rw_w2,
                rw_a0, rw_a2, rw_g2, rw_kk, rw_ka, rw_rk, rw_gn_w, rw_gn_b, w_out, j):
    D = x2.shape[1]
    H = ML_HEADS
    ML = ml_gn_w.shape[0]
    RW = rw_w0.shape[0]
    LO = RW_LORA_W + RW_LORA_A + RW_LORA_G
    ml_main = 3 * ML
    ml_cols = ml_main + 2 * H
    pad_to = 512
    used = ml_main + 3 * RW + LO + LANES
    total = -(-used // pad_to) * pad_to
    w_perm = jnp.concatenate([
        w_in[:, :ml_main], w_in[:, ml_cols:ml_cols + 3 * RW + LO], w_in[:, ml_main:ml_cols],
        jnp.zeros((D, total - used + LANES - 2 * H), w_in.dtype)], axis=1).astype(BF16)
    z3 = _mm(xb, w_perm[None], 0, BF16, 1024, pad_to).reshape(B, S, total)
    gate_col0 = ml_main + 3 * RW + LO
    L = ML_CHUNK
    gates_t = jnp.swapaxes(
        z3[:, :, gate_col0:gate_col0 + 2 * H].astype(F32).reshape(B, S // L, L, 2 * H), 2, 3)
    h_ml = _mlstm(z3, gates_t, conv_w, conv_b, gate_b, ml_gn_w,
                  qk_blk=0, v_blk=1, o_blk=2, gate_blk=gate_col0 // LANES)
    h_rw = _rwkv(z3, rw_mu, rw_w0, rw_w2, rw_a0, rw_a2, rw_g2, rw_kk, rw_ka, rw_rk, rw_gn_w, rw_gn_b,
                 col0=ml_main, lora_col0=ml_main + 3 * RW, width=RW)
    return _mm_res_ln([h_ml.reshape(B * S, ML), h_rw.reshape(B * S, RW)], w_out, j, x2, ln_g, ln_b)


def _odd_mixer(x2, xb, B, S, positions, ln_g, ln_b, w_in, gn_w, gn_b, w_out, j):
    z3 = _mm(xb, w_in, j, BF16, 1024, 512).reshape(B, S, w_in.shape[2])
    h = _retention(z3, positions, gn_w, gn_b)
    return _mm_res_ln([h.reshape(B * S, -1)], w_out, j, x2, ln_g, ln_b)


def _cross_attention(x2, xb, B, S, memb, ln_g, ln_b, wq, wkv, wo, layer):
    D = x2.shape[1]
    q = _mm(xb, wq, layer, BF16, 1024, 512)
    kv = _mm(memb, wkv, layer, BF16, memb.shape[0], 512)
    o = _xattn(q.reshape(B, S, D), kv.reshape(B, -1, 2 * D))
    return _mm_res_ln([o.reshape(B * S, D)], wo, layer, x2, ln_g, ln_b, packed=True)


def _moe(x2, xb, x3p, ln_g, ln_b, router_w, router_bias, w1, w3, w2, sw1, sw3, sw2, layer):
    rows = MOE_ROWS_PER_TILE
    top_e, gate, pos, counts = _router(x2, router_w, router_bias)
    tile_expert, tile_end, n_used, dest = _dispatch_plan(top_e, pos, counts, rows)
    xs = _dispatch(x3p, dest, tile_end, tile_expert.shape[0], rows)
    routed = _combine(_sparse_experts(xs, tile_expert, n_used, w1, w3, w2, layer, rows), dest, gate)
    hs = _glu(xb, sw1, sw3, layer)
    return _mm_res_ln([hs], sw2, layer, x2, ln_g, ln_b, add=routed)


def kernel(x, mem, positions, ln_g, ln_b, xa_wq, xa_wkv, xa_wo, router_w, router_bias, moe_w1, moe_w3, moe_w2, sh_w1, sh_w3, sh_w2, ev_w_in, ml_conv_w, ml_conv_b, ml_gate_b, ml_gn_w, rw_mu, rw_w0, rw_w2, rw_a0, rw_a2, rw_g2, rw_kk, rw_ka, rw_rk, rw_gn_w, rw_gn_b, ev_w_out, od_w_in, ret_gn_w, ret_gn_b, od_w_out):
    B, S, D = x.shape
    x2 = x.reshape(B * S, D)
    xb = x2.astype(BF16)
    memb = mem.reshape(-1, D).astype(BF16)
    (xa_wq, xa_wkv, xa_wo, sh_w1, sh_w3, sh_w2, ev_w_out, od_w_in, od_w_out) = (
        w.astype(BF16) for w in (xa_wq, xa_wkv, xa_wo, sh_w1, sh_w3, sh_w2, ev_w_out, od_w_in, od_w_out))
    for layer in range(ln_g.shape[0]):
        j = layer // 2
        if layer % 2 == 0:
            x2, xb = _even_mixer(x2, xb, B, S, ln_g[layer, 0], ln_b[layer, 0], ev_w_in[j], ml_conv_w[j],
                                 ml_conv_b[j], ml_gate_b[j], ml_gn_w[j], rw_mu[j], rw_w0[j], rw_w2[j],
                                 rw_a0[j], rw_a2[j], rw_g2[j], rw_kk[j], rw_ka[j], rw_rk[j],
                                 rw_gn_w[j], rw_gn_b[j], ev_w_out, j)
        else:
            x2, xb = _odd_mixer(x2, xb, B, S, positions, ln_g[layer, 0], ln_b[layer, 0], od_w_in,
                                ret_gn_w[j], ret_gn_b[j], od_w_out, j)
        x2, xb, x3p = _cross_attention(x2, xb, B, S, memb, ln_g[layer, 1], ln_b[layer, 1], xa_wq, xa_wkv,
                                       xa_wo, layer)
        x2, xb = _moe(x2, xb, x3p, ln_g[layer, 2], ln_b[layer, 2], router_w[layer], router_bias[layer],
                      moe_w1, moe_w3, moe_w2, sh_w1, sh_w3, sh_w2, layer)
    return x2.reshape(B, S, D)
```

```python
import functools

import numpy as np
import jax
import jax.numpy as jnp
from jax import lax
from jax.experimental import pallas as pl
from jax.experimental.pallas import tpu as pltpu

F32 = jnp.float32
BF16 = jnp.bfloat16
HI = lax.Precision.HIGHEST

DEPTH = 4
ALPHA = (2.0 * DEPTH) ** 0.25
LN_EPS = 1e-5
NEG_BIG = -1e30
CONV_K = 4
ML_HEADS = 4
RW_HEAD = 64
RW_LORA_W = 64
RW_LORA_A = 64
RW_LORA_G = 128
RW_GN_EPS = 64e-5
RET_HEADS = 8
ROPE_BASE = 10000.0
XA_HEADS = 4
N_EXPERTS = 64
TOP_K = 8
N_GROUPS = 8
TOPK_GROUPS = 4
ROUTED_SCALE = 2.5

ML_CHUNK = 64
RW_CHUNK = 64
RET_CHUNK = 256
MOE_ROWS_PER_TILE = 512

V7X_VMEM_BYTES = 64 * 2 ** 20
VMEM_LIMIT = (V7X_VMEM_BYTES * 3) // 4
LANES = 128
V7X_MXU_DIM = 256


def _params(*sem):
    return pltpu.CompilerParams(dimension_semantics=sem, vmem_limit_bytes=VMEM_LIMIT)


def _bdot(a, b):
    return jnp.dot(a.astype(BF16), b.astype(BF16), preferred_element_type=F32)


def _bdot_nt(a, b):
    return lax.dot_general(a.astype(BF16), b.astype(BF16), (((1,), (1,)), ((), ())),
                           preferred_element_type=F32)


def _bdot_tn(a, b):
    return lax.dot_general(a.astype(BF16), b.astype(BF16), (((0,), (0,)), ((), ())),
                           preferred_element_type=F32)


def _sigmoid(x):
    return 1.0 / (1.0 + jnp.exp(-x))


def _softplus(x):
    return jnp.maximum(x, 0.0) + jnp.log1p(jnp.exp(-jnp.abs(x)))


def _iota2(shape, dim):
    return lax.broadcasted_iota(jnp.int32, shape, dim)


def _mm_kernel(x_ref, w_ref, o_ref):
    o_ref[...] = _bdot(x_ref[...], w_ref[...]).astype(o_ref.dtype)


def _mm(x, w, layer, out_dtype, tm, tn):
    M, K = x.shape
    N = w.shape[2]
    assert M % tm == 0 and N % tn == 0
    return pl.pallas_call(
        _mm_kernel,
        grid=(M // tm, N // tn),
        in_specs=[pl.BlockSpec((tm, K), lambda i, j: (i, 0)),
                  pl.BlockSpec((None, K, tn), lambda i, j: (layer, 0, j))],
        out_specs=pl.BlockSpec((tm, tn), lambda i, j: (i, j)),
        out_shape=jax.ShapeDtypeStruct((M, N), out_dtype),
        compiler_params=_params("parallel", "parallel"),
        name="mm",
    )(x, w)


def _layer_norm_rows(y, g, b):
    mu = jnp.mean(y, axis=-1, keepdims=True)
    d = y - mu
    var = jnp.mean(d * d, axis=-1, keepdims=True)
    return d * lax.rsqrt(var + LN_EPS) * g + b


def _pack_rows(v):
    half = v.shape[1] // 2
    bits = lax.bitcast_convert_type(v.astype(BF16).astype(F32), jnp.int32)
    return bits[:, half:] | lax.shift_right_logical(bits[:, :half], 16)


def _unpack_rows(w):
    return jnp.concatenate([lax.bitcast_convert_type(lax.shift_left(w, 16), F32),
                            lax.bitcast_convert_type(w & jnp.int32(-65536), F32)], axis=-1)


def _mm_res_ln_kernel(*refs, n_a, has_add, packed):
    a_refs, (w_ref, res_ref), rest = refs[:n_a], refs[n_a:n_a + 2], refs[n_a + 2:]
    add_ref = rest[0] if has_add else None
    n_out = 3 if packed else 2
    g_ref, b_ref = rest[-n_out - 2:-n_out]
    o_ref, ob_ref = rest[-n_out:][:2]
    acc, k0 = None, 0
    for a_ref in a_refs:
        part = _bdot(a_ref[...], w_ref[k0:k0 + a_ref.shape[1], :])
        acc = part if acc is None else acc + part
        k0 += a_ref.shape[1]
    if has_add:
        acc = add_ref[...] + acc
    out = _layer_norm_rows(ALPHA * res_ref[...] + acc, g_ref[...], b_ref[...])
    o_ref[...] = out
    ob_ref[...] = out.astype(ob_ref.dtype)
    if packed:
        rest[-1][...] = _pack_rows(out)


def _mm_res_ln(a_parts, w, layer, res, g, b, add=None, packed=False, tm=256):
    M = res.shape[0]
    K, N = w.shape[1:]
    assert sum(a.shape[1] for a in a_parts) == K
    row = lambda i: (i, 0)
    fixed = lambda i: (0, 0)
    in_specs = [pl.BlockSpec((tm, a.shape[1]), row) for a in a_parts]
    in_specs += [pl.BlockSpec((None, K, N), lambda i: (layer, 0, 0)), pl.BlockSpec((tm, N), row)]
    args = [*a_parts, w, res]
    if add is not None:
        in_specs.append(pl.BlockSpec((tm, N), row))
        args.append(add)
    in_specs += [pl.BlockSpec((1, N), fixed), pl.BlockSpec((1, N), fixed)]
    args += [g.reshape(1, N), b.reshape(1, N)]
    out_specs = [pl.BlockSpec((tm, N), row), pl.BlockSpec((tm, N), row)]
    out_shape = [jax.ShapeDtypeStruct((M, N), F32), jax.ShapeDtypeStruct((M, N), BF16)]
    if packed:
        out_specs.append(pl.BlockSpec((tm, N // 2), row))
        out_shape.append(jax.ShapeDtypeStruct((M, N // 2), jnp.int32))
    return pl.pallas_call(
        functools.partial(_mm_res_ln_kernel, n_a=len(a_parts), has_add=add is not None, packed=packed),
        grid=(M // tm,),
        in_specs=in_specs,
        out_specs=out_specs,
        out_shape=out_shape,
        compiler_params=_params("parallel"),
        name="mm_res_ln",
    )(*args)


def _glu_kernel(x_ref, w1_ref, w3_ref, o_ref):
    x = x_ref[...]
    h1 = jnp.dot(x, w1_ref[...], preferred_element_type=F32)
    h3 = jnp.dot(x, w3_ref[...], preferred_element_type=F32)
    o_ref[...] = (h1 * _sigmoid(h1) * h3).astype(o_ref.dtype)


def _glu(xb, w1, w3, layer, tm=512):
    M, K = xb.shape
    N = w1.shape[2]
    wspec = pl.BlockSpec((None, K, N), lambda i: (layer, 0, 0))
    return pl.pallas_call(
        _glu_kernel,
        grid=(M // tm,),
        in_specs=[pl.BlockSpec((tm, K), lambda i: (i, 0)), wspec, wspec],
        out_specs=pl.BlockSpec((tm, N), lambda i: (i, 0)),
        out_shape=jax.ShapeDtypeStruct((M, N), BF16),
        compiler_params=_params("parallel"),
        name="glu",
    )(xb, w1, w3)


def _log_sigmoid(x):
    return jnp.minimum(x, 0.0) - jnp.log1p(jnp.exp(-jnp.abs(x)))


def _split_bf16(x, terms):
    pieces = []
    for _ in range(terms):
        p = x.astype(BF16)
        pieces.append(p)
        x = x - p.astype(F32)
    return pieces


def _mlstm_kernel(qk_ref, v_ref, o_ref, gc_ref, gr_ref, cw_ref, cb_ref, gbc_ref, gbr_ref, gn_ref,
                  out_ref, prev_ref, c_ref, n_ref, m_ref, *, heads, dk, dv):
    nb, L, _ = qk_ref.shape
    qw = heads * dk

    @pl.when(pl.program_id(0) == 0)
    def _():
        prev_ref[...] = jnp.zeros_like(prev_ref)
        c_ref[...] = jnp.zeros_like(c_ref)
        n_ref[...] = jnp.zeros_like(n_ref)
        m_ref[...] = jnp.full_like(m_ref, NEG_BIG)

    row = _iota2((L, 1), 0)
    qk = []
    for b in range(nb):
        cur = qk_ref[b].astype(F32)
        prev = prev_ref[b]
        acc = cb_ref[...] + cw_ref[CONV_K - 1:CONV_K, :] * cur
        for j in range(CONV_K - 1):
            s = CONV_K - 1 - j
            shifted = jnp.where(row < s, pltpu.roll(prev, s, 0), pltpu.roll(cur, s, 0))
            acc = acc + cw_ref[j:j + 1, :] * shifted
        prev_ref[b] = cur
        qk.append(acc * _sigmoid(acc))

    causal = _iota2((L, L), 0) >= _iota2((L, L), 1)
    tri = causal.astype(F32).astype(BF16)
    g_col = [gc_ref[b].astype(F32) + gbc_ref[...] for b in range(nb)]
    pieces = _split_bf16(jnp.concatenate([_log_sigmoid(g) for g in g_col], axis=1), 3)
    lanes = nb * g_col[0].shape[1]
    b_col = jnp.dot(tri, jnp.concatenate(pieces, axis=1), preferred_element_type=F32)
    b_col = b_col[:, :lanes] + b_col[:, lanes:2 * lanes] + b_col[:, 2 * lanes:]
    g_row = [gr_ref[b, 0] + gbr_ref[...] for b in range(nb)]
    pieces = _split_bf16(jnp.concatenate([_log_sigmoid(g) for g in g_row], axis=0), 3)
    rows = nb * 2 * heads
    b_row = lax.dot_general(jnp.concatenate(pieces, axis=0), tri, (((1,), (1,)), ((), ())),
                            preferred_element_type=F32)
    b_row = b_row[:rows] + b_row[rows:2 * rows] + b_row[2 * rows:]

    def per_problem(f):
        return jnp.stack([f(b, h) for b in range(nb) for h in range(heads)])

    q = per_problem(lambda b, h: qk[b][:, h * dk:(h + 1) * dk])
    k = per_problem(lambda b, h: qk[b][:, qw + h * dk:qw + (h + 1) * dk]) * dk ** -0.5
    v = per_problem(lambda b, h: v_ref[b, :, h * dv:(h + 1) * dv])
    gcw = g_col[0].shape[1]
    bc = per_problem(lambda b, h: b_col[:, b * gcw + heads + h:b * gcw + heads + h + 1])
    ic = per_problem(lambda b, h: g_col[b][:, h:h + 1])
    br = per_problem(lambda b, h: b_row[b * 2 * heads + heads + h:b * 2 * heads + heads + h + 1])
    ir = per_problem(lambda b, h: g_row[b][h:h + 1])
    b_last = bc[:, L - 1:L]
    c_prev = c_ref[...]
    n_prev = n_ref[...]
    m_prev = m_ref[...][:, :, 0:1]

    def bmm(x, y, spec):
        return jnp.einsum(spec, x.astype(BF16), y.astype(BF16), preferred_element_type=F32)

    log_d = jnp.where(causal, bc - br + ir, -jnp.inf)
    g_inter = bc + m_prev
    m_t = jnp.maximum(jnp.max(log_d, axis=-1, keepdims=True), g_inter)
    p = jnp.exp(log_d - m_t) * bmm(q, k, "gtk,gsk->gts")
    e_inter = jnp.exp(g_inter - m_t)
    num = bmm(p, v, "gts,gsv->gtv") + e_inter * bmm(q, c_prev, "gtk,gkv->gtv")
    den = jnp.sum(p, axis=-1, keepdims=True) + e_inter * jnp.sum(q * n_prev, axis=-1, keepdims=True)
    hh = num / jnp.maximum(jnp.abs(den), jnp.exp(-m_t))

    w_end = b_last - bc + ic
    m_loc = jnp.max(w_end, axis=1, keepdims=True)
    e_end = jnp.exp(w_end - m_loc)
    ke = k * e_end
    m_new = jnp.maximum(b_last + m_prev, m_loc)
    s_prev = jnp.exp(b_last + m_prev - m_new)
    s_new = jnp.exp(m_loc - m_new)
    c_ref[...] = s_prev * c_prev + s_new * bmm(ke, v, "gsk,gsv->gkv")
    n_ref[...] = s_prev * n_prev + s_new * jnp.sum(ke, axis=1, keepdims=True)
    m_ref[...] = jnp.broadcast_to(m_new, m_ref.shape)

    mu = jnp.mean(hh, axis=-1, keepdims=True)
    d = hh - mu
    var = jnp.mean(d * d, axis=-1, keepdims=True)
    y = d * lax.rsqrt(var + 1e-6)
    for b in range(nb):
        for h in range(heads):
            sl = slice(h * dv, (h + 1) * dv)
            gate = _sigmoid(o_ref[b, :, sl].astype(F32))
            out_ref[b, :, sl] = (y[b * heads + h] * gn_ref[:, sl] * gate).astype(out_ref.dtype)


def _mlstm(z3, gates_t, conv_w, conv_b, gate_b, gn_w, *, qk_blk, v_blk, o_blk, gate_blk):
    B, S, _ = z3.shape
    H = ML_HEADS
    W = gn_w.shape[0]
    dv = W // H
    dk = conv_w.shape[1] // (2 * H)
    assert conv_w.shape[1] == W, "q|k block and v block share one column-block width"
    L = ML_CHUNK
    gb_col = jnp.zeros((1, LANES), F32).at[0, :2 * H].set(gate_b.reshape(-1))
    gb_row = gate_b.reshape(2 * H, 1)
    fixed = lambda c: (0, 0)
    return pl.pallas_call(
        functools.partial(_mlstm_kernel, heads=H, dk=dk, dv=dv),
        grid=(S // L,),
        in_specs=[pl.BlockSpec((B, L, W), lambda c: (0, c, qk_blk)),
                  pl.BlockSpec((B, L, W), lambda c: (0, c, v_blk)),
                  pl.BlockSpec((B, L, W), lambda c: (0, c, o_blk)),
                  pl.BlockSpec((B, L, LANES), lambda c: (0, c, gate_blk)),
                  pl.BlockSpec((B, 1, 2 * H, L), lambda c: (0, c, 0, 0)),
                  pl.BlockSpec((CONV_K, W), fixed),
                  pl.BlockSpec((1, W), fixed),
                  pl.BlockSpec((1, LANES), fixed),
                  pl.BlockSpec((2 * H, 1), fixed),
                  pl.BlockSpec((1, W), fixed)],
        out_specs=pl.BlockSpec((B, L, W), lambda c: (0, c, 0)),
        out_shape=jax.ShapeDtypeStruct((B, S, W), BF16),
        scratch_shapes=[pltpu.VMEM((B, L, W), F32),
                        pltpu.VMEM((B * H, dk, dv), F32),
                        pltpu.VMEM((B * H, 1, dk), F32),
                        pltpu.VMEM((B * H, 1, LANES), F32)],
        compiler_params=_params("arbitrary"),
        name="mlstm",
    )(z3, z3, z3, z3, gates_t, conv_w, conv_b.reshape(1, W), gb_col, gb_row, gn_w.reshape(1, W))


def _unit_lower_inverse(a, t_idx, i_idx, block_diag, matmul):
    n = a.shape[-2]
    eye = (t_idx == i_idx).astype(F32)
    t = None
    s = 1
    while s < n:
        pair = (t_idx // (2 * s)) == (i_idx // (2 * s))
        off = pair & ((t_idx // s) % 2 == 1) & ((i_idx // s) % 2 == 0)
        q = jnp.where(off, a, 0.0)
        t = eye - q if t is None else t - matmul(matmul(t, block_diag(q)), block_diag(t))
        s *= 2
    return t


def _rwkv_kernel(r_ref, k_ref, v_ref, l_ref, mur_ref, muk_ref, muv_ref, mul_ref, w0_ref, w2_ref,
                 a0_ref, a2_ref, g2_ref, kk_ref, ka_ref, rk_ref, gnw_ref, gnb_ref, out_ref,
                 pr_ref, pk_ref, pv_ref, plo_ref, s_ref, *, hd, gw):
    nb, L, W = r_ref.shape
    ng = W // gw
    hpg = gw // hd
    assert hpg * L == gw

    @pl.when(pl.program_id(1) == 0)
    def _():
        pr_ref[...] = jnp.zeros_like(pr_ref)
        pk_ref[...] = jnp.zeros_like(pk_ref)
        pv_ref[...] = jnp.zeros_like(pv_ref)
        plo_ref[...] = jnp.zeros_like(plo_ref)
        s_ref[...] = jnp.zeros_like(s_ref)

    R = nb * L
    first = _iota2((L, 1), 0) == 0
    same_head = (_iota2((gw, gw), 0) // hd) == (_iota2((gw, gw), 1) // hd)
    zero = jnp.zeros((), BF16)
    same_head_b = same_head.astype(F32).astype(BF16)
    row, col = _iota2((R, R), 0), _iota2((R, R), 1)
    tri = ((row // L == col // L) & (row >= col)).astype(F32).astype(BF16)
    t_idx = _iota2((L, gw), 0)
    i_idx = _iota2((L, gw), 1) % L
    strict = t_idx > i_idx
    lower = t_idx >= i_idx

    def shift_lerp(x_ref, p_ref, mu_ref):
        parts = []
        for b in range(nb):
            cur = x_ref[b].astype(F32)
            shifted = jnp.where(first, p_ref[b], pltpu.roll(cur, 1, 0))
            p_ref[b] = cur[L - 1:L, :]
            parts.append(cur + mu_ref[...] * (shifted - cur))
        return jnp.concatenate(parts, axis=0)

    def head_sums(xs):
        stacked = jnp.concatenate([x[:, i * gw:(i + 1) * gw] for x in xs for i in range(ng)], axis=0)
        n = stacked.shape[0]
        s = jnp.dot(jnp.concatenate(_split_bf16(stacked, 2), axis=0), same_head_b, preferred_element_type=F32)
        s = s[:n] + s[n:]
        return [jnp.concatenate([s[(j * ng + i) * R:(j * ng + i + 1) * R] for i in range(ng)], axis=1)
                for j in range(len(xs))]

    r = shift_lerp(r_ref, pr_ref, mur_ref)
    k = shift_lerp(k_ref, pk_ref, muk_ref)
    v = shift_lerp(v_ref, pv_ref, muv_ref)
    lo = shift_lerp(l_ref, plo_ref, mul_ref)
    xw = lo[:, :RW_LORA_W]
    xa = lo[:, RW_LORA_W:RW_LORA_W + RW_LORA_A]
    xg = lo[:, RW_LORA_W + RW_LORA_A:]

    w_log = -jnp.exp(-_softplus(-(w0_ref[...] + _bdot(jnp.tanh(xw), w2_ref[...]))) - 0.5)
    a = _sigmoid(a0_ref[...] + _bdot(xa, a2_ref[...]))
    g = _bdot(_sigmoid(xg), g2_ref[...])

    kk = k * kk_ref[...]
    k = k * (1.0 + (a - 1.0) * ka_ref[...])
    kk_sq, rk_sum = head_sums([kk * kk, r * k * rk_ref[...]])
    kk = kk * lax.rsqrt(jnp.maximum(kk_sq, 1e-24))
    bv = kk * a

    c3 = jnp.dot(tri, jnp.concatenate(_split_bf16(w_log, 3), axis=1), preferred_element_type=F32)
    c = c3[:, :W] + c3[:, W:2 * W] + c3[:, 2 * W:]
    c_last = jnp.concatenate([jnp.broadcast_to(c[(b + 1) * L - 1:(b + 1) * L], (L, W)) for b in range(nb)],
                             axis=0)
    e_end = jnp.exp(c_last - c)
    e_neg = jnp.exp(-c)
    kt = (kk * jnp.exp(c - w_log)).astype(BF16)
    rt = (r * jnp.exp(c)).astype(BF16)
    kh = (k * e_neg).astype(BF16)
    bh = (bv * e_neg).astype(BF16)
    kp = (k * e_end).astype(BF16)
    bp = (bv * e_end).astype(BF16)
    vb = v.astype(BF16)
    p_last = jnp.exp(c_last)

    def groups(x):
        return jnp.stack([x[b * L:(b + 1) * L, i * gw:(i + 1) * gw] for b in range(nb) for i in range(ng)])

    def block_diag(x):
        return jnp.where(same_head, jnp.concatenate([x.astype(BF16)] * hpg, axis=1), zero)

    def bmm(x, y):
        return jnp.einsum("gmk,gkn->gmn", x.astype(BF16), y.astype(BF16), preferred_element_type=F32)

    def bmm_nt(x, y):
        return jnp.einsum("gmk,gnk->gmn", x.astype(BF16), y.astype(BF16), preferred_element_type=F32)

    lhs = jnp.concatenate([groups(kt), groups(rt)], axis=1)
    m_k = bmm_nt(lhs, block_diag(groups(kh)))
    m_b = bmm_nt(lhs, block_diag(groups(bh)))
    a_kk = jnp.where(strict, m_k[:, :L], 0.0)
    a_kb = jnp.where(strict, m_b[:, :L], 0.0)
    a_rk = jnp.where(lower, m_k[:, L:], 0.0)
    a_rb = jnp.where(lower, m_b[:, L:], 0.0)
    t_inv = _unit_lower_inverse(a_kb, t_idx, i_idx, block_diag, bmm)
    st = s_ref[...]
    gs = bmm_nt(lhs, st)
    vg = groups(vb)
    v_bd = block_diag(vg)
    u = bmm(t_inv, block_diag(gs[:, :L] + bmm(a_kk, v_bd)))
    yg = gs[:, L:] + bmm(jnp.concatenate([a_rk, -a_rb], axis=2), jnp.concatenate([v_bd, block_diag(u)], axis=1))
    vu = jnp.concatenate([vg, (-u).astype(BF16)], axis=1)
    kb = jnp.concatenate([groups(kp), groups(bp)], axis=1)
    pg = groups(p_last)[:, :1]
    outer = jnp.einsum("gtv,gtk->gvk", vu, kb, preferred_element_type=F32)
    s_ref[...] = st * pg + jnp.where(same_head, outer, 0.0)
    ys = [jnp.concatenate([yg[b * ng + i] for i in range(ng)], axis=1) for b in range(nb)]

    y = jnp.concatenate(ys, axis=0)
    mu = head_sums([y])[0] * (1.0 / hd)
    d = y - mu
    var = head_sums([d * d])[0] * (1.0 / hd)
    yn = d * lax.rsqrt(var + RW_GN_EPS) * gnw_ref[...] + gnb_ref[...]
    out = ((yn + rk_sum * v) * g).astype(out_ref.dtype)
    for b in range(nb):
        out_ref[b] = out[b * L:(b + 1) * L]


def _rwkv(z3, mu, w0, w2, a0, a2, g2, k_k, k_a, r_k, gn_w, gn_b, *, col0, lora_col0, width):
    B, S, _ = z3.shape
    RW = w0.shape[0]
    W = width
    G = RW // W
    L = RW_CHUNK
    LO = RW_LORA_W + RW_LORA_A + RW_LORA_G
    assert col0 % W == 0 and RW % W == 0 and lora_col0 % LO == 0 and W % V7X_MXU_DIM == 0
    zc = lambda off: (lambda g, c: (0, c, off // W + g))
    vec = lambda g, c: (0, g)
    row = lambda x: x.reshape(1, -1)
    return pl.pallas_call(
        functools.partial(_rwkv_kernel, hd=RW_HEAD, gw=V7X_MXU_DIM),
        grid=(G, S // L),
        in_specs=[pl.BlockSpec((B, L, W), zc(col0)),
                  pl.BlockSpec((B, L, W), zc(col0 + RW)),
                  pl.BlockSpec((B, L, W), zc(col0 + 2 * RW)),
                  pl.BlockSpec((B, L, LO), lambda g, c: (0, c, lora_col0 // LO)),
                  pl.BlockSpec((1, W), vec), pl.BlockSpec((1, W), vec), pl.BlockSpec((1, W), vec),
                  pl.BlockSpec((1, LO), lambda g, c: (0, 0)),
                  pl.BlockSpec((1, W), vec),
                  pl.BlockSpec((RW_LORA_W, W), vec),
                  pl.BlockSpec((1, W), vec),
                  pl.BlockSpec((RW_LORA_A, W), vec),
                  pl.BlockSpec((RW_LORA_G, W), vec),
                  pl.BlockSpec((1, W), vec), pl.BlockSpec((1, W), vec), pl.BlockSpec((1, W), vec),
                  pl.BlockSpec((1, W), vec), pl.BlockSpec((1, W), vec)],
        out_specs=pl.BlockSpec((B, L, W), lambda g, c: (0, c, g)),
        out_shape=jax.ShapeDtypeStruct((B, S, RW), BF16),
        scratch_shapes=[pltpu.VMEM((B, 1, W), F32), pltpu.VMEM((B, 1, W), F32), pltpu.VMEM((B, 1, W), F32),
                        pltpu.VMEM((B, 1, LO), F32),
                        pltpu.VMEM((B * (W // V7X_MXU_DIM), V7X_MXU_DIM, V7X_MXU_DIM), F32)],
        compiler_params=_params("parallel", "arbitrary"),
        name="rwkv7",
    )(z3, z3, z3, z3, row(mu[:RW]), row(mu[RW:2 * RW]), row(mu[2 * RW:3 * RW]), row(mu[3 * RW:]),
      row(w0), w2.astype(BF16), row(a0), a2.astype(BF16), g2.astype(BF16), row(k_k), row(k_a),
      row(r_k), row(gn_w), row(gn_b))


def _retention_kernel(q_ref, k_ref, v_ref, g_ref, pos_ref, inv_ref, dec_ref, gnw_ref, gnb_ref,
                      out_ref, r_ref, *, heads, log_gamma):
    L = q_ref.shape[1]
    dk = q_ref.shape[2] // heads
    dv = v_ref.shape[2] // heads
    half = dk // 2

    @pl.when(pl.program_id(1) == 0)
    def _():
        r_ref[...] = jnp.zeros_like(r_ref)

    ang = pos_ref[0] * inv_ref[...]
    cos = jnp.cos(ang)
    sin = jnp.sin(ang)
    t_in = _iota2((L, 1), 0).astype(F32)

    def rot(t):
        t1, t2 = t[:, :half], t[:, half:]
        return jnp.concatenate([t1 * cos - t2 * sin, t1 * sin + t2 * cos], axis=-1)

    for h in range(heads):
        lg = log_gamma[h]
        q = rot(q_ref[0, :, h * dk:(h + 1) * dk].astype(F32))
        k = rot(k_ref[0, :, h * dk:(h + 1) * dk].astype(F32)) * dk ** -0.5
        v = v_ref[0, :, h * dv:(h + 1) * dv]
        state = r_ref[h]
        inter = _bdot(q, state) * jnp.exp(lg * (t_in + 1.0))
        intra = _bdot(_bdot_nt(q, k) * dec_ref[h], v)
        r_ref[h] = state * float(np.exp(lg * L)) + _bdot_tn(k * jnp.exp(lg * (L - 1.0 - t_in)), v)
        y = intra + inter
        mu = jnp.mean(y, axis=-1, keepdims=True)
        d = y - mu
        var = jnp.mean(d * d, axis=-1, keepdims=True)
        yn = d * lax.rsqrt(var + 1e-6) * gnw_ref[:, h * dv:(h + 1) * dv] + gnb_ref[:, h * dv:(h + 1) * dv]
        gate = g_ref[0, :, h * dv:(h + 1) * dv].astype(F32)
        out_ref[0, :, h * dv:(h + 1) * dv] = (gate * _sigmoid(gate) * yn).astype(out_ref.dtype)


def _retention(z3, positions, gn_w, gn_b):
    B, S, _ = z3.shape
    W = gn_w.shape[0]
    H = RET_HEADS
    dk = W // H
    half = dk // 2
    L = RET_CHUNK
    log_gamma = np.log1p(-np.exp2(-5.0 - np.arange(H, dtype=np.float64)))
    rel = np.arange(L)[:, None] - np.arange(L)[None, :]
    decay = np.where(rel >= 0, np.exp(log_gamma[:, None, None] * np.maximum(rel, 0)), 0.0).astype(np.float32)
    inv = (ROPE_BASE ** (-jnp.arange(half, dtype=F32) / half)).reshape(1, half)
    pos = positions.astype(F32).reshape(B, S, 1)
    blk = lambda j: pl.BlockSpec((1, L, W), lambda b, c: (b, c, j))
    fixed2 = lambda b, c: (0, 0)
    return pl.pallas_call(
        functools.partial(_retention_kernel, heads=H, log_gamma=tuple(float(x) for x in log_gamma)),
        grid=(B, S // L),
        in_specs=[blk(0), blk(1), blk(2), blk(3),
                  pl.BlockSpec((1, L, 1), lambda b, c: (b, c, 0)),
                  pl.BlockSpec((1, half), fixed2),
                  pl.BlockSpec((H, L, L), lambda b, c: (0, 0, 0)),
                  pl.BlockSpec((1, W), fixed2), pl.BlockSpec((1, W), fixed2)],
        out_specs=pl.BlockSpec((1, L, W), lambda b, c: (b, c, 0)),
        out_shape=jax.ShapeDtypeStruct((B, S, W), BF16),
        scratch_shapes=[pltpu.VMEM((H, dk, W // H), F32)],
        compiler_params=_params("parallel", "arbitrary"),
        name="retention",
    )(z3, z3, z3, z3, pos, inv, jnp.asarray(decay), gn_w.reshape(1, W), gn_b.reshape(1, W))


def _xattn_kernel(q_ref, k_ref, v_ref, o_ref, *, heads):
    hd = q_ref.shape[2] // heads
    q, k, v = (jnp.stack([r[0, :, h * hd:(h + 1) * hd] for h in range(heads)]) for r in (q_ref, k_ref, v_ref))
    s = jnp.einsum("hqd,hmd->hqm", q, k, preferred_element_type=F32) * hd ** -0.5
    e = jnp.exp(s - jnp.max(s, axis=-1, keepdims=True))
    p = e / jnp.sum(e, axis=-1, keepdims=True)
    o = jnp.einsum("hqm,hmd->hqd", p.astype(BF16), v, preferred_element_type=F32)
    for h in range(heads):
        o_ref[0, :, h * hd:(h + 1) * hd] = o[h].astype(o_ref.dtype)


def _xattn(q3, kv3, tq=512):
    B, S, D = q3.shape
    M = kv3.shape[1]
    return pl.pallas_call(
        functools.partial(_xattn_kernel, heads=XA_HEADS),
        grid=(B, S // tq),
        in_specs=[pl.BlockSpec((1, tq, D), lambda b, i: (b, i, 0)),
                  pl.BlockSpec((1, M, D), lambda b, i: (b, 0, 0)),
                  pl.BlockSpec((1, M, D), lambda b, i: (b, 0, 1))],
        out_specs=pl.BlockSpec((1, tq, D), lambda b, i: (b, i, 0)),
        out_shape=jax.ShapeDtypeStruct((B, S, D), BF16),
        compiler_params=_params("parallel", "parallel"),
        name="xattn",
    )(q3, kv3, kv3)


def _first_argmax(vals, index, big):
    m = jnp.max(vals, axis=0, keepdims=True)
    idx = jnp.min(jnp.where(vals == m, index, big), axis=0, keepdims=True)
    return m, idx


def _router_kernel(x_ref, wt_ref, bias_ref, idx_ref, gate_ref, pos_ref, cnt_ref, carry_ref):
    E = wt_ref.shape[0]
    tm = x_ref.shape[0]
    per = E // N_GROUPS
    neg = -jnp.inf
    logits = lax.dot_general(wt_ref[...], x_ref[...], (((1,), (1,)), ((), ())), precision=HI,
                             preferred_element_type=F32)
    scores = _sigmoid(logits)
    biased = scores + bias_ref[...]
    eidx = _iota2((E, tm), 0)

    within = _iota2((per, tm), 0)
    group_scores = []
    for g in range(N_GROUPS):
        vals = biased[g * per:(g + 1) * per]
        m1, i1 = _first_argmax(vals, within, per)
        m2 = jnp.max(jnp.where(within == i1, neg, vals), axis=0, keepdims=True)
        group_scores.append(m1 + m2)
    gs = jnp.concatenate(group_scores, axis=0)

    gidx = _iota2((N_GROUPS, tm), 0)
    keep = jnp.zeros((N_GROUPS, tm), F32)
    for _ in range(TOPK_GROUPS):
        _, gi = _first_argmax(gs, gidx, N_GROUPS)
        hit = gidx == gi
        keep = jnp.where(hit, 1.0, keep)
        gs = jnp.where(hit, neg, gs)
    keep_e = jnp.concatenate([jnp.broadcast_to(keep[g:g + 1], (per, tm)) for g in range(N_GROUPS)], axis=0)

    cand = jnp.where(keep_e > 0.0, biased, neg)
    picks, affs = [], []
    for _ in range(TOP_K):
        _, ei = _first_argmax(cand, eidx, E)
        hit = eidx == ei
        picks.append(ei)
        affs.append(jnp.sum(jnp.where(hit, scores, 0.0), axis=0, keepdims=True))
        cand = jnp.where(hit, neg, cand)
    aff = jnp.concatenate(affs, axis=0)
    idx_ref[...] = jnp.concatenate(picks, axis=0)
    gate_ref[...] = ROUTED_SCALE * aff / jnp.sum(aff, axis=0, keepdims=True)

    @pl.when(pl.program_id(0) == 0)
    def _():
        carry_ref[...] = jnp.zeros_like(carry_ref)

    sel = jnp.where(cand == neg, jnp.where(keep_e > 0.0, 1.0, 0.0), 0.0)
    upper = (_iota2((tm, tm), 0) < _iota2((tm, tm), 1)).astype(F32).astype(BF16)
    before = carry_ref[:, :1] + jnp.dot(sel.astype(BF16), upper, preferred_element_type=F32)
    pos_ref[...] = jnp.concatenate(
        [jnp.sum(jnp.where(eidx == ei, before, 0.0), axis=0, keepdims=True) for ei in picks], axis=0
    ).astype(jnp.int32)
    total = carry_ref[:, :1] + jnp.sum(sel, axis=1, keepdims=True)
    carry_ref[...] = jnp.broadcast_to(total, carry_ref.shape)
    cnt_ref[...] = jnp.broadcast_to(total, cnt_ref.shape).astype(jnp.int32)


def _router(x, router_w, router_bias, tm=512):
    T, D = x.shape
    E = router_w.shape[1]
    idx, gate, pos, cnt = pl.pallas_call(
        _router_kernel,
        grid=(T // tm,),
        in_specs=[pl.BlockSpec((tm, D), lambda i: (i, 0)),
                  pl.BlockSpec((E, D), lambda i: (0, 0)),
                  pl.BlockSpec((E, 1), lambda i: (0, 0))],
        out_specs=[pl.BlockSpec((TOP_K, tm), lambda i: (0, i)), pl.BlockSpec((TOP_K, tm), lambda i: (0, i)),
                   pl.BlockSpec((TOP_K, tm), lambda i: (0, i)), pl.BlockSpec((E, LANES), lambda i: (0, 0))],
        out_shape=[jax.ShapeDtypeStruct((TOP_K, T), jnp.int32), jax.ShapeDtypeStruct((TOP_K, T), F32),
                   jax.ShapeDtypeStruct((TOP_K, T), jnp.int32), jax.ShapeDtypeStruct((E, LANES), jnp.int32)],
        scratch_shapes=[pltpu.VMEM((E, LANES), F32)],
        compiler_params=_params("arbitrary"),
        name="router",
    )(x, router_w.T, router_bias.reshape(E, 1))
    return idx.T, gate.T, pos.T, cnt[:, 0]


def _dispatch_plan(top_e, pos, counts, rows_per_tile):
    T, K = top_e.shape
    E = counts.shape[0]
    n_tiles = (T * K) // rows_per_tile + E
    tiles = (counts + rows_per_tile - 1) // rows_per_tile
    tile_end = jnp.cumsum(tiles)
    n_used = tile_end[-1]
    row_start = (tile_end - tiles) * rows_per_tile
    dest = jnp.take(row_start, top_e) + pos
    tile_id = jnp.minimum(jnp.arange(n_tiles, dtype=jnp.int32), n_used - 1)
    tile_expert = jnp.minimum(jnp.sum(tile_end[None, :] <= tile_id[:, None], axis=1), E - 1)
    return (tile_expert.astype(jnp.int32), tile_end.astype(jnp.int32), n_used.reshape(1).astype(jnp.int32),
            dest.astype(jnp.int32))


def _dispatch_kernel(tend_ref, dest_ref, x_ref, xs_hbm, zero_ref, zsem, sem, *, slots):
    tm = x_ref.shape[0]
    rows = zero_ref.shape[0]
    n_experts = tend_ref.shape[0]

    @pl.when(pl.program_id(0) == 0)
    def _():
        zero_ref[...] = jnp.zeros_like(zero_ref)

        def last_tile(e):
            end = tend_ref[e]
            start = tend_ref[jnp.maximum(e - 1, 0)]
            start = jnp.where(e == 0, 0, start)
            return end > start, pltpu.make_async_copy(
                zero_ref, xs_hbm.at[pl.ds(jnp.maximum(end - 1, 0) * rows, rows)], zsem)

        def start(e, carry):
            owns, copy = last_tile(e)

            @pl.when(owns)
            def _():
                copy.start()
            return carry

        def wait(e, carry):
            owns, copy = last_tile(e)

            @pl.when(owns)
            def _():
                copy.wait()
            return carry

        def unused_tile(n):
            return pltpu.make_async_copy(zero_ref, xs_hbm.at[pl.ds(n * rows, rows)], zsem)

        def start_unused(n, carry):
            unused_tile(n).start()
            return carry

        def wait_unused(n, carry):
            unused_tile(n).wait()
            return carry

        n_used, n_tiles = tend_ref[n_experts - 1], xs_hbm.shape[0] // rows
        lax.fori_loop(0, n_experts, start, 0)
        lax.fori_loop(n_used, n_tiles, start_unused, 0)
        lax.fori_loop(0, n_experts, wait, 0)
        lax.fori_loop(n_used, n_tiles, wait_unused, 0)

    def body(t, carry):
        for k in range(slots):
            pltpu.make_async_copy(x_ref.at[pl.ds(t, 1)], xs_hbm.at[pl.ds(dest_ref[0, t * slots + k], 1)],
                                  sem).start(priority=k % 2)
        return carry
    lax.fori_loop(0, tm, body, 0, unroll=2)
    for k in range(slots):
        pltpu.make_async_copy(x_ref, xs_hbm.at[pl.ds(0, tm)], sem).wait()


def _dispatch(x3p, dest, tile_end, n_tiles, rows_per_tile, tm=512):
    T, half = x3p.shape
    K = dest.shape[1]
    steps = T // tm
    d3 = dest.reshape(steps, 1, tm * K)
    return pl.pallas_call(
        functools.partial(_dispatch_kernel, slots=K),
        grid_spec=pltpu.PrefetchScalarGridSpec(
            num_scalar_prefetch=1,
            grid=(steps,),
            in_specs=[pl.BlockSpec((None, 1, K * tm), lambda i, te: (i, 0, 0), memory_space=pltpu.SMEM),
                      pl.BlockSpec((tm, half), lambda i, te: (i, 0))],
            out_specs=pl.BlockSpec(memory_space=pl.ANY),
            scratch_shapes=[pltpu.VMEM((rows_per_tile, half), jnp.int32),
                            pltpu.SemaphoreType.DMA(()), pltpu.SemaphoreType.DMA(())]),
        out_shape=jax.ShapeDtypeStruct((n_tiles * rows_per_tile, half), jnp.int32),
        compiler_params=_params("arbitrary"),
        name="dispatch",
    )(tile_end, d3, x3p)


def _sparse_experts_kernel(te_ref, nu_ref, xs_ref, w1_ref, w3_ref, w2_ref, y_ref, wb1, wb3, wb2):
    n = pl.program_id(0)

    @pl.when(n < nu_ref[0])
    def _():
        @pl.when((n == 0) | (te_ref[n] != te_ref[jnp.maximum(n - 1, 0)]))
        def _():
            wb1[...] = w1_ref[0].astype(BF16)
            wb3[...] = w3_ref[0].astype(BF16)
            wb2[...] = w2_ref[0].astype(BF16)

        x = _unpack_rows(xs_ref[...]).astype(BF16)
        h1 = jnp.dot(x, wb1[...], preferred_element_type=F32)
        h3 = jnp.dot(x, wb3[...], preferred_element_type=F32)
        hidden = (h1 * _sigmoid(h1) * h3).astype(BF16)
        y_ref[...] = _pack_rows(jnp.dot(hidden, wb2[...], preferred_element_type=F32))

    @pl.when(n >= nu_ref[0])
    def _():
        y_ref[...] = jnp.zeros_like(y_ref)


def _sparse_experts(xs, tile_expert, n_used, w1, w3, w2, layer, rows):
    _, _, D, Hd = w1.shape
    n_tiles = tile_expert.shape[0]
    wmap = lambda n, te, nu: (layer, te[n], 0, 0)
    return pl.pallas_call(
        _sparse_experts_kernel,
        grid_spec=pltpu.PrefetchScalarGridSpec(
            num_scalar_prefetch=2,
            grid=(n_tiles,),
            in_specs=[pl.BlockSpec((rows, D // 2), lambda n, te, nu: (jnp.minimum(n, nu[0] - 1), 0)),
                      pl.BlockSpec((None, 1, D, Hd), wmap),
                      pl.BlockSpec((None, 1, D, Hd), wmap),
                      pl.BlockSpec((None, 1, Hd, D), wmap)],
            out_specs=pl.BlockSpec((rows, D // 2), lambda n, te, nu: (n, 0)),
            scratch_shapes=[pltpu.VMEM((D, Hd), BF16), pltpu.VMEM((D, Hd), BF16), pltpu.VMEM((Hd, D), BF16)]),
        out_shape=jax.ShapeDtypeStruct(xs.shape, jnp.int32),
        compiler_params=_params("arbitrary"),
        name="sparse_experts",
    )(tile_expert, n_used, xs, w1, w3, w2)


def _combine_kernel(dest_ref, dest_next_ref, gate_ref, y_hbm, o_ref, buf, sem):
    i = pl.program_id(0)
    steps = pl.num_programs(0)
    K, tm = buf.shape[1:3]
    slot = i % 2

    def fetch(d_ref, s):
        def body(t, carry):
            for k in range(K):
                pltpu.make_async_copy(y_hbm.at[pl.ds(d_ref[0, t * K + k], 1)], buf.at[s, k, pl.ds(t, 1)],
                                      sem.at[s]).start(priority=k % 2)
            return carry
        lax.fori_loop(0, tm, body, 0, unroll=2)

    @pl.when(i == 0)
    def _():
        fetch(dest_ref, 0)

    @pl.when(i + 1 < steps)
    def _():
        fetch(dest_next_ref, 1 - slot)

    for k in range(K):
        pltpu.make_async_copy(y_hbm.at[pl.ds(0, tm)], buf.at[slot, k], sem.at[slot]).wait()
    gate = gate_ref[...]
    acc = jnp.zeros(o_ref.shape, F32)
    for k in range(K):
        acc = acc + gate[:, k:k + 1] * _unpack_rows(buf[slot, k])
    o_ref[...] = acc


def _combine(y, dest, gate, tm=256):
    T, K = dest.shape
    D = 2 * y.shape[1]
    steps = T // tm
    d3 = dest.reshape(steps, 1, tm * K)
    return pl.pallas_call(
        _combine_kernel,
        grid=(steps,),
        in_specs=[pl.BlockSpec((None, 1, K * tm), lambda i: (i, 0, 0), memory_space=pltpu.SMEM),
                  pl.BlockSpec((None, 1, K * tm), lambda i: (jnp.minimum(i + 1, steps - 1), 0, 0),
                               memory_space=pltpu.SMEM),
                  pl.BlockSpec((tm, K), lambda i: (i, 0)),
                  pl.BlockSpec(memory_space=pl.ANY)],
        out_specs=pl.BlockSpec((tm, D), lambda i: (i, 0)),
        out_shape=jax.ShapeDtypeStruct((T, D), F32),
        scratch_shapes=[pltpu.VMEM((2, K, tm, D // 2), jnp.int32), pltpu.SemaphoreType.DMA((2,))],
        compiler_params=_params("arbitrary"),
        name="combine",
    )(d3, d3, gate, y)


def _even_mixer(x2, xb, B, S, ln_g, ln_b, w_in, conv_w, conv_b, gate_b, ml_gn_w, rw_mu, rw_w0, rw_w2,
                rw_a0, rw_a2, rw_g2, rw_kk, rw_ka, rw_rk, rw_gn_w, rw_gn_b, w_out, j):
    D = x2.shape[1]
    H = ML_HEADS
    ML = ml_gn_w.shape[0]
    RW = rw_w0.shape[0]
    LO = RW_LORA_W + RW_LORA_A + RW_LORA_G
    ml_main = 3 * ML
    ml_cols = ml_main + 2 * H
    pad_to = 512
    used = ml_main + 3 * RW + LO + LANES
    total = -(-used // pad_to) * pad_to
    w_perm = jnp.concatenate([
        w_in[:, :ml_main], w_in[:, ml_cols:ml_cols + 3 * RW + LO], w_in[:, ml_main:ml_cols],
        jnp.zeros((D, total - used + LANES - 2 * H), w_in.dtype)], axis=1).astype(BF16)
    z3 = _mm(xb, w_perm[None], 0, BF16, 1024, pad_to).reshape(B, S, total)
    gate_col0 = ml_main + 3 * RW + LO
    L = ML_CHUNK
    gates_t = jnp.swapaxes(
        z3[:, :, gate_col0:gate_col0 + 2 * H].astype(F32).reshape(B, S // L, L, 2 * H), 2, 3)
    h_ml = _mlstm(z3, gates_t, conv_w, conv_b, gate_b, ml_gn_w,
                  qk_blk=0, v_blk=1, o_blk=2, gate_blk=gate_col0 // LANES)
    h_rw = _rwkv(z3, rw_mu, rw_w0, rw_w2, rw_a0, rw_a2, rw_g2, rw_kk, rw_ka, rw_rk, rw_gn_w, rw_gn_b,
                 col0=ml_main, lora_col0=ml_main + 3 * RW, width=RW)
    return _mm_res_ln([h_ml.reshape(B * S, ML), h_rw.reshape(B * S, RW)], w_out, j, x2, ln_g, ln_b)


def _odd_mixer(x2, xb, B, S, positions, ln_g, ln_b, w_in, gn_w, gn_b, w_out, j):
    z3 = _mm(xb, w_in, j, BF16, 1024, 512).reshape(B, S, w_in.shape[2])
    h = _retention(z3, positions, gn_w, gn_b)
    return _mm_res_ln([h.reshape(B * S, -1)], w_out, j, x2, ln_g, ln_b)


def _cross_attention(x2, xb, B, S, memb, ln_g, ln_b, wq, wkv, wo, layer):
    D = x2.shape[1]
    q = _mm(xb, wq, layer, BF16, 1024, 512)
    kv = _mm(memb, wkv, layer, BF16, memb.shape[0], 512)
    o = _xattn(q.reshape(B, S, D), kv.reshape(B, -1, 2 * D))
    return _mm_res_ln([o.reshape(B * S, D)], wo, layer, x2, ln_g, ln_b, packed=True)


def _moe(x2, xb, x3p, ln_g, ln_b, router_w, router_bias, w1, w3, w2, sw1, sw3, sw2, layer):
    rows = MOE_ROWS_PER_TILE
    top_e, gate, pos, counts = _router(x2, router_w, router_bias)
    tile_expert, tile_end, n_used, dest = _dispatch_plan(top_e, pos, counts, rows)
    xs = _dispatch(x3p, dest, tile_end, tile_expert.shape[0], rows)
    routed = _combine(_sparse_experts(xs, tile_expert, n_used, w1, w3, w2, layer, rows), dest, gate)
    hs = _glu(xb, sw1, sw3, layer)
    return _mm_res_ln([hs], sw2, layer, x2, ln_g, ln_b, add=routed)


def kernel(x, mem, positions, ln_g, ln_b, xa_wq, xa_wkv, xa_wo, router_w, router_bias, moe_w1, moe_w3, moe_w2, sh_w1, sh_w3, sh_w2, ev_w_in, ml_conv_w, ml_conv_b, ml_gate_b, ml_gn_w, rw_mu, rw_w0, rw_w2, rw_a0, rw_a2, rw_g2, rw_kk, rw_ka, rw_rk, rw_gn_w, rw_gn_b, ev_w_out, od_w_in, ret_gn_w, ret_gn_b, od_w_out):
    B, S, D = x.shape
    x2 = x.reshape(B * S, D)
    xb = x2.astype(BF16)
    memb = mem.reshape(-1, D).astype(BF16)
    (xa_wq, xa_wkv, xa_wo, sh_w1, sh_w3, sh_w2, ev_w_out, od_w_in, od_w_out) = (
        w.astype(BF16) for w in (xa_wq, xa_wkv, xa_wo, sh_w1, sh_w3, sh_w2, ev_w_out, od_w_in, od_w_out))
    for layer in range(ln_g.shape[0]):
        j = layer // 2
        if layer % 2 == 0:
            x2, xb = _even_mixer(x2, xb, B, S, ln_g[layer, 0], ln_b[layer, 0], ev_w_in[j], ml_conv_w[j],
                                 ml_conv_b[j], ml_gate_b[j], ml_gn_w[j], rw_mu[j], rw_w0[j], rw_w2[j],
                                 rw_a0[j], rw_a2[j], rw_g2[j], rw_kk[j], rw_ka[j], rw_rk[j],
                                 rw_gn_w[j], rw_gn_b[j], ev_w_out, j)
        else:
            x2, xb = _odd_mixer(x2, xb, B, S, positions, ln_g[layer, 0], ln_b[layer, 0], od_w_in,
                                ret_gn_w[j], ret_gn_b[j], od_w_out, j)
        x2, xb, x3p = _cross_attention(x2, xb, B, S, memb, ln_g[layer, 1], ln_b[layer, 1], xa_wq, xa_wkv,
                                       xa_wo, layer)
        x2, xb = _moe(x2, xb, x3p, ln_g[layer, 2], ln_b[layer, 2], router_w[layer], router_bias[layer],
                      moe_w1, moe_w3, moe_w2, sh_w1, sh_w3, sh_w2, layer)
    return x2.reshape(B, S, D)
```

```python
import functools

import numpy as np
import jax
import jax.numpy as jnp
from jax import lax
from jax.experimental import pallas as pl
from jax.experimental.pallas import tpu as pltpu

F32 = jnp.float32
BF16 = jnp.bfloat16
HI = lax.Precision.HIGHEST

DEPTH = 4
ALPHA = (2.0 * DEPTH) ** 0.25
LN_EPS = 1e-5
NEG_BIG = -1e30
CONV_K = 4
ML_HEADS = 4
RW_HEAD = 64
RW_LORA_W = 64
RW_LORA_A = 64
RW_LORA_G = 128
RW_GN_EPS = 64e-5
RET_HEADS = 8
ROPE_BASE = 10000.0
XA_HEADS = 4
N_EXPERTS = 64
TOP_K = 8
N_GROUPS = 8
TOPK_GROUPS = 4
ROUTED_SCALE = 2.5

ML_CHUNK = 64
RW_CHUNK = 64
RET_CHUNK = 256
MOE_ROWS_PER_TILE = 512

V7X_VMEM_BYTES = 64 * 2 ** 20
VMEM_LIMIT = (V7X_VMEM_BYTES * 3) // 4
LANES = 128
V7X_MXU_DIM = 256


def _params(*sem):
    return pltpu.CompilerParams(dimension_semantics=sem, vmem_limit_bytes=VMEM_LIMIT)


def _bdot(a, b):
    return jnp.dot(a.astype(BF16), b.astype(BF16), preferred_element_type=F32)


def _bdot_nt(a, b):
    return lax.dot_general(a.astype(BF16), b.astype(BF16), (((1,), (1,)), ((), ())),
                           preferred_element_type=F32)


def _bdot_tn(a, b):
    return lax.dot_general(a.astype(BF16), b.astype(BF16), (((0,), (0,)), ((), ())),
                           preferred_element_type=F32)


def _sigmoid(x):
    return 1.0 / (1.0 + jnp.exp(-x))


def _softplus(x):
    return jnp.maximum(x, 0.0) + jnp.log1p(jnp.exp(-jnp.abs(x)))


def _iota2(shape, dim):
    return lax.broadcasted_iota(jnp.int32, shape, dim)


def _mm_kernel(x_ref, w_ref, o_ref):
    o_ref[...] = _bdot(x_ref[...], w_ref[...]).astype(o_ref.dtype)


def _mm(x, w, layer, out_dtype, tm, tn):
    M, K = x.shape
    N = w.shape[2]
    assert M % tm == 0 and N % tn == 0
    return pl.pallas_call(
        _mm_kernel,
        grid=(M // tm, N // tn),
        in_specs=[pl.BlockSpec((tm, K), lambda i, j: (i, 0)),
                  pl.BlockSpec((None, K, tn), lambda i, j: (layer, 0, j))],
        out_specs=pl.BlockSpec((tm, tn), lambda i, j: (i, j)),
        out_shape=jax.ShapeDtypeStruct((M, N), out_dtype),
        compiler_params=_params("parallel", "parallel"),
        name="mm",
    )(x, w)


def _layer_norm_rows(y, g, b):
    mu = jnp.mean(y, axis=-1, keepdims=True)
    d = y - mu
    var = jnp.mean(d * d, axis=-1, keepdims=True)
    return d * lax.rsqrt(var + LN_EPS) * g + b


def _pack_rows(v):
    half = v.shape[1] // 2
    bits = lax.bitcast_convert_type(v.astype(BF16).astype(F32), jnp.int32)
    return bits[:, half:] | lax.shift_right_logical(bits[:, :half], 16)


def _unpack_rows(w):
    return jnp.concatenate([lax.bitcast_convert_type(lax.shift_left(w, 16), F32),
                            lax.bitcast_convert_type(w & jnp.int32(-65536), F32)], axis=-1)


def _mm_res_ln_kernel(*refs, n_a, has_add, packed):
    a_refs, (w_ref, res_ref), rest = refs[:n_a], refs[n_a:n_a + 2], refs[n_a + 2:]
    add_ref = rest[0] if has_add else None
    n_out = 3 if packed else 2
    g_ref, b_ref = rest[-n_out - 2:-n_out]
    o_ref, ob_ref = rest[-n_out:][:2]
    acc, k0 = None, 0
    for a_ref in a_refs:
        part = _bdot(a_ref[...], w_ref[k0:k0 + a_ref.shape[1], :])
        acc = part if acc is None else acc + part
        k0 += a_ref.shape[1]
    if has_add:
        acc = add_ref[...] + acc
    out = _layer_norm_rows(ALPHA * res_ref[...] + acc, g_ref[...], b_ref[...])
    o_ref[...] = out
    ob_ref[...] = out.astype(ob_ref.dtype)
    if packed:
        rest[-1][...] = _pack_rows(out)


def _mm_res_ln(a_parts, w, layer, res, g, b, add=None, packed=False, tm=256):
    M = res.shape[0]
    K, N = w.shape[1:]
    assert sum(a.shape[1] for a in a_parts) == K
    row = lambda i: (i, 0)
    fixed = lambda i: (0, 0)
    in_specs = [pl.BlockSpec((tm, a.shape[1]), row) for a in a_parts]
    in_specs += [pl.BlockSpec((None, K, N), lambda i: (layer, 0, 0)), pl.BlockSpec((tm, N), row)]
    args = [*a_parts, w, res]
    if add is not None:
        in_specs.append(pl.BlockSpec((tm, N), row))
        args.append(add)
    in_specs += [pl.BlockSpec((1, N), fixed), pl.BlockSpec((1, N), fixed)]
    args += [g.reshape(1, N), b.reshape(1, N)]
    out_specs = [pl.BlockSpec((tm, N), row), pl.BlockSpec((tm, N), row)]
    out_shape = [jax.ShapeDtypeStruct((M, N), F32), jax.ShapeDtypeStruct((M, N), BF16)]
    if packed:
        out_specs.append(pl.BlockSpec((tm, N // 2), row))
        out_shape.append(jax.ShapeDtypeStruct((M, N // 2), jnp.int32))
    return pl.pallas_call(
        functools.partial(_mm_res_ln_kernel, n_a=len(a_parts), has_add=add is not None, packed=packed),
        grid=(M // tm,),
        in_specs=in_specs,
        out_specs=out_specs,
        out_shape=out_shape,
        compiler_params=_params("parallel"),
        name="mm_res_ln",
    )(*args)


def _glu_kernel(x_ref, w1_ref, w3_ref, o_ref):
    x = x_ref[...]
    h1 = jnp.dot(x, w1_ref[...], preferred_element_type=F32)
    h3 = jnp.dot(x, w3_ref[...], preferred_element_type=F32)
    o_ref[...] = (h1 * _sigmoid(h1) * h3).astype(o_ref.dtype)


def _glu(xb, w1, w3, layer, tm=512):
    M, K = xb.shape
    N = w1.shape[2]
    wspec = pl.BlockSpec((None, K, N), lambda i: (layer, 0, 0))
    return pl.pallas_call(
        _glu_kernel,
        grid=(M // tm,),
        in_specs=[pl.BlockSpec((tm, K), lambda i: (i, 0)), wspec, wspec],
        out_specs=pl.BlockSpec((tm, N), lambda i: (i, 0)),
        out_shape=jax.ShapeDtypeStruct((M, N), BF16),
        compiler_params=_params("parallel"),
        name="glu",
    )(xb, w1, w3)


def _log_sigmoid(x):
    return jnp.minimum(x, 0.0) - jnp.log1p(jnp.exp(-jnp.abs(x)))


def _split_bf16(x, terms):
    pieces = []
    for _ in range(terms):
        p = x.astype(BF16)
        pieces.append(p)
        x = x - p.astype(F32)
    return pieces


def _mlstm_kernel(qk_ref, v_ref, o_ref, gc_ref, gr_ref, cw_ref, cb_ref, gbc_ref, gbr_ref, gn_ref,
                  out_ref, prev_ref, c_ref, n_ref, m_ref, *, heads, dk, dv):
    nb, L, _ = qk_ref.shape
    qw = heads * dk

    @pl.when(pl.program_id(0) == 0)
    def _():
        prev_ref[...] = jnp.zeros_like(prev_ref)
        c_ref[...] = jnp.zeros_like(c_ref)
        n_ref[...] = jnp.zeros_like(n_ref)
        m_ref[...] = jnp.full_like(m_ref, NEG_BIG)

    row = _iota2((L, 1), 0)
    qk = []
    for b in range(nb):
        cur = qk_ref[b].astype(F32)
        prev = prev_ref[b]
        acc = cb_ref[...] + cw_ref[CONV_K - 1:CONV_K, :] * cur
        for j in range(CONV_K - 1):
            s = CONV_K - 1 - j
            shifted = jnp.where(row < s, pltpu.roll(prev, s, 0), pltpu.roll(cur, s, 0))
            acc = acc + cw_ref[j:j + 1, :] * shifted
        prev_ref[b] = cur
        qk.append(acc * _sigmoid(acc))

    causal = _iota2((L, L), 0) >= _iota2((L, L), 1)
    tri = causal.astype(F32).astype(BF16)
    g_col = [gc_ref[b].astype(F32) + gbc_ref[...] for b in range(nb)]
    pieces = _split_bf16(jnp.concatenate([_log_sigmoid(g) for g in g_col], axis=1), 3)
    lanes = nb * g_col[0].shape[1]
    b_col = jnp.dot(tri, jnp.concatenate(pieces, axis=1), preferred_element_type=F32)
    b_col = b_col[:, :lanes] + b_col[:, lanes:2 * lanes] + b_col[:, 2 * lanes:]
    g_row = [gr_ref[b, 0] + gbr_ref[...] for b in range(nb)]
    pieces = _split_bf16(jnp.concatenate([_log_sigmoid(g) for g in g_row], axis=0), 3)
    rows = nb * 2 * heads
    b_row = lax.dot_general(jnp.concatenate(pieces, axis=0), tri, (((1,), (1,)), ((), ())),
                            preferred_element_type=F32)
    b_row = b_row[:rows] + b_row[rows:2 * rows] + b_row[2 * rows:]

    def per_problem(f):
        return jnp.stack([f(b, h) for b in range(nb) for h in range(heads)])

    q = per_problem(lambda b, h: qk[b][:, h * dk:(h + 1) * dk])
    k = per_problem(lambda b, h: qk[b][:, qw + h * dk:qw + (h + 1) * dk]) * dk ** -0.5
    v = per_problem(lambda b, h: v_ref[b, :, h * dv:(h + 1) * dv])
    gcw = g_col[0].shape[1]
    bc = per_problem(lambda b, h: b_col[:, b * gcw + heads + h:b * gcw + heads + h + 1])
    ic = per_problem(lambda b, h: g_col[b][:, h:h + 1])
    br = per_problem(lambda b, h: b_row[b * 2 * heads + heads + h:b * 2 * heads + heads + h + 1])
    ir = per_problem(lambda b, h: g_row[b][h:h + 1])
    b_last = bc[:, L - 1:L]
    c_prev = c_ref[...]
    n_prev = n_ref[...]
    m_prev = m_ref[...][:, :, 0:1]

    def bmm(x, y, spec):
        return jnp.einsum(spec, x.astype(BF16), y.astype(BF16), preferred_element_type=F32)

    log_d = jnp.where(causal, bc - br + ir, -jnp.inf)
    g_inter = bc + m_prev
    m_t = jnp.maximum(jnp.max(log_d, axis=-1, keepdims=True), g_inter)
    p = jnp.exp(log_d - m_t) * bmm(q, k, "gtk,gsk->gts")
    e_inter = jnp.exp(g_inter - m_t)
    num = bmm(p, v, "gts,gsv->gtv") + e_inter * bmm(q, c_prev, "gtk,gkv->gtv")
    den = jnp.sum(p, axis=-1, keepdims=True) + e_inter * jnp.sum(q * n_prev, axis=-1, keepdims=True)
    hh = num / jnp.maximum(jnp.abs(den), jnp.exp(-m_t))

    w_end = b_last - bc + ic
    m_loc = jnp.max(w_end, axis=1, keepdims=True)
    e_end = jnp.exp(w_end - m_loc)
    ke = k * e_end
    m_new = jnp.maximum(b_last + m_prev, m_loc)
    s_prev = jnp.exp(b_last + m_prev - m_new)
    s_new = jnp.exp(m_loc - m_new)
    c_ref[...] = s_prev * c_prev + s_new * bmm(ke, v, "gsk,gsv->gkv")
    n_ref[...] = s_prev * n_prev + s_new * jnp.sum(ke, axis=1, keepdims=True)
    m_ref[...] = jnp.broadcast_to(m_new, m_ref.shape)

    mu = jnp.mean(hh, axis=-1, keepdims=True)
    d = hh - mu
    var = jnp.mean(d * d, axis=-1, keepdims=True)
    y = d * lax.rsqrt(var + 1e-6)
    for b in range(nb):
        for h in range(heads):
            sl = slice(h * dv, (h + 1) * dv)
            gate = _sigmoid(o_ref[b, :, sl].astype(F32))
            out_ref[b, :, sl] = (y[b * heads + h] * gn_ref[:, sl] * gate).astype(out_ref.dtype)


def _mlstm(z3, gates_t, conv_w, conv_b, gate_b, gn_w, *, qk_blk, v_blk, o_blk, gate_blk):
    B, S, _ = z3.shape
    H = ML_HEADS
    W = gn_w.shape[0]
    dv = W // H
    dk = conv_w.shape[1] // (2 * H)
    assert conv_w.shape[1] == W, "q|k block and v block share one column-block width"
    L = ML_CHUNK
    gb_col = jnp.zeros((1, LANES), F32).at[0, :2 * H].set(gate_b.reshape(-1))
    gb_row = gate_b.reshape(2 * H, 1)
    fixed = lambda c: (0, 0)
    return pl.pallas_call(
        functools.partial(_mlstm_kernel, heads=H, dk=dk, dv=dv),
        grid=(S // L,),
        in_specs=[pl.BlockSpec((B, L, W), lambda c: (0, c, qk_blk)),
                  pl.BlockSpec((B, L, W), lambda c: (0, c, v_blk)),
                  pl.BlockSpec((B, L, W), lambda c: (0, c, o_blk)),
                  pl.BlockSpec((B, L, LANES), lambda c: (0, c, gate_blk)),
                  pl.BlockSpec((B, 1, 2 * H, L), lambda c: (0, c, 0, 0)),
                  pl.BlockSpec((CONV_K, W), fixed),
                  pl.BlockSpec((1, W), fixed),
                  pl.BlockSpec((1, LANES), fixed),
                  pl.BlockSpec((2 * H, 1), fixed),
                  pl.BlockSpec((1, W), fixed)],
        out_specs=pl.BlockSpec((B, L, W), lambda c: (0, c, 0)),
        out_shape=jax.ShapeDtypeStruct((B, S, W), BF16),
        scratch_shapes=[pltpu.VMEM((B, L, W), F32),
                        pltpu.VMEM((B * H, dk, dv), F32),
                        pltpu.VMEM((B * H, 1, dk), F32),
                        pltpu.VMEM((B * H, 1, LANES), F32)],
        compiler_params=_params("arbitrary"),
        name="mlstm",
    )(z3, z3, z3, z3, gates_t, conv_w, conv_b.reshape(1, W), gb_col, gb_row, gn_w.reshape(1, W))


def _unit_lower_inverse(a, t_idx, i_idx, block_diag, matmul):
    n = a.shape[-2]
    eye = (t_idx == i_idx).astype(F32)
    t = None
    s = 1
    while s < n:
        pair = (t_idx // (2 * s)) == (i_idx // (2 * s))
        off = pair & ((t_idx // s) % 2 == 1) & ((i_idx // s) % 2 == 0)
        q = jnp.where(off, a, 0.0)
        t = eye - q if t is None else t - matmul(matmul(t, block_diag(q)), block_diag(t))
        s *= 2
    return t


def _rwkv_kernel(r_ref, k_ref, v_ref, l_ref, mur_ref, muk_ref, muv_ref, mul_ref, w0_ref, w2_ref,
                 a0_ref, a2_ref, g2_ref, kk_ref, ka_ref, rk_ref, gnw_ref, gnb_ref, out_ref,
                 pr_ref, pk_ref, pv_ref, plo_ref, s_ref, *, hd, gw):
    nb, L, W = r_ref.shape
    ng = W // gw
    hpg = gw // hd
    assert hpg * L == gw

    @pl.when(pl.program_id(1) == 0)
    def _():
        pr_ref[...] = jnp.zeros_like(pr_ref)
        pk_ref[...] = jnp.zeros_like(pk_ref)
        pv_ref[...] = jnp.zeros_like(pv_ref)
        plo_ref[...] = jnp.zeros_like(plo_ref)
        s_ref[...] = jnp.zeros_like(s_ref)

    R = nb * L
    first = _iota2((L, 1), 0) == 0
    same_head = (_iota2((gw, gw), 0) // hd) == (_iota2((gw, gw), 1) // hd)
    zero = jnp.zeros((), BF16)
    same_head_b = same_head.astype(F32).astype(BF16)
    row, col = _iota2((R, R), 0), _iota2((R, R), 1)
    tri = ((row // L == col // L) & (row >= col)).astype(F32).astype(BF16)
    t_idx = _iota2((L, gw), 0)
    i_idx = _iota2((L, gw), 1) % L
    strict = t_idx > i_idx
    lower = t_idx >= i_idx

    def shift_lerp(x_ref, p_ref, mu_ref):
        parts = []
        for b in range(nb):
            cur = x_ref[b].astype(F32)
            shifted = jnp.where(first, p_ref[b], pltpu.roll(cur, 1, 0))
            p_ref[b] = cur[L - 1:L, :]
            parts.append(cur + mu_ref[...] * (shifted - cur))
        return jnp.concatenate(parts, axis=0)

    def head_sums(xs):
        stacked = jnp.concatenate([x[:, i * gw:(i + 1) * gw] for x in xs for i in range(ng)], axis=0)
        n = stacked.shape[0]
        s = jnp.dot(jnp.concatenate(_split_bf16(stacked, 2), axis=0), same_head_b, preferred_element_type=F32)
        s = s[:n] + s[n:]
        return [jnp.concatenate([s[(j * ng + i) * R:(j * ng + i + 1) * R] for i in range(ng)], axis=1)
                for j in range(len(xs))]

    r = shift_lerp(r_ref, pr_ref, mur_ref)
    k = shift_lerp(k_ref, pk_ref, muk_ref)
    v = shift_lerp(v_ref, pv_ref, muv_ref)
    lo = shift_lerp(l_ref, plo_ref, mul_ref)
    xw = lo[:, :RW_LORA_W]
    xa = lo[:, RW_LORA_W:RW_LORA_W + RW_LORA_A]
    xg = lo[:, RW_LORA_W + RW_LORA_A:]

    w_log = -jnp.exp(-_softplus(-(w0_ref[...] + _bdot(jnp.tanh(xw), w2_ref[...]))) - 0.5)
    a = _sigmoid(a0_ref[...] + _bdot(xa, a2_ref[...]))
    g = _bdot(_sigmoid(xg), g2_ref[...])

    kk = k * kk_ref[...]
    k = k * (1.0 + (a - 1.0) * ka_ref[...])
    kk_sq, rk_sum = head_sums([kk * kk, r * k * rk_ref[...]])
    kk = kk * lax.rsqrt(jnp.maximum(kk_sq, 1e-24))
    bv = kk * a

    c3 = jnp.dot(tri, jnp.concatenate(_split_bf16(w_log, 3), axis=1), preferred_element_type=F32)
    c = c3[:, :W] + c3[:, W:2 * W] + c3[:, 2 * W:]
    c_last = jnp.concatenate([jnp.broadcast_to(c[(b + 1) * L - 1:(b + 1) * L], (L, W)) for b in range(nb)],
                             axis=0)
    e_end = jnp.exp(c_last - c)
    e_neg = jnp.exp(-c)
    kt = (kk * jnp.exp(c - w_log)).astype(BF16)
    rt = (r * jnp.exp(c)).astype(BF16)
    kh = (k * e_neg).astype(BF16)
    bh = (bv * e_neg).astype(BF16)
    kp = (k * e_end).astype(BF16)
    bp = (bv * e_end).astype(BF16)
    vb = v.astype(BF16)
    p_last = jnp.exp(c_last)

    def groups(x):
        return jnp.stack([x[b * L:(b + 1) * L, i * gw:(i + 1) * gw] for b in range(nb) for i in range(ng)])

    def block_diag(x):
        return jnp.where(same_head, jnp.concatenate([x.astype(BF16)] * hpg, axis=1), zero)

    def bmm(x, y):
        return jnp.einsum("gmk,gkn->gmn", x.astype(BF16), y.astype(BF16), preferred_element_type=F32)

    def bmm_nt(x, y):
        return jnp.einsum("gmk,gnk->gmn", x.astype(BF16), y.astype(BF16), preferred_element_type=F32)

    lhs = jnp.concatenate([groups(kt), groups(rt)], axis=1)
    m_k = bmm_nt(lhs, block_diag(groups(kh)))
    m_b = bmm_nt(lhs, block_diag(groups(bh)))
    a_kk = jnp.where(strict, m_k[:, :L], 0.0)
    a_kb = jnp.where(strict, m_b[:, :L], 0.0)
    a_rk = jnp.where(lower, m_k[:, L:], 0.0)
    a_rb = jnp.where(lower, m_b[:, L:], 0.0)
    t_inv = _unit_lower_inverse(a_kb, t_idx, i_idx, block_diag, bmm)
    st = s_ref[...]
    gs = bmm_nt(lhs, st)
    vg = groups(vb)
    v_bd = block_diag(vg)
    u = bmm(t_inv, block_diag(gs[:, :L] + bmm(a_kk, v_bd)))
    yg = gs[:, L:] + bmm(jnp.concatenate([a_rk, -a_rb], axis=2), jnp.concatenate([v_bd, block_diag(u)], axis=1))
    vu = jnp.concatenate([vg, (-u).astype(BF16)], axis=1)
    kb = jnp.concatenate([groups(kp), groups(bp)], axis=1)
    pg = groups(p_last)[:, :1]
    outer = jnp.einsum("gtv,gtk->gvk", vu, kb, preferred_element_type=F32)
    s_ref[...] = st * pg + jnp.where(same_head, outer, 0.0)
    ys = [jnp.concatenate([yg[b * ng + i] for i in range(ng)], axis=1) for b in range(nb)]

    y = jnp.concatenate(ys, axis=0)
    mu = head_sums([y])[0] * (1.0 / hd)
    d = y - mu
    var = head_sums([d * d])[0] * (1.0 / hd)
    yn = d * lax.rsqrt(var + RW_GN_EPS) * gnw_ref[...] + gnb_ref[...]
    out = ((yn + rk_sum * v) * g).astype(out_ref.dtype)
    for b in range(nb):
        out_ref[b] = out[b * L:(b + 1) * L]


def _rwkv(z3, mu, w0, w2, a0, a2, g2, k_k, k_a, r_k, gn_w, gn_b, *, col0, lora_col0, width):
    B, S, _ = z3.shape
    RW = w0.shape[0]
    W = width
    G = RW // W
    L = RW_CHUNK
    LO = RW_LORA_W + RW_LORA_A + RW_LORA_G
    assert col0 % W == 0 and RW % W == 0 and lora_col0 % LO == 0 and W % V7X_MXU_DIM == 0
    zc = lambda off: (lambda g, c: (0, c, off // W + g))
    vec = lambda g, c: (0, g)
    row = lambda x: x.reshape(1, -1)
    return pl.pallas_call(
        functools.partial(_rwkv_kernel, hd=RW_HEAD, gw=V7X_MXU_DIM),
        grid=(G, S // L),
        in_specs=[pl.BlockSpec((B, L, W), zc(col0)),
                  pl.BlockSpec((B, L, W), zc(col0 + RW)),
                  pl.BlockSpec((B, L, W), zc(col0 + 2 * RW)),
                  pl.BlockSpec((B, L, LO), lambda g, c: (0, c, lora_col0 // LO)),
                  pl.BlockSpec((1, W), vec), pl.BlockSpec((1, W), vec), pl.BlockSpec((1, W), vec),
                  pl.BlockSpec((1, LO), lambda g, c: (0, 0)),
                  pl.BlockSpec((1, W), vec),
                  pl.BlockSpec((RW_LORA_W, W), vec),
                  pl.BlockSpec((1, W), vec),
                  pl.BlockSpec((RW_LORA_A, W), vec),
                  pl.BlockSpec((RW_LORA_G, W), vec),
                  pl.BlockSpec((1, W), vec), pl.BlockSpec((1, W), vec), pl.BlockSpec((1, W), vec),
                  pl.BlockSpec((1, W), vec), pl.BlockSpec((1, W), vec)],
        out_specs=pl.BlockSpec((B, L, W), lambda g, c: (0, c, g)),
        out_shape=jax.ShapeDtypeStruct((B, S, RW), BF16),
        scratch_shapes=[pltpu.VMEM((B, 1, W), F32), pltpu.VMEM((B, 1, W), F32), pltpu.VMEM((B, 1, W), F32),
                        pltpu.VMEM((B, 1, LO), F32),
                        pltpu.VMEM((B * (W // V7X_MXU_DIM), V7X_MXU_DIM, V7X_MXU_DIM), F32)],
        compiler_params=_params("parallel", "arbitrary"),
        name="rwkv7",
    )(z3, z3, z3, z3, row(mu[:RW]), row(mu[RW:2 * RW]), row(mu[2 * RW:3 * RW]), row(mu[3 * RW:]),
      row(w0), w2.astype(BF16), row(a0), a2.astype(BF16), g2.astype(BF16), row(k_k), row(k_a),
      row(r_k), row(gn_w), row(gn_b))


def _retention_kernel(q_ref, k_ref, v_ref, g_ref, pos_ref, inv_ref, dec_ref, gnw_ref, gnb_ref,
                      out_ref, r_ref, *, heads, log_gamma):
    L = q_ref.shape[1]
    dk = q_ref.shape[2] // heads
    dv = v_ref.shape[2] // heads
    half = dk // 2

    @pl.when(pl.program_id(1) == 0)
    def _():
        r_ref[...] = jnp.zeros_like(r_ref)

    ang = pos_ref[0] * inv_ref[...]
    cos = jnp.cos(ang)
    sin = jnp.sin(ang)
    t_in = _iota2((L, 1), 0).astype(F32)

    def rot(t):
        t1, t2 = t[:, :half], t[:, half:]
        return jnp.concatenate([t1 * cos - t2 * sin, t1 * sin + t2 * cos], axis=-1)

    for h in range(heads):
        lg = log_gamma[h]
        q = rot(q_ref[0, :, h * dk:(h + 1) * dk].astype(F32))
        k = rot(k_ref[0, :, h * dk:(h + 1) * dk].astype(F32)) * dk ** -0.5
        v = v_ref[0, :, h * dv:(h + 1) * dv]
        state = r_ref[h]
        inter = _bdot(q, state) * jnp.exp(lg * (t_in + 1.0))
        intra = _bdot(_bdot_nt(q, k) * dec_ref[h], v)
        r_ref[h] = state * float(np.exp(lg * L)) + _bdot_tn(k * jnp.exp(lg * (L - 1.0 - t_in)), v)
        y = intra + inter
        mu = jnp.mean(y, axis=-1, keepdims=True)
        d = y - mu
        var = jnp.mean(d * d, axis=-1, keepdims=True)
        yn = d * lax.rsqrt(var + 1e-6) * gnw_ref[:, h * dv:(h + 1) * dv] + gnb_ref[:, h * dv:(h + 1) * dv]
        gate = g_ref[0, :, h * dv:(h + 1) * dv].astype(F32)
        out_ref[0, :, h * dv:(h + 1) * dv] = (gate * _sigmoid(gate) * yn).astype(out_ref.dtype)


def _retention(z3, positions, gn_w, gn_b):
    B, S, _ = z3.shape
    W = gn_w.shape[0]
    H = RET_HEADS
    dk = W // H
    half = dk // 2
    L = RET_CHUNK
    log_gamma = np.log1p(-np.exp2(-5.0 - np.arange(H, dtype=np.float64)))
    rel = np.arange(L)[:, None] - np.arange(L)[None, :]
    decay = np.where(rel >= 0, np.exp(log_gamma[:, None, None] * np.maximum(rel, 0)), 0.0).astype(np.float32)
    inv = (ROPE_BASE ** (-jnp.arange(half, dtype=F32) / half)).reshape(1, half)
    pos = positions.astype(F32).reshape(B, S, 1)
    blk = lambda j: pl.BlockSpec((1, L, W), lambda b, c: (b, c, j))
    fixed2 = lambda b, c: (0, 0)
    return pl.pallas_call(
        functools.partial(_retention_kernel, heads=H, log_gamma=tuple(float(x) for x in log_gamma)),
        grid=(B, S // L),
        in_specs=[blk(0), blk(1), blk(2), blk(3),
                  pl.BlockSpec((1, L, 1), lambda b, c: (b, c, 0)),
                  pl.BlockSpec((1, half), fixed2),
                  pl.BlockSpec((H, L, L), lambda b, c: (0, 0, 0)),
                  pl.BlockSpec((1, W), fixed2), pl.BlockSpec((1, W), fixed2)],
        out_specs=pl.BlockSpec((1, L, W), lambda b, c: (b, c, 0)),
        out_shape=jax.ShapeDtypeStruct((B, S, W), BF16),
        scratch_shapes=[pltpu.VMEM((H, dk, W // H), F32)],
        compiler_params=_params("parallel", "arbitrary"),
        name="retention",
    )(z3, z3, z3, z3, pos, inv, jnp.asarray(decay), gn_w.reshape(1, W), gn_b.reshape(1, W))


def _xattn_kernel(q_ref, k_ref, v_ref, o_ref, *, heads):
    hd = q_ref.shape[2] // heads
    q, k, v = (jnp.stack([r[0, :, h * hd:(h + 1) * hd] for h in range(heads)]) for r in (q_ref, k_ref, v_ref))
    s = jnp.einsum("hqd,hmd->hqm", q, k, preferred_element_type=F32) * hd ** -0.5
    e = jnp.exp(s - jnp.max(s, axis=-1, keepdims=True))
    p = e / jnp.sum(e, axis=-1, keepdims=True)
    o = jnp.einsum("hqm,hmd->hqd", p.astype(BF16), v, preferred_element_type=F32)
    for h in range(heads):
        o_ref[0, :, h * hd:(h + 1) * hd] = o[h].astype(o_ref.dtype)


def _xattn(q3, kv3, tq=512):
    B, S, D = q3.shape
    M = kv3.shape[1]
    return pl.pallas_call(
        functools.partial(_xattn_kernel, heads=XA_HEADS),
        grid=(B, S // tq),
        in_specs=[pl.BlockSpec((1, tq, D), lambda b, i: (b, i, 0)),
                  pl.BlockSpec((1, M, D), lambda b, i: (b, 0, 0)),
                  pl.BlockSpec((1, M, D), lambda b, i: (b, 0, 1))],
        out_specs=pl.BlockSpec((1, tq, D), lambda b, i: (b, i, 0)),
        out_shape=jax.ShapeDtypeStruct((B, S, D), BF16),
        compiler_params=_params("parallel", "parallel"),
        name="xattn",
    )(q3, kv3, kv3)


def _first_argmax(vals, index, big):
    m = jnp.max(vals, axis=0, keepdims=True)
    idx = jnp.min(jnp.where(vals == m, index, big), axis=0, keepdims=True)
    return m, idx


def _router_kernel(x_ref, wt_ref, bias_ref, idx_ref, gate_ref, pos_ref, cnt_ref, carry_ref):
    E = wt_ref.shape[0]
    tm = x_ref.shape[0]
    per = E // N_GROUPS
    neg = -jnp.inf
    logits = lax.dot_general(wt_ref[...], x_ref[...], (((1,), (1,)), ((), ())), precision=HI,
                             preferred_element_type=F32)
    scores = _sigmoid(logits)
    biased = scores + bias_ref[...]
    eidx = _iota2((E, tm), 0)

    within = _iota2((per, tm), 0)
    group_scores = []
    for g in range(N_GROUPS):
        vals = biased[g * per:(g + 1) * per]
        m1, i1 = _first_argmax(vals, within, per)
        m2 = jnp.max(jnp.where(within == i1, neg, vals), axis=0, keepdims=True)
        group_scores.append(m1 + m2)
    gs = jnp.concatenate(group_scores, axis=0)

    gidx = _iota2((N_GROUPS, tm), 0)
    keep = jnp.zeros((N_GROUPS, tm), F32)
    for _ in range(TOPK_GROUPS):
        _, gi = _first_argmax(gs, gidx, N_GROUPS)
        hit = gidx == gi
        keep = jnp.where(hit, 1.0, keep)
        gs = jnp.where(hit, neg, gs)
    keep_e = jnp.concatenate([jnp.broadcast_to(keep[g:g + 1], (per, tm)) for g in range(N_GROUPS)], axis=0)

    cand = jnp.where(keep_e > 0.0, biased, neg)
    picks, affs = [], []
    for _ in range(TOP_K):
        _, ei = _first_argmax(cand, eidx, E)
        hit = eidx == ei
        picks.append(ei)
        affs.append(jnp.sum(jnp.where(hit, scores, 0.0), axis=0, keepdims=True))
        cand = jnp.where(hit, neg, cand)
    aff = jnp.concatenate(affs, axis=0)
    idx_ref[...] = jnp.concatenate(picks, axis=0)
    gate_ref[...] = ROUTED_SCALE * aff / jnp.sum(aff, axis=0, keepdims=True)

    @pl.when(pl.program_id(0) == 0)
    def _():
        carry_ref[...] = jnp.zeros_like(carry_ref)

    sel = jnp.where(cand == neg, jnp.where(keep_e > 0.0, 1.0, 0.0), 0.0)
    upper = (_iota2((tm, tm), 0) < _iota2((tm, tm), 1)).astype(F32).astype(BF16)
    before = carry_ref[:, :1] + jnp.dot(sel.astype(BF16), upper, preferred_element_type=F32)
    pos_ref[...] = jnp.concatenate(
        [jnp.sum(jnp.where(eidx == ei, before, 0.0), axis=0, keepdims=True) for ei in picks], axis=0
    ).astype(jnp.int32)
    total = carry_ref[:, :1] + jnp.sum(sel, axis=1, keepdims=True)
    carry_ref[...] = jnp.broadcast_to(total, carry_ref.shape)
    cnt_ref[...] = jnp.broadcast_to(total, cnt_ref.shape).astype(jnp.int32)


def _router(x, router_w, router_bias, tm=512):
    T, D = x.shape
    E = router_w.shape[1]
    idx, gate, pos, cnt = pl.pallas_call(
        _router_kernel,
        grid=(T // tm,),
        in_specs=[pl.BlockSpec((tm, D), lambda i: (i, 0)),
                  pl.BlockSpec((E, D), lambda i: (0, 0)),
                  pl.BlockSpec((E, 1), lambda i: (0, 0))],
        out_specs=[pl.BlockSpec((TOP_K, tm), lambda i: (0, i)), pl.BlockSpec((TOP_K, tm), lambda i: (0, i)),
                   pl.BlockSpec((TOP_K, tm), lambda i: (0, i)), pl.BlockSpec((E, LANES), lambda i: (0, 0))],
        out_shape=[jax.ShapeDtypeStruct((TOP_K, T), jnp.int32), jax.ShapeDtypeStruct((TOP_K, T), F32),
                   jax.ShapeDtypeStruct((TOP_K, T), jnp.int32), jax.ShapeDtypeStruct((E, LANES), jnp.int32)],
        scratch_shapes=[pltpu.VMEM((E, LANES), F32)],
        compiler_params=_params("arbitrary"),
        name="router",
    )(x, router_w.T, router_bias.reshape(E, 1))
    return idx.T, gate.T, pos.T, cnt[:, 0]


def _dispatch_plan(top_e, pos, counts, rows_per_tile):
    T, K = top_e.shape
    E = counts.shape[0]
    n_tiles = (T * K) // rows_per_tile + E
    tiles = (counts + rows_per_tile - 1) // rows_per_tile
    tile_end = jnp.cumsum(tiles)
    n_used = tile_end[-1]
    row_start = (tile_end - tiles) * rows_per_tile
    dest = jnp.take(row_start, top_e) + pos
    tile_id = jnp.minimum(jnp.arange(n_tiles, dtype=jnp.int32), n_used - 1)
    tile_expert = jnp.minimum(jnp.sum(tile_end[None, :] <= tile_id[:, None], axis=1), E - 1)
    owns = tiles > 0
    expert = jnp.arange(E, dtype=jnp.int32)
    later = jnp.where(owns[None, :] & (expert[None, :] > expert[:, None]), expert[None, :], E)
    next_owner = jnp.min(later, axis=1)
    next_owner = jnp.where(next_owner == E, -1, next_owner)
    run_parity = (jnp.cumsum(owns.astype(jnp.int32)) - 1) % 2
    i32 = lambda a: a.astype(jnp.int32)
    return (i32(tile_expert), i32(jnp.take(next_owner, tile_expert)), i32(jnp.take(run_parity, tile_expert)),
            i32(tile_end), i32(n_used.reshape(1)), i32(dest))


def _dispatch_kernel(tend_ref, dest_ref, x_ref, xs_hbm, zero_ref, zsem, sem, *, slots):
    tm = x_ref.shape[0]
    rows = zero_ref.shape[0]
    n_experts = tend_ref.shape[0]

    @pl.when(pl.program_id(0) == 0)
    def _():
        zero_ref[...] = jnp.zeros_like(zero_ref)

        def last_tile(e):
            end = tend_ref[e]
            start = tend_ref[jnp.maximum(e - 1, 0)]
            start = jnp.where(e == 0, 0, start)
            return end > start, pltpu.make_async_copy(
                zero_ref, xs_hbm.at[pl.ds(jnp.maximum(end - 1, 0) * rows, rows)], zsem)

        def start(e, carry):
            owns, copy = last_tile(e)

            @pl.when(owns)
            def _():
                copy.start()
            return carry

        def wait(e, carry):
            owns, copy = last_tile(e)

            @pl.when(owns)
            def _():
                copy.wait()
            return carry

        def unused_tile(n):
            return pltpu.make_async_copy(zero_ref, xs_hbm.at[pl.ds(n * rows, rows)], zsem)

        def start_unused(n, carry):
            unused_tile(n).start()
            return carry

        def wait_unused(n, carry):
            unused_tile(n).wait()
            return carry

        n_used, n_tiles = tend_ref[n_experts - 1], xs_hbm.shape[0] // rows
        lax.fori_loop(0, n_experts, start, 0)
        lax.fori_loop(n_used, n_tiles, start_unused, 0)
        lax.fori_loop(0, n_experts, wait, 0)
        lax.fori_loop(n_used, n_tiles, wait_unused, 0)

    def body(t, carry):
        for k in range(slots):
            pltpu.make_async_copy(x_ref.at[pl.ds(t, 1)], xs_hbm.at[pl.ds(dest_ref[0, t * slots + k], 1)],
                                  sem).start(priority=k % 2)
        return carry
    lax.fori_loop(0, tm, body, 0, unroll=2)
    for k in range(slots):
        pltpu.make_async_copy(x_ref, xs_hbm.at[pl.ds(0, tm)], sem).wait()


def _dispatch(x3p, dest, tile_end, n_tiles, rows_per_tile, tm=512):
    T, half = x3p.shape
    K = dest.shape[1]
    steps = T // tm
    d3 = dest.reshape(steps, 1, tm * K)
    return pl.pallas_call(
        functools.partial(_dispatch_kernel, slots=K),
        grid_spec=pltpu.PrefetchScalarGridSpec(
            num_scalar_prefetch=1,
            grid=(steps,),
            in_specs=[pl.BlockSpec((None, 1, K * tm), lambda i, te: (i, 0, 0), memory_space=pltpu.SMEM),
                      pl.BlockSpec((tm, half), lambda i, te: (i, 0))],
            out_specs=pl.BlockSpec(memory_space=pl.ANY),
            scratch_shapes=[pltpu.VMEM((rows_per_tile, half), jnp.int32),
                            pltpu.SemaphoreType.DMA(()), pltpu.SemaphoreType.DMA(())]),
        out_shape=jax.ShapeDtypeStruct((n_tiles * rows_per_tile, half), jnp.int32),
        compiler_params=_params("arbitrary"),
        name="dispatch",
    )(tile_end, d3, x3p)


def _sparse_experts_kernel(te_ref, nu_ref, next_ref, par_ref, xs_ref, w1_hbm, w3_hbm, w2_hbm, y_ref,
                           wf1, wf3, wf2, wb1, wb3, wb2, sem, *, layer):
    n = pl.program_id(0)

    def weight_copies(e, slot):
        return [pltpu.make_async_copy(w.at[layer, e], buf.at[slot], sem.at[slot])
                for w, buf in ((w1_hbm, wf1), (w3_hbm, wf3), (w2_hbm, wf2))]

    @pl.when(n < nu_ref[0])
    def _():
        @pl.when((n == 0) | (te_ref[n] != te_ref[jnp.maximum(n - 1, 0)]))
        def _():
            slot = par_ref[n]

            @pl.when(n == 0)
            def _():
                for c in weight_copies(te_ref[n], slot):
                    c.start()

            @pl.when(next_ref[n] >= 0)
            def _():
                for c in weight_copies(next_ref[n], 1 - slot):
                    c.start()

            for c in weight_copies(te_ref[n], slot):
                c.wait()
            wb1[...] = wf1[slot].astype(BF16)
            wb3[...] = wf3[slot].astype(BF16)
            wb2[...] = wf2[slot].astype(BF16)

        x = _unpack_rows(xs_ref[...]).astype(BF16)
        h1 = jnp.dot(x, wb1[...], preferred_element_type=F32)
        h3 = jnp.dot(x, wb3[...], preferred_element_type=F32)
        hidden = (h1 * _sigmoid(h1) * h3).astype(BF16)
        y_ref[...] = _pack_rows(jnp.dot(hidden, wb2[...], preferred_element_type=F32))

    @pl.when(n >= nu_ref[0])
    def _():
        y_ref[...] = jnp.zeros_like(y_ref)


def _sparse_experts(xs, tile_expert, n_used, tile_next, tile_parity, w1, w3, w2, layer, rows):
    _, _, D, Hd = w1.shape
    n_tiles = tile_expert.shape[0]
    return pl.pallas_call(
        functools.partial(_sparse_experts_kernel, layer=layer),
        grid_spec=pltpu.PrefetchScalarGridSpec(
            num_scalar_prefetch=4,
            grid=(n_tiles,),
            in_specs=[pl.BlockSpec((rows, D // 2), lambda n, te, nu, nx, par: (jnp.minimum(n, nu[0] - 1), 0)),
                      pl.BlockSpec(memory_space=pl.ANY),
                      pl.BlockSpec(memory_space=pl.ANY),
                      pl.BlockSpec(memory_space=pl.ANY)],
            out_specs=pl.BlockSpec((rows, D // 2), lambda n, te, nu, nx, par: (n, 0)),
            scratch_shapes=[pltpu.VMEM((2, D, Hd), F32), pltpu.VMEM((2, D, Hd), F32), pltpu.VMEM((2, Hd, D), F32),
                            pltpu.VMEM((D, Hd), BF16), pltpu.VMEM((D, Hd), BF16), pltpu.VMEM((Hd, D), BF16),
                            pltpu.SemaphoreType.DMA((2,))]),
        out_shape=jax.ShapeDtypeStruct(xs.shape, jnp.int32),
        compiler_params=_params("arbitrary"),
        name="sparse_experts",
    )(tile_expert, n_used, tile_next, tile_parity, xs, w1, w3, w2)


def _combine_kernel(dest_ref, dest_next_ref, gate_ref, y_hbm, o_ref, buf, sem):
    i = pl.program_id(0)
    steps = pl.num_programs(0)
    K, tm = buf.shape[1:3]
    slot = i % 2

    def fetch(d_ref, s):
        def body(t, carry):
            for k in range(K):
                pltpu.make_async_copy(y_hbm.at[pl.ds(d_ref[0, t * K + k], 1)], buf.at[s, k, pl.ds(t, 1)],
                                      sem.at[s]).start(priority=k % 2)
            return carry
        lax.fori_loop(0, tm, body, 0, unroll=2)

    @pl.when(i == 0)
    def _():
        fetch(dest_ref, 0)

    @pl.when(i + 1 < steps)
    def _():
        fetch(dest_next_ref, 1 - slot)

    for k in range(K):
        pltpu.make_async_copy(y_hbm.at[pl.ds(0, tm)], buf.at[slot, k], sem.at[slot]).wait()
    gate = gate_ref[...]
    acc = jnp.zeros(o_ref.shape, F32)
    for k in range(K):
        acc = acc + gate[:, k:k + 1] * _unpack_rows(buf[slot, k])
    o_ref[...] = acc


def _combine(y, dest, gate, tm=256):
    T, K = dest.shape
    D = 2 * y.shape[1]
    steps = T // tm
    d3 = dest.reshape(steps, 1, tm * K)
    return pl.pallas_call(
        _combine_kernel,
        grid=(steps,),
        in_specs=[pl.BlockSpec((None, 1, K * tm), lambda i: (i, 0, 0), memory_space=pltpu.SMEM),
                  pl.BlockSpec((None, 1, K * tm), lambda i: (jnp.minimum(i + 1, steps - 1), 0, 0),
                               memory_space=pltpu.SMEM),
                  pl.BlockSpec((tm, K), lambda i: (i, 0)),
                  pl.BlockSpec(memory_space=pl.ANY)],
        out_specs=pl.BlockSpec((tm, D), lambda i: (i, 0)),
        out_shape=jax.ShapeDtypeStruct((T, D), F32),
        scratch_shapes=[pltpu.VMEM((2, K, tm, D // 2), jnp.int32), pltpu.SemaphoreType.DMA((2,))],
        compiler_params=_params("arbitrary"),
        name="combine",
    )(d3, d3, gate, y)


def _even_mixer(x2, xb, B, S, ln_g, ln_b, w_in, conv_w, conv_b, gate_b, ml_gn_w, rw_mu, rw_w0, rw_w2,
                rw_a0, rw_a2, rw_g2, rw_kk, rw_ka, rw_rk, rw_gn_w, rw_gn_b, w_out, j):
    D = x2.shape[1]
    H = ML_HEADS
    ML = ml_gn_w.shape[0]
    RW = rw_w0.shape[0]
    LO = RW_LORA_W + RW_LORA_A + RW_LORA_G
    ml_main = 3 * ML
    ml_cols = ml_main + 2 * H
    pad_to = 512
    used = ml_main + 3 * RW + LO + LANES
    total = -(-used // pad_to) * pad_to
    w_perm = jnp.concatenate([
        w_in[:, :ml_main], w_in[:, ml_cols:ml_cols + 3 * RW + LO], w_in[:, ml_main:ml_cols],
        jnp.zeros((D, total - used + LANES - 2 * H), w_in.dtype)], axis=1).astype(BF16)
    z3 = _mm(xb, w_perm[None], 0, BF16, 1024, pad_to).reshape(B, S, total)
    gate_col0 = ml_main + 3 * RW + LO
    L = ML_CHUNK
    gates_t = jnp.swapaxes(
        z3[:, :, gate_col0:gate_col0 + 2 * H].astype(F32).reshape(B, S // L, L, 2 * H), 2, 3)
    h_ml = _mlstm(z3, gates_t, conv_w, conv_b, gate_b, ml_gn_w,
                  qk_blk=0, v_blk=1, o_blk=2, gate_blk=gate_col0 // LANES)
    h_rw = _rwkv(z3, rw_mu, rw_w0, rw_w2, rw_a0, rw_a2, rw_g2, rw_kk, rw_ka, rw_rk, rw_gn_w, rw_gn_b,
                 col0=ml_main, lora_col0=ml_main + 3 * RW, width=RW)
    return _mm_res_ln([h_ml.reshape(B * S, ML), h_rw.reshape(B * S, RW)], w_out, j, x2, ln_g, ln_b)


def _odd_mixer(x2, xb, B, S, positions, ln_g, ln_b, w_in, gn_w, gn_b, w_out, j):
    z3 = _mm(xb, w_in, j, BF16, 1024, 512).reshape(B, S, w_in.shape[2])
    h = _retention(z3, positions, gn_w, gn_b)
    return _mm_res_ln([h.reshape(B * S, -1)], w_out, j, x2, ln_g, ln_b)


def _cross_attention(x2, xb, B, S, memb, ln_g, ln_b, wq, wkv, wo, layer):
    D = x2.shape[1]
    q = _mm(xb, wq, layer, BF16, 1024, 512)
    kv = _mm(memb, wkv, layer, BF16, memb.shape[0], 512)
    o = _xattn(q.reshape(B, S, D), kv.reshape(B, -1, 2 * D))
    return _mm_res_ln([o.reshape(B * S, D)], wo, layer, x2, ln_g, ln_b, packed=True)


def _moe(x2, xb, x3p, ln_g, ln_b, router_w, router_bias, w1, w3, w2, sw1, sw3, sw2, layer):
    rows = MOE_ROWS_PER_TILE
    top_e, gate, pos, counts = _router(x2, router_w, router_bias)
    tile_expert, tile_next, tile_parity, tile_end, n_used, dest = _dispatch_plan(top_e, pos, counts, rows)
    xs = _dispatch(x3p, dest, tile_end, tile_expert.shape[0], rows)
    y = _sparse_experts(xs, tile_expert, n_used, tile_next, tile_parity, w1, w3, w2, layer, rows)
    routed = _combine(y, dest, gate)
    hs = _glu(xb, sw1, sw3, layer)
    return _mm_res_ln([hs], sw2, layer, x2, ln_g, ln_b, add=routed)


def kernel(x, mem, positions, ln_g, ln_b, xa_wq, xa_wkv, xa_wo, router_w, router_bias, moe_w1, moe_w3, moe_w2, sh_w1, sh_w3, sh_w2, ev_w_in, ml_conv_w, ml_conv_b, ml_gate_b, ml_gn_w, rw_mu, rw_w0, rw_w2, rw_a0, rw_a2, rw_g2, rw_kk, rw_ka, rw_rk, rw_gn_w, rw_gn_b, ev_w_out, od_w_in, ret_gn_w, ret_gn_b, od_w_out):
    B, S, D = x.shape
    x2 = x.reshape(B * S, D)
    xb = x2.astype(BF16)
    memb = mem.reshape(-1, D).astype(BF16)
    (xa_wq, xa_wkv, xa_wo, sh_w1, sh_w3, sh_w2, ev_w_out, od_w_in, od_w_out) = (
        w.astype(BF16) for w in (xa_wq, xa_wkv, xa_wo, sh_w1, sh_w3, sh_w2, ev_w_out, od_w_in, od_w_out))
    for layer in range(ln_g.shape[0]):
        j = layer // 2
        if layer % 2 == 0:
            x2, xb = _even_mixer(x2, xb, B, S, ln_g[layer, 0], ln_b[layer, 0], ev_w_in[j], ml_conv_w[j],
                                 ml_conv_b[j], ml_gate_b[j], ml_gn_w[j], rw_mu[j], rw_w0[j], rw_w2[j],
                                 rw_a0[j], rw_a2[j], rw_g2[j], rw_kk[j], rw_ka[j], rw_rk[j],
                                 rw_gn_w[j], rw_gn_b[j], ev_w_out, j)
        else:
            x2, xb = _odd_mixer(x2, xb, B, S, positions, ln_g[layer, 0], ln_b[layer, 0], od_w_in,
                                ret_gn_w[j], ret_gn_b[j], od_w_out, j)
        x2, xb, x3p = _cross_attention(x2, xb, B, S, memb, ln_g[layer, 1], ln_b[layer, 1], xa_wq, xa_wkv,
                                       xa_wo, layer)
        x2, xb = _moe(x2, xb, x3p, ln_g[layer, 2], ln_b[layer, 2], router_w[layer], router_bias[layer],
                      moe_w1, moe_w3, moe_w2, sh_w1, sh_w3, sh_w2, layer)
    return x2.reshape(B, S, D)
```

```python
import functools

import numpy as np
import jax
import jax.numpy as jnp
from jax import lax
from jax.experimental import pallas as pl
from jax.experimental.pallas import tpu as pltpu

F32 = jnp.float32
BF16 = jnp.bfloat16
HI = lax.Precision.HIGHEST

DEPTH = 4
ALPHA = (2.0 * DEPTH) ** 0.25
LN_EPS = 1e-5
NEG_BIG = -1e30
CONV_K = 4
ML_HEADS = 4
RW_HEAD = 64
RW_LORA_W = 64
RW_LORA_A = 64
RW_LORA_G = 128
RW_GN_EPS = 64e-5
RET_HEADS = 8
ROPE_BASE = 10000.0
XA_HEADS = 4
TOP_K = 8
N_GROUPS = 8
TOPK_GROUPS = 4
ROUTED_SCALE = 2.5

ML_CHUNK = 64
RW_CHUNK = 64
RET_CHUNK = 256
MOE_ROWS_PER_TILE = 512

V7X_VMEM_BYTES = 64 * 2 ** 20
VMEM_LIMIT = (V7X_VMEM_BYTES * 3) // 4
LANES = 128
V7X_MXU_DIM = 256


def _params(*sem):
    return pltpu.CompilerParams(dimension_semantics=sem, vmem_limit_bytes=VMEM_LIMIT)


def _bdot(a, b):
    return jnp.dot(a.astype(BF16), b.astype(BF16), preferred_element_type=F32)


def _bdot_nt(a, b):
    return lax.dot_general(a.astype(BF16), b.astype(BF16), (((1,), (1,)), ((), ())),
                           preferred_element_type=F32)


def _bdot_tn(a, b):
    return lax.dot_general(a.astype(BF16), b.astype(BF16), (((0,), (0,)), ((), ())),
                           preferred_element_type=F32)


def _sigmoid(x):
    return 1.0 / (1.0 + jnp.exp(-x))


def _softplus(x):
    return jnp.maximum(x, 0.0) + jnp.log1p(jnp.exp(-jnp.abs(x)))


def _iota2(shape, dim):
    return lax.broadcasted_iota(jnp.int32, shape, dim)


def _mm_kernel(x_ref, w_ref, o_ref):
    o_ref[...] = _bdot(x_ref[...], w_ref[...]).astype(o_ref.dtype)


def _mm(x, w, layer, out_dtype, tm, tn):
    M, K = x.shape
    N = w.shape[2]
    assert M % tm == 0 and N % tn == 0
    return pl.pallas_call(
        _mm_kernel,
        grid=(M // tm, N // tn),
        in_specs=[pl.BlockSpec((tm, K), lambda i, j: (i, 0)),
                  pl.BlockSpec((None, K, tn), lambda i, j: (layer, 0, j))],
        out_specs=pl.BlockSpec((tm, tn), lambda i, j: (i, j)),
        out_shape=jax.ShapeDtypeStruct((M, N), out_dtype),
        compiler_params=_params("parallel", "parallel"),
        name="mm",
    )(x, w)


def _layer_norm_rows(y, g, b):
    mu = jnp.mean(y, axis=-1, keepdims=True)
    d = y - mu
    var = jnp.mean(d * d, axis=-1, keepdims=True)
    return d * lax.rsqrt(var + LN_EPS) * g + b


def _pack_rows(v):
    half = v.shape[1] // 2
    bits = lax.bitcast_convert_type(v.astype(BF16).astype(F32), jnp.int32)
    return bits[:, half:] | lax.shift_right_logical(bits[:, :half], 16)


def _unpack_rows(w):
    return jnp.concatenate([lax.bitcast_convert_type(lax.shift_left(w, 16), F32),
                            lax.bitcast_convert_type(w & jnp.int32(-65536), F32)], axis=-1)


def _mm_res_ln_kernel(*refs, n_a, has_add, packed):
    a_refs, (w_ref, res_ref), rest = refs[:n_a], refs[n_a:n_a + 2], refs[n_a + 2:]
    add_ref = rest[0] if has_add else None
    n_out = 3 if packed else 2
    g_ref, b_ref = rest[-n_out - 2:-n_out]
    o_ref, ob_ref = rest[-n_out:][:2]
    acc, k0 = None, 0
    for a_ref in a_refs:
        part = _bdot(a_ref[...], w_ref[k0:k0 + a_ref.shape[1], :])
        acc = part if acc is None else acc + part
        k0 += a_ref.shape[1]
    if has_add:
        acc = add_ref[...] + acc
    out = _layer_norm_rows(ALPHA * res_ref[...] + acc, g_ref[...], b_ref[...])
    o_ref[...] = out
    ob_ref[...] = out.astype(ob_ref.dtype)
    if packed:
        rest[-1][...] = _pack_rows(out)


def _mm_res_ln(a_parts, w, layer, res, g, b, add=None, packed=False, tm=256):
    M = res.shape[0]
    K, N = w.shape[1:]
    assert sum(a.shape[1] for a in a_parts) == K
    row = lambda i: (i, 0)
    fixed = lambda i: (0, 0)
    in_specs = [pl.BlockSpec((tm, a.shape[1]), row) for a in a_parts]
    in_specs += [pl.BlockSpec((None, K, N), lambda i: (layer, 0, 0)), pl.BlockSpec((tm, N), row)]
    args = [*a_parts, w, res]
    if add is not None:
        in_specs.append(pl.BlockSpec((tm, N), row))
        args.append(add)
    in_specs += [pl.BlockSpec((1, N), fixed), pl.BlockSpec((1, N), fixed)]
    args += [g.reshape(1, N), b.reshape(1, N)]
    out_specs = [pl.BlockSpec((tm, N), row), pl.BlockSpec((tm, N), row)]
    out_shape = [jax.ShapeDtypeStruct((M, N), F32), jax.ShapeDtypeStruct((M, N), BF16)]
    if packed:
        out_specs.append(pl.BlockSpec((tm, N // 2), row))
        out_shape.append(jax.ShapeDtypeStruct((M, N // 2), jnp.int32))
    return pl.pallas_call(
        functools.partial(_mm_res_ln_kernel, n_a=len(a_parts), has_add=add is not None, packed=packed),
        grid=(M // tm,),
        in_specs=in_specs,
        out_specs=out_specs,
        out_shape=out_shape,
        compiler_params=_params("parallel"),
        name="mm_res_ln",
    )(*args)


def _glu_kernel(x_ref, w1_ref, w3_ref, o_ref):
    x = x_ref[...]
    h1 = jnp.dot(x, w1_ref[...], preferred_element_type=F32)
    h3 = jnp.dot(x, w3_ref[...], preferred_element_type=F32)
    o_ref[...] = (h1 * _sigmoid(h1) * h3).astype(o_ref.dtype)


def _glu(xb, w1, w3, layer, tm=512):
    M, K = xb.shape
    N = w1.shape[2]
    wspec = pl.BlockSpec((None, K, N), lambda i: (layer, 0, 0))
    return pl.pallas_call(
        _glu_kernel,
        grid=(M // tm,),
        in_specs=[pl.BlockSpec((tm, K), lambda i: (i, 0)), wspec, wspec],
        out_specs=pl.BlockSpec((tm, N), lambda i: (i, 0)),
        out_shape=jax.ShapeDtypeStruct((M, N), BF16),
        compiler_params=_params("parallel"),
        name="glu",
    )(xb, w1, w3)


def _log_sigmoid(x):
    return jnp.minimum(x, 0.0) - jnp.log1p(jnp.exp(-jnp.abs(x)))


def _split_bf16(x, terms):
    pieces = []
    for _ in range(terms):
        p = x.astype(BF16)
        pieces.append(p)
        x = x - p.astype(F32)
    return pieces


def _mlstm_kernel(qk_ref, v_ref, o_ref, gc_ref, gr_ref, cw_ref, cb_ref, gbc_ref, gbr_ref, gn_ref,
                  out_ref, prev_ref, c_ref, n_ref, m_ref, *, heads, dk, dv):
    nb, L, _ = qk_ref.shape
    qw = heads * dk

    @pl.when(pl.program_id(0) == 0)
    def _():
        prev_ref[...] = jnp.zeros_like(prev_ref)
        c_ref[...] = jnp.zeros_like(c_ref)
        n_ref[...] = jnp.zeros_like(n_ref)
        m_ref[...] = jnp.full_like(m_ref, NEG_BIG)

    row = _iota2((L, 1), 0)
    qk = []
    for b in range(nb):
        cur = qk_ref[b].astype(F32)
        prev = prev_ref[b]
        acc = cb_ref[...] + cw_ref[CONV_K - 1:CONV_K, :] * cur
        for j in range(CONV_K - 1):
            s = CONV_K - 1 - j
            shifted = jnp.where(row < s, pltpu.roll(prev, s, 0), pltpu.roll(cur, s, 0))
            acc = acc + cw_ref[j:j + 1, :] * shifted
        prev_ref[b] = cur
        qk.append(acc * _sigmoid(acc))

    causal = _iota2((L, L), 0) >= _iota2((L, L), 1)
    tri = causal.astype(F32).astype(BF16)
    g_col = [gc_ref[b].astype(F32) + gbc_ref[...] for b in range(nb)]
    pieces = _split_bf16(jnp.concatenate([_log_sigmoid(g) for g in g_col], axis=1), 3)
    lanes = nb * g_col[0].shape[1]
    b_col = jnp.dot(tri, jnp.concatenate(pieces, axis=1), preferred_element_type=F32)
    b_col = b_col[:, :lanes] + b_col[:, lanes:2 * lanes] + b_col[:, 2 * lanes:]
    g_row = [gr_ref[b, 0] + gbr_ref[...] for b in range(nb)]
    pieces = _split_bf16(jnp.concatenate([_log_sigmoid(g) for g in g_row], axis=0), 3)
    rows = nb * 2 * heads
    b_row = lax.dot_general(jnp.concatenate(pieces, axis=0), tri, (((1,), (1,)), ((), ())),
                            preferred_element_type=F32)
    b_row = b_row[:rows] + b_row[rows:2 * rows] + b_row[2 * rows:]

    def per_problem(f):
        return jnp.stack([f(b, h) for b in range(nb) for h in range(heads)])

    q = per_problem(lambda b, h: qk[b][:, h * dk:(h + 1) * dk])
    k = per_problem(lambda b, h: qk[b][:, qw + h * dk:qw + (h + 1) * dk]) * dk ** -0.5
    v = per_problem(lambda b, h: v_ref[b, :, h * dv:(h + 1) * dv])
    gcw = g_col[0].shape[1]
    bc = per_problem(lambda b, h: b_col[:, b * gcw + heads + h:b * gcw + heads + h + 1])
    ic = per_problem(lambda b, h: g_col[b][:, h:h + 1])
    br = per_problem(lambda b, h: b_row[b * 2 * heads + heads + h:b * 2 * heads + heads + h + 1])
    ir = per_problem(lambda b, h: g_row[b][h:h + 1])
    b_last = bc[:, L - 1:L]
    c_prev = c_ref[...]
    n_prev = n_ref[...]
    m_prev = m_ref[...][:, :, 0:1]

    def bmm(x, y, spec):
        return jnp.einsum(spec, x.astype(BF16), y.astype(BF16), preferred_element_type=F32)

    log_d = jnp.where(causal, bc - br + ir, -jnp.inf)
    g_inter = bc + m_prev
    m_t = jnp.maximum(jnp.max(log_d, axis=-1, keepdims=True), g_inter)
    p = jnp.exp(log_d - m_t) * bmm(q, k, "gtk,gsk->gts")
    e_inter = jnp.exp(g_inter - m_t)
    num = bmm(p, v, "gts,gsv->gtv") + e_inter * bmm(q, c_prev, "gtk,gkv->gtv")
    den = jnp.sum(p, axis=-1, keepdims=True) + e_inter * jnp.sum(q * n_prev, axis=-1, keepdims=True)
    hh = num / jnp.maximum(jnp.abs(den), jnp.exp(-m_t))

    w_end = b_last - bc + ic
    m_loc = jnp.max(w_end, axis=1, keepdims=True)
    e_end = jnp.exp(w_end - m_loc)
    ke = k * e_end
    m_new = jnp.maximum(b_last + m_prev, m_loc)
    s_prev = jnp.exp(b_last + m_prev - m_new)
    s_new = jnp.exp(m_loc - m_new)
    c_ref[...] = s_prev * c_prev + s_new * bmm(ke, v, "gsk,gsv->gkv")
    n_ref[...] = s_prev * n_prev + s_new * jnp.sum(ke, axis=1, keepdims=True)
    m_ref[...] = jnp.broadcast_to(m_new, m_ref.shape)

    mu = jnp.mean(hh, axis=-1, keepdims=True)
    d = hh - mu
    var = jnp.mean(d * d, axis=-1, keepdims=True)
    y = d * lax.rsqrt(var + 1e-6)
    for b in range(nb):
        for h in range(heads):
            sl = slice(h * dv, (h + 1) * dv)
            gate = _sigmoid(o_ref[b, :, sl].astype(F32))
            out_ref[b, :, sl] = (y[b * heads + h] * gn_ref[:, sl] * gate).astype(out_ref.dtype)


def _mlstm(z3, gates_t, conv_w, conv_b, gate_b, gn_w, *, qk_blk, v_blk, o_blk, gate_blk):
    B, S, _ = z3.shape
    H = ML_HEADS
    W = gn_w.shape[0]
    dv = W // H
    dk = conv_w.shape[1] // (2 * H)
    assert conv_w.shape[1] == W, "q|k block and v block share one column-block width"
    L = ML_CHUNK
    gb_col = jnp.zeros((1, LANES), F32).at[0, :2 * H].set(gate_b.reshape(-1))
    gb_row = gate_b.reshape(2 * H, 1)
    fixed = lambda c: (0, 0)
    return pl.pallas_call(
        functools.partial(_mlstm_kernel, heads=H, dk=dk, dv=dv),
        grid=(S // L,),
        in_specs=[pl.BlockSpec((B, L, W), lambda c: (0, c, qk_blk)),
                  pl.BlockSpec((B, L, W), lambda c: (0, c, v_blk)),
                  pl.BlockSpec((B, L, W), lambda c: (0, c, o_blk)),
                  pl.BlockSpec((B, L, LANES), lambda c: (0, c, gate_blk)),
                  pl.BlockSpec((B, 1, 2 * H, L), lambda c: (0, c, 0, 0)),
                  pl.BlockSpec((CONV_K, W), fixed),
                  pl.BlockSpec((1, W), fixed),
                  pl.BlockSpec((1, LANES), fixed),
                  pl.BlockSpec((2 * H, 1), fixed),
                  pl.BlockSpec((1, W), fixed)],
        out_specs=pl.BlockSpec((B, L, W), lambda c: (0, c, 0)),
        out_shape=jax.ShapeDtypeStruct((B, S, W), BF16),
        scratch_shapes=[pltpu.VMEM((B, L, W), F32),
                        pltpu.VMEM((B * H, dk, dv), F32),
                        pltpu.VMEM((B * H, 1, dk), F32),
                        pltpu.VMEM((B * H, 1, LANES), F32)],
        compiler_params=_params("arbitrary"),
        name="mlstm",
    )(z3, z3, z3, z3, gates_t, conv_w, conv_b.reshape(1, W), gb_col, gb_row, gn_w.reshape(1, W))


def _unit_lower_inverse(a, t_idx, i_idx, block_diag, matmul):
    n = a.shape[-2]
    eye = (t_idx == i_idx).astype(F32)
    t = None
    s = 1
    while s < n:
        pair = (t_idx // (2 * s)) == (i_idx // (2 * s))
        off = pair & ((t_idx // s) % 2 == 1) & ((i_idx // s) % 2 == 0)
        q = jnp.where(off, a, 0.0)
        t = eye - q if t is None else t - matmul(matmul(t, block_diag(q)), block_diag(t))
        s *= 2
    return t


def _rwkv_kernel(r_ref, k_ref, v_ref, l_ref, mur_ref, muk_ref, muv_ref, mul_ref, w0_ref, w2_ref,
                 a0_ref, a2_ref, g2_ref, kk_ref, ka_ref, rk_ref, gnw_ref, gnb_ref, out_ref,
                 pr_ref, pk_ref, pv_ref, plo_ref, s_ref, *, hd, gw):
    nb, L, W = r_ref.shape
    ng = W // gw
    hpg = gw // hd
    assert hpg * L == gw

    @pl.when(pl.program_id(1) == 0)
    def _():
        pr_ref[...] = jnp.zeros_like(pr_ref)
        pk_ref[...] = jnp.zeros_like(pk_ref)
        pv_ref[...] = jnp.zeros_like(pv_ref)
        plo_ref[...] = jnp.zeros_like(plo_ref)
        s_ref[...] = jnp.zeros_like(s_ref)

    R = nb * L
    first = _iota2((L, 1), 0) == 0
    same_head = (_iota2((gw, gw), 0) // hd) == (_iota2((gw, gw), 1) // hd)
    zero = jnp.zeros((), BF16)
    same_head_b = same_head.astype(F32).astype(BF16)
    row, col = _iota2((R, R), 0), _iota2((R, R), 1)
    tri = ((row // L == col // L) & (row >= col)).astype(F32).astype(BF16)
    t_idx = _iota2((L, gw), 0)
    i_idx = _iota2((L, gw), 1) % L
    strict = t_idx > i_idx
    lower = t_idx >= i_idx

    def shift_lerp(x_ref, p_ref, mu_ref):
        parts = []
        for b in range(nb):
            cur = x_ref[b].astype(F32)
            shifted = jnp.where(first, p_ref[b], pltpu.roll(cur, 1, 0))
            p_ref[b] = cur[L - 1:L, :]
            parts.append(cur + mu_ref[...] * (shifted - cur))
        return jnp.concatenate(parts, axis=0)

    def head_sums(xs):
        stacked = jnp.concatenate([x[:, i * gw:(i + 1) * gw] for x in xs for i in range(ng)], axis=0)
        n = stacked.shape[0]
        s = jnp.dot(jnp.concatenate(_split_bf16(stacked, 2), axis=0), same_head_b, preferred_element_type=F32)
        s = s[:n] + s[n:]
        return [jnp.concatenate([s[(j * ng + i) * R:(j * ng + i + 1) * R] for i in range(ng)], axis=1)
                for j in range(len(xs))]

    r = shift_lerp(r_ref, pr_ref, mur_ref)
    k = shift_lerp(k_ref, pk_ref, muk_ref)
    v = shift_lerp(v_ref, pv_ref, muv_ref)
    lo = shift_lerp(l_ref, plo_ref, mul_ref)
    xw = lo[:, :RW_LORA_W]
    xa = lo[:, RW_LORA_W:RW_LORA_W + RW_LORA_A]
    xg = lo[:, RW_LORA_W + RW_LORA_A:]

    w_log = -jnp.exp(-_softplus(-(w0_ref[...] + _bdot(jnp.tanh(xw), w2_ref[...]))) - 0.5)
    a = _sigmoid(a0_ref[...] + _bdot(xa, a2_ref[...]))
    g = _bdot(_sigmoid(xg), g2_ref[...])

    kk = k * kk_ref[...]
    k = k * (1.0 + (a - 1.0) * ka_ref[...])
    kk_sq, rk_sum = head_sums([kk * kk, r * k * rk_ref[...]])
    kk = kk * lax.rsqrt(jnp.maximum(kk_sq, 1e-24))
    bv = kk * a

    c3 = jnp.dot(tri, jnp.concatenate(_split_bf16(w_log, 3), axis=1), preferred_element_type=F32)
    c = c3[:, :W] + c3[:, W:2 * W] + c3[:, 2 * W:]
    c_last = jnp.concatenate([jnp.broadcast_to(c[(b + 1) * L - 1:(b + 1) * L], (L, W)) for b in range(nb)],
                             axis=0)
    e_end = jnp.exp(c_last - c)
    e_neg = jnp.exp(-c)
    kt = (kk * jnp.exp(c - w_log)).astype(BF16)
    rt = (r * jnp.exp(c)).astype(BF16)
    kh = (k * e_neg).astype(BF16)
    bh = (bv * e_neg).astype(BF16)
    kp = (k * e_end).astype(BF16)
    bp = (bv * e_end).astype(BF16)
    vb = v.astype(BF16)
    p_last = jnp.exp(c_last)

    def groups(x):
        return jnp.stack([x[b * L:(b + 1) * L, i * gw:(i + 1) * gw] for b in range(nb) for i in range(ng)])

    def block_diag(x):
        return jnp.where(same_head, jnp.concatenate([x.astype(BF16)] * hpg, axis=1), zero)

    def bmm(x, y):
        return jnp.einsum("gmk,gkn->gmn", x.astype(BF16), y.astype(BF16), preferred_element_type=F32)

    def bmm_nt(x, y):
        return jnp.einsum("gmk,gnk->gmn", x.astype(BF16), y.astype(BF16), preferred_element_type=F32)

    lhs = jnp.concatenate([groups(kt), groups(rt)], axis=1)
    m_k = bmm_nt(lhs, block_diag(groups(kh)))
    m_b = bmm_nt(lhs, block_diag(groups(bh)))
    a_kk = jnp.where(strict, m_k[:, :L], 0.0)
    a_kb = jnp.where(strict, m_b[:, :L], 0.0)
    a_rk = jnp.where(lower, m_k[:, L:], 0.0)
    a_rb = jnp.where(lower, m_b[:, L:], 0.0)
    t_inv = _unit_lower_inverse(a_kb, t_idx, i_idx, block_diag, bmm)
    st = s_ref[...]
    gs = bmm_nt(lhs, st)
    vg = groups(vb)
    v_bd = block_diag(vg)
    u = bmm(t_inv, block_diag(gs[:, :L] + bmm(a_kk, v_bd)))
    yg = gs[:, L:] + bmm(jnp.concatenate([a_rk, -a_rb], axis=2), jnp.concatenate([v_bd, block_diag(u)], axis=1))
    vu = jnp.concatenate([vg, (-u).astype(BF16)], axis=1)
    kb = jnp.concatenate([groups(kp), groups(bp)], axis=1)
    pg = groups(p_last)[:, :1]
    outer = jnp.einsum("gtv,gtk->gvk", vu, kb, preferred_element_type=F32)
    s_ref[...] = st * pg + jnp.where(same_head, outer, 0.0)
    ys = [jnp.concatenate([yg[b * ng + i] for i in range(ng)], axis=1) for b in range(nb)]

    y = jnp.concatenate(ys, axis=0)
    mu = head_sums([y])[0] * (1.0 / hd)
    d = y - mu
    var = head_sums([d * d])[0] * (1.0 / hd)
    yn = d * lax.rsqrt(var + RW_GN_EPS) * gnw_ref[...] + gnb_ref[...]
    out = ((yn + rk_sum * v) * g).astype(out_ref.dtype)
    for b in range(nb):
        out_ref[b] = out[b * L:(b + 1) * L]


def _rwkv(z3, mu, w0, w2, a0, a2, g2, k_k, k_a, r_k, gn_w, gn_b, *, col0, lora_col0, width):
    B, S, _ = z3.shape
    RW = w0.shape[0]
    W = width
    G = RW // W
    L = RW_CHUNK
    LO = RW_LORA_W + RW_LORA_A + RW_LORA_G
    assert col0 % W == 0 and RW % W == 0 and lora_col0 % LO == 0 and W % V7X_MXU_DIM == 0
    zc = lambda off: (lambda g, c: (0, c, off // W + g))
    vec = lambda g, c: (0, g)
    row = lambda x: x.reshape(1, -1)
    return pl.pallas_call(
        functools.partial(_rwkv_kernel, hd=RW_HEAD, gw=V7X_MXU_DIM),
        grid=(G, S // L),
        in_specs=[pl.BlockSpec((B, L, W), zc(col0)),
                  pl.BlockSpec((B, L, W), zc(col0 + RW)),
                  pl.BlockSpec((B, L, W), zc(col0 + 2 * RW)),
                  pl.BlockSpec((B, L, LO), lambda g, c: (0, c, lora_col0 // LO)),
                  pl.BlockSpec((1, W), vec), pl.BlockSpec((1, W), vec), pl.BlockSpec((1, W), vec),
                  pl.BlockSpec((1, LO), lambda g, c: (0, 0)),
                  pl.BlockSpec((1, W), vec),
                  pl.BlockSpec((RW_LORA_W, W), vec),
                  pl.BlockSpec((1, W), vec),
                  pl.BlockSpec((RW_LORA_A, W), vec),
                  pl.BlockSpec((RW_LORA_G, W), vec),
                  pl.BlockSpec((1, W), vec), pl.BlockSpec((1, W), vec), pl.BlockSpec((1, W), vec),
                  pl.BlockSpec((1, W), vec), pl.BlockSpec((1, W), vec)],
        out_specs=pl.BlockSpec((B, L, W), lambda g, c: (0, c, g)),
        out_shape=jax.ShapeDtypeStruct((B, S, RW), BF16),
        scratch_shapes=[pltpu.VMEM((B, 1, W), F32), pltpu.VMEM((B, 1, W), F32), pltpu.VMEM((B, 1, W), F32),
                        pltpu.VMEM((B, 1, LO), F32),
                        pltpu.VMEM((B * (W // V7X_MXU_DIM), V7X_MXU_DIM, V7X_MXU_DIM), F32)],
        compiler_params=_params("parallel", "arbitrary"),
        name="rwkv7",
    )(z3, z3, z3, z3, row(mu[:RW]), row(mu[RW:2 * RW]), row(mu[2 * RW:3 * RW]), row(mu[3 * RW:]),
      row(w0), w2.astype(BF16), row(a0), a2.astype(BF16), g2.astype(BF16), row(k_k), row(k_a),
      row(r_k), row(gn_w), row(gn_b))


def _retention_kernel(q_ref, k_ref, v_ref, g_ref, pos_ref, inv_ref, dec_ref, gnw_ref, gnb_ref,
                      out_ref, r_ref, *, heads, log_gamma):
    L = q_ref.shape[1]
    dk = q_ref.shape[2] // heads
    dv = v_ref.shape[2] // heads
    half = dk // 2

    @pl.when(pl.program_id(1) == 0)
    def _():
        r_ref[...] = jnp.zeros_like(r_ref)

    ang = pos_ref[0] * inv_ref[...]
    cos = jnp.cos(ang)
    sin = jnp.sin(ang)
    t_in = _iota2((L, 1), 0).astype(F32)

    def rot(t):
        t1, t2 = t[:, :half], t[:, half:]
        return jnp.concatenate([t1 * cos - t2 * sin, t1 * sin + t2 * cos], axis=-1)

    for h in range(heads):
        lg = log_gamma[h]
        q = rot(q_ref[0, :, h * dk:(h + 1) * dk].astype(F32))
        k = rot(k_ref[0, :, h * dk:(h + 1) * dk].astype(F32)) * dk ** -0.5
        v = v_ref[0, :, h * dv:(h + 1) * dv]
        state = r_ref[h]
        inter = _bdot(q, state) * jnp.exp(lg * (t_in + 1.0))
        intra = _bdot(_bdot_nt(q, k) * dec_ref[h], v)
        r_ref[h] = state * float(np.exp(lg * L)) + _bdot_tn(k * jnp.exp(lg * (L - 1.0 - t_in)), v)
        y = intra + inter
        mu = jnp.mean(y, axis=-1, keepdims=True)
        d = y - mu
        var = jnp.mean(d * d, axis=-1, keepdims=True)
        yn = d * lax.rsqrt(var + 1e-6) * gnw_ref[:, h * dv:(h + 1) * dv] + gnb_ref[:, h * dv:(h + 1) * dv]
        gate = g_ref[0, :, h * dv:(h + 1) * dv].astype(F32)
        out_ref[0, :, h * dv:(h + 1) * dv] = (gate * _sigmoid(gate) * yn).astype(out_ref.dtype)


def _retention(z3, positions, gn_w, gn_b):
    B, S, _ = z3.shape
    W = gn_w.shape[0]
    H = RET_HEADS
    dk = W // H
    half = dk // 2
    L = RET_CHUNK
    log_gamma = np.log1p(-np.exp2(-5.0 - np.arange(H, dtype=np.float64)))
    rel = np.arange(L)[:, None] - np.arange(L)[None, :]
    decay = np.where(rel >= 0, np.exp(log_gamma[:, None, None] * np.maximum(rel, 0)), 0.0).astype(np.float32)
    inv = (ROPE_BASE ** (-jnp.arange(half, dtype=F32) / half)).reshape(1, half)
    pos = positions.astype(F32).reshape(B, S, 1)
    blk = lambda j: pl.BlockSpec((1, L, W), lambda b, c: (b, c, j))
    fixed2 = lambda b, c: (0, 0)
    return pl.pallas_call(
        functools.partial(_retention_kernel, heads=H, log_gamma=tuple(float(x) for x in log_gamma)),
        grid=(B, S // L),
        in_specs=[blk(0), blk(1), blk(2), blk(3),
                  pl.BlockSpec((1, L, 1), lambda b, c: (b, c, 0)),
                  pl.BlockSpec((1, half), fixed2),
                  pl.BlockSpec((H, L, L), lambda b, c: (0, 0, 0)),
                  pl.BlockSpec((1, W), fixed2), pl.BlockSpec((1, W), fixed2)],
        out_specs=pl.BlockSpec((1, L, W), lambda b, c: (b, c, 0)),
        out_shape=jax.ShapeDtypeStruct((B, S, W), BF16),
        scratch_shapes=[pltpu.VMEM((H, dk, W // H), F32)],
        compiler_params=_params("parallel", "arbitrary"),
        name="retention",
    )(z3, z3, z3, z3, pos, inv, jnp.asarray(decay), gn_w.reshape(1, W), gn_b.reshape(1, W))


def _xattn_kernel(q_ref, k_ref, v_ref, o_ref, *, heads):
    hd = q_ref.shape[2] // heads
    q, k, v = (jnp.stack([r[0, :, h * hd:(h + 1) * hd] for h in range(heads)]) for r in (q_ref, k_ref, v_ref))
    s = jnp.einsum("hqd,hmd->hqm", q, k, preferred_element_type=F32) * hd ** -0.5
    e = jnp.exp(s - jnp.max(s, axis=-1, keepdims=True))
    p = e / jnp.sum(e, axis=-1, keepdims=True)
    o = jnp.einsum("hqm,hmd->hqd", p.astype(BF16), v, preferred_element_type=F32)
    for h in range(heads):
        o_ref[0, :, h * hd:(h + 1) * hd] = o[h].astype(o_ref.dtype)


def _xattn(q3, kv3, tq=512):
    B, S, D = q3.shape
    M = kv3.shape[1]
    return pl.pallas_call(
        functools.partial(_xattn_kernel, heads=XA_HEADS),
        grid=(B, S // tq),
        in_specs=[pl.BlockSpec((1, tq, D), lambda b, i: (b, i, 0)),
                  pl.BlockSpec((1, M, D), lambda b, i: (b, 0, 0)),
                  pl.BlockSpec((1, M, D), lambda b, i: (b, 0, 1))],
        out_specs=pl.BlockSpec((1, tq, D), lambda b, i: (b, i, 0)),
        out_shape=jax.ShapeDtypeStruct((B, S, D), BF16),
        compiler_params=_params("parallel", "parallel"),
        name="xattn",
    )(q3, kv3, kv3)


def _first_argmax(vals, index, big):
    m = jnp.max(vals, axis=0, keepdims=True)
    idx = jnp.min(jnp.where(vals == m, index, big), axis=0, keepdims=True)
    return m, idx


def _router_kernel(x_ref, wt_ref, bias_ref, idx_ref, gate_ref, pos_ref, cnt_ref, carry_ref):
    E = wt_ref.shape[0]
    tm = x_ref.shape[0]
    per = E // N_GROUPS
    neg = -jnp.inf
    logits = lax.dot_general(wt_ref[...], x_ref[...], (((1,), (1,)), ((), ())), precision=HI,
                             preferred_element_type=F32)
    scores = _sigmoid(logits)
    biased = scores + bias_ref[...]
    eidx = _iota2((E, tm), 0)

    within = _iota2((per, tm), 0)
    group_scores = []
    for g in range(N_GROUPS):
        vals = biased[g * per:(g + 1) * per]
        m1, i1 = _first_argmax(vals, within, per)
        m2 = jnp.max(jnp.where(within == i1, neg, vals), axis=0, keepdims=True)
        group_scores.append(m1 + m2)
    gs = jnp.concatenate(group_scores, axis=0)

    gidx = _iota2((N_GROUPS, tm), 0)
    keep = jnp.zeros((N_GROUPS, tm), F32)
    for _ in range(TOPK_GROUPS):
        _, gi = _first_argmax(gs, gidx, N_GROUPS)
        hit = gidx == gi
        keep = jnp.where(hit, 1.0, keep)
        gs = jnp.where(hit, neg, gs)
    keep_e = jnp.concatenate([jnp.broadcast_to(keep[g:g + 1], (per, tm)) for g in range(N_GROUPS)], axis=0)

    cand = jnp.where(keep_e > 0.0, biased, neg)
    picks, affs = [], []
    for _ in range(TOP_K):
        _, ei = _first_argmax(cand, eidx, E)
        hit = eidx == ei
        picks.append(ei)
        affs.append(jnp.sum(jnp.where(hit, scores, 0.0), axis=0, keepdims=True))
        cand = jnp.where(hit, neg, cand)
    aff = jnp.concatenate(affs, axis=0)
    idx_ref[...] = jnp.concatenate(picks, axis=0)
    gate_ref[...] = ROUTED_SCALE * aff / jnp.sum(aff, axis=0, keepdims=True)

    @pl.when(pl.program_id(0) == 0)
    def _():
        carry_ref[...] = jnp.zeros_like(carry_ref)

    sel = jnp.where(cand == neg, jnp.where(keep_e > 0.0, 1.0, 0.0), 0.0)
    upper = (_iota2((tm, tm), 0) < _iota2((tm, tm), 1)).astype(F32).astype(BF16)
    before = carry_ref[:, :1] + jnp.dot(sel.astype(BF16), upper, preferred_element_type=F32)
    pos_ref[...] = jnp.concatenate(
        [jnp.sum(jnp.where(eidx == ei, before, 0.0), axis=0, keepdims=True) for ei in picks], axis=0
    ).astype(jnp.int32)
    total = carry_ref[:, :1] + jnp.sum(sel, axis=1, keepdims=True)
    carry_ref[...] = jnp.broadcast_to(total, carry_ref.shape)
    cnt_ref[...] = jnp.broadcast_to(total, cnt_ref.shape).astype(jnp.int32)


def _router(x, router_w, router_bias, tm=512):
    T, D = x.shape
    E = router_w.shape[1]
    idx, gate, pos, cnt = pl.pallas_call(
        _router_kernel,
        grid=(T // tm,),
        in_specs=[pl.BlockSpec((tm, D), lambda i: (i, 0)),
                  pl.BlockSpec((E, D), lambda i: (0, 0)),
                  pl.BlockSpec((E, 1), lambda i: (0, 0))],
        out_specs=[pl.BlockSpec((TOP_K, tm), lambda i: (0, i)), pl.BlockSpec((TOP_K, tm), lambda i: (0, i)),
                   pl.BlockSpec((TOP_K, tm), lambda i: (0, i)), pl.BlockSpec((E, LANES), lambda i: (0, 0))],
        out_shape=[jax.ShapeDtypeStruct((TOP_K, T), jnp.int32), jax.ShapeDtypeStruct((TOP_K, T), F32),
                   jax.ShapeDtypeStruct((TOP_K, T), jnp.int32), jax.ShapeDtypeStruct((E, LANES), jnp.int32)],
        scratch_shapes=[pltpu.VMEM((E, LANES), F32)],
        compiler_params=_params("arbitrary"),
        name="router",
    )(x, router_w.T, router_bias.reshape(E, 1))
    return idx.T, gate.T, pos.T, cnt[:, 0]


def _dispatch_plan(top_e, pos, counts, rows_per_tile):
    T, K = top_e.shape
    E = counts.shape[0]
    n_tiles = (T * K) // rows_per_tile + E
    tiles = (counts + rows_per_tile - 1) // rows_per_tile
    tile_end = jnp.cumsum(tiles)
    n_used = tile_end[-1]
    row_start = (tile_end - tiles) * rows_per_tile
    dest = jnp.take(row_start, top_e) + pos
    tile_id = jnp.minimum(jnp.arange(n_tiles, dtype=jnp.int32), n_used - 1)
    tile_expert = jnp.minimum(jnp.sum(tile_end[None, :] <= tile_id[:, None], axis=1), E - 1)
    owns = tiles > 0
    expert = jnp.arange(E, dtype=jnp.int32)
    later = jnp.where(owns[None, :] & (expert[None, :] > expert[:, None]), expert[None, :], E)
    next_owner = jnp.min(later, axis=1)
    next_owner = jnp.where(next_owner == E, -1, next_owner)
    run_parity = (jnp.cumsum(owns.astype(jnp.int32)) - 1) % 2
    i32 = lambda a: a.astype(jnp.int32)
    return (i32(tile_expert), i32(jnp.take(next_owner, tile_expert)), i32(jnp.take(run_parity, tile_expert)),
            i32(tile_end), i32(n_used.reshape(1)), i32(dest))


def _dispatch_kernel(tend_ref, dest_ref, x_ref, xs_hbm, zero_ref, zsem, sem, *, slots):
    tm = x_ref.shape[0]
    rows = zero_ref.shape[0]
    n_experts = tend_ref.shape[0]

    @pl.when(pl.program_id(0) == 0)
    def _():
        zero_ref[...] = jnp.zeros_like(zero_ref)

        def last_tile(e):
            end = tend_ref[e]
            start = tend_ref[jnp.maximum(e - 1, 0)]
            start = jnp.where(e == 0, 0, start)
            return end > start, pltpu.make_async_copy(
                zero_ref, xs_hbm.at[pl.ds(jnp.maximum(end - 1, 0) * rows, rows)], zsem)

        def start(e, carry):
            owns, copy = last_tile(e)

            @pl.when(owns)
            def _():
                copy.start()
            return carry

        def wait(e, carry):
            owns, copy = last_tile(e)

            @pl.when(owns)
            def _():
                copy.wait()
            return carry

        def unused_tile(n):
            return pltpu.make_async_copy(zero_ref, xs_hbm.at[pl.ds(n * rows, rows)], zsem)

        def start_unused(n, carry):
            unused_tile(n).start()
            return carry

        def wait_unused(n, carry):
            unused_tile(n).wait()
            return carry

        n_used, n_tiles = tend_ref[n_experts - 1], xs_hbm.shape[0] // rows
        lax.fori_loop(0, n_experts, start, 0)
        lax.fori_loop(n_used, n_tiles, start_unused, 0)
        lax.fori_loop(0, n_experts, wait, 0)
        lax.fori_loop(n_used, n_tiles, wait_unused, 0)

    def body(t, carry):
        for k in range(slots):
            pltpu.make_async_copy(x_ref.at[pl.ds(t, 1)], xs_hbm.at[pl.ds(dest_ref[0, t * slots + k], 1)],
                                  sem).start(priority=k % 2)
        return carry
    lax.fori_loop(0, tm, body, 0, unroll=2)
    for k in range(slots):
        pltpu.make_async_copy(x_ref, xs_hbm.at[pl.ds(0, tm)], sem).wait()


def _dispatch(x3p, dest, tile_end, n_tiles, rows_per_tile, tm=512):
    T, half = x3p.shape
    K = dest.shape[1]
    steps = T // tm
    d3 = dest.reshape(steps, 1, tm * K)
    return pl.pallas_call(
        functools.partial(_dispatch_kernel, slots=K),
        grid_spec=pltpu.PrefetchScalarGridSpec(
            num_scalar_prefetch=1,
            grid=(steps,),
            in_specs=[pl.BlockSpec((None, 1, K * tm), lambda i, te: (i, 0, 0), memory_space=pltpu.SMEM),
                      pl.BlockSpec((tm, half), lambda i, te: (i, 0))],
            out_specs=pl.BlockSpec(memory_space=pl.ANY),
            scratch_shapes=[pltpu.VMEM((rows_per_tile, half), jnp.int32),
                            pltpu.SemaphoreType.DMA(()), pltpu.SemaphoreType.DMA(())]),
        out_shape=jax.ShapeDtypeStruct((n_tiles * rows_per_tile, half), jnp.int32),
        compiler_params=_params("arbitrary"),
        name="dispatch",
    )(tile_end, d3, x3p)


def _sparse_experts_kernel(te_ref, nu_ref, next_ref, par_ref, xs_ref, w1_hbm, w3_hbm, w2_hbm, y_ref,
                           wf1, wf3, wf2, wb1, wb3, wb2, sem, *, layer):
    n = pl.program_id(0)

    def weight_copies(e, slot):
        return [pltpu.make_async_copy(w.at[layer, e], buf.at[slot], sem.at[slot])
                for w, buf in ((w1_hbm, wf1), (w3_hbm, wf3), (w2_hbm, wf2))]

    @pl.when(n < nu_ref[0])
    def _():
        @pl.when((n == 0) | (te_ref[n] != te_ref[jnp.maximum(n - 1, 0)]))
        def _():
            slot = par_ref[n]

            @pl.when(n == 0)
            def _():
                for c in weight_copies(te_ref[n], slot):
                    c.start()

            @pl.when(next_ref[n] >= 0)
            def _():
                for c in weight_copies(next_ref[n], 1 - slot):
                    c.start()

            for c in weight_copies(te_ref[n], slot):
                c.wait()
            wb1[...] = wf1[slot].astype(BF16)
            wb3[...] = wf3[slot].astype(BF16)
            wb2[...] = wf2[slot].astype(BF16)

        x = _unpack_rows(xs_ref[...]).astype(BF16)
        h1 = jnp.dot(x, wb1[...], preferred_element_type=F32)
        h3 = jnp.dot(x, wb3[...], preferred_element_type=F32)
        hidden = (h1 * _sigmoid(h1) * h3).astype(BF16)
        y_ref[...] = _pack_rows(jnp.dot(hidden, wb2[...], preferred_element_type=F32))

    @pl.when(n >= nu_ref[0])
    def _():
        y_ref[...] = jnp.zeros_like(y_ref)


def _sparse_experts(xs, tile_expert, n_used, tile_next, tile_parity, w1, w3, w2, layer, rows):
    _, _, D, Hd = w1.shape
    n_tiles = tile_expert.shape[0]
    return pl.pallas_call(
        functools.partial(_sparse_experts_kernel, layer=layer),
        grid_spec=pltpu.PrefetchScalarGridSpec(
            num_scalar_prefetch=4,
            grid=(n_tiles,),
            in_specs=[pl.BlockSpec((rows, D // 2), lambda n, te, nu, nx, par: (jnp.minimum(n, nu[0] - 1), 0)),
                      pl.BlockSpec(memory_space=pl.ANY),
                      pl.BlockSpec(memory_space=pl.ANY),
                      pl.BlockSpec(memory_space=pl.ANY)],
            out_specs=pl.BlockSpec((rows, D // 2), lambda n, te, nu, nx, par: (n, 0)),
            scratch_shapes=[pltpu.VMEM((2, D, Hd), F32), pltpu.VMEM((2, D, Hd), F32), pltpu.VMEM((2, Hd, D), F32),
                            pltpu.VMEM((D, Hd), BF16), pltpu.VMEM((D, Hd), BF16), pltpu.VMEM((Hd, D), BF16),
                            pltpu.SemaphoreType.DMA((2,))]),
        out_shape=jax.ShapeDtypeStruct(xs.shape, jnp.int32),
        compiler_params=_params("arbitrary"),
        name="sparse_experts",
    )(tile_expert, n_used, tile_next, tile_parity, xs, w1, w3, w2)


def _combine_kernel(dest_ref, dest_next_ref, gate_ref, y_hbm, o_ref, buf, sem):
    i = pl.program_id(0)
    steps = pl.num_programs(0)
    K, tm = buf.shape[1:3]
    slot = i % 2

    def fetch(d_ref, s):
        def body(t, carry):
            for k in range(K):
                pltpu.make_async_copy(y_hbm.at[pl.ds(d_ref[0, t * K + k], 1)], buf.at[s, k, pl.ds(t, 1)],
                                      sem.at[s]).start(priority=k % 2)
            return carry
        lax.fori_loop(0, tm, body, 0, unroll=2)

    @pl.when(i == 0)
    def _():
        fetch(dest_ref, 0)

    @pl.when(i + 1 < steps)
    def _():
        fetch(dest_next_ref, 1 - slot)

    for k in range(K):
        pltpu.make_async_copy(y_hbm.at[pl.ds(0, tm)], buf.at[slot, k], sem.at[slot]).wait()
    gate = gate_ref[...]
    acc = jnp.zeros(o_ref.shape, F32)
    for k in range(K):
        acc = acc + gate[:, k:k + 1] * _unpack_rows(buf[slot, k])
    o_ref[...] = acc


def _combine(y, dest, gate, tm=256):
    T, K = dest.shape
    D = 2 * y.shape[1]
    steps = T // tm
    d3 = dest.reshape(steps, 1, tm * K)
    return pl.pallas_call(
        _combine_kernel,
        grid=(steps,),
        in_specs=[pl.BlockSpec((None, 1, K * tm), lambda i: (i, 0, 0), memory_space=pltpu.SMEM),
                  pl.BlockSpec((None, 1, K * tm), lambda i: (jnp.minimum(i + 1, steps - 1), 0, 0),
                               memory_space=pltpu.SMEM),
                  pl.BlockSpec((tm, K), lambda i: (i, 0)),
                  pl.BlockSpec(memory_space=pl.ANY)],
        out_specs=pl.BlockSpec((tm, D), lambda i: (i, 0)),
        out_shape=jax.ShapeDtypeStruct((T, D), F32),
        scratch_shapes=[pltpu.VMEM((2, K, tm, D // 2), jnp.int32), pltpu.SemaphoreType.DMA((2,))],
        compiler_params=_params("arbitrary"),
        name="combine",
    )(d3, d3, gate, y)


def _even_mixer(x2, xb, B, S, ln_g, ln_b, w_in, conv_w, conv_b, gate_b, ml_gn_w, rw_mu, rw_w0, rw_w2,
                rw_a0, rw_a2, rw_g2, rw_kk, rw_ka, rw_rk, rw_gn_w, rw_gn_b, w_out, j):
    D = x2.shape[1]
    H = ML_HEADS
    ML = ml_gn_w.shape[0]
    RW = rw_w0.shape[0]
    LO = RW_LORA_W + RW_LORA_A + RW_LORA_G
    ml_main = 3 * ML
    ml_cols = ml_main + 2 * H
    pad_to = 512
    used = ml_main + 3 * RW + LO + LANES
    total = -(-used // pad_to) * pad_to
    w_perm = jnp.concatenate([
        w_in[:, :ml_main], w_in[:, ml_cols:ml_cols + 3 * RW + LO], w_in[:, ml_main:ml_cols],
        jnp.zeros((D, total - used + LANES - 2 * H), w_in.dtype)], axis=1).astype(BF16)
    z3 = _mm(xb, w_perm[None], 0, BF16, 1024, pad_to).reshape(B, S, total)
    gate_col0 = ml_main + 3 * RW + LO
    L = ML_CHUNK
    gates_t = jnp.swapaxes(
        z3[:, :, gate_col0:gate_col0 + 2 * H].astype(F32).reshape(B, S // L, L, 2 * H), 2, 3)
    h_ml = _mlstm(z3, gates_t, conv_w, conv_b, gate_b, ml_gn_w,
                  qk_blk=0, v_blk=1, o_blk=2, gate_blk=gate_col0 // LANES)
    h_rw = _rwkv(z3, rw_mu, rw_w0, rw_w2, rw_a0, rw_a2, rw_g2, rw_kk, rw_ka, rw_rk, rw_gn_w, rw_gn_b,
                 col0=ml_main, lora_col0=ml_main + 3 * RW, width=RW)
    return _mm_res_ln([h_ml.reshape(B * S, ML), h_rw.reshape(B * S, RW)], w_out, j, x2, ln_g, ln_b)


def _odd_mixer(x2, xb, B, S, positions, ln_g, ln_b, w_in, gn_w, gn_b, w_out, j):
    z3 = _mm(xb, w_in, j, BF16, 1024, 512).reshape(B, S, w_in.shape[2])
    h = _retention(z3, positions, gn_w, gn_b)
    return _mm_res_ln([h.reshape(B * S, -1)], w_out, j, x2, ln_g, ln_b)


def _cross_attention(x2, xb, B, S, memb, ln_g, ln_b, wq, wkv, wo, layer):
    D = x2.shape[1]
    q = _mm(xb, wq, layer, BF16, 1024, 512)
    kv = _mm(memb, wkv, layer, BF16, memb.shape[0], 512)
    o = _xattn(q.reshape(B, S, D), kv.reshape(B, -1, 2 * D))
    return _mm_res_ln([o.reshape(B * S, D)], wo, layer, x2, ln_g, ln_b, packed=True)


def _moe(x2, xb, x3p, ln_g, ln_b, router_w, router_bias, w1, w3, w2, sw1, sw3, sw2, layer):
    rows = MOE_ROWS_PER_TILE
    top_e, gate, pos, counts = _router(x2, router_w, router_bias)
    tile_expert, tile_next, tile_parity, tile_end, n_used, dest = _dispatch_plan(top_e, pos, counts, rows)
    xs = _dispatch(x3p, dest, tile_end, tile_expert.shape[0], rows)
    y = _sparse_experts(xs, tile_expert, n_used, tile_next, tile_parity, w1, w3, w2, layer, rows)
    routed = _combine(y, dest, gate)
    hs = _glu(xb, sw1, sw3, layer)
    return _mm_res_ln([hs], sw2, layer, x2, ln_g, ln_b, add=routed)


def kernel(x, mem, positions, ln_g, ln_b, xa_wq, xa_wkv, xa_wo, router_w, router_bias, moe_w1, moe_w3, moe_w2, sh_w1, sh_w3, sh_w2, ev_w_in, ml_conv_w, ml_conv_b, ml_gate_b, ml_gn_w, rw_mu, rw_w0, rw_w2, rw_a0, rw_a2, rw_g2, rw_kk, rw_ka, rw_rk, rw_gn_w, rw_gn_b, ev_w_out, od_w_in, ret_gn_w, ret_gn_b, od_w_out):
    B, S, D = x.shape
    x2 = x.reshape(B * S, D)
    xb = x2.astype(BF16)
    memb = mem.reshape(-1, D).astype(BF16)
    (xa_wq, xa_wkv, xa_wo, sh_w1, sh_w3, sh_w2, ev_w_out, od_w_in, od_w_out) = (
        w.astype(BF16) for w in (xa_wq, xa_wkv, xa_wo, sh_w1, sh_w3, sh_w2, ev_w_out, od_w_in, od_w_out))
    for layer in range(ln_g.shape[0]):
        j = layer // 2
        if layer % 2 == 0:
            x2, xb = _even_mixer(x2, xb, B, S, ln_g[layer, 0], ln_b[layer, 0], ev_w_in[j], ml_conv_w[j],
                                 ml_conv_b[j], ml_gate_b[j], ml_gn_w[j], rw_mu[j], rw_w0[j], rw_w2[j],
                                 rw_a0[j], rw_a2[j], rw_g2[j], rw_kk[j], rw_ka[j], rw_rk[j],
                                 rw_gn_w[j], rw_gn_b[j], ev_w_out, j)
        else:
            x2, xb = _odd_mixer(x2, xb, B, S, positions, ln_g[layer, 0], ln_b[layer, 0], od_w_in,
                                ret_gn_w[j], ret_gn_b[j], od_w_out, j)
        x2, xb, x3p = _cross_attention(x2, xb, B, S, memb, ln_g[layer, 1], ln_b[layer, 1], xa_wq, xa_wkv,
                                       xa_wo, layer)
        x2, xb = _moe(x2, xb, x3p, ln_g[layer, 2], ln_b[layer, 2], router_w[layer], router_bias[layer],
                      moe_w1, moe_w3, moe_w2, sh_w1, sh_w3, sh_w2, layer)
    return x2.reshape(B, S, D)
```

```python
import functools

import numpy as np
import jax
import jax.numpy as jnp
from jax import lax
from jax.experimental import pallas as pl
from jax.experimental.pallas import tpu as pltpu

F32 = jnp.float32
BF16 = jnp.bfloat16
HI = lax.Precision.HIGHEST

DEPTH = 4
ALPHA = (2.0 * DEPTH) ** 0.25
LN_EPS = 1e-5
NEG_BIG = -1e30
CONV_K = 4
ML_HEADS = 4
RW_HEAD = 64
RW_LORA_W = 64
RW_LORA_A = 64
RW_LORA_G = 128
RW_GN_EPS = 64e-5
RET_HEADS = 8
ROPE_BASE = 10000.0
XA_HEADS = 4
TOP_K = 8
N_GROUPS = 8
TOPK_GROUPS = 4
ROUTED_SCALE = 2.5

ML_CHUNK = 64
RW_CHUNK = 64
RET_CHUNK = 256
MOE_ROWS_PER_TILE = 512

V7X_VMEM_BYTES = 64 * 2 ** 20
VMEM_LIMIT = (V7X_VMEM_BYTES * 3) // 4
LANES = 128
V7X_MXU_DIM = 256


def _params(*sem):
    return pltpu.CompilerParams(dimension_semantics=sem, vmem_limit_bytes=VMEM_LIMIT)


def _bdot(a, b):
    return jnp.dot(a.astype(BF16), b.astype(BF16), preferred_element_type=F32)


def _bdot_nt(a, b):
    return lax.dot_general(a.astype(BF16), b.astype(BF16), (((1,), (1,)), ((), ())),
                           preferred_element_type=F32)


def _bdot_tn(a, b):
    return lax.dot_general(a.astype(BF16), b.astype(BF16), (((0,), (0,)), ((), ())),
                           preferred_element_type=F32)


def _sigmoid(x):
    return 1.0 / (1.0 + jnp.exp(-x))


def _softplus(x):
    return jnp.maximum(x, 0.0) + jnp.log1p(jnp.exp(-jnp.abs(x)))


def _iota2(shape, dim):
    return lax.broadcasted_iota(jnp.int32, shape, dim)


def _mm_kernel(x_ref, w_ref, o_ref):
    o_ref[...] = _bdot(x_ref[...], w_ref[...]).astype(o_ref.dtype)


def _mm(x, w, layer, out_dtype, tm, tn):
    M, K = x.shape
    N = w.shape[2]
    assert M % tm == 0 and N % tn == 0
    return pl.pallas_call(
        _mm_kernel,
        grid=(M // tm, N // tn),
        in_specs=[pl.BlockSpec((tm, K), lambda i, j: (i, 0)),
                  pl.BlockSpec((None, K, tn), lambda i, j: (layer, 0, j))],
        out_specs=pl.BlockSpec((tm, tn), lambda i, j: (i, j)),
        out_shape=jax.ShapeDtypeStruct((M, N), out_dtype),
        compiler_params=_params("parallel", "parallel"),
        name="mm",
    )(x, w)


def _layer_norm_rows(y, g, b):
    mu = jnp.mean(y, axis=-1, keepdims=True)
    d = y - mu
    var = jnp.mean(d * d, axis=-1, keepdims=True)
    return d * lax.rsqrt(var + LN_EPS) * g + b


def _pack_rows(v):
    half = v.shape[1] // 2
    bits = lax.bitcast_convert_type(v.astype(BF16).astype(F32), jnp.int32)
    return bits[:, half:] | lax.shift_right_logical(bits[:, :half], 16)


def _unpack_rows(w):
    return jnp.concatenate([lax.bitcast_convert_type(lax.shift_left(w, 16), F32),
                            lax.bitcast_convert_type(w & jnp.int32(-65536), F32)], axis=-1)


def _mm_res_ln_kernel(*refs, n_a, has_add, packed):
    a_refs, (w_ref, res_ref), rest = refs[:n_a], refs[n_a:n_a + 2], refs[n_a + 2:]
    add_ref = rest[0] if has_add else None
    n_out = 3 if packed else 2
    g_ref, b_ref = rest[-n_out - 2:-n_out]
    o_ref, ob_ref = rest[-n_out:][:2]
    acc, k0 = None, 0
    for a_ref in a_refs:
        part = _bdot(a_ref[...], w_ref[k0:k0 + a_ref.shape[1], :])
        acc = part if acc is None else acc + part
        k0 += a_ref.shape[1]
    if has_add:
        acc = add_ref[...] + acc
    out = _layer_norm_rows(ALPHA * res_ref[...] + acc, g_ref[...], b_ref[...])
    o_ref[...] = out
    ob_ref[...] = out.astype(ob_ref.dtype)
    if packed:
        rest[-1][...] = _pack_rows(out)


def _mm_res_ln(a_parts, w, layer, res, g, b, add=None, packed=False, tm=256):
    M = res.shape[0]
    K, N = w.shape[1:]
    assert sum(a.shape[1] for a in a_parts) == K
    row = lambda i: (i, 0)
    fixed = lambda i: (0, 0)
    in_specs = [pl.BlockSpec((tm, a.shape[1]), row) for a in a_parts]
    in_specs += [pl.BlockSpec((None, K, N), lambda i: (layer, 0, 0)), pl.BlockSpec((tm, N), row)]
    args = [*a_parts, w, res]
    if add is not None:
        in_specs.append(pl.BlockSpec((tm, N), row))
        args.append(add)
    in_specs += [pl.BlockSpec((1, N), fixed), pl.BlockSpec((1, N), fixed)]
    args += [g.reshape(1, N), b.reshape(1, N)]
    out_specs = [pl.BlockSpec((tm, N), row), pl.BlockSpec((tm, N), row)]
    out_shape = [jax.ShapeDtypeStruct((M, N), F32), jax.ShapeDtypeStruct((M, N), BF16)]
    if packed:
        out_specs.append(pl.BlockSpec((tm, N // 2), row))
        out_shape.append(jax.ShapeDtypeStruct((M, N // 2), jnp.int32))
    return pl.pallas_call(
        functools.partial(_mm_res_ln_kernel, n_a=len(a_parts), has_add=add is not None, packed=packed),
        grid=(M // tm,),
        in_specs=in_specs,
        out_specs=out_specs,
        out_shape=out_shape,
        compiler_params=_params("parallel"),
        name="mm_res_ln",
    )(*args)


def _glu_kernel(x_ref, w1_ref, w3_ref, o_ref):
    x = x_ref[...]
    h1 = jnp.dot(x, w1_ref[...], preferred_element_type=F32)
    h3 = jnp.dot(x, w3_ref[...], preferred_element_type=F32)
    o_ref[...] = (h1 * _sigmoid(h1) * h3).astype(o_ref.dtype)


def _glu(xb, w1, w3, layer, tm=512):
    M, K = xb.shape
    N = w1.shape[2]
    wspec = pl.BlockSpec((None, K, N), lambda i: (layer, 0, 0))
    return pl.pallas_call(
        _glu_kernel,
        grid=(M // tm,),
        in_specs=[pl.BlockSpec((tm, K), lambda i: (i, 0)), wspec, wspec],
        out_specs=pl.BlockSpec((tm, N), lambda i: (i, 0)),
        out_shape=jax.ShapeDtypeStruct((M, N), BF16),
        compiler_params=_params("parallel"),
        name="glu",
    )(xb, w1, w3)


def _log_sigmoid(x):
    return jnp.minimum(x, 0.0) - jnp.log1p(jnp.exp(-jnp.abs(x)))


def _split_bf16(x, terms):
    pieces = []
    for _ in range(terms):
        p = x.astype(BF16)
        pieces.append(p)
        x = x - p.astype(F32)
    return pieces


def _mlstm_kernel(qk_ref, v_ref, o_ref, gc_ref, gr_ref, cw_ref, cb_ref, gbc_ref, gbr_ref, gn_ref,
                  out_ref, prev_ref, c_ref, n_ref, m_ref, *, heads, dk, dv):
    nb, L, _ = qk_ref.shape
    qw = heads * dk

    @pl.when(pl.program_id(0) == 0)
    def _():
        prev_ref[...] = jnp.zeros_like(prev_ref)
        c_ref[...] = jnp.zeros_like(c_ref)
        n_ref[...] = jnp.zeros_like(n_ref)
        m_ref[...] = jnp.full_like(m_ref, NEG_BIG)

    row = _iota2((L, 1), 0)
    qk = []
    for b in range(nb):
        cur = qk_ref[b].astype(F32)
        prev = prev_ref[b]
        acc = cb_ref[...] + cw_ref[CONV_K - 1:CONV_K, :] * cur
        for j in range(CONV_K - 1):
            s = CONV_K - 1 - j
            shifted = jnp.where(row < s, pltpu.roll(prev, s, 0), pltpu.roll(cur, s, 0))
            acc = acc + cw_ref[j:j + 1, :] * shifted
        prev_ref[b] = cur
        qk.append(acc * _sigmoid(acc))

    causal = _iota2((L, L), 0) >= _iota2((L, L), 1)
    tri = causal.astype(F32).astype(BF16)
    g_col = [gc_ref[b].astype(F32) + gbc_ref[...] for b in range(nb)]
    pieces = _split_bf16(jnp.concatenate([_log_sigmoid(g) for g in g_col], axis=1), 3)
    lanes = nb * g_col[0].shape[1]
    b_col = jnp.dot(tri, jnp.concatenate(pieces, axis=1), preferred_element_type=F32)
    b_col = b_col[:, :lanes] + b_col[:, lanes:2 * lanes] + b_col[:, 2 * lanes:]
    g_row = [gr_ref[b, 0] + gbr_ref[...] for b in range(nb)]
    pieces = _split_bf16(jnp.concatenate([_log_sigmoid(g) for g in g_row], axis=0), 3)
    rows = nb * 2 * heads
    b_row = lax.dot_general(jnp.concatenate(pieces, axis=0), tri, (((1,), (1,)), ((), ())),
                            preferred_element_type=F32)
    b_row = b_row[:rows] + b_row[rows:2 * rows] + b_row[2 * rows:]

    def per_problem(f):
        return jnp.stack([f(b, h) for b in range(nb) for h in range(heads)])

    q = per_problem(lambda b, h: qk[b][:, h * dk:(h + 1) * dk])
    k = per_problem(lambda b, h: qk[b][:, qw + h * dk:qw + (h + 1) * dk]) * dk ** -0.5
    v = per_problem(lambda b, h: v_ref[b, :, h * dv:(h + 1) * dv])
    gcw = g_col[0].shape[1]
    bc = per_problem(lambda b, h: b_col[:, b * gcw + heads + h:b * gcw + heads + h + 1])
    ic = per_problem(lambda b, h: g_col[b][:, h:h + 1])
    br = per_problem(lambda b, h: b_row[b * 2 * heads + heads + h:b * 2 * heads + heads + h + 1])
    ir = per_problem(lambda b, h: g_row[b][h:h + 1])
    b_last = bc[:, L - 1:L]
    c_prev = c_ref[...]
    n_prev = n_ref[...]
    m_prev = m_ref[...][:, :, 0:1]

    def bmm(x, y, spec):
        return jnp.einsum(spec, x.astype(BF16), y.astype(BF16), preferred_element_type=F32)

    log_d = jnp.where(causal, bc - br + ir, -jnp.inf)
    g_inter = bc + m_prev
    m_t = jnp.maximum(jnp.max(log_d, axis=-1, keepdims=True), g_inter)
    p = jnp.exp(log_d - m_t) * bmm(q, k, "gtk,gsk->gts")
    e_inter = jnp.exp(g_inter - m_t)
    num = bmm(p, v, "gts,gsv->gtv") + e_inter * bmm(q, c_prev, "gtk,gkv->gtv")
    den = jnp.sum(p, axis=-1, keepdims=True) + e_inter * jnp.sum(q * n_prev, axis=-1, keepdims=True)
    hh = num / jnp.maximum(jnp.abs(den), jnp.exp(-m_t))

    w_end = b_last - bc + ic
    m_loc = jnp.max(w_end, axis=1, keepdims=True)
    e_end = jnp.exp(w_end - m_loc)
    ke = k * e_end
    m_new = jnp.maximum(b_last + m_prev, m_loc)
    s_prev = jnp.exp(b_last + m_prev - m_new)
    s_new = jnp.exp(m_loc - m_new)
    c_ref[...] = s_prev * c_prev + s_new * bmm(ke, v, "gsk,gsv->gkv")
    n_ref[...] = s_prev * n_prev + s_new * jnp.sum(ke, axis=1, keepdims=True)
    m_ref[...] = jnp.broadcast_to(m_new, m_ref.shape)

    mu = jnp.mean(hh, axis=-1, keepdims=True)
    d = hh - mu
    var = jnp.mean(d * d, axis=-1, keepdims=True)
    y = d * lax.rsqrt(var + 1e-6)
    for b in range(nb):
        for h in range(heads):
            sl = slice(h * dv, (h + 1) * dv)
            gate = _sigmoid(o_ref[b, :, sl].astype(F32))
            out_ref[b, :, sl] = (y[b * heads + h] * gn_ref[:, sl] * gate).astype(out_ref.dtype)


def _mlstm(z3, gates_t, conv_w, conv_b, gate_b, gn_w, *, qk_blk, v_blk, o_blk, gate_blk):
    B, S, _ = z3.shape
    H = ML_HEADS
    W = gn_w.shape[0]
    dv = W // H
    dk = conv_w.shape[1] // (2 * H)
    assert conv_w.shape[1] == W, "q|k block and v block share one column-block width"
    L = ML_CHUNK
    gb_col = jnp.zeros((1, LANES), F32).at[0, :2 * H].set(gate_b.reshape(-1))
    gb_row = gate_b.reshape(2 * H, 1)
    fixed = lambda c: (0, 0)
    return pl.pallas_call(
        functools.partial(_mlstm_kernel, heads=H, dk=dk, dv=dv),
        grid=(S // L,),
        in_specs=[pl.BlockSpec((B, L, W), lambda c: (0, c, qk_blk)),
                  pl.BlockSpec((B, L, W), lambda c: (0, c, v_blk)),
                  pl.BlockSpec((B, L, W), lambda c: (0, c, o_blk)),
                  pl.BlockSpec((B, L, LANES), lambda c: (0, c, gate_blk)),
                  pl.BlockSpec((B, 1, 2 * H, L), lambda c: (0, c, 0, 0)),
                  pl.BlockSpec((CONV_K, W), fixed),
                  pl.BlockSpec((1, W), fixed),
                  pl.BlockSpec((1, LANES), fixed),
                  pl.BlockSpec((2 * H, 1), fixed),
                  pl.BlockSpec((1, W), fixed)],
        out_specs=pl.BlockSpec((B, L, W), lambda c: (0, c, 0)),
        out_shape=jax.ShapeDtypeStruct((B, S, W), BF16),
        scratch_shapes=[pltpu.VMEM((B, L, W), F32),
                        pltpu.VMEM((B * H, dk, dv), F32),
                        pltpu.VMEM((B * H, 1, dk), F32),
                        pltpu.VMEM((B * H, 1, LANES), F32)],
        compiler_params=_params("arbitrary"),
        name="mlstm",
    )(z3, z3, z3, z3, gates_t, conv_w, conv_b.reshape(1, W), gb_col, gb_row, gn_w.reshape(1, W))


def _unit_lower_inverse(a, t_idx, i_idx, block_diag, matmul):
    n = a.shape[-2]
    eye = (t_idx == i_idx).astype(F32)
    t = None
    s = 1
    while s < n:
        pair = (t_idx // (2 * s)) == (i_idx // (2 * s))
        off = pair & ((t_idx // s) % 2 == 1) & ((i_idx // s) % 2 == 0)
        q = jnp.where(off, a, 0.0)
        t = eye - q if t is None else t - matmul(matmul(t, block_diag(q)), block_diag(t))
        s *= 2
    return t


def _rwkv_kernel(r_ref, k_ref, v_ref, l_ref, mur_ref, muk_ref, muv_ref, mul_ref, w0_ref, w2_ref,
                 a0_ref, a2_ref, g2_ref, kk_ref, ka_ref, rk_ref, gnw_ref, gnb_ref, out_ref,
                 pr_ref, pk_ref, pv_ref, plo_ref, s_ref, *, hd, gw):
    nb, L, W = r_ref.shape
    ng = W // gw
    hpg = gw // hd
    assert hpg * L == gw

    @pl.when(pl.program_id(1) == 0)
    def _():
        pr_ref[...] = jnp.zeros_like(pr_ref)
        pk_ref[...] = jnp.zeros_like(pk_ref)
        pv_ref[...] = jnp.zeros_like(pv_ref)
        plo_ref[...] = jnp.zeros_like(plo_ref)
        s_ref[...] = jnp.zeros_like(s_ref)

    R = nb * L
    first = _iota2((L, 1), 0) == 0
    same_head = (_iota2((gw, gw), 0) // hd) == (_iota2((gw, gw), 1) // hd)
    zero = jnp.zeros((), BF16)
    same_head_b = same_head.astype(F32).astype(BF16)
    row, col = _iota2((R, R), 0), _iota2((R, R), 1)
    tri = ((row // L == col // L) & (row >= col)).astype(F32).astype(BF16)
    t_idx = _iota2((L, gw), 0)
    i_idx = _iota2((L, gw), 1) % L
    strict = t_idx > i_idx
    lower = t_idx >= i_idx

    def shift_lerp(x_ref, p_ref, mu_ref):
        parts = []
        for b in range(nb):
            cur = x_ref[b].astype(F32)
            shifted = jnp.where(first, p_ref[b], pltpu.roll(cur, 1, 0))
            p_ref[b] = cur[L - 1:L, :]
            parts.append(cur + mu_ref[...] * (shifted - cur))
        return jnp.concatenate(parts, axis=0)

    def head_sums(xs):
        stacked = jnp.concatenate([x[:, i * gw:(i + 1) * gw] for x in xs for i in range(ng)], axis=0)
        n = stacked.shape[0]
        s = jnp.dot(jnp.concatenate(_split_bf16(stacked, 2), axis=0), same_head_b, preferred_element_type=F32)
        s = s[:n] + s[n:]
        return [jnp.concatenate([s[(j * ng + i) * R:(j * ng + i + 1) * R] for i in range(ng)], axis=1)
                for j in range(len(xs))]

    r = shift_lerp(r_ref, pr_ref, mur_ref)
    k = shift_lerp(k_ref, pk_ref, muk_ref)
    v = shift_lerp(v_ref, pv_ref, muv_ref)
    lo = shift_lerp(l_ref, plo_ref, mul_ref)
    xw = lo[:, :RW_LORA_W]
    xa = lo[:, RW_LORA_W:RW_LORA_W + RW_LORA_A]
    xg = lo[:, RW_LORA_W + RW_LORA_A:]

    w_log = -jnp.exp(-_softplus(-(w0_ref[...] + _bdot(jnp.tanh(xw), w2_ref[...]))) - 0.5)
    a = _sigmoid(a0_ref[...] + _bdot(xa, a2_ref[...]))
    g = _bdot(_sigmoid(xg), g2_ref[...])

    kk = k * kk_ref[...]
    k = k * (1.0 + (a - 1.0) * ka_ref[...])
    kk_sq, rk_sum = head_sums([kk * kk, r * k * rk_ref[...]])
    kk = kk * lax.rsqrt(jnp.maximum(kk_sq, 1e-24))
    bv = kk * a

    c3 = jnp.dot(tri, jnp.concatenate(_split_bf16(w_log, 3), axis=1), preferred_element_type=F32)
    c = c3[:, :W] + c3[:, W:2 * W] + c3[:, 2 * W:]
    c_last = jnp.concatenate([jnp.broadcast_to(c[(b + 1) * L - 1:(b + 1) * L], (L, W)) for b in range(nb)],
                             axis=0)
    e_end = jnp.exp(c_last - c)
    e_neg = jnp.exp(-c)
    kt = (kk * jnp.exp(c - w_log)).astype(BF16)
    rt = (r * jnp.exp(c)).astype(BF16)
    kh = (k * e_neg).astype(BF16)
    bh = (bv * e_neg).astype(BF16)
    kp = (k * e_end).astype(BF16)
    bp = (bv * e_end).astype(BF16)
    vb = v.astype(BF16)
    p_last = jnp.exp(c_last)

    def groups(x):
        return jnp.stack([x[b * L:(b + 1) * L, i * gw:(i + 1) * gw] for b in range(nb) for i in range(ng)])

    def block_diag(x):
        return jnp.where(same_head, jnp.concatenate([x.astype(BF16)] * hpg, axis=1), zero)

    def bmm(x, y):
        return jnp.einsum("gmk,gkn->gmn", x.astype(BF16), y.astype(BF16), preferred_element_type=F32)

    def bmm_nt(x, y):
        return jnp.einsum("gmk,gnk->gmn", x.astype(BF16), y.astype(BF16), preferred_element_type=F32)

    lhs = jnp.concatenate([groups(kt), groups(rt)], axis=1)
    m_k = bmm_nt(lhs, block_diag(groups(kh)))
    m_b = bmm_nt(lhs, block_diag(groups(bh)))
    a_kk = jnp.where(strict, m_k[:, :L], 0.0)
    a_kb = jnp.where(strict, m_b[:, :L], 0.0)
    a_rk = jnp.where(lower, m_k[:, L:], 0.0)
    a_rb = jnp.where(lower, m_b[:, L:], 0.0)
    t_inv = _unit_lower_inverse(a_kb, t_idx, i_idx, block_diag, bmm)
    st = s_ref[...]
    gs = bmm_nt(lhs, st)
    vg = groups(vb)
    v_bd = block_diag(vg)
    u = bmm(t_inv, block_diag(gs[:, :L] + bmm(a_kk, v_bd)))
    yg = gs[:, L:] + bmm(jnp.concatenate([a_rk, -a_rb], axis=2), jnp.concatenate([v_bd, block_diag(u)], axis=1))
    vu = jnp.concatenate([vg, (-u).astype(BF16)], axis=1)
    kb = jnp.concatenate([groups(kp), groups(bp)], axis=1)
    pg = groups(p_last)[:, :1]
    outer = jnp.einsum("gtv,gtk->gvk", vu, kb, preferred_element_type=F32)
    s_ref[...] = st * pg + jnp.where(same_head, outer, 0.0)
    ys = [jnp.concatenate([yg[b * ng + i] for i in range(ng)], axis=1) for b in range(nb)]

    y = jnp.concatenate(ys, axis=0)
    mu = head_sums([y])[0] * (1.0 / hd)
    d = y - mu
    var = head_sums([d * d])[0] * (1.0 / hd)
    yn = d * lax.rsqrt(var + RW_GN_EPS) * gnw_ref[...] + gnb_ref[...]
    out = ((yn + rk_sum * v) * g).astype(out_ref.dtype)
    for b in range(nb):
        out_ref[b] = out[b * L:(b + 1) * L]


def _rwkv(z3, mu, w0, w2, a0, a2, g2, k_k, k_a, r_k, gn_w, gn_b, *, col0, lora_col0, width):
    B, S, _ = z3.shape
    RW = w0.shape[0]
    W = width
    G = RW // W
    L = RW_CHUNK
    LO = RW_LORA_W + RW_LORA_A + RW_LORA_G
    assert col0 % W == 0 and RW % W == 0 and lora_col0 % LO == 0 and W % V7X_MXU_DIM == 0
    zc = lambda off: (lambda g, c: (0, c, off // W + g))
    vec = lambda g, c: (0, g)
    row = lambda x: x.reshape(1, -1)
    return pl.pallas_call(
        functools.partial(_rwkv_kernel, hd=RW_HEAD, gw=V7X_MXU_DIM),
        grid=(G, S // L),
        in_specs=[pl.BlockSpec((B, L, W), zc(col0)),
                  pl.BlockSpec((B, L, W), zc(col0 + RW)),
                  pl.BlockSpec((B, L, W), zc(col0 + 2 * RW)),
                  pl.BlockSpec((B, L, LO), lambda g, c: (0, c, lora_col0 // LO)),
                  pl.BlockSpec((1, W), vec), pl.BlockSpec((1, W), vec), pl.BlockSpec((1, W), vec),
                  pl.BlockSpec((1, LO), lambda g, c: (0, 0)),
                  pl.BlockSpec((1, W), vec),
                  pl.BlockSpec((RW_LORA_W, W), vec),
                  pl.BlockSpec((1, W), vec),
                  pl.BlockSpec((RW_LORA_A, W), vec),
                  pl.BlockSpec((RW_LORA_G, W), vec),
                  pl.BlockSpec((1, W), vec), pl.BlockSpec((1, W), vec), pl.BlockSpec((1, W), vec),
                  pl.BlockSpec((1, W), vec), pl.BlockSpec((1, W), vec)],
        out_specs=pl.BlockSpec((B, L, W), lambda g, c: (0, c, g)),
        out_shape=jax.ShapeDtypeStruct((B, S, RW), BF16),
        scratch_shapes=[pltpu.VMEM((B, 1, W), F32), pltpu.VMEM((B, 1, W), F32), pltpu.VMEM((B, 1, W), F32),
                        pltpu.VMEM((B, 1, LO), F32),
                        pltpu.VMEM((B * (W // V7X_MXU_DIM), V7X_MXU_DIM, V7X_MXU_DIM), F32)],
        compiler_params=_params("parallel", "arbitrary"),
        name="rwkv7",
    )(z3, z3, z3, z3, row(mu[:RW]), row(mu[RW:2 * RW]), row(mu[2 * RW:3 * RW]), row(mu[3 * RW:]),
      row(w0), w2.astype(BF16), row(a0), a2.astype(BF16), g2.astype(BF16), row(k_k), row(k_a),
      row(r_k), row(gn_w), row(gn_b))


def _retention_kernel(q_ref, k_ref, v_ref, g_ref, pos_ref, inv_ref, dec_ref, gnw_ref, gnb_ref,
                      out_ref, r_ref, *, heads, log_gamma):
    L = q_ref.shape[1]
    dk = q_ref.shape[2] // heads
    dv = v_ref.shape[2] // heads
    half = dk // 2

    @pl.when(pl.program_id(1) == 0)
    def _():
        r_ref[...] = jnp.zeros_like(r_ref)

    ang = pos_ref[0] * inv_ref[...]
    cos = jnp.cos(ang)
    sin = jnp.sin(ang)
    t_in = _iota2((L, 1), 0).astype(F32)

    def rot(t):
        t1, t2 = t[:, :half], t[:, half:]
        return jnp.concatenate([t1 * cos - t2 * sin, t1 * sin + t2 * cos], axis=-1)

    for h in range(heads):
        lg = log_gamma[h]
        q = rot(q_ref[0, :, h * dk:(h + 1) * dk].astype(F32))
        k = rot(k_ref[0, :, h * dk:(h + 1) * dk].astype(F32)) * dk ** -0.5
        v = v_ref[0, :, h * dv:(h + 1) * dv]
        state = r_ref[h]
        inter = _bdot(q, state) * jnp.exp(lg * (t_in + 1.0))
        intra = _bdot(_bdot_nt(q, k) * dec_ref[h], v)
        r_ref[h] = state * float(np.exp(lg * L)) + _bdot_tn(k * jnp.exp(lg * (L - 1.0 - t_in)), v)
        y = intra + inter
        mu = jnp.mean(y, axis=-1, keepdims=True)
        d = y - mu
        var = jnp.mean(d * d, axis=-1, keepdims=True)
        yn = d * lax.rsqrt(var + 1e-6) * gnw_ref[:, h * dv:(h + 1) * dv] + gnb_ref[:, h * dv:(h + 1) * dv]
        gate = g_ref[0, :, h * dv:(h + 1) * dv].astype(F32)
        out_ref[0, :, h * dv:(h + 1) * dv] = (gate * _sigmoid(gate) * yn).astype(out_ref.dtype)


def _retention(z3, positions, gn_w, gn_b):
    B, S, _ = z3.shape
    W = gn_w.shape[0]
    H = RET_HEADS
    dk = W // H
    half = dk // 2
    L = RET_CHUNK
    log_gamma = np.log1p(-np.exp2(-5.0 - np.arange(H, dtype=np.float64)))
    rel = np.arange(L)[:, None] - np.arange(L)[None, :]
    decay = np.where(rel >= 0, np.exp(log_gamma[:, None, None] * np.maximum(rel, 0)), 0.0).astype(np.float32)
    inv = (ROPE_BASE ** (-jnp.arange(half, dtype=F32) / half)).reshape(1, half)
    pos = positions.astype(F32).reshape(B, S, 1)
    blk = lambda j: pl.BlockSpec((1, L, W), lambda b, c: (b, c, j))
    fixed2 = lambda b, c: (0, 0)
    return pl.pallas_call(
        functools.partial(_retention_kernel, heads=H, log_gamma=tuple(float(x) for x in log_gamma)),
        grid=(B, S // L),
        in_specs=[blk(0), blk(1), blk(2), blk(3),
                  pl.BlockSpec((1, L, 1), lambda b, c: (b, c, 0)),
                  pl.BlockSpec((1, half), fixed2),
                  pl.BlockSpec((H, L, L), lambda b, c: (0, 0, 0)),
                  pl.BlockSpec((1, W), fixed2), pl.BlockSpec((1, W), fixed2)],
        out_specs=pl.BlockSpec((1, L, W), lambda b, c: (b, c, 0)),
        out_shape=jax.ShapeDtypeStruct((B, S, W), BF16),
        scratch_shapes=[pltpu.VMEM((H, dk, W // H), F32)],
        compiler_params=_params("parallel", "arbitrary"),
        name="retention",
    )(z3, z3, z3, z3, pos, inv, jnp.asarray(decay), gn_w.reshape(1, W), gn_b.reshape(1, W))


def _xattn_kernel(q_ref, k_ref, v_ref, o_ref, *, heads):
    hd = q_ref.shape[2] // heads
    q, k, v = (jnp.stack([r[0, :, h * hd:(h + 1) * hd] for h in range(heads)]) for r in (q_ref, k_ref, v_ref))
    s = jnp.einsum("hqd,hmd->hqm", q, k, preferred_element_type=F32) * hd ** -0.5
    e = jnp.exp(s - jnp.max(s, axis=-1, keepdims=True))
    p = e / jnp.sum(e, axis=-1, keepdims=True)
    o = jnp.einsum("hqm,hmd->hqd", p.astype(BF16), v, preferred_element_type=F32)
    for h in range(heads):
        o_ref[0, :, h * hd:(h + 1) * hd] = o[h].astype(o_ref.dtype)


def _xattn(q3, kv3, tq=512):
    B, S, D = q3.shape
    M = kv3.shape[1]
    return pl.pallas_call(
        functools.partial(_xattn_kernel, heads=XA_HEADS),
        grid=(B, S // tq),
        in_specs=[pl.BlockSpec((1, tq, D), lambda b, i: (b, i, 0)),
                  pl.BlockSpec((1, M, D), lambda b, i: (b, 0, 0)),
                  pl.BlockSpec((1, M, D), lambda b, i: (b, 0, 1))],
        out_specs=pl.BlockSpec((1, tq, D), lambda b, i: (b, i, 0)),
        out_shape=jax.ShapeDtypeStruct((B, S, D), BF16),
        compiler_params=_params("parallel", "parallel"),
        name="xattn",
    )(q3, kv3, kv3)


def _first_argmax(vals, index, big):
    m = jnp.max(vals, axis=0, keepdims=True)
    idx = jnp.min(jnp.where(vals == m, index, big), axis=0, keepdims=True)
    return m, idx


def _router_kernel(x_ref, wt_ref, bias_ref, idx_ref, gate_ref, pos_ref, cnt_ref, carry_ref):
    E = wt_ref.shape[0]
    tm = x_ref.shape[0]
    per = E // N_GROUPS
    neg = -jnp.inf
    logits = lax.dot_general(wt_ref[...], x_ref[...], (((1,), (1,)), ((), ())), precision=HI,
                             preferred_element_type=F32)
    scores = _sigmoid(logits)
    biased = scores + bias_ref[...]
    eidx = _iota2((E, tm), 0)

    within = _iota2((per, tm), 0)
    group_scores = []
    for g in range(N_GROUPS):
        vals = biased[g * per:(g + 1) * per]
        m1, i1 = _first_argmax(vals, within, per)
        m2 = jnp.max(jnp.where(within == i1, neg, vals), axis=0, keepdims=True)
        group_scores.append(m1 + m2)
    gs = jnp.concatenate(group_scores, axis=0)

    gidx = _iota2((N_GROUPS, tm), 0)
    keep = jnp.zeros((N_GROUPS, tm), F32)
    for _ in range(TOPK_GROUPS):
        _, gi = _first_argmax(gs, gidx, N_GROUPS)
        hit = gidx == gi
        keep = jnp.where(hit, 1.0, keep)
        gs = jnp.where(hit, neg, gs)
    keep_e = jnp.concatenate([jnp.broadcast_to(keep[g:g + 1], (per, tm)) for g in range(N_GROUPS)], axis=0)

    cand = jnp.where(keep_e > 0.0, biased, neg)
    picks, affs = [], []
    for _ in range(TOP_K):
        _, ei = _first_argmax(cand, eidx, E)
        hit = eidx == ei
        picks.append(ei)
        affs.append(jnp.sum(jnp.where(hit, scores, 0.0), axis=0, keepdims=True))
        cand = jnp.where(hit, neg, cand)
    aff = jnp.concatenate(affs, axis=0)
    idx_ref[...] = jnp.concatenate(picks, axis=0)
    gate_ref[...] = ROUTED_SCALE * aff / jnp.sum(aff, axis=0, keepdims=True)

    @pl.when(pl.program_id(0) == 0)
    def _():
        carry_ref[...] = jnp.zeros_like(carry_ref)

    sel = jnp.where(cand == neg, jnp.where(keep_e > 0.0, 1.0, 0.0), 0.0)
    upper = (_iota2((tm, tm), 0) < _iota2((tm, tm), 1)).astype(F32).astype(BF16)
    before = carry_ref[:, :1] + jnp.dot(sel.astype(BF16), upper, preferred_element_type=F32)
    pos_ref[...] = jnp.concatenate(
        [jnp.sum(jnp.where(eidx == ei, before, 0.0), axis=0, keepdims=True) for ei in picks], axis=0
    ).astype(jnp.int32)
    total = carry_ref[:, :1] + jnp.sum(sel, axis=1, keepdims=True)
    carry_ref[...] = jnp.broadcast_to(total, carry_ref.shape)
    cnt_ref[...] = jnp.broadcast_to(total, cnt_ref.shape).astype(jnp.int32)


def _router(x, router_w, router_bias, tm=512):
    T, D = x.shape
    E = router_w.shape[1]
    idx, gate, pos, cnt = pl.pallas_call(
        _router_kernel,
        grid=(T // tm,),
        in_specs=[pl.BlockSpec((tm, D), lambda i: (i, 0)),
                  pl.BlockSpec((E, D), lambda i: (0, 0)),
                  pl.BlockSpec((E, 1), lambda i: (0, 0))],
        out_specs=[pl.BlockSpec((TOP_K, tm), lambda i: (0, i)), pl.BlockSpec((TOP_K, tm), lambda i: (0, i)),
                   pl.BlockSpec((TOP_K, tm), lambda i: (0, i)), pl.BlockSpec((E, LANES), lambda i: (0, 0))],
        out_shape=[jax.ShapeDtypeStruct((TOP_K, T), jnp.int32), jax.ShapeDtypeStruct((TOP_K, T), F32),
                   jax.ShapeDtypeStruct((TOP_K, T), jnp.int32), jax.ShapeDtypeStruct((E, LANES), jnp.int32)],
        scratch_shapes=[pltpu.VMEM((E, LANES), F32)],
        compiler_params=_params("arbitrary"),
        name="router",
    )(x, router_w.T, router_bias.reshape(E, 1))
    return idx.T, gate.T, pos.T, cnt[:, 0]


def _dispatch_plan(top_e, pos, counts, rows_per_tile):
    T, K = top_e.shape
    E = counts.shape[0]
    n_tiles = (T * K) // rows_per_tile + E
    tiles = (counts + rows_per_tile - 1) // rows_per_tile
    tile_end = jnp.cumsum(tiles)
    n_used = tile_end[-1]
    row_start = (tile_end - tiles) * rows_per_tile
    dest = jnp.take(row_start, top_e) + pos
    tile_id = jnp.minimum(jnp.arange(n_tiles, dtype=jnp.int32), n_used - 1)
    tile_expert = jnp.minimum(jnp.sum(tile_end[None, :] <= tile_id[:, None], axis=1), E - 1)
    owns = tiles > 0
    expert = jnp.arange(E, dtype=jnp.int32)
    later = jnp.where(owns[None, :] & (expert[None, :] > expert[:, None]), expert[None, :], E)
    next_owner = jnp.min(later, axis=1)
    next_owner = jnp.where(next_owner == E, -1, next_owner)
    run_parity = (jnp.cumsum(owns.astype(jnp.int32)) - 1) % 2
    i32 = lambda a: a.astype(jnp.int32)
    return (i32(tile_expert), i32(jnp.take(next_owner, tile_expert)), i32(jnp.take(run_parity, tile_expert)),
            i32(tile_end), i32(n_used.reshape(1)), i32(dest))


def _dispatch_kernel(tend_ref, dest_ref, x_ref, xs_hbm, zero_ref, zsem, sem, *, slots):
    tm = x_ref.shape[0]
    rows = zero_ref.shape[0]
    n_experts = tend_ref.shape[0]

    @pl.when(pl.program_id(0) == 0)
    def _():
        zero_ref[...] = jnp.zeros_like(zero_ref)

        def last_tile(e):
            end = tend_ref[e]
            start = tend_ref[jnp.maximum(e - 1, 0)]
            start = jnp.where(e == 0, 0, start)
            return end > start, pltpu.make_async_copy(
                zero_ref, xs_hbm.at[pl.ds(jnp.maximum(end - 1, 0) * rows, rows)], zsem)

        def start(e, carry):
            owns, copy = last_tile(e)

            @pl.when(owns)
            def _():
                copy.start()
            return carry

        def wait(e, carry):
            owns, copy = last_tile(e)

            @pl.when(owns)
            def _():
                copy.wait()
            return carry

        def unused_tile(n):
            return pltpu.make_async_copy(zero_ref, xs_hbm.at[pl.ds(n * rows, rows)], zsem)

        def start_unused(n, carry):
            unused_tile(n).start()
            return carry

        def wait_unused(n, carry):
            unused_tile(n).wait()
            return carry

        n_used, n_tiles = tend_ref[n_experts - 1], xs_hbm.shape[0] // rows
        lax.fori_loop(0, n_experts, start, 0)
        lax.fori_loop(n_used, n_tiles, start_unused, 0)
        lax.fori_loop(0, n_experts, wait, 0)
        lax.fori_loop(n_used, n_tiles, wait_unused, 0)

    def body(t, carry):
        for k in range(slots):
            pltpu.make_async_copy(x_ref.at[pl.ds(t, 1)], xs_hbm.at[pl.ds(dest_ref[0, t * slots + k], 1)],
                                  sem).start(priority=k % 2)
        return carry
    lax.fori_loop(0, tm, body, 0, unroll=2)
    for k in range(slots):
        pltpu.make_async_copy(x_ref, xs_hbm.at[pl.ds(0, tm)], sem).wait()


def _dispatch(x3p, dest, tile_end, n_tiles, rows_per_tile, tm=1024):
    T, half = x3p.shape
    K = dest.shape[1]
    steps = T // tm
    d3 = dest.reshape(steps, 1, tm * K)
    return pl.pallas_call(
        functools.partial(_dispatch_kernel, slots=K),
        grid_spec=pltpu.PrefetchScalarGridSpec(
            num_scalar_prefetch=1,
            grid=(steps,),
            in_specs=[pl.BlockSpec((None, 1, K * tm), lambda i, te: (i, 0, 0), memory_space=pltpu.SMEM),
                      pl.BlockSpec((tm, half), lambda i, te: (i, 0))],
            out_specs=pl.BlockSpec(memory_space=pl.ANY),
            scratch_shapes=[pltpu.VMEM((rows_per_tile, half), jnp.int32),
                            pltpu.SemaphoreType.DMA(()), pltpu.SemaphoreType.DMA(())]),
        out_shape=jax.ShapeDtypeStruct((n_tiles * rows_per_tile, half), jnp.int32),
        compiler_params=_params("arbitrary"),
        name="dispatch",
    )(tile_end, d3, x3p)


def _sparse_experts_kernel(te_ref, nu_ref, next_ref, par_ref, xs_ref, w1_hbm, w3_hbm, w2_hbm, y_ref,
                           wf1, wf3, wf2, wb1, wb3, wb2, sem, *, layer):
    n = pl.program_id(0)

    def weight_copies(e, slot):
        return [pltpu.make_async_copy(w.at[layer, e], buf.at[slot], sem.at[slot])
                for w, buf in ((w1_hbm, wf1), (w3_hbm, wf3), (w2_hbm, wf2))]

    @pl.when(n < nu_ref[0])
    def _():
        @pl.when((n == 0) | (te_ref[n] != te_ref[jnp.maximum(n - 1, 0)]))
        def _():
            slot = par_ref[n]

            @pl.when(n == 0)
            def _():
                for c in weight_copies(te_ref[n], slot):
                    c.start()

            @pl.when(next_ref[n] >= 0)
            def _():
                for c in weight_copies(next_ref[n], 1 - slot):
                    c.start()

            for c in weight_copies(te_ref[n], slot):
                c.wait()
            wb1[...] = wf1[slot].astype(BF16)
            wb3[...] = wf3[slot].astype(BF16)
            wb2[...] = wf2[slot].astype(BF16)

        x = _unpack_rows(xs_ref[...]).astype(BF16)
        h1 = jnp.dot(x, wb1[...], preferred_element_type=F32)
        h3 = jnp.dot(x, wb3[...], preferred_element_type=F32)
        hidden = (h1 * _sigmoid(h1) * h3).astype(BF16)
        y_ref[...] = _pack_rows(jnp.dot(hidden, wb2[...], preferred_element_type=F32))

    @pl.when(n >= nu_ref[0])
    def _():
        y_ref[...] = jnp.zeros_like(y_ref)


def _sparse_experts(xs, tile_expert, n_used, tile_next, tile_parity, w1, w3, w2, layer, rows):
    _, _, D, Hd = w1.shape
    n_tiles = tile_expert.shape[0]
    return pl.pallas_call(
        functools.partial(_sparse_experts_kernel, layer=layer),
        grid_spec=pltpu.PrefetchScalarGridSpec(
            num_scalar_prefetch=4,
            grid=(n_tiles,),
            in_specs=[pl.BlockSpec((rows, D // 2), lambda n, te, nu, nx, par: (jnp.minimum(n, nu[0] - 1), 0)),
                      pl.BlockSpec(memory_space=pl.ANY),
                      pl.BlockSpec(memory_space=pl.ANY),
                      pl.BlockSpec(memory_space=pl.ANY)],
            out_specs=pl.BlockSpec((rows, D // 2), lambda n, te, nu, nx, par: (n, 0)),
            scratch_shapes=[pltpu.VMEM((2, D, Hd), F32), pltpu.VMEM((2, D, Hd), F32), pltpu.VMEM((2, Hd, D), F32),
                            pltpu.VMEM((D, Hd), BF16), pltpu.VMEM((D, Hd), BF16), pltpu.VMEM((Hd, D), BF16),
                            pltpu.SemaphoreType.DMA((2,))]),
        out_shape=jax.ShapeDtypeStruct(xs.shape, jnp.int32),
        compiler_params=_params("arbitrary"),
        name="sparse_experts",
    )(tile_expert, n_used, tile_next, tile_parity, xs, w1, w3, w2)


def _combine_kernel(dest_ref, dest_next_ref, gate_ref, y_hbm, o_ref, buf, sem):
    i = pl.program_id(0)
    steps = pl.num_programs(0)
    K, tm = buf.shape[1:3]
    slot = i % 2

    def fetch(d_ref, s):
        def body(t, carry):
            for k in range(K):
                pltpu.make_async_copy(y_hbm.at[pl.ds(d_ref[0, t * K + k], 1)], buf.at[s, k, pl.ds(t, 1)],
                                      sem.at[s]).start(priority=k % 2)
            return carry
        lax.fori_loop(0, tm, body, 0, unroll=2)

    @pl.when(i == 0)
    def _():
        fetch(dest_ref, 0)

    @pl.when(i + 1 < steps)
    def _():
        fetch(dest_next_ref, 1 - slot)

    for k in range(K):
        pltpu.make_async_copy(y_hbm.at[pl.ds(0, tm)], buf.at[slot, k], sem.at[slot]).wait()
    gate = gate_ref[...]
    acc = jnp.zeros(o_ref.shape, F32)
    for k in range(K):
        acc = acc + gate[:, k:k + 1] * _unpack_rows(buf[slot, k])
    o_ref[...] = acc


def _combine(y, dest, gate, tm=256):
    T, K = dest.shape
    D = 2 * y.shape[1]
    steps = T // tm
    d3 = dest.reshape(steps, 1, tm * K)
    return pl.pallas_call(
        _combine_kernel,
        grid=(steps,),
        in_specs=[pl.BlockSpec((None, 1, K * tm), lambda i: (i, 0, 0), memory_space=pltpu.SMEM),
                  pl.BlockSpec((None, 1, K * tm), lambda i: (jnp.minimum(i + 1, steps - 1), 0, 0),
                               memory_space=pltpu.SMEM),
                  pl.BlockSpec((tm, K), lambda i: (i, 0)),
                  pl.BlockSpec(memory_space=pl.ANY)],
        out_specs=pl.BlockSpec((tm, D), lambda i: (i, 0)),
        out_shape=jax.ShapeDtypeStruct((T, D), F32),
        scratch_shapes=[pltpu.VMEM((2, K, tm, D // 2), jnp.int32), pltpu.SemaphoreType.DMA((2,))],
        compiler_params=_params("arbitrary"),
        name="combine",
    )(d3, d3, gate, y)


def _even_mixer(x2, xb, B, S, ln_g, ln_b, w_in, conv_w, conv_b, gate_b, ml_gn_w, rw_mu, rw_w0, rw_w2,
                rw_a0, rw_a2, rw_g2, rw_kk, rw_ka, rw_rk, rw_gn_w, rw_gn_b, w_out, j):
    D = x2.shape[1]
    H = ML_HEADS
    ML = ml_gn_w.shape[0]
    RW = rw_w0.shape[0]
    LO = RW_LORA_W + RW_LORA_A + RW_LORA_G
    ml_main = 3 * ML
    ml_cols = ml_main + 2 * H
    pad_to = 512
    used = ml_main + 3 * RW + LO + LANES
    total = -(-used // pad_to) * pad_to
    w_perm = jnp.concatenate([
        w_in[:, :ml_main], w_in[:, ml_cols:ml_cols + 3 * RW + LO], w_in[:, ml_main:ml_cols],
        jnp.zeros((D, total - used + LANES - 2 * H), w_in.dtype)], axis=1).astype(BF16)
    z3 = _mm(xb, w_perm[None], 0, BF16, 1024, pad_to).reshape(B, S, total)
    gate_col0 = ml_main + 3 * RW + LO
    L = ML_CHUNK
    gates_t = jnp.swapaxes(
        z3[:, :, gate_col0:gate_col0 + 2 * H].astype(F32).reshape(B, S // L, L, 2 * H), 2, 3)
    h_ml = _mlstm(z3, gates_t, conv_w, conv_b, gate_b, ml_gn_w,
                  qk_blk=0, v_blk=1, o_blk=2, gate_blk=gate_col0 // LANES)
    h_rw = _rwkv(z3, rw_mu, rw_w0, rw_w2, rw_a0, rw_a2, rw_g2, rw_kk, rw_ka, rw_rk, rw_gn_w, rw_gn_b,
                 col0=ml_main, lora_col0=ml_main + 3 * RW, width=RW)
    return _mm_res_ln([h_ml.reshape(B * S, ML), h_rw.reshape(B * S, RW)], w_out, j, x2, ln_g, ln_b)


def _odd_mixer(x2, xb, B, S, positions, ln_g, ln_b, w_in, gn_w, gn_b, w_out, j):
    z3 = _mm(xb, w_in, j, BF16, 1024, 512).reshape(B, S, w_in.shape[2])
    h = _retention(z3, positions, gn_w, gn_b)
    return _mm_res_ln([h.reshape(B * S, -1)], w_out, j, x2, ln_g, ln_b)


def _cross_attention(x2, xb, B, S, memb, ln_g, ln_b, wq, wkv, wo, layer):
    D = x2.shape[1]
    q = _mm(xb, wq, layer, BF16, 1024, 512)
    kv = _mm(memb, wkv, layer, BF16, memb.shape[0], 512)
    o = _xattn(q.reshape(B, S, D), kv.reshape(B, -1, 2 * D))
    return _mm_res_ln([o.reshape(B * S, D)], wo, layer, x2, ln_g, ln_b, packed=True)


def _moe(x2, xb, x3p, ln_g, ln_b, router_w, router_bias, w1, w3, w2, sw1, sw3, sw2, layer):
    rows = MOE_ROWS_PER_TILE
    top_e, gate, pos, counts = _router(x2, router_w, router_bias)
    tile_expert, tile_next, tile_parity, tile_end, n_used, dest = _dispatch_plan(top_e, pos, counts, rows)
    xs = _dispatch(x3p, dest, tile_end, tile_expert.shape[0], rows)
    y = _sparse_experts(xs, tile_expert, n_used, tile_next, tile_parity, w1, w3, w2, layer, rows)
    routed = _combine(y, dest, gate)
    hs = _glu(xb, sw1, sw3, layer)
    return _mm_res_ln([hs], sw2, layer, x2, ln_g, ln_b, add=routed)


def kernel(x, mem, positions, ln_g, ln_b, xa_wq, xa_wkv, xa_wo, router_w, router_bias, moe_w1, moe_w3, moe_w2, sh_w1, sh_w3, sh_w2, ev_w_in, ml_conv_w, ml_conv_b, ml_gate_b, ml_gn_w, rw_mu, rw_w0, rw_w2, rw_a0, rw_a2, rw_g2, rw_kk, rw_ka, rw_rk, rw_gn_w, rw_gn_b, ev_w_out, od_w_in, ret_gn_w, ret_gn_b, od_w_out):
    B, S, D = x.shape
    x2 = x.reshape(B * S, D)
    xb = x2.astype(BF16)
    memb = mem.reshape(-1, D).astype(BF16)
    (xa_wq, xa_wkv, xa_wo, sh_w1, sh_w3, sh_w2, ev_w_out, od_w_in, od_w_out) = (
        w.astype(BF16) for w in (xa_wq, xa_wkv, xa_wo, sh_w1, sh_w3, sh_w2, ev_w_out, od_w_in, od_w_out))
    for layer in range(ln_g.shape[0]):
        j = layer // 2
        if layer % 2 == 0:
            x2, xb = _even_mixer(x2, xb, B, S, ln_g[layer, 0], ln_b[layer, 0], ev_w_in[j], ml_conv_w[j],
                                 ml_conv_b[j], ml_gate_b[j], ml_gn_w[j], rw_mu[j], rw_w0[j], rw_w2[j],
                                 rw_a0[j], rw_a2[j], rw_g2[j], rw_kk[j], rw_ka[j], rw_rk[j],
                                 rw_gn_w[j], rw_gn_b[j], ev_w_out, j)
        else:
            x2, xb = _odd_mixer(x2, xb, B, S, positions, ln_g[layer, 0], ln_b[layer, 0], od_w_in,
                                ret_gn_w[j], ret_gn_b[j], od_w_out, j)
        x2, xb, x3p = _cross_attention(x2, xb, B, S, memb, ln_g[layer, 1], ln_b[layer, 1], xa_wq, xa_wkv,
                                       xa_wo, layer)
        x2, xb = _moe(x2, xb, x3p, ln_g[layer, 2], ln_b[layer, 2], router_w[layer], router_bias[layer],
                      moe_w1, moe_w3, moe_w2, sh_w1, sh_w3, sh_w2, layer)
    return x2.reshape(B, S, D)
```

```python
import functools

import numpy as np
import jax
import jax.numpy as jnp
from jax import lax
from jax.experimental import pallas as pl
from jax.experimental.pallas import tpu as pltpu

F32 = jnp.float32
BF16 = jnp.bfloat16
HI = lax.Precision.HIGHEST

DEPTH = 4
ALPHA = (2.0 * DEPTH) ** 0.25
LN_EPS = 1e-5
NEG_BIG = -1e30
CONV_K = 4
ML_HEADS = 4
RW_HEAD = 64
RW_LORA_W = 64
RW_LORA_A = 64
RW_LORA_G = 128
RW_GN_EPS = 64e-5
RET_HEADS = 8
ROPE_BASE = 10000.0
XA_HEADS = 4
TOP_K = 8
N_GROUPS = 8
TOPK_GROUPS = 4
ROUTED_SCALE = 2.5

ML_CHUNK = 64
RW_CHUNK = 64
RET_CHUNK = 256
MOE_ROWS_PER_TILE = 512

V7X_VMEM_BYTES = 64 * 2 ** 20
VMEM_LIMIT = (V7X_VMEM_BYTES * 3) // 4
LANES = 128
V7X_MXU_DIM = 256


def _params(*sem):
    return pltpu.CompilerParams(dimension_semantics=sem, vmem_limit_bytes=VMEM_LIMIT)


def _bdot(a, b):
    return jnp.dot(a.astype(BF16), b.astype(BF16), preferred_element_type=F32)


def _bdot_nt(a, b):
    return lax.dot_general(a.astype(BF16), b.astype(BF16), (((1,), (1,)), ((), ())),
                           preferred_element_type=F32)


def _bdot_tn(a, b):
    return lax.dot_general(a.astype(BF16), b.astype(BF16), (((0,), (0,)), ((), ())),
                           preferred_element_type=F32)


def _sigmoid(x):
    return 1.0 / (1.0 + jnp.exp(-x))


def _softplus(x):
    return jnp.maximum(x, 0.0) + jnp.log1p(jnp.exp(-jnp.abs(x)))


def _iota2(shape, dim):
    return lax.broadcasted_iota(jnp.int32, shape, dim)


def _mm_kernel(x_ref, w_ref, o_ref):
    o_ref[...] = _bdot(x_ref[...], w_ref[...]).astype(o_ref.dtype)


def _mm(x, w, layer, out_dtype, tm, tn):
    M, K = x.shape
    N = w.shape[2]
    assert M % tm == 0 and N % tn == 0
    return pl.pallas_call(
        _mm_kernel,
        grid=(M // tm, N // tn),
        in_specs=[pl.BlockSpec((tm, K), lambda i, j: (i, 0)),
                  pl.BlockSpec((None, K, tn), lambda i, j: (layer, 0, j))],
        out_specs=pl.BlockSpec((tm, tn), lambda i, j: (i, j)),
        out_shape=jax.ShapeDtypeStruct((M, N), out_dtype),
        compiler_params=_params("parallel", "parallel"),
        name="mm",
    )(x, w)


def _layer_norm_rows(y, g, b):
    mu = jnp.mean(y, axis=-1, keepdims=True)
    d = y - mu
    var = jnp.mean(d * d, axis=-1, keepdims=True)
    return d * lax.rsqrt(var + LN_EPS) * g + b


def _pack_rows(v):
    half = v.shape[1] // 2
    bits = lax.bitcast_convert_type(v.astype(BF16).astype(F32), jnp.int32)
    return bits[:, half:] | lax.shift_right_logical(bits[:, :half], 16)


def _unpack_rows(w):
    return jnp.concatenate([lax.bitcast_convert_type(lax.shift_left(w, 16), F32),
                            lax.bitcast_convert_type(w & jnp.int32(-65536), F32)], axis=-1)


def _mm_res_ln_kernel(*refs, n_a, has_add, packed):
    a_refs, (w_ref, res_ref), rest = refs[:n_a], refs[n_a:n_a + 2], refs[n_a + 2:]
    add_ref = rest[0] if has_add else None
    n_out = 3 if packed else 2
    g_ref, b_ref = rest[-n_out - 2:-n_out]
    o_ref, ob_ref = rest[-n_out:][:2]
    acc, k0 = None, 0
    for a_ref in a_refs:
        part = _bdot(a_ref[...], w_ref[k0:k0 + a_ref.shape[1], :])
        acc = part if acc is None else acc + part
        k0 += a_ref.shape[1]
    if has_add:
        acc = add_ref[...] + acc
    out = _layer_norm_rows(ALPHA * res_ref[...] + acc, g_ref[...], b_ref[...])
    o_ref[...] = out
    ob_ref[...] = out.astype(ob_ref.dtype)
    if packed:
        rest[-1][...] = _pack_rows(out)


def _mm_res_ln(a_parts, w, layer, res, g, b, add=None, packed=False, tm=256):
    M = res.shape[0]
    K, N = w.shape[1:]
    assert sum(a.shape[1] for a in a_parts) == K
    row = lambda i: (i, 0)
    fixed = lambda i: (0, 0)
    in_specs = [pl.BlockSpec((tm, a.shape[1]), row) for a in a_parts]
    in_specs += [pl.BlockSpec((None, K, N), lambda i: (layer, 0, 0)), pl.BlockSpec((tm, N), row)]
    args = [*a_parts, w, res]
    if add is not None:
        in_specs.append(pl.BlockSpec((tm, N), row))
        args.append(add)
    in_specs += [pl.BlockSpec((1, N), fixed), pl.BlockSpec((1, N), fixed)]
    args += [g.reshape(1, N), b.reshape(1, N)]
    out_specs = [pl.BlockSpec((tm, N), row), pl.BlockSpec((tm, N), row)]
    out_shape = [jax.ShapeDtypeStruct((M, N), F32), jax.ShapeDtypeStruct((M, N), BF16)]
    if packed:
        out_specs.append(pl.BlockSpec((tm, N // 2), row))
        out_shape.append(jax.ShapeDtypeStruct((M, N // 2), jnp.int32))
    return pl.pallas_call(
        functools.partial(_mm_res_ln_kernel, n_a=len(a_parts), has_add=add is not None, packed=packed),
        grid=(M // tm,),
        in_specs=in_specs,
        out_specs=out_specs,
        out_shape=out_shape,
        compiler_params=_params("parallel"),
        name="mm_res_ln",
    )(*args)


def _glu_kernel(x_ref, w1_ref, w3_ref, o_ref):
    x = x_ref[...]
    h1 = jnp.dot(x, w1_ref[...], preferred_element_type=F32)
    h3 = jnp.dot(x, w3_ref[...], preferred_element_type=F32)
    o_ref[...] = (h1 * _sigmoid(h1) * h3).astype(o_ref.dtype)


def _glu(xb, w1, w3, layer, tm=512):
    M, K = xb.shape
    N = w1.shape[2]
    wspec = pl.BlockSpec((None, K, N), lambda i: (layer, 0, 0))
    return pl.pallas_call(
        _glu_kernel,
        grid=(M // tm,),
        in_specs=[pl.BlockSpec((tm, K), lambda i: (i, 0)), wspec, wspec],
        out_specs=pl.BlockSpec((tm, N), lambda i: (i, 0)),
        out_shape=jax.ShapeDtypeStruct((M, N), BF16),
        compiler_params=_params("parallel"),
        name="glu",
    )(xb, w1, w3)


def _log_sigmoid(x):
    return jnp.minimum(x, 0.0) - jnp.log1p(jnp.exp(-jnp.abs(x)))


def _split_bf16(x, terms):
    pieces = []
    for _ in range(terms):
        p = x.astype(BF16)
        pieces.append(p)
        x = x - p.astype(F32)
    return pieces


def _mlstm_kernel(qk_ref, v_ref, o_ref, gc_ref, gr_ref, cw_ref, cb_ref, gbc_ref, gbr_ref, gn_ref,
                  out_ref, prev_ref, c_ref, n_ref, m_ref, *, heads, dk, dv):
    nb, L, _ = qk_ref.shape
    qw = heads * dk

    @pl.when(pl.program_id(0) == 0)
    def _():
        prev_ref[...] = jnp.zeros_like(prev_ref)
        c_ref[...] = jnp.zeros_like(c_ref)
        n_ref[...] = jnp.zeros_like(n_ref)
        m_ref[...] = jnp.full_like(m_ref, NEG_BIG)

    row = _iota2((L, 1), 0)
    qk = []
    for b in range(nb):
        cur = qk_ref[b].astype(F32)
        prev = prev_ref[b]
        acc = cb_ref[...] + cw_ref[CONV_K - 1:CONV_K, :] * cur
        for j in range(CONV_K - 1):
            s = CONV_K - 1 - j
            shifted = jnp.where(row < s, pltpu.roll(prev, s, 0), pltpu.roll(cur, s, 0))
            acc = acc + cw_ref[j:j + 1, :] * shifted
        prev_ref[b] = cur
        qk.append(acc * _sigmoid(acc))

    causal = _iota2((L, L), 0) >= _iota2((L, L), 1)
    tri = causal.astype(F32).astype(BF16)
    g_col = [gc_ref[b].astype(F32) + gbc_ref[...] for b in range(nb)]
    pieces = _split_bf16(jnp.concatenate([_log_sigmoid(g) for g in g_col], axis=1), 3)
    lanes = nb * g_col[0].shape[1]
    b_col = jnp.dot(tri, jnp.concatenate(pieces, axis=1), preferred_element_type=F32)
    b_col = b_col[:, :lanes] + b_col[:, lanes:2 * lanes] + b_col[:, 2 * lanes:]
    g_row = [gr_ref[b, 0] + gbr_ref[...] for b in range(nb)]
    pieces = _split_bf16(jnp.concatenate([_log_sigmoid(g) for g in g_row], axis=0), 3)
    rows = nb * 2 * heads
    b_row = lax.dot_general(jnp.concatenate(pieces, axis=0), tri, (((1,), (1,)), ((), ())),
                            preferred_element_type=F32)
    b_row = b_row[:rows] + b_row[rows:2 * rows] + b_row[2 * rows:]

    def per_problem(f):
        return jnp.stack([f(b, h) for b in range(nb) for h in range(heads)])

    q = per_problem(lambda b, h: qk[b][:, h * dk:(h + 1) * dk])
    k = per_problem(lambda b, h: qk[b][:, qw + h * dk:qw + (h + 1) * dk]) * dk ** -0.5
    v = per_problem(lambda b, h: v_ref[b, :, h * dv:(h + 1) * dv])
    gcw = g_col[0].shape[1]
    bc = per_problem(lambda b, h: b_col[:, b * gcw + heads + h:b * gcw + heads + h + 1])
    ic = per_problem(lambda b, h: g_col[b][:, h:h + 1])
    br = per_problem(lambda b, h: b_row[b * 2 * heads + heads + h:b * 2 * heads + heads + h + 1])
    ir = per_problem(lambda b, h: g_row[b][h:h + 1])
    b_last = bc[:, L - 1:L]
    c_prev = c_ref[...]
    n_prev = n_ref[...]
    m_prev = m_ref[...][:, :, 0:1]

    def bmm(x, y, spec):
        return jnp.einsum(spec, x.astype(BF16), y.astype(BF16), preferred_element_type=F32)

    log_d = jnp.where(causal, bc - br + ir, -jnp.inf)
    g_inter = bc + m_prev
    m_t = jnp.maximum(jnp.max(log_d, axis=-1, keepdims=True), g_inter)
    p = jnp.exp(log_d - m_t) * bmm(q, k, "gtk,gsk->gts")
    e_inter = jnp.exp(g_inter - m_t)
    num = bmm(p, v, "gts,gsv->gtv") + e_inter * bmm(q, c_prev, "gtk,gkv->gtv")
    den = jnp.sum(p, axis=-1, keepdims=True) + e_inter * jnp.sum(q * n_prev, axis=-1, keepdims=True)
    hh = num / jnp.maximum(jnp.abs(den), jnp.exp(-m_t))

    w_end = b_last - bc + ic
    m_loc = jnp.max(w_end, axis=1, keepdims=True)
    e_end = jnp.exp(w_end - m_loc)
    ke = k * e_end
    m_new = jnp.maximum(b_last + m_prev, m_loc)
    s_prev = jnp.exp(b_last + m_prev - m_new)
    s_new = jnp.exp(m_loc - m_new)
    c_ref[...] = s_prev * c_prev + s_new * bmm(ke, v, "gsk,gsv->gkv")
    n_ref[...] = s_prev * n_prev + s_new * jnp.sum(ke, axis=1, keepdims=True)
    m_ref[...] = jnp.broadcast_to(m_new, m_ref.shape)

    mu = jnp.mean(hh, axis=-1, keepdims=True)
    d = hh - mu
    var = jnp.mean(d * d, axis=-1, keepdims=True)
    y = d * lax.rsqrt(var + 1e-6)
    for b in range(nb):
        for h in range(heads):
            sl = slice(h * dv, (h + 1) * dv)
            gate = _sigmoid(o_ref[b, :, sl].astype(F32))
            out_ref[b, :, sl] = (y[b * heads + h] * gn_ref[:, sl] * gate).astype(out_ref.dtype)


def _mlstm(z3, gates_t, conv_w, conv_b, gate_b, gn_w, *, qk_blk, v_blk, o_blk, gate_blk):
    B, S, _ = z3.shape
    H = ML_HEADS
    W = gn_w.shape[0]
    dv = W // H
    dk = conv_w.shape[1] // (2 * H)
    assert conv_w.shape[1] == W, "q|k block and v block share one column-block width"
    L = ML_CHUNK
    gb_col = jnp.zeros((1, LANES), F32).at[0, :2 * H].set(gate_b.reshape(-1))
    gb_row = gate_b.reshape(2 * H, 1)
    fixed = lambda c: (0, 0)
    return pl.pallas_call(
        functools.partial(_mlstm_kernel, heads=H, dk=dk, dv=dv),
        grid=(S // L,),
        in_specs=[pl.BlockSpec((B, L, W), lambda c: (0, c, qk_blk)),
                  pl.BlockSpec((B, L, W), lambda c: (0, c, v_blk)),
                  pl.BlockSpec((B, L, W), lambda c: (0, c, o_blk)),
                  pl.BlockSpec((B, L, LANES), lambda c: (0, c, gate_blk)),
                  pl.BlockSpec((B, 1, 2 * H, L), lambda c: (0, c, 0, 0)),
                  pl.BlockSpec((CONV_K, W), fixed),
                  pl.BlockSpec((1, W), fixed),
                  pl.BlockSpec((1, LANES), fixed),
                  pl.BlockSpec((2 * H, 1), fixed),
                  pl.BlockSpec((1, W), fixed)],
        out_specs=pl.BlockSpec((B, L, W), lambda c: (0, c, 0)),
        out_shape=jax.ShapeDtypeStruct((B, S, W), BF16),
        scratch_shapes=[pltpu.VMEM((B, L, W), F32),
                        pltpu.VMEM((B * H, dk, dv), F32),
                        pltpu.VMEM((B * H, 1, dk), F32),
                        pltpu.VMEM((B * H, 1, LANES), F32)],
        compiler_params=_params("arbitrary"),
        name="mlstm",
    )(z3, z3, z3, z3, gates_t, conv_w, conv_b.reshape(1, W), gb_col, gb_row, gn_w.reshape(1, W))


def _unit_lower_inverse(a, t_idx, i_idx, block_diag, matmul):
    n = a.shape[-2]
    eye = (t_idx == i_idx).astype(F32)
    t = None
    s = 1
    while s < n:
        pair = (t_idx // (2 * s)) == (i_idx // (2 * s))
        off = pair & ((t_idx // s) % 2 == 1) & ((i_idx // s) % 2 == 0)
        q = jnp.where(off, a, 0.0)
        t = eye - q if t is None else t - matmul(matmul(t, block_diag(q)), block_diag(t))
        s *= 2
    return t


def _rwkv_kernel(r_ref, k_ref, v_ref, l_ref, mur_ref, muk_ref, muv_ref, mul_ref, w0_ref, w2_ref,
                 a0_ref, a2_ref, g2_ref, kk_ref, ka_ref, rk_ref, gnw_ref, gnb_ref, out_ref,
                 pr_ref, pk_ref, pv_ref, plo_ref, s_ref, *, hd, gw):
    nb, L, W = r_ref.shape
    ng = W // gw
    hpg = gw // hd
    assert hpg * L == gw

    @pl.when(pl.program_id(1) == 0)
    def _():
        pr_ref[...] = jnp.zeros_like(pr_ref)
        pk_ref[...] = jnp.zeros_like(pk_ref)
        pv_ref[...] = jnp.zeros_like(pv_ref)
        plo_ref[...] = jnp.zeros_like(plo_ref)
        s_ref[...] = jnp.zeros_like(s_ref)

    R = nb * L
    first = _iota2((L, 1), 0) == 0
    same_head = (_iota2((gw, gw), 0) // hd) == (_iota2((gw, gw), 1) // hd)
    zero = jnp.zeros((), BF16)
    same_head_b = same_head.astype(F32).astype(BF16)
    row, col = _iota2((R, R), 0), _iota2((R, R), 1)
    tri = ((row // L == col // L) & (row >= col)).astype(F32).astype(BF16)
    t_idx = _iota2((L, gw), 0)
    i_idx = _iota2((L, gw), 1) % L
    strict = t_idx > i_idx
    lower = t_idx >= i_idx

    def shift_lerp(x_ref, p_ref, mu_ref):
        parts = []
        for b in range(nb):
            cur = x_ref[b].astype(F32)
            shifted = jnp.where(first, p_ref[b], pltpu.roll(cur, 1, 0))
            p_ref[b] = cur[L - 1:L, :]
            parts.append(cur + mu_ref[...] * (shifted - cur))
        return jnp.concatenate(parts, axis=0)

    def head_sums(xs):
        stacked = jnp.concatenate([x[:, i * gw:(i + 1) * gw] for x in xs for i in range(ng)], axis=0)
        n = stacked.shape[0]
        s = jnp.dot(jnp.concatenate(_split_bf16(stacked, 2), axis=0), same_head_b, preferred_element_type=F32)
        s = s[:n] + s[n:]
        return [jnp.concatenate([s[(j * ng + i) * R:(j * ng + i + 1) * R] for i in range(ng)], axis=1)
                for j in range(len(xs))]

    r = shift_lerp(r_ref, pr_ref, mur_ref)
    k = shift_lerp(k_ref, pk_ref, muk_ref)
    v = shift_lerp(v_ref, pv_ref, muv_ref)
    lo = shift_lerp(l_ref, plo_ref, mul_ref)
    xw = lo[:, :RW_LORA_W]
    xa = lo[:, RW_LORA_W:RW_LORA_W + RW_LORA_A]
    xg = lo[:, RW_LORA_W + RW_LORA_A:]

    w_log = -jnp.exp(-_softplus(-(w0_ref[...] + _bdot(jnp.tanh(xw), w2_ref[...]))) - 0.5)
    a = _sigmoid(a0_ref[...] + _bdot(xa, a2_ref[...]))
    g = _bdot(_sigmoid(xg), g2_ref[...])

    kk = k * kk_ref[...]
    k = k * (1.0 + (a - 1.0) * ka_ref[...])
    kk_sq, rk_sum = head_sums([kk * kk, r * k * rk_ref[...]])
    kk = kk * lax.rsqrt(jnp.maximum(kk_sq, 1e-24))
    bv = kk * a

    c3 = jnp.dot(tri, jnp.concatenate(_split_bf16(w_log, 3), axis=1), preferred_element_type=F32)
    c = c3[:, :W] + c3[:, W:2 * W] + c3[:, 2 * W:]
    c_last = jnp.concatenate([jnp.broadcast_to(c[(b + 1) * L - 1:(b + 1) * L], (L, W)) for b in range(nb)],
                             axis=0)
    e_end = jnp.exp(c_last - c)
    e_neg = jnp.exp(-c)
    kt = (kk * jnp.exp(c - w_log)).astype(BF16)
    rt = (r * jnp.exp(c)).astype(BF16)
    kh = (k * e_neg).astype(BF16)
    bh = (bv * e_neg).astype(BF16)
    kp = (k * e_end).astype(BF16)
    bp = (bv * e_end).astype(BF16)
    vb = v.astype(BF16)
    p_last = jnp.exp(c_last)

    def groups(x):
        return jnp.stack([x[b * L:(b + 1) * L, i * gw:(i + 1) * gw] for b in range(nb) for i in range(ng)])

    def block_diag(x):
        return jnp.where(same_head, jnp.concatenate([x.astype(BF16)] * hpg, axis=1), zero)

    def bmm(x, y):
        return jnp.einsum("gmk,gkn->gmn", x.astype(BF16), y.astype(BF16), preferred_element_type=F32)

    def bmm_nt(x, y):
        return jnp.einsum("gmk,gnk->gmn", x.astype(BF16), y.astype(BF16), preferred_element_type=F32)

    lhs = jnp.concatenate([groups(kt), groups(rt)], axis=1)
    m_k = bmm_nt(lhs, block_diag(groups(kh)))
    m_b = bmm_nt(lhs, block_diag(groups(bh)))
    a_kk = jnp.where(strict, m_k[:, :L], 0.0)
    a_kb = jnp.where(strict, m_b[:, :L], 0.0)
    a_rk = jnp.where(lower, m_k[:, L:], 0.0)
    a_rb = jnp.where(lower, m_b[:, L:], 0.0)
    t_inv = _unit_lower_inverse(a_kb, t_idx, i_idx, block_diag, bmm)
    st = s_ref[...]
    gs = bmm_nt(lhs, st)
    vg = groups(vb)
    v_bd = block_diag(vg)
    u = bmm(t_inv, block_diag(gs[:, :L] + bmm(a_kk, v_bd)))
    yg = gs[:, L:] + bmm(jnp.concatenate([a_rk, -a_rb], axis=2), jnp.concatenate([v_bd, block_diag(u)], axis=1))
    vu = jnp.concatenate([vg, (-u).astype(BF16)], axis=1)
    kb = jnp.concatenate([groups(kp), groups(bp)], axis=1)
    pg = groups(p_last)[:, :1]
    outer = jnp.einsum("gtv,gtk->gvk", vu, kb, preferred_element_type=F32)
    s_ref[...] = st * pg + jnp.where(same_head, outer, 0.0)
    ys = [jnp.concatenate([yg[b * ng + i] for i in range(ng)], axis=1) for b in range(nb)]

    y = jnp.concatenate(ys, axis=0)
    mu = head_sums([y])[0] * (1.0 / hd)
    d = y - mu
    var = head_sums([d * d])[0] * (1.0 / hd)
    yn = d * lax.rsqrt(var + RW_GN_EPS) * gnw_ref[...] + gnb_ref[...]
    out = ((yn + rk_sum * v) * g).astype(out_ref.dtype)
    for b in range(nb):
        out_ref[b] = out[b * L:(b + 1) * L]


def _rwkv(z3, mu, w0, w2, a0, a2, g2, k_k, k_a, r_k, gn_w, gn_b, *, col0, lora_col0, width):
    B, S, _ = z3.shape
    RW = w0.shape[0]
    W = width
    G = RW // W
    L = RW_CHUNK
    LO = RW_LORA_W + RW_LORA_A + RW_LORA_G
    assert col0 % W == 0 and RW % W == 0 and lora_col0 % LO == 0 and W % V7X_MXU_DIM == 0
    zc = lambda off: (lambda g, c: (0, c, off // W + g))
    vec = lambda g, c: (0, g)
    row = lambda x: x.reshape(1, -1)
    return pl.pallas_call(
        functools.partial(_rwkv_kernel, hd=RW_HEAD, gw=V7X_MXU_DIM),
        grid=(G, S // L),
        in_specs=[pl.BlockSpec((B, L, W), zc(col0)),
                  pl.BlockSpec((B, L, W), zc(col0 + RW)),
                  pl.BlockSpec((B, L, W), zc(col0 + 2 * RW)),
                  pl.BlockSpec((B, L, LO), lambda g, c: (0, c, lora_col0 // LO)),
                  pl.BlockSpec((1, W), vec), pl.BlockSpec((1, W), vec), pl.BlockSpec((1, W), vec),
                  pl.BlockSpec((1, LO), lambda g, c: (0, 0)),
                  pl.BlockSpec((1, W), vec),
                  pl.BlockSpec((RW_LORA_W, W), vec),
                  pl.BlockSpec((1, W), vec),
                  pl.BlockSpec((RW_LORA_A, W), vec),
                  pl.BlockSpec((RW_LORA_G, W), vec),
                  pl.BlockSpec((1, W), vec), pl.BlockSpec((1, W), vec), pl.BlockSpec((1, W), vec),
                  pl.BlockSpec((1, W), vec), pl.BlockSpec((1, W), vec)],
        out_specs=pl.BlockSpec((B, L, W), lambda g, c: (0, c, g)),
        out_shape=jax.ShapeDtypeStruct((B, S, RW), BF16),
        scratch_shapes=[pltpu.VMEM((B, 1, W), F32), pltpu.VMEM((B, 1, W), F32), pltpu.VMEM((B, 1, W), F32),
                        pltpu.VMEM((B, 1, LO), F32),
                        pltpu.VMEM((B * (W // V7X_MXU_DIM), V7X_MXU_DIM, V7X_MXU_DIM), F32)],
        compiler_params=_params("parallel", "arbitrary"),
        name="rwkv7",
    )(z3, z3, z3, z3, row(mu[:RW]), row(mu[RW:2 * RW]), row(mu[2 * RW:3 * RW]), row(mu[3 * RW:]),
      row(w0), w2.astype(BF16), row(a0), a2.astype(BF16), g2.astype(BF16), row(k_k), row(k_a),
      row(r_k), row(gn_w), row(gn_b))


def _retention_kernel(q_ref, k_ref, v_ref, g_ref, pos_ref, inv_ref, dec_ref, gnw_ref, gnb_ref,
                      out_ref, r_ref, *, heads, log_gamma):
    L = q_ref.shape[1]
    dk = q_ref.shape[2] // heads
    dv = v_ref.shape[2] // heads
    half = dk // 2

    @pl.when(pl.program_id(1) == 0)
    def _():
        r_ref[...] = jnp.zeros_like(r_ref)

    ang = pos_ref[0] * inv_ref[...]
    cos = jnp.cos(ang)
    sin = jnp.sin(ang)
    t_in = _iota2((L, 1), 0).astype(F32)

    def rot(t):
        t1, t2 = t[:, :half], t[:, half:]
        return jnp.concatenate([t1 * cos - t2 * sin, t1 * sin + t2 * cos], axis=-1)

    for h in range(heads):
        lg = log_gamma[h]
        q = rot(q_ref[0, :, h * dk:(h + 1) * dk].astype(F32))
        k = rot(k_ref[0, :, h * dk:(h + 1) * dk].astype(F32)) * dk ** -0.5
        v = v_ref[0, :, h * dv:(h + 1) * dv]
        state = r_ref[h]
        inter = _bdot(q, state) * jnp.exp(lg * (t_in + 1.0))
        intra = _bdot(_bdot_nt(q, k) * dec_ref[h], v)
        r_ref[h] = state * float(np.exp(lg * L)) + _bdot_tn(k * jnp.exp(lg * (L - 1.0 - t_in)), v)
        y = intra + inter
        mu = jnp.mean(y, axis=-1, keepdims=True)
        d = y - mu
        var = jnp.mean(d * d, axis=-1, keepdims=True)
        yn = d * lax.rsqrt(var + 1e-6) * gnw_ref[:, h * dv:(h + 1) * dv] + gnb_ref[:, h * dv:(h + 1) * dv]
        gate = g_ref[0, :, h * dv:(h + 1) * dv].astype(F32)
        out_ref[0, :, h * dv:(h + 1) * dv] = (gate * _sigmoid(gate) * yn).astype(out_ref.dtype)


def _retention(z3, positions, gn_w, gn_b):
    B, S, _ = z3.shape
    W = gn_w.shape[0]
    H = RET_HEADS
    dk = W // H
    half = dk // 2
    L = RET_CHUNK
    log_gamma = np.log1p(-np.exp2(-5.0 - np.arange(H, dtype=np.float64)))
    rel = np.arange(L)[:, None] - np.arange(L)[None, :]
    decay = np.where(rel >= 0, np.exp(log_gamma[:, None, None] * np.maximum(rel, 0)), 0.0).astype(np.float32)
    inv = (ROPE_BASE ** (-jnp.arange(half, dtype=F32) / half)).reshape(1, half)
    pos = positions.astype(F32).reshape(B, S, 1)
    blk = lambda j: pl.BlockSpec((1, L, W), lambda b, c: (b, c, j))
    fixed2 = lambda b, c: (0, 0)
    return pl.pallas_call(
        functools.partial(_retention_kernel, heads=H, log_gamma=tuple(float(x) for x in log_gamma)),
        grid=(B, S // L),
        in_specs=[blk(0), blk(1), blk(2), blk(3),
                  pl.BlockSpec((1, L, 1), lambda b, c: (b, c, 0)),
                  pl.BlockSpec((1, half), fixed2),
                  pl.BlockSpec((H, L, L), lambda b, c: (0, 0, 0)),
                  pl.BlockSpec((1, W), fixed2), pl.BlockSpec((1, W), fixed2)],
        out_specs=pl.BlockSpec((1, L, W), lambda b, c: (b, c, 0)),
        out_shape=jax.ShapeDtypeStruct((B, S, W), BF16),
        scratch_shapes=[pltpu.VMEM((H, dk, W // H), F32)],
        compiler_params=_params("parallel", "arbitrary"),
        name="retention",
    )(z3, z3, z3, z3, pos, inv, jnp.asarray(decay), gn_w.reshape(1, W), gn_b.reshape(1, W))


def _xattn_kernel(q_ref, k_ref, v_ref, o_ref, *, heads):
    hd = q_ref.shape[2] // heads
    q, k, v = (jnp.stack([r[0, :, h * hd:(h + 1) * hd] for h in range(heads)]) for r in (q_ref, k_ref, v_ref))
    s = jnp.einsum("hqd,hmd->hqm", q, k, preferred_element_type=F32) * hd ** -0.5
    e = jnp.exp(s - jnp.max(s, axis=-1, keepdims=True))
    p = e / jnp.sum(e, axis=-1, keepdims=True)
    o = jnp.einsum("hqm,hmd->hqd", p.astype(BF16), v, preferred_element_type=F32)
    for h in range(heads):
        o_ref[0, :, h * hd:(h + 1) * hd] = o[h].astype(o_ref.dtype)


def _xattn(q3, kv3, tq=512):
    B, S, D = q3.shape
    M = kv3.shape[1]
    return pl.pallas_call(
        functools.partial(_xattn_kernel, heads=XA_HEADS),
        grid=(B, S // tq),
        in_specs=[pl.BlockSpec((1, tq, D), lambda b, i: (b, i, 0)),
                  pl.BlockSpec((1, M, D), lambda b, i: (b, 0, 0)),
                  pl.BlockSpec((1, M, D), lambda b, i: (b, 0, 1))],
        out_specs=pl.BlockSpec((1, tq, D), lambda b, i: (b, i, 0)),
        out_shape=jax.ShapeDtypeStruct((B, S, D), BF16),
        compiler_params=_params("parallel", "parallel"),
        name="xattn",
    )(q3, kv3, kv3)


def _first_argmax(vals, index, big):
    m = jnp.max(vals, axis=0, keepdims=True)
    idx = jnp.min(jnp.where(vals == m, index, big), axis=0, keepdims=True)
    return m, idx


def _router_kernel(x_ref, wt_ref, bias_ref, idx_ref, gate_ref, pos_ref, cnt_ref, carry_ref):
    E = wt_ref.shape[0]
    tm = x_ref.shape[0]
    per = E // N_GROUPS
    neg = -jnp.inf
    logits = lax.dot_general(wt_ref[...], x_ref[...], (((1,), (1,)), ((), ())), precision=HI,
                             preferred_element_type=F32)
    scores = _sigmoid(logits)
    biased = scores + bias_ref[...]
    eidx = _iota2((E, tm), 0)

    within = _iota2((per, tm), 0)
    group_scores = []
    for g in range(N_GROUPS):
        vals = biased[g * per:(g + 1) * per]
        m1, i1 = _first_argmax(vals, within, per)
        m2 = jnp.max(jnp.where(within == i1, neg, vals), axis=0, keepdims=True)
        group_scores.append(m1 + m2)
    gs = jnp.concatenate(group_scores, axis=0)

    gidx = _iota2((N_GROUPS, tm), 0)
    keep = jnp.zeros((N_GROUPS, tm), F32)
    for _ in range(TOPK_GROUPS):
        _, gi = _first_argmax(gs, gidx, N_GROUPS)
        hit = gidx == gi
        keep = jnp.where(hit, 1.0, keep)
        gs = jnp.where(hit, neg, gs)
    keep_e = jnp.concatenate([jnp.broadcast_to(keep[g:g + 1], (per, tm)) for g in range(N_GROUPS)], axis=0)

    cand = jnp.where(keep_e > 0.0, biased, neg)
    picks, affs = [], []
    for _ in range(TOP_K):
        _, ei = _first_argmax(cand, eidx, E)
        hit = eidx == ei
        picks.append(ei)
        affs.append(jnp.sum(jnp.where(hit, scores, 0.0), axis=0, keepdims=True))
        cand = jnp.where(hit, neg, cand)
    aff = jnp.concatenate(affs, axis=0)
    idx_ref[...] = jnp.concatenate(picks, axis=0)
    gate_ref[...] = ROUTED_SCALE * aff / jnp.sum(aff, axis=0, keepdims=True)

    @pl.when(pl.program_id(0) == 0)
    def _():
        carry_ref[...] = jnp.zeros_like(carry_ref)

    sel = jnp.where(cand == neg, jnp.where(keep_e > 0.0, 1.0, 0.0), 0.0)
    upper = (_iota2((tm, tm), 0) < _iota2((tm, tm), 1)).astype(F32).astype(BF16)
    before = carry_ref[:, :1] + jnp.dot(sel.astype(BF16), upper, preferred_element_type=F32)
    pos_ref[...] = jnp.concatenate(
        [jnp.sum(jnp.where(eidx == ei, before, 0.0), axis=0, keepdims=True) for ei in picks], axis=0
    ).astype(jnp.int32)
    total = carry_ref[:, :1] + jnp.sum(sel, axis=1, keepdims=True)
    carry_ref[...] = jnp.broadcast_to(total, carry_ref.shape)
    cnt_ref[...] = jnp.broadcast_to(total, cnt_ref.shape).astype(jnp.int32)


def _router(x, router_w, router_bias, tm=512):
    T, D = x.shape
    E = router_w.shape[1]
    idx, gate, pos, cnt = pl.pallas_call(
        _router_kernel,
        grid=(T // tm,),
        in_specs=[pl.BlockSpec((tm, D), lambda i: (i, 0)),
                  pl.BlockSpec((E, D), lambda i: (0, 0)),
                  pl.BlockSpec((E, 1), lambda i: (0, 0))],
        out_specs=[pl.BlockSpec((TOP_K, tm), lambda i: (0, i)), pl.BlockSpec((TOP_K, tm), lambda i: (0, i)),
                   pl.BlockSpec((TOP_K, tm), lambda i: (0, i)), pl.BlockSpec((E, LANES), lambda i: (0, 0))],
        out_shape=[jax.ShapeDtypeStruct((TOP_K, T), jnp.int32), jax.ShapeDtypeStruct((TOP_K, T), F32),
                   jax.ShapeDtypeStruct((TOP_K, T), jnp.int32), jax.ShapeDtypeStruct((E, LANES), jnp.int32)],
        scratch_shapes=[pltpu.VMEM((E, LANES), F32)],
        compiler_params=_params("arbitrary"),
        name="router",
    )(x, router_w.T, router_bias.reshape(E, 1))
    return idx.T, gate.T, pos.T, cnt[:, 0]


def _dispatch_plan(top_e, pos, counts, rows_per_tile):
    T, K = top_e.shape
    E = counts.shape[0]
    n_tiles = (T * K) // rows_per_tile + E
    tiles = (counts + rows_per_tile - 1) // rows_per_tile
    tile_end = jnp.cumsum(tiles)
    n_used = tile_end[-1]
    row_start = (tile_end - tiles) * rows_per_tile
    dest = jnp.take(row_start, top_e) + pos
    tile_id = jnp.minimum(jnp.arange(n_tiles, dtype=jnp.int32), n_used - 1)
    tile_expert = jnp.minimum(jnp.sum(tile_end[None, :] <= tile_id[:, None], axis=1), E - 1)
    owns = tiles > 0
    expert = jnp.arange(E, dtype=jnp.int32)
    later = jnp.where(owns[None, :] & (expert[None, :] > expert[:, None]), expert[None, :], E)
    next_owner = jnp.min(later, axis=1)
    next_owner = jnp.where(next_owner == E, -1, next_owner)
    run_parity = (jnp.cumsum(owns.astype(jnp.int32)) - 1) % 2
    i32 = lambda a: a.astype(jnp.int32)
    return (i32(tile_expert), i32(jnp.take(next_owner, tile_expert)), i32(jnp.take(run_parity, tile_expert)),
            i32(tile_end), i32(n_used.reshape(1)), i32(dest))


def _dispatch_kernel(tend_ref, dest_ref, x_ref, xs_hbm, zero_ref, zsem, sem, *, slots):
    tm = x_ref.shape[0]
    rows = zero_ref.shape[0]
    n_experts = tend_ref.shape[0]

    @pl.when(pl.program_id(0) == 0)
    def _():
        zero_ref[...] = jnp.zeros_like(zero_ref)

        def last_tile(e):
            end = tend_ref[e]
            start = tend_ref[jnp.maximum(e - 1, 0)]
            start = jnp.where(e == 0, 0, start)
            return end > start, pltpu.make_async_copy(
                zero_ref, xs_hbm.at[pl.ds(jnp.maximum(end - 1, 0) * rows, rows)], zsem)

        def start(e, carry):
            owns, copy = last_tile(e)

            @pl.when(owns)
            def _():
                copy.start()
            return carry

        def wait(e, carry):
            owns, copy = last_tile(e)

            @pl.when(owns)
            def _():
                copy.wait()
            return carry

        def unused_tile(n):
            return pltpu.make_async_copy(zero_ref, xs_hbm.at[pl.ds(n * rows, rows)], zsem)

        def start_unused(n, carry):
            unused_tile(n).start()
            return carry

        def wait_unused(n, carry):
            unused_tile(n).wait()
            return carry

        n_used, n_tiles = tend_ref[n_experts - 1], xs_hbm.shape[0] // rows
        lax.fori_loop(0, n_experts, start, 0)
        lax.fori_loop(n_used, n_tiles, start_unused, 0)
        lax.fori_loop(0, n_experts, wait, 0)
        lax.fori_loop(n_used, n_tiles, wait_unused, 0)

    def body(t, carry):
        for k in range(slots):
            pltpu.make_async_copy(x_ref.at[pl.ds(t, 1)], xs_hbm.at[pl.ds(dest_ref[0, t * slots + k], 1)],
                                  sem).start(priority=k % 2)
        return carry
    lax.fori_loop(0, tm, body, 0, unroll=2)
    for k in range(slots):
        pltpu.make_async_copy(x_ref, xs_hbm.at[pl.ds(0, tm)], sem).wait()


def _dispatch(x3p, dest, tile_end, n_tiles, rows_per_tile, tm=512):
    T, half = x3p.shape
    K = dest.shape[1]
    steps = T // tm
    d3 = dest.reshape(steps, 1, tm * K)
    return pl.pallas_call(
        functools.partial(_dispatch_kernel, slots=K),
        grid_spec=pltpu.PrefetchScalarGridSpec(
            num_scalar_prefetch=1,
            grid=(steps,),
            in_specs=[pl.BlockSpec((None, 1, K * tm), lambda i, te: (i, 0, 0), memory_space=pltpu.SMEM),
                      pl.BlockSpec((tm, half), lambda i, te: (i, 0))],
            out_specs=pl.BlockSpec(memory_space=pl.ANY),
            scratch_shapes=[pltpu.VMEM((rows_per_tile, half), jnp.int32),
                            pltpu.SemaphoreType.DMA(()), pltpu.SemaphoreType.DMA(())]),
        out_shape=jax.ShapeDtypeStruct((n_tiles * rows_per_tile, half), jnp.int32),
        compiler_params=_params("arbitrary"),
        name="dispatch",
    )(tile_end, d3, x3p)


def _sparse_experts_kernel(te_ref, nu_ref, next_ref, par_ref, xs_ref, w1_hbm, w3_hbm, w2_hbm, y_ref,
                           wf1, wf3, wf2, wb1, wb3, wb2, sem, *, layer):
    n = pl.program_id(0)

    def weight_copies(e, slot):
        return [pltpu.make_async_copy(w.at[layer, e], buf.at[slot], sem.at[slot])
                for w, buf in ((w1_hbm, wf1), (w3_hbm, wf3), (w2_hbm, wf2))]

    @pl.when(n < nu_ref[0])
    def _():
        @pl.when((n == 0) | (te_ref[n] != te_ref[jnp.maximum(n - 1, 0)]))
        def _():
            slot = par_ref[n]

            @pl.when(n == 0)
            def _():
                for c in weight_copies(te_ref[n], slot):
                    c.start()

            @pl.when(next_ref[n] >= 0)
            def _():
                for c in weight_copies(next_ref[n], 1 - slot):
                    c.start()

            for c in weight_copies(te_ref[n], slot):
                c.wait()
            wb1[...] = wf1[slot].astype(BF16)
            wb3[...] = wf3[slot].astype(BF16)
            wb2[...] = wf2[slot].astype(BF16)

        x = _unpack_rows(xs_ref[...]).astype(BF16)
        h1 = jnp.dot(x, wb1[...], preferred_element_type=F32)
        h3 = jnp.dot(x, wb3[...], preferred_element_type=F32)
        hidden = (h1 * _sigmoid(h1) * h3).astype(BF16)
        y_ref[...] = _pack_rows(jnp.dot(hidden, wb2[...], preferred_element_type=F32))

    @pl.when(n >= nu_ref[0])
    def _():
        y_ref[...] = jnp.zeros_like(y_ref)


def _sparse_experts(xs, tile_expert, n_used, tile_next, tile_parity, w1, w3, w2, layer, rows):
    _, _, D, Hd = w1.shape
    n_tiles = tile_expert.shape[0]
    return pl.pallas_call(
        functools.partial(_sparse_experts_kernel, layer=layer),
        grid_spec=pltpu.PrefetchScalarGridSpec(
            num_scalar_prefetch=4,
            grid=(n_tiles,),
            in_specs=[pl.BlockSpec((rows, D // 2), lambda n, te, nu, nx, par: (jnp.minimum(n, nu[0] - 1), 0)),
                      pl.BlockSpec(memory_space=pl.ANY),
                      pl.BlockSpec(memory_space=pl.ANY),
                      pl.BlockSpec(memory_space=pl.ANY)],
            out_specs=pl.BlockSpec((rows, D // 2), lambda n, te, nu, nx, par: (n, 0)),
            scratch_shapes=[pltpu.VMEM((2, D, Hd), F32), pltpu.VMEM((2, D, Hd), F32), pltpu.VMEM((2, Hd, D), F32),
                            pltpu.VMEM((D, Hd), BF16), pltpu.VMEM((D, Hd), BF16), pltpu.VMEM((Hd, D), BF16),
                            pltpu.SemaphoreType.DMA((2,))]),
        out_shape=jax.ShapeDtypeStruct(xs.shape, jnp.int32),
        compiler_params=_params("arbitrary"),
        name="sparse_experts",
    )(tile_expert, n_used, tile_next, tile_parity, xs, w1, w3, w2)


def _combine_kernel(dest_ref, dest_next_ref, gate_ref, hs_ref, w_ref, res_ref, g_ref, b_ref, y_hbm,
                    o_ref, ob_ref, buf, sem):
    i = pl.program_id(0)
    steps = pl.num_programs(0)
    K, tm = buf.shape[1:3]
    slot = i % 2

    def fetch(d_ref, s):
        def body(t, carry):
            for k in range(K):
                pltpu.make_async_copy(y_hbm.at[pl.ds(d_ref[0, t * K + k], 1)], buf.at[s, k, pl.ds(t, 1)],
                                      sem.at[s]).start(priority=k % 2)
            return carry
        lax.fori_loop(0, tm, body, 0, unroll=2)

    @pl.when(i == 0)
    def _():
        fetch(dest_ref, 0)

    @pl.when(i + 1 < steps)
    def _():
        fetch(dest_next_ref, 1 - slot)

    for k in range(K):
        pltpu.make_async_copy(y_hbm.at[pl.ds(0, tm)], buf.at[slot, k], sem.at[slot]).wait()
    gate = gate_ref[...]
    acc = jnp.zeros(o_ref.shape, F32)
    for k in range(K):
        acc = acc + gate[:, k:k + 1] * _unpack_rows(buf[slot, k])
    acc = acc + _bdot(hs_ref[...], w_ref[...])
    out = _layer_norm_rows(ALPHA * res_ref[...] + acc, g_ref[...], b_ref[...])
    o_ref[...] = out
    ob_ref[...] = out.astype(ob_ref.dtype)


def _combine_ln(y, dest, gate, hs, w, layer, res, g, b, tm=256):
    T, K = dest.shape
    D = 2 * y.shape[1]
    Kh = hs.shape[1]
    steps = T // tm
    d3 = dest.reshape(steps, 1, tm * K)
    row = lambda i: (i, 0)
    fixed = lambda i: (0, 0)
    return pl.pallas_call(
        _combine_kernel,
        grid=(steps,),
        in_specs=[pl.BlockSpec((None, 1, K * tm), lambda i: (i, 0, 0), memory_space=pltpu.SMEM),
                  pl.BlockSpec((None, 1, K * tm), lambda i: (jnp.minimum(i + 1, steps - 1), 0, 0),
                               memory_space=pltpu.SMEM),
                  pl.BlockSpec((tm, K), row),
                  pl.BlockSpec((tm, Kh), row),
                  pl.BlockSpec((None, Kh, D), lambda i: (layer, 0, 0)),
                  pl.BlockSpec((tm, D), row),
                  pl.BlockSpec((1, D), fixed), pl.BlockSpec((1, D), fixed),
                  pl.BlockSpec(memory_space=pl.ANY)],
        out_specs=[pl.BlockSpec((tm, D), row), pl.BlockSpec((tm, D), row)],
        out_shape=[jax.ShapeDtypeStruct((T, D), F32), jax.ShapeDtypeStruct((T, D), BF16)],
        scratch_shapes=[pltpu.VMEM((2, K, tm, D // 2), jnp.int32), pltpu.SemaphoreType.DMA((2,))],
        compiler_params=_params("arbitrary"),
        name="combine_ln",
    )(d3, d3, gate, hs, w, res, g.reshape(1, D), b.reshape(1, D), y)


def _even_mixer(x2, xb, B, S, ln_g, ln_b, w_in, conv_w, conv_b, gate_b, ml_gn_w, rw_mu, rw_w0, rw_w2,
                rw_a0, rw_a2, rw_g2, rw_kk, rw_ka, rw_rk, rw_gn_w, rw_gn_b, w_out, j):
    D = x2.shape[1]
    H = ML_HEADS
    ML = ml_gn_w.shape[0]
    RW = rw_w0.shape[0]
    LO = RW_LORA_W + RW_LORA_A + RW_LORA_G
    ml_main = 3 * ML
    ml_cols = ml_main + 2 * H
    pad_to = 512
    used = ml_main + 3 * RW + LO + LANES
    total = -(-used // pad_to) * pad_to
    w_perm = jnp.concatenate([
        w_in[:, :ml_main], w_in[:, ml_cols:ml_cols + 3 * RW + LO], w_in[:, ml_main:ml_cols],
        jnp.zeros((D, total - used + LANES - 2 * H), w_in.dtype)], axis=1).astype(BF16)
    z3 = _mm(xb, w_perm[None], 0, BF16, 1024, pad_to).reshape(B, S, total)
    gate_col0 = ml_main + 3 * RW + LO
    L = ML_CHUNK
    gates_t = jnp.swapaxes(
        z3[:, :, gate_col0:gate_col0 + 2 * H].astype(F32).reshape(B, S // L, L, 2 * H), 2, 3)
    h_ml = _mlstm(z3, gates_t, conv_w, conv_b, gate_b, ml_gn_w,
                  qk_blk=0, v_blk=1, o_blk=2, gate_blk=gate_col0 // LANES)
    h_rw = _rwkv(z3, rw_mu, rw_w0, rw_w2, rw_a0, rw_a2, rw_g2, rw_kk, rw_ka, rw_rk, rw_gn_w, rw_gn_b,
                 col0=ml_main, lora_col0=ml_main + 3 * RW, width=RW)
    return _mm_res_ln([h_ml.reshape(B * S, ML), h_rw.reshape(B * S, RW)], w_out, j, x2, ln_g, ln_b)


def _odd_mixer(x2, xb, B, S, positions, ln_g, ln_b, w_in, gn_w, gn_b, w_out, j):
    z3 = _mm(xb, w_in, j, BF16, 1024, 512).reshape(B, S, w_in.shape[2])
    h = _retention(z3, positions, gn_w, gn_b)
    return _mm_res_ln([h.reshape(B * S, -1)], w_out, j, x2, ln_g, ln_b)


def _cross_attention(x2, xb, B, S, memb, ln_g, ln_b, wq, wkv, wo, layer):
    D = x2.shape[1]
    q = _mm(xb, wq, layer, BF16, 1024, 512)
    kv = _mm(memb, wkv, layer, BF16, memb.shape[0], 512)
    o = _xattn(q.reshape(B, S, D), kv.reshape(B, -1, 2 * D))
    return _mm_res_ln([o.reshape(B * S, D)], wo, layer, x2, ln_g, ln_b, packed=True)


def _moe(x2, xb, x3p, ln_g, ln_b, router_w, router_bias, w1, w3, w2, sw1, sw3, sw2, layer):
    rows = MOE_ROWS_PER_TILE
    top_e, gate, pos, counts = _router(x2, router_w, router_bias)
    tile_expert, tile_next, tile_parity, tile_end, n_used, dest = _dispatch_plan(top_e, pos, counts, rows)
    xs = _dispatch(x3p, dest, tile_end, tile_expert.shape[0], rows)
    y = _sparse_experts(xs, tile_expert, n_used, tile_next, tile_parity, w1, w3, w2, layer, rows)
    hs = _glu(xb, sw1, sw3, layer)
    return _combine_ln(y, dest, gate, hs, sw2, layer, x2, ln_g, ln_b)


def kernel(x, mem, positions, ln_g, ln_b, xa_wq, xa_wkv, xa_wo, router_w, router_bias, moe_w1, moe_w3, moe_w2, sh_w1, sh_w3, sh_w2, ev_w_in, ml_conv_w, ml_conv_b, ml_gate_b, ml_gn_w, rw_mu, rw_w0, rw_w2, rw_a0, rw_a2, rw_g2, rw_kk, rw_ka, rw_rk, rw_gn_w, rw_gn_b, ev_w_out, od_w_in, ret_gn_w, ret_gn_b, od_w_out):
    B, S, D = x.shape
    x2 = x.reshape(B * S, D)
    xb = x2.astype(BF16)
    memb = mem.reshape(-1, D).astype(BF16)
    (xa_wq, xa_wkv, xa_wo, sh_w1, sh_w3, sh_w2, ev_w_out, od_w_in, od_w_out) = (
        w.astype(BF16) for w in (xa_wq, xa_wkv, xa_wo, sh_w1, sh_w3, sh_w2, ev_w_out, od_w_in, od_w_out))
    for layer in range(ln_g.shape[0]):
        j = layer // 2
        if layer % 2 == 0:
            x2, xb = _even_mixer(x2, xb, B, S, ln_g[layer, 0], ln_b[layer, 0], ev_w_in[j], ml_conv_w[j],
                                 ml_conv_b[j], ml_gate_b[j], ml_gn_w[j], rw_mu[j], rw_w0[j], rw_w2[j],
                                 rw_a0[j], rw_a2[j], rw_g2[j], rw_kk[j], rw_ka[j], rw_rk[j],
                                 rw_gn_w[j], rw_gn_b[j], ev_w_out, j)
        else:
            x2, xb = _odd_mixer(x2, xb, B, S, positions, ln_g[layer, 0], ln_b[layer, 0], od_w_in,
                                ret_gn_w[j], ret_gn_b[j], od_w_out, j)
        x2, xb, x3p = _cross_attention(x2, xb, B, S, memb, ln_g[layer, 1], ln_b[layer, 1], xa_wq, xa_wkv,
                                       xa_wo, layer)
        x2, xb = _moe(x2, xb, x3p, ln_g[layer, 2], ln_b[layer, 2], router_w[layer], router_bias[layer],
                      moe_w1, moe_w3, moe_w2, sh_w1, sh_w3, sh_w2, layer)
    return x2.reshape(B, S, D)
```

```python
import functools

import numpy as np
import jax
import jax.numpy as jnp
from jax import lax
from jax.experimental import pallas as pl
from jax.experimental.pallas import tpu as pltpu

F32 = jnp.float32
BF16 = jnp.bfloat16
HI = lax.Precision.HIGHEST

DEPTH = 4
ALPHA = (2.0 * DEPTH) ** 0.25
LN_EPS = 1e-5
NEG_BIG = -1e30
CONV_K = 4
ML_HEADS = 4
RW_HEAD = 64
RW_LORA_W = 64
RW_LORA_A = 64
RW_LORA_G = 128
RW_GN_EPS = 64e-5
RET_HEADS = 8
ROPE_BASE = 10000.0
XA_HEADS = 4
TOP_K = 8
N_GROUPS = 8
TOPK_GROUPS = 4
ROUTED_SCALE = 2.5

ML_CHUNK = 64
RW_CHUNK = 64
RET_CHUNK = 256
MOE_ROWS_PER_TILE = 512

V7X_VMEM_BYTES = 64 * 2 ** 20
VMEM_LIMIT = (V7X_VMEM_BYTES * 3) // 4
LANES = 128
V7X_MXU_DIM = 256


def _params(*sem):
    return pltpu.CompilerParams(dimension_semantics=sem, vmem_limit_bytes=VMEM_LIMIT)


def _bdot(a, b):
    return jnp.dot(a.astype(BF16), b.astype(BF16), preferred_element_type=F32)


def _bdot_nt(a, b):
    return lax.dot_general(a.astype(BF16), b.astype(BF16), (((1,), (1,)), ((), ())),
                           preferred_element_type=F32)


def _bdot_tn(a, b):
    return lax.dot_general(a.astype(BF16), b.astype(BF16), (((0,), (0,)), ((), ())),
                           preferred_element_type=F32)


def _sigmoid(x):
    return 1.0 / (1.0 + jnp.exp(-x))


def _softplus(x):
    return jnp.maximum(x, 0.0) + jnp.log1p(jnp.exp(-jnp.abs(x)))


def _iota2(shape, dim):
    return lax.broadcasted_iota(jnp.int32, shape, dim)


def _mm_kernel(x_ref, w_ref, o_ref):
    o_ref[...] = _bdot(x_ref[...], w_ref[...]).astype(o_ref.dtype)


def _mm(x, w, layer, out_dtype, tm, tn):
    M, K = x.shape
    N = w.shape[2]
    assert M % tm == 0 and N % tn == 0
    return pl.pallas_call(
        _mm_kernel,
        grid=(M // tm, N // tn),
        in_specs=[pl.BlockSpec((tm, K), lambda i, j: (i, 0)),
                  pl.BlockSpec((None, K, tn), lambda i, j: (layer, 0, j))],
        out_specs=pl.BlockSpec((tm, tn), lambda i, j: (i, j)),
        out_shape=jax.ShapeDtypeStruct((M, N), out_dtype),
        compiler_params=_params("parallel", "parallel"),
        name="mm",
    )(x, w)


def _layer_norm_rows(y, g, b):
    mu = jnp.mean(y, axis=-1, keepdims=True)
    d = y - mu
    var = jnp.mean(d * d, axis=-1, keepdims=True)
    return d * lax.rsqrt(var + LN_EPS) * g + b


def _pack_rows(v):
    half = v.shape[1] // 2
    bits = lax.bitcast_convert_type(v.astype(BF16).astype(F32), jnp.int32)
    return bits[:, half:] | lax.shift_right_logical(bits[:, :half], 16)


def _unpack_rows(w):
    return jnp.concatenate([lax.bitcast_convert_type(lax.shift_left(w, 16), F32),
                            lax.bitcast_convert_type(w & jnp.int32(-65536), F32)], axis=-1)


def _mm_res_ln_kernel(*refs, n_a, packed):
    a_refs, (w_ref, res_ref), rest = refs[:n_a], refs[n_a:n_a + 2], refs[n_a + 2:]
    n_out = 3 if packed else 2
    g_ref, b_ref = rest[-n_out - 2:-n_out]
    o_ref, ob_ref = rest[-n_out:][:2]
    acc, k0 = None, 0
    for a_ref in a_refs:
        part = _bdot(a_ref[...], w_ref[k0:k0 + a_ref.shape[1], :])
        acc = part if acc is None else acc + part
        k0 += a_ref.shape[1]
    out = _layer_norm_rows(ALPHA * res_ref[...] + acc, g_ref[...], b_ref[...])
    o_ref[...] = out
    ob_ref[...] = out.astype(ob_ref.dtype)
    if packed:
        rest[-1][...] = _pack_rows(out)


def _mm_res_ln(a_parts, w, layer, res, g, b, packed=False, tm=256):
    M = res.shape[0]
    K, N = w.shape[1:]
    assert sum(a.shape[1] for a in a_parts) == K
    row = lambda i: (i, 0)
    fixed = lambda i: (0, 0)
    in_specs = [pl.BlockSpec((tm, a.shape[1]), row) for a in a_parts]
    in_specs += [pl.BlockSpec((None, K, N), lambda i: (layer, 0, 0)), pl.BlockSpec((tm, N), row)]
    args = [*a_parts, w, res]
    in_specs += [pl.BlockSpec((1, N), fixed), pl.BlockSpec((1, N), fixed)]
    args += [g.reshape(1, N), b.reshape(1, N)]
    out_specs = [pl.BlockSpec((tm, N), row), pl.BlockSpec((tm, N), row)]
    out_shape = [jax.ShapeDtypeStruct((M, N), F32), jax.ShapeDtypeStruct((M, N), BF16)]
    if packed:
        out_specs.append(pl.BlockSpec((tm, N // 2), row))
        out_shape.append(jax.ShapeDtypeStruct((M, N // 2), jnp.int32))
    return pl.pallas_call(
        functools.partial(_mm_res_ln_kernel, n_a=len(a_parts), packed=packed),
        grid=(M // tm,),
        in_specs=in_specs,
        out_specs=out_specs,
        out_shape=out_shape,
        compiler_params=_params("parallel"),
        name="mm_res_ln",
    )(*args)


def _glu_kernel(x_ref, w1_ref, w3_ref, o_ref):
    x = x_ref[...]
    h1 = jnp.dot(x, w1_ref[...], preferred_element_type=F32)
    h3 = jnp.dot(x, w3_ref[...], preferred_element_type=F32)
    o_ref[...] = (h1 * _sigmoid(h1) * h3).astype(o_ref.dtype)


def _glu(xb, w1, w3, layer, tm=512):
    M, K = xb.shape
    N = w1.shape[2]
    wspec = pl.BlockSpec((None, K, N), lambda i: (layer, 0, 0))
    return pl.pallas_call(
        _glu_kernel,
        grid=(M // tm,),
        in_specs=[pl.BlockSpec((tm, K), lambda i: (i, 0)), wspec, wspec],
        out_specs=pl.BlockSpec((tm, N), lambda i: (i, 0)),
        out_shape=jax.ShapeDtypeStruct((M, N), BF16),
        compiler_params=_params("parallel"),
        name="glu",
    )(xb, w1, w3)


def _log_sigmoid(x):
    return jnp.minimum(x, 0.0) - jnp.log1p(jnp.exp(-jnp.abs(x)))


def _split_bf16(x, terms):
    pieces = []
    for _ in range(terms):
        p = x.astype(BF16)
        pieces.append(p)
        x = x - p.astype(F32)
    return pieces


def _mlstm_kernel(qk_ref, v_ref, o_ref, gc_ref, gr_ref, cw_ref, cb_ref, gbc_ref, gbr_ref, gn_ref,
                  out_ref, prev_ref, c_ref, n_ref, m_ref, *, heads, dk, dv):
    nb, L, _ = qk_ref.shape
    qw = heads * dk

    @pl.when(pl.program_id(0) == 0)
    def _():
        prev_ref[...] = jnp.zeros_like(prev_ref)
        c_ref[...] = jnp.zeros_like(c_ref)
        n_ref[...] = jnp.zeros_like(n_ref)
        m_ref[...] = jnp.full_like(m_ref, NEG_BIG)

    row = _iota2((L, 1), 0)
    qk = []
    for b in range(nb):
        cur = qk_ref[b].astype(F32)
        prev = prev_ref[b]
        acc = cb_ref[...] + cw_ref[CONV_K - 1:CONV_K, :] * cur
        for j in range(CONV_K - 1):
            s = CONV_K - 1 - j
            shifted = jnp.where(row < s, pltpu.roll(prev, s, 0), pltpu.roll(cur, s, 0))
            acc = acc + cw_ref[j:j + 1, :] * shifted
        prev_ref[b] = cur
        qk.append(acc * _sigmoid(acc))

    causal = _iota2((L, L), 0) >= _iota2((L, L), 1)
    tri = causal.astype(F32).astype(BF16)
    g_col = [gc_ref[b].astype(F32) + gbc_ref[...] for b in range(nb)]
    pieces = _split_bf16(jnp.concatenate([_log_sigmoid(g) for g in g_col], axis=1), 3)
    lanes = nb * g_col[0].shape[1]
    b_col = jnp.dot(tri, jnp.concatenate(pieces, axis=1), preferred_element_type=F32)
    b_col = b_col[:, :lanes] + b_col[:, lanes:2 * lanes] + b_col[:, 2 * lanes:]
    g_row = [gr_ref[b, 0] + gbr_ref[...] for b in range(nb)]
    pieces = _split_bf16(jnp.concatenate([_log_sigmoid(g) for g in g_row], axis=0), 3)
    rows = nb * 2 * heads
    b_row = lax.dot_general(jnp.concatenate(pieces, axis=0), tri, (((1,), (1,)), ((), ())),
                            preferred_element_type=F32)
    b_row = b_row[:rows] + b_row[rows:2 * rows] + b_row[2 * rows:]

    def per_problem(f):
        return jnp.stack([f(b, h) for b in range(nb) for h in range(heads)])

    q = per_problem(lambda b, h: qk[b][:, h * dk:(h + 1) * dk])
    k = per_problem(lambda b, h: qk[b][:, qw + h * dk:qw + (h + 1) * dk]) * dk ** -0.5
    v = per_problem(lambda b, h: v_ref[b, :, h * dv:(h + 1) * dv])
    gcw = g_col[0].shape[1]
    bc = per_problem(lambda b, h: b_col[:, b * gcw + heads + h:b * gcw + heads + h + 1])
    ic = per_problem(lambda b, h: g_col[b][:, h:h + 1])
    br = per_problem(lambda b, h: b_row[b * 2 * heads + heads + h:b * 2 * heads + heads + h + 1])
    ir = per_problem(lambda b, h: g_row[b][h:h + 1])
    b_last = bc[:, L - 1:L]
    c_prev = c_ref[...]
    n_prev = n_ref[...]
    m_prev = m_ref[...][:, :, 0:1]

    def bmm(x, y, spec):
        return jnp.einsum(spec, x.astype(BF16), y.astype(BF16), preferred_element_type=F32)

    log_d = jnp.where(causal, bc - br + ir, -jnp.inf)
    g_inter = bc + m_prev
    m_t = jnp.maximum(jnp.max(log_d, axis=-1, keepdims=True), g_inter)
    p = jnp.exp(log_d - m_t) * bmm(q, k, "gtk,gsk->gts")
    e_inter = jnp.exp(g_inter - m_t)
    num = bmm(p, v, "gts,gsv->gtv") + e_inter * bmm(q, c_prev, "gtk,gkv->gtv")
    den = jnp.sum(p, axis=-1, keepdims=True) + e_inter * jnp.sum(q * n_prev, axis=-1, keepdims=True)
    hh = num / jnp.maximum(jnp.abs(den), jnp.exp(-m_t))

    w_end = b_last - bc + ic
    m_loc = jnp.max(w_end, axis=1, keepdims=True)
    e_end = jnp.exp(w_end - m_loc)
    ke = k * e_end
    m_new = jnp.maximum(b_last + m_prev, m_loc)
    s_prev = jnp.exp(b_last + m_prev - m_new)
    s_new = jnp.exp(m_loc - m_new)
    c_ref[...] = s_prev * c_prev + s_new * bmm(ke, v, "gsk,gsv->gkv")
    n_ref[...] = s_prev * n_prev + s_new * jnp.sum(ke, axis=1, keepdims=True)
    m_ref[...] = jnp.broadcast_to(m_new, m_ref.shape)

    mu = jnp.mean(hh, axis=-1, keepdims=True)
    d = hh - mu
    var = jnp.mean(d * d, axis=-1, keepdims=True)
    y = d * lax.rsqrt(var + 1e-6)
    for b in range(nb):
        for h in range(heads):
            sl = slice(h * dv, (h + 1) * dv)
            gate = _sigmoid(o_ref[b, :, sl].astype(F32))
            out_ref[b, :, sl] = (y[b * heads + h] * gn_ref[:, sl] * gate).astype(out_ref.dtype)


def _mlstm(z3, gates_t, conv_w, conv_b, gate_b, gn_w, *, qk_blk, v_blk, o_blk, gate_blk):
    B, S, _ = z3.shape
    H = ML_HEADS
    W = gn_w.shape[0]
    dv = W // H
    dk = conv_w.shape[1] // (2 * H)
    assert conv_w.shape[1] == W, "q|k block and v block share one column-block width"
    L = ML_CHUNK
    gb_col = jnp.zeros((1, LANES), F32).at[0, :2 * H].set(gate_b.reshape(-1))
    gb_row = gate_b.reshape(2 * H, 1)
    fixed = lambda c: (0, 0)
    return pl.pallas_call(
        functools.partial(_mlstm_kernel, heads=H, dk=dk, dv=dv),
        grid=(S // L,),
        in_specs=[pl.BlockSpec((B, L, W), lambda c: (0, c, qk_blk)),
                  pl.BlockSpec((B, L, W), lambda c: (0, c, v_blk)),
                  pl.BlockSpec((B, L, W), lambda c: (0, c, o_blk)),
                  pl.BlockSpec((B, L, LANES), lambda c: (0, c, gate_blk)),
                  pl.BlockSpec((B, 1, 2 * H, L), lambda c: (0, c, 0, 0)),
                  pl.BlockSpec((CONV_K, W), fixed),
                  pl.BlockSpec((1, W), fixed),
                  pl.BlockSpec((1, LANES), fixed),
                  pl.BlockSpec((2 * H, 1), fixed),
                  pl.BlockSpec((1, W), fixed)],
        out_specs=pl.BlockSpec((B, L, W), lambda c: (0, c, 0)),
        out_shape=jax.ShapeDtypeStruct((B, S, W), BF16),
        scratch_shapes=[pltpu.VMEM((B, L, W), F32),
                        pltpu.VMEM((B * H, dk, dv), F32),
                        pltpu.VMEM((B * H, 1, dk), F32),
                        pltpu.VMEM((B * H, 1, LANES), F32)],
        compiler_params=_params("arbitrary"),
        name="mlstm",
    )(z3, z3, z3, z3, gates_t, conv_w, conv_b.reshape(1, W), gb_col, gb_row, gn_w.reshape(1, W))


def _unit_lower_inverse(a, t_idx, i_idx, block_diag, matmul):
    n = a.shape[-2]
    eye = (t_idx == i_idx).astype(F32)
    t = None
    s = 1
    while s < n:
        pair = (t_idx // (2 * s)) == (i_idx // (2 * s))
        off = pair & ((t_idx // s) % 2 == 1) & ((i_idx // s) % 2 == 0)
        q = jnp.where(off, a, 0.0)
        t = eye - q if t is None else t - matmul(matmul(t, block_diag(q)), block_diag(t))
        s *= 2
    return t


def _rwkv_kernel(r_ref, k_ref, v_ref, l_ref, mur_ref, muk_ref, muv_ref, mul_ref, w0_ref, w2_ref,
                 a0_ref, a2_ref, g2_ref, kk_ref, ka_ref, rk_ref, gnw_ref, gnb_ref, out_ref,
                 pr_ref, pk_ref, pv_ref, plo_ref, s_ref, *, hd, gw):
    nb, L, W = r_ref.shape
    ng = W // gw
    hpg = gw // hd
    assert hpg * L == gw

    @pl.when(pl.program_id(1) == 0)
    def _():
        pr_ref[...] = jnp.zeros_like(pr_ref)
        pk_ref[...] = jnp.zeros_like(pk_ref)
        pv_ref[...] = jnp.zeros_like(pv_ref)
        plo_ref[...] = jnp.zeros_like(plo_ref)
        s_ref[...] = jnp.zeros_like(s_ref)

    R = nb * L
    first = _iota2((L, 1), 0) == 0
    same_head = (_iota2((gw, gw), 0) // hd) == (_iota2((gw, gw), 1) // hd)
    zero = jnp.zeros((), BF16)
    same_head_b = same_head.astype(F32).astype(BF16)
    row, col = _iota2((R, R), 0), _iota2((R, R), 1)
    tri = ((row // L == col // L) & (row >= col)).astype(F32).astype(BF16)
    t_idx = _iota2((L, gw), 0)
    i_idx = _iota2((L, gw), 1) % L
    strict = t_idx > i_idx
    lower = t_idx >= i_idx

    def shift_lerp(x_ref, p_ref, mu_ref):
        parts = []
        for b in range(nb):
            cur = x_ref[b].astype(F32)
            shifted = jnp.where(first, p_ref[b], pltpu.roll(cur, 1, 0))
            p_ref[b] = cur[L - 1:L, :]
            parts.append(cur + mu_ref[...] * (shifted - cur))
        return jnp.concatenate(parts, axis=0)

    def head_sums(xs):
        stacked = jnp.concatenate([x[:, i * gw:(i + 1) * gw] for x in xs for i in range(ng)], axis=0)
        n = stacked.shape[0]
        s = jnp.dot(jnp.concatenate(_split_bf16(stacked, 2), axis=0), same_head_b, preferred_element_type=F32)
        s = s[:n] + s[n:]
        return [jnp.concatenate([s[(j * ng + i) * R:(j * ng + i + 1) * R] for i in range(ng)], axis=1)
                for j in range(len(xs))]

    r = shift_lerp(r_ref, pr_ref, mur_ref)
    k = shift_lerp(k_ref, pk_ref, muk_ref)
    v = shift_lerp(v_ref, pv_ref, muv_ref)
    lo = shift_lerp(l_ref, plo_ref, mul_ref)
    xw = lo[:, :RW_LORA_W]
    xa = lo[:, RW_LORA_W:RW_LORA_W + RW_LORA_A]
    xg = lo[:, RW_LORA_W + RW_LORA_A:]

    w_log = -jnp.exp(-_softplus(-(w0_ref[...] + _bdot(jnp.tanh(xw), w2_ref[...]))) - 0.5)
    a = _sigmoid(a0_ref[...] + _bdot(xa, a2_ref[...]))
    g = _bdot(_sigmoid(xg), g2_ref[...])

    kk = k * kk_ref[...]
    k = k * (1.0 + (a - 1.0) * ka_ref[...])
    kk_sq, rk_sum = head_sums([kk * kk, r * k * rk_ref[...]])
    kk = kk * lax.rsqrt(jnp.maximum(kk_sq, 1e-24))
    bv = kk * a

    c3 = jnp.dot(tri, jnp.concatenate(_split_bf16(w_log, 3), axis=1), preferred_element_type=F32)
    c = c3[:, :W] + c3[:, W:2 * W] + c3[:, 2 * W:]
    c_last = jnp.concatenate([jnp.broadcast_to(c[(b + 1) * L - 1:(b + 1) * L], (L, W)) for b in range(nb)],
                             axis=0)
    e_end = jnp.exp(c_last - c)
    e_neg = jnp.exp(-c)
    kt = (kk * jnp.exp(c - w_log)).astype(BF16)
    rt = (r * jnp.exp(c)).astype(BF16)
    kh = (k * e_neg).astype(BF16)
    bh = (bv * e_neg).astype(BF16)
    kp = (k * e_end).astype(BF16)
    bp = (bv * e_end).astype(BF16)
    vb = v.astype(BF16)
    p_last = jnp.exp(c_last)

    def groups(x):
        return jnp.stack([x[b * L:(b + 1) * L, i * gw:(i + 1) * gw] for b in range(nb) for i in range(ng)])

    def block_diag(x):
        return jnp.where(same_head, jnp.concatenate([x.astype(BF16)] * hpg, axis=1), zero)

    def bmm(x, y):
        return jnp.einsum("gmk,gkn->gmn", x.astype(BF16), y.astype(BF16), preferred_element_type=F32)

    def bmm_nt(x, y):
        return jnp.einsum("gmk,gnk->gmn", x.astype(BF16), y.astype(BF16), preferred_element_type=F32)

    lhs = jnp.concatenate([groups(kt), groups(rt)], axis=1)
    m_k = bmm_nt(lhs, block_diag(groups(kh)))
    m_b = bmm_nt(lhs, block_diag(groups(bh)))
    a_kk = jnp.where(strict, m_k[:, :L], 0.0)
    a_kb = jnp.where(strict, m_b[:, :L], 0.0)
    a_rk = jnp.where(lower, m_k[:, L:], 0.0)
    a_rb = jnp.where(lower, m_b[:, L:], 0.0)
    t_inv = _unit_lower_inverse(a_kb, t_idx, i_idx, block_diag, bmm)
    st = s_ref[...]
    gs = bmm_nt(lhs, st)
    vg = groups(vb)
    v_bd = block_diag(vg)
    u = bmm(t_inv, block_diag(gs[:, :L] + bmm(a_kk, v_bd)))
    yg = gs[:, L:] + bmm(jnp.concatenate([a_rk, -a_rb], axis=2), jnp.concatenate([v_bd, block_diag(u)], axis=1))
    vu = jnp.concatenate([vg, (-u).astype(BF16)], axis=1)
    kb = jnp.concatenate([groups(kp), groups(bp)], axis=1)
    pg = groups(p_last)[:, :1]
    outer = jnp.einsum("gtv,gtk->gvk", vu, kb, preferred_element_type=F32)
    s_ref[...] = st * pg + jnp.where(same_head, outer, 0.0)
    ys = [jnp.concatenate([yg[b * ng + i] for i in range(ng)], axis=1) for b in range(nb)]

    y = jnp.concatenate(ys, axis=0)
    mu = head_sums([y])[0] * (1.0 / hd)
    d = y - mu
    var = head_sums([d * d])[0] * (1.0 / hd)
    yn = d * lax.rsqrt(var + RW_GN_EPS) * gnw_ref[...] + gnb_ref[...]
    out = ((yn + rk_sum * v) * g).astype(out_ref.dtype)
    for b in range(nb):
        out_ref[b] = out[b * L:(b + 1) * L]


def _rwkv(z3, mu, w0, w2, a0, a2, g2, k_k, k_a, r_k, gn_w, gn_b, *, col0, lora_col0, width):
    B, S, _ = z3.shape
    RW = w0.shape[0]
    W = width
    G = RW // W
    L = RW_CHUNK
    LO = RW_LORA_W + RW_LORA_A + RW_LORA_G
    assert col0 % W == 0 and RW % W == 0 and lora_col0 % LO == 0 and W % V7X_MXU_DIM == 0
    zc = lambda off: (lambda g, c: (0, c, off // W + g))
    vec = lambda g, c: (0, g)
    row = lambda x: x.reshape(1, -1)
    return pl.pallas_call(
        functools.partial(_rwkv_kernel, hd=RW_HEAD, gw=V7X_MXU_DIM),
        grid=(G, S // L),
        in_specs=[pl.BlockSpec((B, L, W), zc(col0)),
                  pl.BlockSpec((B, L, W), zc(col0 + RW)),
                  pl.BlockSpec((B, L, W), zc(col0 + 2 * RW)),
                  pl.BlockSpec((B, L, LO), lambda g, c: (0, c, lora_col0 // LO)),
                  pl.BlockSpec((1, W), vec), pl.BlockSpec((1, W), vec), pl.BlockSpec((1, W), vec),
                  pl.BlockSpec((1, LO), lambda g, c: (0, 0)),
                  pl.BlockSpec((1, W), vec),
                  pl.BlockSpec((RW_LORA_W, W), vec),
                  pl.BlockSpec((1, W), vec),
                  pl.BlockSpec((RW_LORA_A, W), vec),
                  pl.BlockSpec((RW_LORA_G, W), vec),
                  pl.BlockSpec((1, W), vec), pl.BlockSpec((1, W), vec), pl.BlockSpec((1, W), vec),
                  pl.BlockSpec((1, W), vec), pl.BlockSpec((1, W), vec)],
        out_specs=pl.BlockSpec((B, L, W), lambda g, c: (0, c, g)),
        out_shape=jax.ShapeDtypeStruct((B, S, RW), BF16),
        scratch_shapes=[pltpu.VMEM((B, 1, W), F32), pltpu.VMEM((B, 1, W), F32), pltpu.VMEM((B, 1, W), F32),
                        pltpu.VMEM((B, 1, LO), F32),
                        pltpu.VMEM((B * (W // V7X_MXU_DIM), V7X_MXU_DIM, V7X_MXU_DIM), F32)],
        compiler_params=_params("parallel", "arbitrary"),
        name="rwkv7",
    )(z3, z3, z3, z3, row(mu[:RW]), row(mu[RW:2 * RW]), row(mu[2 * RW:3 * RW]), row(mu[3 * RW:]),
      row(w0), w2.astype(BF16), row(a0), a2.astype(BF16), g2.astype(BF16), row(k_k), row(k_a),
      row(r_k), row(gn_w), row(gn_b))


def _retention_kernel(q_ref, k_ref, v_ref, g_ref, pos_ref, inv_ref, dec_ref, gnw_ref, gnb_ref,
                      out_ref, r_ref, *, heads, log_gamma):
    L = q_ref.shape[1]
    dk = q_ref.shape[2] // heads
    dv = v_ref.shape[2] // heads
    half = dk // 2

    @pl.when(pl.program_id(1) == 0)
    def _():
        r_ref[...] = jnp.zeros_like(r_ref)

    ang = pos_ref[0] * inv_ref[...]
    cos = jnp.cos(ang)
    sin = jnp.sin(ang)
    t_in = _iota2((L, 1), 0).astype(F32)

    def rot(t):
        t1, t2 = t[:, :half], t[:, half:]
        return jnp.concatenate([t1 * cos - t2 * sin, t1 * sin + t2 * cos], axis=-1)

    for h in range(heads):
        lg = log_gamma[h]
        q = rot(q_ref[0, :, h * dk:(h + 1) * dk].astype(F32))
        k = rot(k_ref[0, :, h * dk:(h + 1) * dk].astype(F32)) * dk ** -0.5
        v = v_ref[0, :, h * dv:(h + 1) * dv]
        state = r_ref[h]
        inter = _bdot(q, state) * jnp.exp(lg * (t_in + 1.0))
        intra = _bdot(_bdot_nt(q, k) * dec_ref[h], v)
        r_ref[h] = state * float(np.exp(lg * L)) + _bdot_tn(k * jnp.exp(lg * (L - 1.0 - t_in)), v)
        y = intra + inter
        mu = jnp.mean(y, axis=-1, keepdims=True)
        d = y - mu
        var = jnp.mean(d * d, axis=-1, keepdims=True)
        yn = d * lax.rsqrt(var + 1e-6) * gnw_ref[:, h * dv:(h + 1) * dv] + gnb_ref[:, h * dv:(h + 1) * dv]
        gate = g_ref[0, :, h * dv:(h + 1) * dv].astype(F32)
        out_ref[0, :, h * dv:(h + 1) * dv] = (gate * _sigmoid(gate) * yn).astype(out_ref.dtype)


def _retention(z3, positions, gn_w, gn_b):
    B, S, _ = z3.shape
    W = gn_w.shape[0]
    H = RET_HEADS
    dk = W // H
    half = dk // 2
    L = RET_CHUNK
    log_gamma = np.log1p(-np.exp2(-5.0 - np.arange(H, dtype=np.float64)))
    rel = np.arange(L)[:, None] - np.arange(L)[None, :]
    decay = np.where(rel >= 0, np.exp(log_gamma[:, None, None] * np.maximum(rel, 0)), 0.0).astype(np.float32)
    inv = (ROPE_BASE ** (-jnp.arange(half, dtype=F32) / half)).reshape(1, half)
    pos = positions.astype(F32).reshape(B, S, 1)
    blk = lambda j: pl.BlockSpec((1, L, W), lambda b, c: (b, c, j))
    fixed2 = lambda b, c: (0, 0)
    return pl.pallas_call(
        functools.partial(_retention_kernel, heads=H, log_gamma=tuple(float(x) for x in log_gamma)),
        grid=(B, S // L),
        in_specs=[blk(0), blk(1), blk(2), blk(3),
                  pl.BlockSpec((1, L, 1), lambda b, c: (b, c, 0)),
                  pl.BlockSpec((1, half), fixed2),
                  pl.BlockSpec((H, L, L), lambda b, c: (0, 0, 0)),
                  pl.BlockSpec((1, W), fixed2), pl.BlockSpec((1, W), fixed2)],
        out_specs=pl.BlockSpec((1, L, W), lambda b, c: (b, c, 0)),
        out_shape=jax.ShapeDtypeStruct((B, S, W), BF16),
        scratch_shapes=[pltpu.VMEM((H, dk, W // H), F32)],
        compiler_params=_params("parallel", "arbitrary"),
        name="retention",
    )(z3, z3, z3, z3, pos, inv, jnp.asarray(decay), gn_w.reshape(1, W), gn_b.reshape(1, W))


def _xattn_kernel(q_ref, k_ref, v_ref, o_ref, *, heads):
    hd = q_ref.shape[2] // heads
    q, k, v = (jnp.stack([r[0, :, h * hd:(h + 1) * hd] for h in range(heads)]) for r in (q_ref, k_ref, v_ref))
    s = jnp.einsum("hqd,hmd->hqm", q, k, preferred_element_type=F32) * hd ** -0.5
    e = jnp.exp(s - jnp.max(s, axis=-1, keepdims=True))
    p = e / jnp.sum(e, axis=-1, keepdims=True)
    o = jnp.einsum("hqm,hmd->hqd", p.astype(BF16), v, preferred_element_type=F32)
    for h in range(heads):
        o_ref[0, :, h * hd:(h + 1) * hd] = o[h].astype(o_ref.dtype)


def _xattn(q3, kv3, tq=512):
    B, S, D = q3.shape
    M = kv3.shape[1]
    return pl.pallas_call(
        functools.partial(_xattn_kernel, heads=XA_HEADS),
        grid=(B, S // tq),
        in_specs=[pl.BlockSpec((1, tq, D), lambda b, i: (b, i, 0)),
                  pl.BlockSpec((1, M, D), lambda b, i: (b, 0, 0)),
                  pl.BlockSpec((1, M, D), lambda b, i: (b, 0, 1))],
        out_specs=pl.BlockSpec((1, tq, D), lambda b, i: (b, i, 0)),
        out_shape=jax.ShapeDtypeStruct((B, S, D), BF16),
        compiler_params=_params("parallel", "parallel"),
        name="xattn",
    )(q3, kv3, kv3)


def _first_argmax(vals, index, big):
    m = jnp.max(vals, axis=0, keepdims=True)
    idx = jnp.min(jnp.where(vals == m, index, big), axis=0, keepdims=True)
    return m, idx


def _router_kernel(x_ref, wt_ref, bias_ref, idx_ref, gate_ref, pos_ref, cnt_ref, carry_ref):
    E = wt_ref.shape[0]
    tm = x_ref.shape[0]
    per = E // N_GROUPS
    neg = -jnp.inf
    logits = lax.dot_general(wt_ref[...], x_ref[...], (((1,), (1,)), ((), ())), precision=HI,
                             preferred_element_type=F32)
    scores = _sigmoid(logits)
    biased = scores + bias_ref[...]
    eidx = _iota2((E, tm), 0)

    within = _iota2((per, tm), 0)
    group_scores = []
    for g in range(N_GROUPS):
        vals = biased[g * per:(g + 1) * per]
        m1, i1 = _first_argmax(vals, within, per)
        m2 = jnp.max(jnp.where(within == i1, neg, vals), axis=0, keepdims=True)
        group_scores.append(m1 + m2)
    gs = jnp.concatenate(group_scores, axis=0)

    gidx = _iota2((N_GROUPS, tm), 0)
    keep = jnp.zeros((N_GROUPS, tm), F32)
    for _ in range(TOPK_GROUPS):
        _, gi = _first_argmax(gs, gidx, N_GROUPS)
        hit = gidx == gi
        keep = jnp.where(hit, 1.0, keep)
        gs = jnp.where(hit, neg, gs)
    keep_e = jnp.concatenate([jnp.broadcast_to(keep[g:g + 1], (per, tm)) for g in range(N_GROUPS)], axis=0)

    cand = jnp.where(keep_e > 0.0, biased, neg)
    picks, affs = [], []
    for _ in range(TOP_K):
        _, ei = _first_argmax(cand, eidx, E)
        hit = eidx == ei
        picks.append(ei)
        affs.append(jnp.sum(jnp.where(hit, scores, 0.0), axis=0, keepdims=True))
        cand = jnp.where(hit, neg, cand)
    aff = jnp.concatenate(affs, axis=0)
    idx_ref[...] = jnp.concatenate(picks, axis=0)
    gate_ref[...] = ROUTED_SCALE * aff / jnp.sum(aff, axis=0, keepdims=True)

    @pl.when(pl.program_id(0) == 0)
    def _():
        carry_ref[...] = jnp.zeros_like(carry_ref)

    sel = jnp.where(cand == neg, jnp.where(keep_e > 0.0, 1.0, 0.0), 0.0)
    upper = (_iota2((tm, tm), 0) < _iota2((tm, tm), 1)).astype(F32).astype(BF16)
    before = carry_ref[:, :1] + jnp.dot(sel.astype(BF16), upper, preferred_element_type=F32)
    pos_ref[...] = jnp.concatenate(
        [jnp.sum(jnp.where(eidx == ei, before, 0.0), axis=0, keepdims=True) for ei in picks], axis=0
    ).astype(jnp.int32)
    total = carry_ref[:, :1] + jnp.sum(sel, axis=1, keepdims=True)
    carry_ref[...] = jnp.broadcast_to(total, carry_ref.shape)
    cnt_ref[...] = jnp.broadcast_to(total, cnt_ref.shape).astype(jnp.int32)


def _router(x, router_w, router_bias, tm=512):
    T, D = x.shape
    E = router_w.shape[1]
    idx, gate, pos, cnt = pl.pallas_call(
        _router_kernel,
        grid=(T // tm,),
        in_specs=[pl.BlockSpec((tm, D), lambda i: (i, 0)),
                  pl.BlockSpec((E, D), lambda i: (0, 0)),
                  pl.BlockSpec((E, 1), lambda i: (0, 0))],
        out_specs=[pl.BlockSpec((TOP_K, tm), lambda i: (0, i)), pl.BlockSpec((TOP_K, tm), lambda i: (0, i)),
                   pl.BlockSpec((TOP_K, tm), lambda i: (0, i)), pl.BlockSpec((E, LANES), lambda i: (0, 0))],
        out_shape=[jax.ShapeDtypeStruct((TOP_K, T), jnp.int32), jax.ShapeDtypeStruct((TOP_K, T), F32),
                   jax.ShapeDtypeStruct((TOP_K, T), jnp.int32), jax.ShapeDtypeStruct((E, LANES), jnp.int32)],
        scratch_shapes=[pltpu.VMEM((E, LANES), F32)],
        compiler_params=_params("arbitrary"),
        name="router",
    )(x, router_w.T, router_bias.reshape(E, 1))
    return idx.T, gate.T, pos.T, cnt[:, 0]


def _dispatch_plan(top_e, pos, counts, rows_per_tile):
    T, K = top_e.shape
    E = counts.shape[0]
    n_tiles = (T * K) // rows_per_tile + E
    tiles = (counts + rows_per_tile - 1) // rows_per_tile
    tile_end = jnp.cumsum(tiles)
    n_used = tile_end[-1]
    row_start = (tile_end - tiles) * rows_per_tile
    dest = jnp.take(row_start, top_e) + pos
    tile_id = jnp.minimum(jnp.arange(n_tiles, dtype=jnp.int32), n_used - 1)
    tile_expert = jnp.minimum(jnp.sum(tile_end[None, :] <= tile_id[:, None], axis=1), E - 1)
    owns = tiles > 0
    expert = jnp.arange(E, dtype=jnp.int32)
    later = jnp.where(owns[None, :] & (expert[None, :] > expert[:, None]), expert[None, :], E)
    next_owner = jnp.min(later, axis=1)
    next_owner = jnp.where(next_owner == E, -1, next_owner)
    run_parity = (jnp.cumsum(owns.astype(jnp.int32)) - 1) % 2
    i32 = lambda a: a.astype(jnp.int32)
    return (i32(tile_expert), i32(jnp.take(next_owner, tile_expert)), i32(jnp.take(run_parity, tile_expert)),
            i32(tile_end), i32(n_used.reshape(1)), i32(dest))


def _dispatch_kernel(tend_ref, dest_ref, x_ref, xs_hbm, zero_ref, zsem, sem, *, slots):
    tm = x_ref.shape[0]
    rows = zero_ref.shape[0]
    n_experts = tend_ref.shape[0]

    @pl.when(pl.program_id(0) == 0)
    def _():
        zero_ref[...] = jnp.zeros_like(zero_ref)

        def last_tile(e):
            end = tend_ref[e]
            start = tend_ref[jnp.maximum(e - 1, 0)]
            start = jnp.where(e == 0, 0, start)
            return end > start, pltpu.make_async_copy(
                zero_ref, xs_hbm.at[pl.ds(jnp.maximum(end - 1, 0) * rows, rows)], zsem)

        def start(e, carry):
            owns, copy = last_tile(e)

            @pl.when(owns)
            def _():
                copy.start()
            return carry

        def wait(e, carry):
            owns, copy = last_tile(e)

            @pl.when(owns)
            def _():
                copy.wait()
            return carry

        def unused_tile(n):
            return pltpu.make_async_copy(zero_ref, xs_hbm.at[pl.ds(n * rows, rows)], zsem)

        def start_unused(n, carry):
            unused_tile(n).start()
            return carry

        def wait_unused(n, carry):
            unused_tile(n).wait()
            return carry

        n_used, n_tiles = tend_ref[n_experts - 1], xs_hbm.shape[0] // rows
        lax.fori_loop(0, n_experts, start, 0)
        lax.fori_loop(n_used, n_tiles, start_unused, 0)
        lax.fori_loop(0, n_experts, wait, 0)
        lax.fori_loop(n_used, n_tiles, wait_unused, 0)

    def body(t, carry):
        for k in range(slots):
            pltpu.make_async_copy(x_ref.at[pl.ds(t, 1)], xs_hbm.at[pl.ds(dest_ref[0, t * slots + k], 1)],
                                  sem).start(priority=k % 2)
        return carry
    lax.fori_loop(0, tm, body, 0, unroll=2)
    for k in range(slots):
        pltpu.make_async_copy(x_ref, xs_hbm.at[pl.ds(0, tm)], sem).wait()


def _dispatch(x3p, dest, tile_end, n_tiles, rows_per_tile, tm=512):
    T, half = x3p.shape
    K = dest.shape[1]
    steps = T // tm
    d3 = dest.reshape(steps, 1, tm * K)
    return pl.pallas_call(
        functools.partial(_dispatch_kernel, slots=K),
        grid_spec=pltpu.PrefetchScalarGridSpec(
            num_scalar_prefetch=1,
            grid=(steps,),
            in_specs=[pl.BlockSpec((None, 1, K * tm), lambda i, te: (i, 0, 0), memory_space=pltpu.SMEM),
                      pl.BlockSpec((tm, half), lambda i, te: (i, 0))],
            out_specs=pl.BlockSpec(memory_space=pl.ANY),
            scratch_shapes=[pltpu.VMEM((rows_per_tile, half), jnp.int32),
                            pltpu.SemaphoreType.DMA(()), pltpu.SemaphoreType.DMA(())]),
        out_shape=jax.ShapeDtypeStruct((n_tiles * rows_per_tile, half), jnp.int32),
        compiler_params=_params("arbitrary"),
        name="dispatch",
    )(tile_end, d3, x3p)


def _sparse_experts_kernel(te_ref, nu_ref, next_ref, par_ref, xs_ref, w1_hbm, w3_hbm, w2_hbm, y_ref,
                           wf1, wf3, wf2, wb1, wb3, wb2, sem, *, layer):
    n = pl.program_id(0)

    def weight_copies(e, slot):
        return [pltpu.make_async_copy(w.at[layer, e], buf.at[slot], sem.at[slot])
                for w, buf in ((w1_hbm, wf1), (w3_hbm, wf3), (w2_hbm, wf2))]

    @pl.when(n < nu_ref[0])
    def _():
        @pl.when((n == 0) | (te_ref[n] != te_ref[jnp.maximum(n - 1, 0)]))
        def _():
            slot = par_ref[n]

            @pl.when(n == 0)
            def _():
                for c in weight_copies(te_ref[n], slot):
                    c.start()

            @pl.when(next_ref[n] >= 0)
            def _():
                for c in weight_copies(next_ref[n], 1 - slot):
                    c.start()

            for c in weight_copies(te_ref[n], slot):
                c.wait()
            wb1[...] = wf1[slot].astype(BF16)
            wb3[...] = wf3[slot].astype(BF16)
            wb2[...] = wf2[slot].astype(BF16)

        x = _unpack_rows(xs_ref[...]).astype(BF16)
        h1 = jnp.dot(x, wb1[...], preferred_element_type=F32)
        h3 = jnp.dot(x, wb3[...], preferred_element_type=F32)
        hidden = (h1 * _sigmoid(h1) * h3).astype(BF16)
        y_ref[...] = _pack_rows(jnp.dot(hidden, wb2[...], preferred_element_type=F32))

    @pl.when(n >= nu_ref[0])
    def _():
        y_ref[...] = jnp.zeros_like(y_ref)


def _sparse_experts(xs, tile_expert, n_used, tile_next, tile_parity, w1, w3, w2, layer, rows):
    _, _, D, Hd = w1.shape
    n_tiles = tile_expert.shape[0]
    return pl.pallas_call(
        functools.partial(_sparse_experts_kernel, layer=layer),
        grid_spec=pltpu.PrefetchScalarGridSpec(
            num_scalar_prefetch=4,
            grid=(n_tiles,),
            in_specs=[pl.BlockSpec((rows, D // 2), lambda n, te, nu, nx, par: (jnp.minimum(n, nu[0] - 1), 0)),
                      pl.BlockSpec(memory_space=pl.ANY),
                      pl.BlockSpec(memory_space=pl.ANY),
                      pl.BlockSpec(memory_space=pl.ANY)],
            out_specs=pl.BlockSpec((rows, D // 2), lambda n, te, nu, nx, par: (n, 0)),
            scratch_shapes=[pltpu.VMEM((2, D, Hd), F32), pltpu.VMEM((2, D, Hd), F32), pltpu.VMEM((2, Hd, D), F32),
                            pltpu.VMEM((D, Hd), BF16), pltpu.VMEM((D, Hd), BF16), pltpu.VMEM((Hd, D), BF16),
                            pltpu.SemaphoreType.DMA((2,))]),
        out_shape=jax.ShapeDtypeStruct(xs.shape, jnp.int32),
        compiler_params=_params("arbitrary"),
        name="sparse_experts",
    )(tile_expert, n_used, tile_next, tile_parity, xs, w1, w3, w2)


def _combine_kernel(dest_ref, dest_next_ref, gate_ref, hs_ref, w_ref, res_ref, g_ref, b_ref, y_hbm,
                    o_ref, ob_ref, buf, sem):
    i = pl.program_id(0)
    steps = pl.num_programs(0)
    K, tm = buf.shape[1:3]
    slot = i % 2

    def fetch(d_ref, s):
        def body(t, carry):
            for k in range(K):
                pltpu.make_async_copy(y_hbm.at[pl.ds(d_ref[0, t * K + k], 1)], buf.at[s, k, pl.ds(t, 1)],
                                      sem.at[s]).start(priority=k % 2)
            return carry
        lax.fori_loop(0, tm, body, 0, unroll=2)

    @pl.when(i == 0)
    def _():
        fetch(dest_ref, 0)

    @pl.when(i + 1 < steps)
    def _():
        fetch(dest_next_ref, 1 - slot)

    for k in range(K):
        pltpu.make_async_copy(y_hbm.at[pl.ds(0, tm)], buf.at[slot, k], sem.at[slot]).wait()
    gate = gate_ref[...]
    acc = jnp.zeros(o_ref.shape, F32)
    for k in range(K):
        acc = acc + gate[:, k:k + 1] * _unpack_rows(buf[slot, k])
    acc = acc + _bdot(hs_ref[...], w_ref[...])
    out = _layer_norm_rows(ALPHA * res_ref[...] + acc, g_ref[...], b_ref[...])
    o_ref[...] = out
    ob_ref[...] = out.astype(ob_ref.dtype)


def _combine_ln(y, dest, gate, hs, w, layer, res, g, b, tm=256):
    T, K = dest.shape
    D = 2 * y.shape[1]
    Kh = hs.shape[1]
    steps = T // tm
    d3 = dest.reshape(steps, 1, tm * K)
    row = lambda i: (i, 0)
    fixed = lambda i: (0, 0)
    return pl.pallas_call(
        _combine_kernel,
        grid=(steps,),
        in_specs=[pl.BlockSpec((None, 1, K * tm), lambda i: (i, 0, 0), memory_space=pltpu.SMEM),
                  pl.BlockSpec((None, 1, K * tm), lambda i: (jnp.minimum(i + 1, steps - 1), 0, 0),
                               memory_space=pltpu.SMEM),
                  pl.BlockSpec((tm, K), row),
                  pl.BlockSpec((tm, Kh), row),
                  pl.BlockSpec((None, Kh, D), lambda i: (layer, 0, 0)),
                  pl.BlockSpec((tm, D), row),
                  pl.BlockSpec((1, D), fixed), pl.BlockSpec((1, D), fixed),
                  pl.BlockSpec(memory_space=pl.ANY)],
        out_specs=[pl.BlockSpec((tm, D), row), pl.BlockSpec((tm, D), row)],
        out_shape=[jax.ShapeDtypeStruct((T, D), F32), jax.ShapeDtypeStruct((T, D), BF16)],
        scratch_shapes=[pltpu.VMEM((2, K, tm, D // 2), jnp.int32), pltpu.SemaphoreType.DMA((2,))],
        compiler_params=_params("arbitrary"),
        name="combine_ln",
    )(d3, d3, gate, hs, w, res, g.reshape(1, D), b.reshape(1, D), y)


def _even_mixer(x2, xb, B, S, ln_g, ln_b, w_in, conv_w, conv_b, gate_b, ml_gn_w, rw_mu, rw_w0, rw_w2,
                rw_a0, rw_a2, rw_g2, rw_kk, rw_ka, rw_rk, rw_gn_w, rw_gn_b, w_out, j):
    D = x2.shape[1]
    H = ML_HEADS
    ML = ml_gn_w.shape[0]
    RW = rw_w0.shape[0]
    LO = RW_LORA_W + RW_LORA_A + RW_LORA_G
    ml_main = 3 * ML
    ml_cols = ml_main + 2 * H
    pad_to = 512
    used = ml_main + 3 * RW + LO + LANES
    total = -(-used // pad_to) * pad_to
    w_perm = jnp.concatenate([
        w_in[:, :ml_main], w_in[:, ml_cols:ml_cols + 3 * RW + LO], w_in[:, ml_main:ml_cols],
        jnp.zeros((D, total - used + LANES - 2 * H), w_in.dtype)], axis=1).astype(BF16)
    z3 = _mm(xb, w_perm[None], 0, BF16, 1024, pad_to).reshape(B, S, total)
    gate_col0 = ml_main + 3 * RW + LO
    L = ML_CHUNK
    gates_t = jnp.swapaxes(
        z3[:, :, gate_col0:gate_col0 + 2 * H].astype(F32).reshape(B, S // L, L, 2 * H), 2, 3)
    h_ml = _mlstm(z3, gates_t, conv_w, conv_b, gate_b, ml_gn_w,
                  qk_blk=0, v_blk=1, o_blk=2, gate_blk=gate_col0 // LANES)
    h_rw = _rwkv(z3, rw_mu, rw_w0, rw_w2, rw_a0, rw_a2, rw_g2, rw_kk, rw_ka, rw_rk, rw_gn_w, rw_gn_b,
                 col0=ml_main, lora_col0=ml_main + 3 * RW, width=RW)
    return _mm_res_ln([h_ml.reshape(B * S, ML), h_rw.reshape(B * S, RW)], w_out, j, x2, ln_g, ln_b)


def _odd_mixer(x2, xb, B, S, positions, ln_g, ln_b, w_in, gn_w, gn_b, w_out, j):
    z3 = _mm(xb, w_in, j, BF16, 1024, 512).reshape(B, S, w_in.shape[2])
    h = _retention(z3, positions, gn_w, gn_b)
    return _mm_res_ln([h.reshape(B * S, -1)], w_out, j, x2, ln_g, ln_b)


def _cross_attention(x2, xb, B, S, memb, ln_g, ln_b, wq, wkv, wo, layer):
    D = x2.shape[1]
    q = _mm(xb, wq, layer, BF16, 1024, 512)
    kv = _mm(memb, wkv, layer, BF16, memb.shape[0], 512)
    o = _xattn(q.reshape(B, S, D), kv.reshape(B, -1, 2 * D))
    return _mm_res_ln([o.reshape(B * S, D)], wo, layer, x2, ln_g, ln_b, packed=True)


def _moe(x2, xb, x3p, ln_g, ln_b, router_w, router_bias, w1, w3, w2, sw1, sw3, sw2, layer):
    rows = MOE_ROWS_PER_TILE
    top_e, gate, pos, counts = _router(x2, router_w, router_bias)
    tile_expert, tile_next, tile_parity, tile_end, n_used, dest = _dispatch_plan(top_e, pos, counts, rows)
    xs = _dispatch(x3p, dest, tile_end, tile_expert.shape[0], rows)
    y = _sparse_experts(xs, tile_expert, n_used, tile_next, tile_parity, w1, w3, w2, layer, rows)
    hs = _glu(xb, sw1, sw3, layer)
    return _combine_ln(y, dest, gate, hs, sw2, layer, x2, ln_g, ln_b)


def kernel(x, mem, positions, ln_g, ln_b, xa_wq, xa_wkv, xa_wo, router_w, router_bias, moe_w1, moe_w3, moe_w2, sh_w1, sh_w3, sh_w2, ev_w_in, ml_conv_w, ml_conv_b, ml_gate_b, ml_gn_w, rw_mu, rw_w0, rw_w2, rw_a0, rw_a2, rw_g2, rw_kk, rw_ka, rw_rk, rw_gn_w, rw_gn_b, ev_w_out, od_w_in, ret_gn_w, ret_gn_b, od_w_out):
    B, S, D = x.shape
    x2 = x.reshape(B * S, D)
    xb = x2.astype(BF16)
    memb = mem.reshape(-1, D).astype(BF16)
    (xa_wq, xa_wkv, xa_wo, sh_w1, sh_w3, sh_w2, ev_w_out, od_w_in, od_w_out) = (
        w.astype(BF16) for w in (xa_wq, xa_wkv, xa_wo, sh_w1, sh_w3, sh_w2, ev_w_out, od_w_in, od_w_out))
    for layer in range(ln_g.shape[0]):
        j = layer // 2
        if layer % 2 == 0:
            x2, xb = _even_mixer(x2, xb, B, S, ln_g[layer, 0], ln_b[layer, 0], ev_w_in[j], ml_conv_w[j],
                                 ml_conv_b[j], ml_gate_b[j], ml_gn_w[j], rw_mu[j], rw_w0[j], rw_w2[j],
                                 rw_a0[j], rw_a2[j], rw_g2[j], rw_kk[j], rw_ka[j], rw_rk[j],
                                 rw_gn_w[j], rw_gn_b[j], ev_w_out, j)
        else:
            x2, xb = _odd_mixer(x2, xb, B, S, positions, ln_g[layer, 0], ln_b[layer, 0], od_w_in,
                                ret_gn_w[j], ret_gn_b[j], od_w_out, j)
        x2, xb, x3p = _cross_attention(x2, xb, B, S, memb, ln_g[layer, 1], ln_b[layer, 1], xa_wq, xa_wkv,
                                       xa_wo, layer)
        x2, xb = _moe(x2, xb, x3p, ln_g[layer, 2], ln_b[layer, 2], router_w[layer], router_bias[layer],
                      moe_w1, moe_w3, moe_w2, sh_w1, sh_w3, sh_w2, layer)
    return x2.reshape(B, S, D)
```
